```python
import math
import jax, jax.numpy as jnp
from jax import lax
import numpy as np

D_MODEL = 1024
BATCH = 8
SEQ = 2048
DEPTH = 2
DEC_BATCH = 128
DEC_SEQ = 4
PAST_LEN = 16384
PAGE_SIZE = 128

N_BRANCH = 4
BRANCH_W = D_MODEL // 2
CONV_W = 4
EPS = 1e-6
SSD_HEADDIM = 64
SSD_HEADS = BRANCH_W // SSD_HEADDIM
SSD_GROUPS = 2
SSD_RPG = SSD_HEADS // SSD_GROUPS
SSD_STATE = 64
SSD_CONV_CH = BRANCH_W + 2 * SSD_GROUPS * SSD_STATE
SSD_CHUNK = 128
S5_GROUP = 16
S5_GROUPS = BRANCH_W // S5_GROUP
S5_STATE = 64
ML_HEADS = 4
ML_HEADDIM = BRANCH_W // ML_HEADS
ML_CHUNK = 128
HG_HEADS = 4
HG_HEADDIM = BRANCH_W // HG_HEADS
HG_CHUNK = 64

IN_SPLITS = (BRANCH_W, SSD_CONV_CH, SSD_HEADS,
             BRANCH_W, BRANCH_W,
             BRANCH_W, BRANCH_W, BRANCH_W, ML_HEADS, ML_HEADS,
             BRANCH_W, BRANCH_W, BRANCH_W, BRANCH_W,
             N_BRANCH * D_MODEL)
D_IN = sum(IN_SPLITS)

kernel_name = "hybrid_ssd_s5_mlstm_hgrn2_decode_step"

F32 = jnp.float32


def rmsnorm(x, w):
    xf = x.astype(F32)
    y = xf * lax.rsqrt(jnp.mean(xf * xf, -1, keepdims=True) + EPS)
    return (y * w.astype(F32)).astype(x.dtype)


def head_rmsnorm(h, w, n_heads):
    b, l, wd = h.shape
    hh = h.reshape(b, l, n_heads, wd // n_heads)
    hh = hh * lax.rsqrt(jnp.mean(hh * hh, -1, keepdims=True) + EPS)
    return hh.reshape(b, l, wd) * w.astype(F32)


def causal_conv(x, buf, w, b):
    L = x.shape[1]
    xp = jnp.concatenate([buf.astype(x.dtype), x], axis=1)
    y = b
    for j in range(CONV_W):
        y = y + xp[:, j:j + L] * w[j]
    return y, xp[:, -(CONV_W - 1):]


def to_chunks(a, q):
    b, l = a.shape[:2]
    return jnp.moveaxis(a.reshape((b, l // q, q) + a.shape[2:]), 1, 0)


def from_chunks(a):
    n, b, q = a.shape[:3]
    return jnp.moveaxis(a, 0, 1).reshape((b, n * q) + a.shape[3:])


def cmul(ar, ai, br, bi):
    return ar * br - ai * bi, ar * bi + ai * br


def ssd_scan(x, a, bm, cm, h0):
    q = math.gcd(x.shape[1], SSD_CHUNK)
    tri = jnp.tril(jnp.ones((q, q), bool))

    def step(h, inp):
        xc, ac, bc, cc = inp
        acum = jnp.cumsum(ac, axis=1)
        seg = acum[:, :, None] - acum[:, None, :]
        decay = jnp.exp(jnp.where(tri[None, :, :, None, None], seg, -jnp.inf))
        cb = jnp.einsum('btgn,bsgn->btsg', cc, bc)
        y = jnp.einsum('btsg,btsgr,bsgrp->btgrp', cb, decay, xc)
        y = y + jnp.einsum('btgn,bgrpn->btgrp', cc, h) * jnp.exp(acum)[..., None]
        to_end = jnp.exp(acum[:, -1:] - acum)
        h = h * jnp.exp(acum[:, -1])[..., None, None] + jnp.einsum('bsgn,bsgr,bsgrp->bgrpn', bc, to_end, xc)
        return h, y

    h, ys = lax.scan(step, h0, (to_chunks(x, q), to_chunks(a, q), to_chunks(bm, q), to_chunks(cm, q)))
    return from_chunks(ys), h


def ssd_branch(z, xbc, dt_raw, conv_buf, h0, p):
    bsz, L, _ = z.shape
    xbc, conv_new = causal_conv(xbc, conv_buf, p['ssd_conv_w'], p['ssd_conv_b'])
    xbc = jax.nn.silu(xbc.astype(F32))
    xs, bm, cm = jnp.split(xbc, [BRANCH_W, BRANCH_W + SSD_GROUPS * SSD_STATE], axis=-1)
    xs = xs.reshape(bsz, L, SSD_GROUPS, SSD_RPG, SSD_HEADDIM)
    bm = bm.reshape(bsz, L, SSD_GROUPS, SSD_STATE)
    cm = cm.reshape(bsz, L, SSD_GROUPS, SSD_STATE)
    dt = jax.nn.softplus(dt_raw.astype(F32) + p['ssd_dt_bias'].astype(F32)).reshape(bsz, L, SSD_GROUPS, SSD_RPG)
    A = -jnp.exp(p['ssd_A_log'].astype(F32)).reshape(SSD_GROUPS, SSD_RPG)
    h0g = h0.astype(F32).reshape(bsz, SSD_GROUPS, SSD_RPG, SSD_HEADDIM, SSD_STATE)
    y, h = ssd_scan(xs * dt[..., None], dt * A, bm, cm, h0g)
    y = y + xs * p['ssd_D'].astype(F32).reshape(SSD_GROUPS, SSD_RPG, 1)
    y = y.reshape(bsz, L, BRANCH_W) * jax.nn.silu(z.astype(F32))
    y = rmsnorm(y, p['ssd_norm_w'])
    h = h.reshape(bsz, SSD_HEADS, SSD_HEADDIM, SSD_STATE)
    return y, conv_new.astype(conv_buf.dtype), h.astype(h0.dtype)


def s5_branch(u, gate, h0r, h0i, p):
    bsz, L, _ = u.shape
    uf = u.astype(F32)
    ug = uf.reshape(bsz, L, S5_GROUPS, S5_GROUP)
    dt = jnp.exp(p['s5_log_dt'].astype(F32))[:, None]
    lr = p['s5_A_re'].astype(F32)
    li = p['s5_A_im'].astype(F32)
    mag = jnp.exp(lr * dt)
    abr, abi = mag * jnp.cos(li * dt), mag * jnp.sin(li * dt)
    den = lr * lr + li * li
    cr = ((abr - 1.0) * lr + abi * li) / den
    ci = (abi * lr - (abr - 1.0) * li) / den
    bbr, bbi = cmul(cr[..., None], ci[..., None], p['s5_B_re'].astype(F32), p['s5_B_im'].astype(F32))
    bur = jnp.einsum('blgc,gpc->blgp', ug, bbr)
    bui = jnp.einsum('blgc,gpc->blgp', ug, bbi)
    ir, ii = cmul(abr, abi, h0r.astype(F32), h0i.astype(F32))
    bur = bur.at[:, 0].add(ir)
    bui = bui.at[:, 0].add(ii)
    a_r = jnp.broadcast_to(abr, (1, L) + abr.shape)
    a_i = jnp.broadcast_to(abi, (1, L) + abi.shape)

    def combine(e1, e2):
        a1r, a1i, b1r, b1i = e1
        a2r, a2i, b2r, b2i = e2
        nar, nai = cmul(a2r, a2i, a1r, a1i)
        nbr, nbi = cmul(a2r, a2i, b1r, b1i)
        return nar, nai, nbr + b2r, nbi + b2i

    _, _, hr, hi = lax.associative_scan(combine, (a_r, a_i, bur, bui), axis=1)
    y = (jnp.einsum('blgp,gcp->blgc', hr, p['s5_C_re'].astype(F32))
         - jnp.einsum('blgp,gcp->blgc', hi, p['s5_C_im'].astype(F32)))
    y = y.reshape(bsz, L, BRANCH_W) + uf * p['s5_D'].astype(F32)
    g = jax.nn.gelu(y)
    y = g * jax.nn.sigmoid(g @ p['s5_glu_w'].astype(F32))
    y = y * jax.nn.silu(gate.astype(F32))
    return y, hr[:, -1].astype(h0r.dtype), hi[:, -1].astype(h0i.dtype)


def mlstm_scan(q, k, v, ig, lf, C0, n0, m0):
    qn = math.gcd(q.shape[1], ML_CHUNK)
    tri = jnp.tril(jnp.ones((qn, qn), bool))

    def step(carry, inp):
        C, n, m = carry
        qc, kc, vc, ic, fc = inp
        b = jnp.cumsum(fc, axis=1)
        dlog = jnp.where(tri[None, :, :, None], b[:, :, None] - b[:, None] + ic[:, None], -jnp.inf)
        inter = b + m[:, None]
        m_t = jnp.maximum(inter, jnp.max(dlog, axis=2))
        w = jnp.exp(dlog - m_t[:, :, None])
        s = jnp.einsum('bthk,bshk->btsh', qc, kc) * w
        scale = jnp.exp(inter - m_t)
        num = jnp.einsum('btsh,bshv->bthv', s, vc) + scale[..., None] * jnp.einsum('bthk,bhkv->bthv', qc, C)
        dot = jnp.sum(s, axis=2) + scale * jnp.einsum('bthk,bhk->bth', qc, n)
        h = num / jnp.maximum(jnp.abs(dot), jnp.exp(-m_t))[..., None]
        m_new = m_t[:, -1]
        w_end = jnp.exp(b[:, -1:] - b + ic - m_new[:, None])
        carry_scale = jnp.exp(b[:, -1] + m - m_new)
        C = carry_scale[..., None, None] * C + jnp.einsum('bsh,bshk,bshv->bhkv', w_end, kc, vc)
        n = carry_scale[..., None] * n + jnp.einsum('bsh,bshk->bhk', w_end, kc)
        return (C, n, m_new), h

    (C, n, m), hs = lax.scan(step, (C0, n0, m0),
                             (to_chunks(q, qn), to_chunks(k, qn), to_chunks(v, qn), to_chunks(ig, qn), to_chunks(lf, qn)))
    return from_chunks(hs), C, n, m


def mlstm_branch(xm, z, o_pre, i_pre, f_pre, conv_buf, C0, n0, m0, p):
    bsz, L, _ = xm.shape
    shp = (bsz, L, ML_HEADS, ML_HEADDIM)
    xc, conv_new = causal_conv(xm, conv_buf, p['ml_conv_w'], p['ml_conv_b'])
    xc = jax.nn.silu(xc.astype(F32))
    xch = xc.reshape(shp)
    q = jnp.einsum('blhc,hck->blhk', xch, p['ml_wq'].astype(F32))
    k = jnp.einsum('blhc,hck->blhk', xch, p['ml_wk'].astype(F32)) * ML_HEADDIM ** -0.5
    v = jnp.einsum('blhc,hcv->blhv', xm.astype(F32).reshape(shp), p['ml_wv'].astype(F32))
    ig = i_pre.astype(F32) + p['ml_i_bias'].astype(F32)
    lf = jax.nn.log_sigmoid(f_pre.astype(F32) + p['ml_f_bias'].astype(F32))
    h, C, n, m = mlstm_scan(q, k, v, ig, lf, C0.astype(F32), n0.astype(F32), m0.astype(F32))
    h = h * jax.nn.sigmoid(o_pre.astype(F32)).reshape(shp)
    hc = h - jnp.mean(h, -1, keepdims=True)
    h = hc * lax.rsqrt(jnp.mean(hc * hc, -1, keepdims=True) + EPS)
    h = h.reshape(bsz, L, BRANCH_W) * p['ml_norm_w'].astype(F32) + p['ml_skip'].astype(F32) * xc
    y = h * jax.nn.silu(z.astype(F32))
    return y, conv_new.astype(conv_buf.dtype), C.astype(C0.dtype), n.astype(n0.dtype), m.astype(m0.dtype)


def hgrn_scan(q, k, v, lf, S0):
    qn = math.gcd(q.shape[1], HG_CHUNK)
    tri = jnp.tril(jnp.ones((qn, qn), bool))

    def step(S, inp):
        qc, kc, vc, lc = inp
        g = jnp.cumsum(lc, axis=1)
        diff = g[:, :, None] - g[:, None, :]
        decay = jnp.exp(jnp.where(tri[None, :, :, None, None], diff, -jnp.inf))
        att = jnp.einsum('bthk,btshk,bshk->btsh', qc, decay, kc)
        o = jnp.einsum('btsh,bshv->bthv', att, vc) + jnp.einsum('bthk,bhkv->bthv', qc * jnp.exp(g), S)
        g_last = g[:, -1]
        S = S * jnp.exp(g_last)[..., None] + jnp.einsum('bshk,bshv->bhkv', kc * jnp.exp(g_last[:, None] - g), vc)
        return S, o

    S, os_ = lax.scan(step, S0, (to_chunks(q, qn), to_chunks(k, qn), to_chunks(v, qn), to_chunks(lf, qn)))
    return from_chunks(os_), S


def hgrn_branch(f_pre, i_in, q_pre, g, S0, lb, p):
    bsz, L, _ = f_pre.shape
    shp = (bsz, L, HG_HEADS, HG_HEADDIM)
    fg = lb + (1.0 - lb) * jax.nn.sigmoid(f_pre.astype(F32))
    lf = jnp.log(fg).reshape(shp)
    k = (1.0 - fg).reshape(shp)
    q = (jax.nn.silu(q_pre.astype(F32)) * HG_HEADDIM ** -0.5).reshape(shp)
    v = i_in.astype(F32).reshape(shp)
    o, S = hgrn_scan(q, k, v, lf, S0.astype(F32))
    o = head_rmsnorm(o.reshape(bsz, L, BRANCH_W), p['hg_norm_w'], HG_HEADS) * jax.nn.silu(g.astype(F32))
    return o, S.astype(S0.dtype)


def layer(x, st, p, lb):
    ssd_conv, ssd_h, s5_re, s5_im, ml_conv, ml_C, ml_n, ml_m, hg_S = st
    bsz, L, _ = x.shape
    proj = rmsnorm(x, p['norm_w']) @ p['w_in']
    (a_z, a_xbc, a_dt, b_u, b_gate, c_x, c_z, c_o, c_i, c_f,
     d_f, d_i, d_q, d_g, merge) = jnp.split(proj, np.cumsum(IN_SPLITS)[:-1].tolist(), axis=-1)
    y_a, ssd_conv, ssd_h = ssd_branch(a_z, a_xbc, a_dt, ssd_conv, ssd_h, p)
    y_b, s5_re, s5_im = s5_branch(b_u, b_gate, s5_re, s5_im, p)
    y_c, ml_conv, ml_C, ml_n, ml_m = mlstm_branch(c_x, c_z, c_o, c_i, c_f, ml_conv, ml_C, ml_n, ml_m, p)
    y_d, hg_S = hgrn_branch(d_f, d_i, d_q, d_g, hg_S, lb, p)
    gates = jax.nn.sigmoid(merge.astype(F32)).reshape(bsz, L, N_BRANCH, D_MODEL)
    merged = gates[:, :, 0] * (y_a @ p['w_branch'][0].astype(F32))
    merged = merged + gates[:, :, 1] * (y_b @ p['w_branch'][1].astype(F32))
    merged = merged + gates[:, :, 2] * (y_c @ p['w_branch'][2].astype(F32))
    merged = merged + gates[:, :, 3] * (y_d @ p['w_branch'][3].astype(F32))
    out = merged.astype(x.dtype) @ p['w_out']
    return x + out, (ssd_conv, ssd_h, s5_re, s5_im, ml_conv, ml_C, ml_n, ml_m, hg_S)


def run_trunk(x, states, w, hg_lb, final_norm_w):
    per_layer = []
    for l in range(DEPTH):
        p = {name: arr[l] for name, arr in w.items()}
        x, new = layer(x, tuple(s[l] for s in states), p, hg_lb[l])
        per_layer.append(new)
    new_states = tuple(jnp.stack(ss, axis=0) for ss in zip(*per_layer))
    return rmsnorm(x, final_norm_w), new_states


def setup_inputs(seed: int = 0) -> dict:
    key = jax.random.key(seed)
    ks = iter(jax.random.split(key, 64))

    def nrm(shape, scale=1.0):
        return scale * jax.random.normal(next(ks), shape, F32)

    def unif(shape, lo, hi):
        return jax.random.uniform(next(ks), shape, F32, lo, hi)

    ssd_dt = jnp.exp(unif((DEPTH, SSD_HEADS), math.log(1e-3), math.log(1e-1)))
    return {
        'x_prompt': nrm((BATCH, SEQ, D_MODEL)),
        'x_sample': nrm((DEC_BATCH, DEC_SEQ, D_MODEL)),
        'state_ssd_conv': nrm((DEPTH, DEC_BATCH, CONV_W - 1, SSD_CONV_CH)),
        'state_ssd': nrm((DEPTH, DEC_BATCH, SSD_HEADS, SSD_HEADDIM, SSD_STATE), 0.1),
        'state_s5_re': nrm((DEPTH, DEC_BATCH, S5_GROUPS, S5_STATE), 0.1),
        'state_s5_im': nrm((DEPTH, DEC_BATCH, S5_GROUPS, S5_STATE), 0.1),
        'state_mlstm_conv': nrm((DEPTH, DEC_BATCH, CONV_W - 1, BRANCH_W)),
        'state_mlstm_C': nrm((DEPTH, DEC_BATCH, ML_HEADS, ML_HEADDIM, ML_HEADDIM), 0.1),
        'state_mlstm_n': nrm((DEPTH, DEC_BATCH, ML_HEADS, ML_HEADDIM), 0.1),
        'state_mlstm_m': nrm((DEPTH, DEC_BATCH, ML_HEADS), 0.5),
        'state_hgrn': nrm((DEPTH, DEC_BATCH, HG_HEADS, HG_HEADDIM, HG_HEADDIM), 0.1),
        'norm_w': 1.0 + nrm((DEPTH, D_MODEL), 0.02),
        'w_in': nrm((DEPTH, D_MODEL, D_IN), D_MODEL ** -0.5),
        'ssd_conv_w': nrm((DEPTH, CONV_W, SSD_CONV_CH), 0.5),
        'ssd_conv_b': nrm((DEPTH, SSD_CONV_CH), 0.02),
        'ssd_dt_bias': ssd_dt + jnp.log(-jnp.expm1(-ssd_dt)),
        'ssd_A_log': jnp.log(unif((DEPTH, SSD_HEADS), 1.0, 16.0)),
        'ssd_D': 1.0 + nrm((DEPTH, SSD_HEADS), 0.1),
        'ssd_norm_w': 1.0 + nrm((DEPTH, BRANCH_W), 0.02),
        's5_A_re': -0.5 + nrm((DEPTH, S5_GROUPS, S5_STATE), 0.01),
        's5_A_im': math.pi * jnp.arange(S5_STATE, dtype=F32) + nrm((DEPTH, S5_GROUPS, S5_STATE), 0.01),
        's5_B_re': nrm((DEPTH, S5_GROUPS, S5_STATE, S5_GROUP), (2.0 * S5_GROUP) ** -0.5),
        's5_B_im': nrm((DEPTH, S5_GROUPS, S5_STATE, S5_GROUP), (2.0 * S5_GROUP) ** -0.5),
        's5_C_re': nrm((DEPTH, S5_GROUPS, S5_GROUP, S5_STATE), (2.0 * S5_STATE) ** -0.5),
        's5_C_im': nrm((DEPTH, S5_GROUPS, S5_GROUP, S5_STATE), (2.0 * S5_STATE) ** -0.5),
        's5_D': nrm((DEPTH, BRANCH_W)),
        's5_log_dt': unif((DEPTH, S5_GROUPS), math.log(1e-3), math.log(1e-1)),
        's5_glu_w': nrm((DEPTH, BRANCH_W, BRANCH_W), BRANCH_W ** -0.5),
        'ml_conv_w': nrm((DEPTH, CONV_W, BRANCH_W), 0.5),
        'ml_conv_b': nrm((DEPTH, BRANCH_W), 0.02),
        'ml_wq': nrm((DEPTH, ML_HEADS, ML_HEADDIM, ML_HEADDIM), ML_HEADDIM ** -0.5),
        'ml_wk': nrm((DEPTH, ML_HEADS, ML_HEADDIM, ML_HEADDIM), ML_HEADDIM ** -0.5),
        'ml_wv': nrm((DEPTH, ML_HEADS, ML_HEADDIM, ML_HEADDIM), ML_HEADDIM ** -0.5),
        'ml_i_bias': nrm((DEPTH, ML_HEADS), 0.1),
        'ml_f_bias': jnp.linspace(3.0, 6.0, ML_HEADS, dtype=F32) + nrm((DEPTH, ML_HEADS), 0.1),
        'ml_norm_w': 1.0 + nrm((DEPTH, BRANCH_W), 0.02),
        'ml_skip': 1.0 + nrm((DEPTH, BRANCH_W), 0.02),
        'hg_lb_logits': nrm((DEPTH, BRANCH_W)),
        'hg_norm_w': 1.0 + nrm((DEPTH, BRANCH_W), 0.02),
        'w_branch': nrm((DEPTH, N_BRANCH, BRANCH_W, D_MODEL), BRANCH_W ** -0.5),
        'w_out': nrm((DEPTH, D_MODEL, D_MODEL), D_MODEL ** -0.5),
        'final_norm_w': 1.0 + nrm((D_MODEL,), 0.02),
    }


def reference(x_prompt, x_sample, state_ssd_conv, state_ssd, state_s5_re, state_s5_im,
              state_mlstm_conv, state_mlstm_C, state_mlstm_n, state_mlstm_m, state_hgrn,
              norm_w, w_in, ssd_conv_w, ssd_conv_b, ssd_dt_bias, ssd_A_log, ssd_D, ssd_norm_w,
              s5_A_re, s5_A_im, s5_B_re, s5_B_im, s5_C_re, s5_C_im, s5_D, s5_log_dt, s5_glu_w,
              ml_conv_w, ml_conv_b, ml_wq, ml_wk, ml_wv, ml_i_bias, ml_f_bias, ml_norm_w, ml_skip,
              hg_lb_logits, hg_norm_w, w_branch, w_out, final_norm_w):
    w = {'norm_w': norm_w, 'w_in': w_in,
         'ssd_conv_w': ssd_conv_w, 'ssd_conv_b': ssd_conv_b, 'ssd_dt_bias': ssd_dt_bias,
         'ssd_A_log': ssd_A_log, 'ssd_D': ssd_D, 'ssd_norm_w': ssd_norm_w,
         's5_A_re': s5_A_re, 's5_A_im': s5_A_im, 's5_B_re': s5_B_re, 's5_B_im': s5_B_im,
         's5_C_re': s5_C_re, 's5_C_im': s5_C_im, 's5_D': s5_D, 's5_log_dt': s5_log_dt, 's5_glu_w': s5_glu_w,
         'ml_conv_w': ml_conv_w, 'ml_conv_b': ml_conv_b, 'ml_wq': ml_wq, 'ml_wk': ml_wk, 'ml_wv': ml_wv,
         'ml_i_bias': ml_i_bias, 'ml_f_bias': ml_f_bias, 'ml_norm_w': ml_norm_w, 'ml_skip': ml_skip,
         'hg_norm_w': hg_norm_w, 'w_branch': w_branch, 'w_out': w_out}
    lb_cum = jnp.cumsum(jax.nn.softmax(hg_lb_logits.astype(F32), axis=0), axis=0)
    hg_lb = lb_cum - lb_cum[0]
    sample_states = (state_ssd_conv, state_ssd, state_s5_re, state_s5_im, state_mlstm_conv,
                     state_mlstm_C, state_mlstm_n, state_mlstm_m, state_hgrn)
    prompt_states = tuple(jnp.zeros((DEPTH, x_prompt.shape[0]) + s.shape[2:], x_prompt.dtype) for s in sample_states)
    y_prompt, new_p = run_trunk(x_prompt, prompt_states, w, hg_lb, final_norm_w)
    y_sample, new_s = run_trunk(x_sample, sample_states, w, hg_lb, final_norm_w)
    p_ssd_conv, p_ssd, p_s5_re, p_s5_im, p_ml_conv, p_ml_C, p_ml_n, p_ml_m, p_hgrn = new_p
    s_ssd_conv, s_ssd, s_s5_re, s_s5_im, s_ml_conv, s_ml_C, s_ml_n, s_ml_m, s_hgrn = new_s
    return (y_prompt, y_sample, p_ssd_conv, s_ssd_conv, p_ssd, s_ssd, p_s5_re, s_s5_re, p_s5_im, s_s5_im,
            p_ml_conv, s_ml_conv, p_ml_C, s_ml_C, p_ml_n, s_ml_n, p_ml_m, s_ml_m, p_hgrn, s_hgrn)
```

```python
import functools
import math

import numpy as np
import jax
import jax.numpy as jnp
from jax import lax
from jax.experimental import pallas as pl
from jax.experimental.pallas import tpu as pltpu

F32 = jnp.float32
BF16 = jnp.bfloat16

D_MODEL = 1024
DEPTH = 2
BRANCH_W = 512
CONV_W = 4
EPS = 1e-6
SSD_HEADS = 8
SSD_HEADDIM = 64
SSD_STATE = 64
SSD_GROUPS = 2
SSD_CONV_CH = 768
S5_GROUPS = 32
S5_GROUP = 16
S5_STATE = 64
S5_LANES = S5_GROUPS * S5_STATE
ML_HEADS = 4
ML_HEADDIM = 128
HG_HEADS = 4
HG_HEADDIM = 128
N_BRANCH = 4

NEG = -1e30

N_PROJ = 10240
COL_MERGE = 0
COL_A_Z = 4096
COL_A_X = 4608
COL_B_U = 5120
COL_B_GATE = 5632
COL_C_X = 6144
COL_C_Z = 6656
COL_C_O = 7168
COL_D_F = 7680
COL_D_I = 8192
COL_D_Q = 8704
COL_D_G = 9216
COL_A_BC = 9728
COL_SMALL = 9984
N_SMALL = 16
SMALL_DT = 0
SMALL_I = 8
SMALL_F = 12

NN = (((1,), (0,)), ((), ()))
NT = (((1,), (1,)), ((), ()))
TN = (((0,), (0,)), ((), ()))


def _mm(a, b, lowp, dims=NN):
    if lowp:
        a = a.astype(BF16)
        b = b.astype(BF16)
    return lax.dot_general(a, b, dims, preferred_element_type=F32)


def _split3(a):
    hi = a.astype(BF16)
    r = a - hi.astype(F32)
    mid = r.astype(BF16)
    lo = (r - mid.astype(F32)).astype(BF16)
    return hi, mid, lo


def _xdot(a, b, dims, split, lowp):
    if not lowp:
        return lax.dot_general(a, b, dims, precision=lax.Precision.HIGHEST,
                               preferred_element_type=F32)
    if split == 0:
        other = b.astype(BF16)
        return sum(lax.dot_general(p, other, dims, preferred_element_type=F32) for p in _split3(a))
    other = a.astype(BF16)
    return sum(lax.dot_general(other, p, dims, preferred_element_type=F32) for p in _split3(b))


def _sigmoid(x):
    return jax.nn.sigmoid(x)


def _silu(x):
    return x * jax.nn.sigmoid(x)


def _softplus(x):
    return jnp.maximum(x, 0.0) + jnp.log1p(jnp.exp(-jnp.abs(x)))


def _conv_chunk(xp_scr, cw_ref, cb_ref, q, valid):
    cw = cw_ref[...]
    conv = (cb_ref[...]
            + cw[0:1] * xp_scr[5:5 + q, :]
            + cw[1:2] * xp_scr[6:6 + q, :]
            + cw[2:3] * xp_scr[7:7 + q, :]
            + cw[3:4] * xp_scr[8:8 + q, :])
    carry = xp_scr[5 + valid:8 + valid, :]
    xp_scr[5:8, :] = carry
    return conv, carry


def _inproj_kernel(x_ref, nw_ref, w_ref, wst_ref, o_ref, st_ref, xn_scr):
    @pl.when(pl.program_id(1) == 0)
    def _():
        x = x_ref[...]
        xn = x * lax.rsqrt(jnp.mean(x * x, -1, keepdims=True) + EPS) * nw_ref[...]
        xb = xn.astype(BF16)
        xn_scr[...] = xb
        st_ref[...] = lax.dot_general(wst_ref[...], xb, NT, preferred_element_type=F32)

    o_ref[...] = jnp.dot(xn_scr[...], w_ref[...], preferred_element_type=F32)


def _inproj(x, nw, w, wst):
    t = x.shape[0]
    tm = min(1024, t)
    tn = 1024
    return pl.pallas_call(
        _inproj_kernel,
        grid=(t // tm, N_PROJ // tn),
        in_specs=[
            pl.BlockSpec((tm, D_MODEL), lambda i, j: (i, 0)),
            pl.BlockSpec((1, D_MODEL), lambda i, j: (0, 0)),
            pl.BlockSpec((D_MODEL, tn), lambda i, j: (0, j)),
            pl.BlockSpec((N_SMALL, D_MODEL), lambda i, j: (0, 0)),
        ],
        out_specs=[
            pl.BlockSpec((tm, tn), lambda i, j: (i, j)),
            pl.BlockSpec((N_SMALL, tm), lambda i, j: (0, i)),
        ],
        out_shape=[
            jax.ShapeDtypeStruct((t, N_PROJ), F32),
            jax.ShapeDtypeStruct((N_SMALL, t), F32),
        ],
        scratch_shapes=[pltpu.VMEM((tm, D_MODEL), BF16)],
        compiler_params=pltpu.CompilerParams(
            dimension_semantics=("arbitrary", "arbitrary"),
            vmem_limit_bytes=48 * 1024 * 1024),
        name="inproj",
    )(x, nw, w, wst)


def _merge_kernel(g_ref, ya_ref, yb_ref, yc_ref, yd_ref, x_ref, wb_ref, wo_ref, fnw_ref, o_ref, *, final):
    acc = None
    for i, y_ref in enumerate((ya_ref, yb_ref, yc_ref, yd_ref)):
        gate = _sigmoid(g_ref[:, i * D_MODEL:(i + 1) * D_MODEL])
        term = gate * jnp.dot(y_ref[...].astype(BF16), wb_ref[i], preferred_element_type=F32)
        acc = term if acc is None else acc + term
    out = x_ref[...] + jnp.dot(acc.astype(BF16), wo_ref[...], preferred_element_type=F32)
    if final:
        out = out * lax.rsqrt(jnp.mean(out * out, -1, keepdims=True) + EPS) * fnw_ref[...]
    o_ref[...] = out


def _merge(proj, ys, x, wb, wo, fnw, final):
    t = x.shape[0]
    tm = min(256, t)
    yspec = pl.BlockSpec((tm, BRANCH_W), lambda i: (i, 0))
    return pl.pallas_call(
        functools.partial(_merge_kernel, final=final),
        grid=(t // tm,),
        in_specs=[
            pl.BlockSpec((tm, N_BRANCH * D_MODEL), lambda i: (i, COL_MERGE // (N_BRANCH * D_MODEL))),
            yspec, yspec, yspec, yspec,
            pl.BlockSpec((tm, D_MODEL), lambda i: (i, 0)),
            pl.BlockSpec((N_BRANCH, BRANCH_W, D_MODEL), lambda i: (0, 0, 0)),
            pl.BlockSpec((D_MODEL, D_MODEL), lambda i: (0, 0)),
            pl.BlockSpec((1, D_MODEL), lambda i: (0, 0)),
        ],
        out_specs=pl.BlockSpec((tm, D_MODEL), lambda i: (i, 0)),
        out_shape=jax.ShapeDtypeStruct((t, D_MODEL), F32),
        compiler_params=pltpu.CompilerParams(
            dimension_semantics=("arbitrary",),
            vmem_limit_bytes=48 * 1024 * 1024),
        name="merge",
    )(proj, *ys, x, wb, wo, fnw)


def _col_spec(q, nc, col, width):
    return pl.BlockSpec((q, width), lambda b, c: (b * nc + c, col // width))


def _row_spec(q):
    return pl.BlockSpec((1, N_SMALL, q), lambda b, c: (b, 0, c))


def _const_spec(shape):
    nd = len(shape)
    return pl.BlockSpec(shape, lambda b, c: (0,) * nd)


def _state_in_spec(layer, tail):
    nd = len(tail)
    return pl.BlockSpec((1, 1) + tail, lambda b, c: (layer, b) + (0,) * nd)


def _state_out_spec(tail):
    nd = len(tail)
    return pl.BlockSpec((1,) + tail, lambda b, c: (b,) + (0,) * nd)


def _y_spec(q, nc):
    return pl.BlockSpec((q, BRANCH_W), lambda b, c: (b * nc + c, 0))


_MIXER_PARAMS = pltpu.CompilerParams(
    dimension_semantics=("arbitrary", "arbitrary"),
    vmem_limit_bytes=48 * 1024 * 1024)


def _ssd_kernel(z_ref, x_ref, bc_ref, sc_ref, sr_ref, conv0_ref, h0_ref,
                cw_ref, cb_ref, bcol_ref, brow_ref, alc_ref, alr_ref, dexp_ref, nw_ref,
                e_ref, tl_ref, tu_ref,
                y_ref, convo_ref, ho_ref,
                xp_scr, h_scr, y_scr, *, q, nc, valid, lowp):
    c = pl.program_id(1)

    @pl.when(c == 0)
    def _():
        xp_scr[5:8, :] = conv0_ref[0, 0]
        h_scr[...] = h0_ref[0, 0]

    xp_scr[8:8 + q, 0:BRANCH_W] = x_ref[...]
    xp_scr[8:8 + q, BRANCH_W:SSD_CONV_CH] = bc_ref[...]
    conv, carry = _conv_chunk(xp_scr, cw_ref, cb_ref, q, valid)
    convo_ref[0] = carry
    xbc = _silu(conv)
    xs = xbc[:, 0:BRANCH_W]
    bm = xbc[:, BRANCH_W:BRANCH_W + 128]
    cm = xbc[:, BRANCH_W + 128:BRANCH_W + 256]

    dt_c = _softplus(sc_ref[...] + bcol_ref[...])
    a_c = dt_c * (-jnp.exp(alc_ref[...]))
    dt_r = _softplus(sr_ref[0] + brow_ref[...])
    a_r = dt_r * (-jnp.exp(alr_ref[...]))
    if valid < q:
        tcol = lax.broadcasted_iota(jnp.int32, (q, 1), 0)
        trow = lax.broadcasted_iota(jnp.int32, (1, q), 1)
        a_c = jnp.where(tcol < valid, a_c, 0.0)
        dt_c = jnp.where(tcol < valid, dt_c, 0.0)
        a_r = jnp.where(trow < valid, a_r, 0.0)

    tl = tl_ref[...]
    acum_c = _xdot(tl, a_c, NN, 1, lowp)
    acum_r = _xdot(a_r, tu_ref[...], NN, 0, lowp)
    e = e_ref[...]
    dt_e = _xdot(dt_c, e, NN, 0, lowp)
    acum_e = _xdot(acum_c, e, NN, 0, lowp)
    xdt = xs * dt_e
    xend = xdt * jnp.exp(acum_e[q - 1:q, :] - acum_e)
    eac = jnp.exp(acum_e)
    dec_last = jnp.exp(acum_c[q - 1:q, :])

    tri = (lax.broadcasted_iota(jnp.int32, (q, q), 0) >= lax.broadcasted_iota(jnp.int32, (q, q), 1))
    rpg = SSD_HEADS // SSD_GROUPS
    for g in range(SSD_GROUPS):
        bg = bm[:, g * SSD_STATE:(g + 1) * SSD_STATE]
        cg = cm[:, g * SSD_STATE:(g + 1) * SSD_STATE]
        cb = _mm(cg, bg, lowp, NT)
        for r in range(rpg):
            h = g * rpg + r
            hs = slice(h * SSD_HEADDIM, (h + 1) * SSD_HEADDIM)
            seg = acum_c[:, h:h + 1] - acum_r[h:h + 1, :]
            m = cb * jnp.exp(jnp.where(tri, seg, NEG))
            hp = h_scr[h]
            yh = _mm(m, xdt[:, hs], lowp) + _mm(cg, hp, lowp, NT) * eac[:, hs]
            y_scr[:, hs] = yh
            h_scr[h] = hp * dec_last[:, h:h + 1] + _mm(xend[:, hs], bg, lowp, TN)

    y = y_scr[...] + xs * dexp_ref[...]
    y = y * _silu(z_ref[...])
    y = y * lax.rsqrt(jnp.mean(y * y, -1, keepdims=True) + EPS) * nw_ref[...]
    y_ref[...] = y

    @pl.when(c == nc - 1)
    def _():
        ho_ref[0] = h_scr[...]


def _ssd(proj, srow, conv_state, h_state, layer, w, consts, *, b, q, nc, valid, lowp):
    t = b * nc * q
    kern = functools.partial(_ssd_kernel, q=q, nc=nc, valid=valid, lowp=lowp)
    return pl.pallas_call(
        kern,
        grid=(b, nc),
        in_specs=[
            _col_spec(q, nc, COL_A_Z, BRANCH_W),
            _col_spec(q, nc, COL_A_X, BRANCH_W),
            _col_spec(q, nc, COL_A_BC, 256),
            _col_spec(q, nc, COL_SMALL, 128),
            _row_spec(q),
            _state_in_spec(layer, (CONV_W - 1, SSD_CONV_CH)),
            _state_in_spec(layer, (SSD_HEADS, SSD_HEADDIM, SSD_STATE)),
            _const_spec((CONV_W, SSD_CONV_CH)),
            _const_spec((1, SSD_CONV_CH)),
            _const_spec((1, 128)),
            _const_spec((N_SMALL, 1)),
            _const_spec((1, 128)),
            _const_spec((N_SMALL, 1)),
            _const_spec((1, BRANCH_W)),
            _const_spec((1, BRANCH_W)),
            _const_spec((128, BRANCH_W)),
            _const_spec((q, q)),
            _const_spec((q, q)),
        ],
        out_specs=[
            _y_spec(q, nc),
            _state_out_spec((CONV_W - 1, SSD_CONV_CH)),
            _state_out_spec((SSD_HEADS, SSD_HEADDIM, SSD_STATE)),
        ],
        out_shape=[
            jax.ShapeDtypeStruct((t, BRANCH_W), F32),
            jax.ShapeDtypeStruct((b, CONV_W - 1, SSD_CONV_CH), F32),
            jax.ShapeDtypeStruct((b, SSD_HEADS, SSD_HEADDIM, SSD_STATE), F32),
        ],
        scratch_shapes=[
            pltpu.VMEM((q + 8, SSD_CONV_CH), F32),
            pltpu.VMEM((SSD_HEADS, SSD_HEADDIM, SSD_STATE), F32),
            pltpu.VMEM((q, BRANCH_W), F32),
        ],
        compiler_params=_MIXER_PARAMS,
        name="ssd",
    )(proj, proj, proj, proj, srow, conv_state, h_state,
      w['ssd_conv_w'], w['ssd_conv_b'], w['bias_col'], w['bias_row'], w['alog_col'], w['alog_row'],
      w['ssd_D_exp'], w['ssd_norm_w'], consts['expand'], consts['tril'], consts['triu'])


def _s5_kernel(u_ref, gate_ref, hr0_ref, hi0_ref,
               wbr_ref, wbi_ref, wcr_ref, wci_ref, pr_ref, pi_ref, tr_ref, ti_ref, d_ref, glu_ref,
               y_ref, hro_ref, hio_ref,
               hr_scr, hi_scr, cr_scr, ci_scr, *, q, nc, valid, lowp):
    c = pl.program_id(1)

    @pl.when(c == 0)
    def _():
        cr_scr[...] = hr0_ref[0, 0]
        ci_scr[...] = hi0_ref[0, 0]

    u = u_ref[...]
    nblk = S5_LANES // 512
    for j in range(nblk):
        uj = u[:, j * 128:(j + 1) * 128]
        hr_scr[:, j * 512:(j + 1) * 512] = _mm(uj, wbr_ref[j], lowp)
        hi_scr[:, j * 512:(j + 1) * 512] = _mm(uj, wbi_ref[j], lowp)

    last = (valid - 1) % 8
    for j in range(nblk):
        sl = slice(j * 512, (j + 1) * 512)

        def body(blk, carry, sl=sl):
            cr, ci = carry
            r0 = pl.multiple_of(blk * 8, 8)
            xr = hr_scr[pl.ds(r0, 8), sl]
            xi = hi_scr[pl.ds(r0, 8), sl]
            for di, d in enumerate((1, 2, 4)):
                sr = pltpu.roll(xr, d, 0)
                si = pltpu.roll(xi, d, 0)
                ar = pr_ref[di, :, sl]
                ai = pi_ref[di, :, sl]
                xr, xi = xr + (ar * sr - ai * si), xi + (ar * si + ai * sr)
            tr = tr_ref[:, sl]
            ti = ti_ref[:, sl]
            xr, xi = xr + (tr * cr - ti * ci), xi + (tr * ci + ti * cr)
            hr_scr[pl.ds(r0, 8), sl] = xr
            hi_scr[pl.ds(r0, 8), sl] = xi
            return xr[last:last + 1, :], xi[last:last + 1, :]

        cr, ci = lax.fori_loop(0, q // 8, body, (cr_scr[:, sl], ci_scr[:, sl]))
        cr_scr[:, sl] = cr
        ci_scr[:, sl] = ci

    parts = []
    for j in range(nblk):
        sl = slice(j * 512, (j + 1) * 512)
        parts.append(_mm(hr_scr[:, sl], wcr_ref[j], lowp) - _mm(hi_scr[:, sl], wci_ref[j], lowp))
    y = jnp.concatenate(parts, axis=1) + u * d_ref[...]
    g = jax.nn.gelu(y)
    y = g * _sigmoid(_mm(g, glu_ref[...], lowp))
    y_ref[...] = y * _silu(gate_ref[...])

    @pl.when(c == nc - 1)
    def _():
        hro_ref[0] = cr_scr[...]
        hio_ref[0] = ci_scr[...]


def _s5(proj, hr_state, hi_state, layer, w, *, b, q, nc, valid, lowp):
    t = b * nc * q
    kern = functools.partial(_s5_kernel, q=q, nc=nc, valid=valid, lowp=lowp)
    return pl.pallas_call(
        kern,
        grid=(b, nc),
        in_specs=[
            _col_spec(q, nc, COL_B_U, BRANCH_W),
            _col_spec(q, nc, COL_B_GATE, BRANCH_W),
            _state_in_spec(layer, (1, S5_LANES)),
            _state_in_spec(layer, (1, S5_LANES)),
            _const_spec((4, 128, 512)),
            _const_spec((4, 128, 512)),
            _const_spec((4, 512, 128)),
            _const_spec((4, 512, 128)),
            _const_spec((3, 8, S5_LANES)),
            _const_spec((3, 8, S5_LANES)),
            _const_spec((8, S5_LANES)),
            _const_spec((8, S5_LANES)),
            _const_spec((1, BRANCH_W)),
            _const_spec((BRANCH_W, BRANCH_W)),
        ],
        out_specs=[
            _y_spec(q, nc),
            _state_out_spec((1, S5_LANES)),
            _state_out_spec((1, S5_LANES)),
        ],
        out_shape=[
            jax.ShapeDtypeStruct((t, BRANCH_W), F32),
            jax.ShapeDtypeStruct((b, 1, S5_LANES), F32),
            jax.ShapeDtypeStruct((b, 1, S5_LANES), F32),
        ],
        scratch_shapes=[
            pltpu.VMEM((q, S5_LANES), F32),
            pltpu.VMEM((q, S5_LANES), F32),
            pltpu.VMEM((1, S5_LANES), F32),
            pltpu.VMEM((1, S5_LANES), F32),
        ],
        compiler_params=_MIXER_PARAMS,
        name="s5",
    )(proj, proj, hr_state, hi_state,
      w['s5_wbr'], w['s5_wbi'], w['s5_wcr'], w['s5_wci'], w['s5_pr'], w['s5_pi'], w['s5_tr'], w['s5_ti'],
      w['s5_D'], w['s5_glu_w'])


def _mlstm_kernel(x_ref, z_ref, o_ref, sc_ref, sr_ref, conv0_ref, c0_ref, n0_ref, m0_ref,
                  cw_ref, cb_ref, wq_ref, wk_ref, wv_ref, bcol_ref, brow_ref, nw_ref, skip_ref,
                  tl_ref, tu_ref,
                  y_ref, convo_ref, co_ref, no_ref, mo_ref,
                  xp_scr, c_scr, n_scr, m_scr, h_scr, *, q, nc, valid, lowp):
    c = pl.program_id(1)

    @pl.when(c == 0)
    def _():
        xp_scr[5:8, :] = conv0_ref[0, 0]
        c_scr[...] = c0_ref[0, 0]
        n_scr[...] = n0_ref[0, 0]
        m_scr[...] = m0_ref[0, 0]

    x = x_ref[...]
    xp_scr[8:8 + q, :] = x
    conv, carry = _conv_chunk(xp_scr, cw_ref, cb_ref, q, valid)
    convo_ref[0] = carry
    xc = _silu(conv)

    pre_c = sc_ref[...] + bcol_ref[...]
    pre_r = sr_ref[0] + brow_ref[...]
    ig_c = pre_c
    lf_c = -_softplus(-pre_c)
    ig_r = pre_r
    lf_r = -_softplus(-pre_r)
    if valid < q:
        tcol = lax.broadcasted_iota(jnp.int32, (q, 1), 0)
        trow = lax.broadcasted_iota(jnp.int32, (1, q), 1)
        lf_c = jnp.where(tcol < valid, lf_c, 0.0)
        ig_c = jnp.where(tcol < valid, ig_c, NEG)
        lf_r = jnp.where(trow < valid, lf_r, 0.0)
        ig_r = jnp.where(trow < valid, ig_r, NEG)
    b_c = _xdot(tl_ref[...], lf_c, NN, 1, lowp)
    b_r = _xdot(lf_r, tu_ref[...], NN, 0, lowp)

    tri = (lax.broadcasted_iota(jnp.int32, (q, q), 0) >= lax.broadcasted_iota(jnp.int32, (q, q), 1))
    o_all = o_ref[...]
    for h in range(ML_HEADS):
        hs = slice(h * ML_HEADDIM, (h + 1) * ML_HEADDIM)
        xh = xc[:, hs]
        qh = _mm(xh, wq_ref[h], lowp)
        kh = _mm(xh, wk_ref[h], lowp) * (ML_HEADDIM ** -0.5)
        vh = _mm(x[:, hs], wv_ref[h], lowp)
        bc = b_c[:, SMALL_F + h:SMALL_F + h + 1]
        br = b_r[SMALL_F + h:SMALL_F + h + 1, :]
        ic = ig_c[:, SMALL_I + h:SMALL_I + h + 1]
        ir = ig_r[SMALL_I + h:SMALL_I + h + 1, :]
        mp = m_scr[h:h + 1, 0:1]
        dlog = jnp.where(tri, bc - br + ir, NEG)
        inter = bc + mp
        m_t = jnp.maximum(inter, jnp.max(dlog, axis=1, keepdims=True))
        wgt = jnp.exp(dlog - m_t)
        s = _mm(qh, kh, lowp, NT) * wgt
        scale = jnp.exp(inter - m_t)
        cp = c_scr[h]
        npv = n_scr[h:h + 1, :]
        num = _mm(s, vh, lowp) + scale * _mm(qh, cp, lowp)
        dot = jnp.sum(s, axis=1, keepdims=True) + scale * jnp.sum(qh * npv, axis=1, keepdims=True)
        hh = num / jnp.maximum(jnp.abs(dot), jnp.exp(-m_t))
        m_new = m_t[q - 1:q, :]
        b_last = bc[q - 1:q, :]
        w_end = jnp.exp(b_last - bc + ic - m_new)
        cs = jnp.exp(b_last + mp - m_new)
        kw = kh * w_end
        c_scr[h] = cs * cp + _mm(kw, vh, lowp, TN)
        n_scr[h:h + 1, :] = cs * npv + jnp.sum(kw, axis=0, keepdims=True)
        m_scr[h:h + 1, :] = jnp.broadcast_to(m_new, (1, 128))
        hh = hh * _sigmoid(o_all[:, hs])
        hc = hh - jnp.mean(hh, -1, keepdims=True)
        h_scr[:, hs] = hc * lax.rsqrt(jnp.mean(hc * hc, -1, keepdims=True) + EPS)

    y = h_scr[...] * nw_ref[...] + skip_ref[...] * xc
    y_ref[...] = y * _silu(z_ref[...])

    @pl.when(c == nc - 1)
    def _():
        co_ref[0] = c_scr[...]
        no_ref[0] = n_scr[...]
        mo_ref[0] = m_scr[...]


def _mlstm(proj, srow, conv_state, c_state, n_state, m_state, layer, w, consts, *, b, q, nc, valid, lowp):
    t = b * nc * q
    kern = functools.partial(_mlstm_kernel, q=q, nc=nc, valid=valid, lowp=lowp)
    return pl.pallas_call(
        kern,
        grid=(b, nc),
        in_specs=[
            _col_spec(q, nc, COL_C_X, BRANCH_W),
            _col_spec(q, nc, COL_C_Z, BRANCH_W),
            _col_spec(q, nc, COL_C_O, BRANCH_W),
            _col_spec(q, nc, COL_SMALL, 128),
            _row_spec(q),
            _state_in_spec(layer, (CONV_W - 1, BRANCH_W)),
            _state_in_spec(layer, (ML_HEADS, ML_HEADDIM, ML_HEADDIM)),
            _state_in_spec(layer, (ML_HEADS, ML_HEADDIM)),
            _state_in_spec(layer, (8, 128)),
            _const_spec((CONV_W, BRANCH_W)),
            _const_spec((1, BRANCH_W)),
            _const_spec((ML_HEADS, ML_HEADDIM, ML_HEADDIM)),
            _const_spec((ML_HEADS, ML_HEADDIM, ML_HEADDIM)),
            _const_spec((ML_HEADS, ML_HEADDIM, ML_HEADDIM)),
            _const_spec((1, 128)),
            _const_spec((N_SMALL, 1)),
            _const_spec((1, BRANCH_W)),
            _const_spec((1, BRANCH_W)),
            _const_spec((q, q)),
            _const_spec((q, q)),
        ],
        out_specs=[
            _y_spec(q, nc),
            _state_out_spec((CONV_W - 1, BRANCH_W)),
            _state_out_spec((ML_HEADS, ML_HEADDIM, ML_HEADDIM)),
            _state_out_spec((ML_HEADS, ML_HEADDIM)),
            _state_out_spec((8, 128)),
        ],
        out_shape=[
            jax.ShapeDtypeStruct((t, BRANCH_W), F32),
            jax.ShapeDtypeStruct((b, CONV_W - 1, BRANCH_W), F32),
            jax.ShapeDtypeStruct((b, ML_HEADS, ML_HEADDIM, ML_HEADDIM), F32),
            jax.ShapeDtypeStruct((b, ML_HEADS, ML_HEADDIM), F32),
            jax.ShapeDtypeStruct((b, 8, 128), F32),
        ],
        scratch_shapes=[
            pltpu.VMEM((q + 8, BRANCH_W), F32),
            pltpu.VMEM((ML_HEADS, ML_HEADDIM, ML_HEADDIM), F32),
            pltpu.VMEM((ML_HEADS, ML_HEADDIM), F32),
            pltpu.VMEM((8, 128), F32),
            pltpu.VMEM((q, BRANCH_W), F32),
        ],
        compiler_params=_MIXER_PARAMS,
        name="mlstm",
    )(proj, proj, proj, proj, srow, conv_state, c_state, n_state, m_state,
      w['ml_conv_w'], w['ml_conv_b'], w['ml_wq'], w['ml_wk'], w['ml_wv'], w['bias_col'], w['bias_row'],
      w['ml_norm_w'], w['ml_skip'], consts['tril'], consts['triu'])


def _hgrn_kernel(f_ref, i_ref, q_ref, g_ref, s0_ref, lb_ref, nw_ref,
                 tl_ref, sel_ref, pm_ref, penq_ref, penk_ref,
                 y_ref, so_ref,
                 s_scr, o_scr, *, q, nc, valid, lowp, nlev):
    c = pl.program_id(1)

    @pl.when(c == 0)
    def _():
        s_scr[...] = s0_ref[0, 0]

    lb = lb_ref[...]
    fg = lb + (1.0 - lb) * _sigmoid(f_ref[...])
    lf = jnp.log(fg)
    kk = 1.0 - fg
    qq = _silu(q_ref[...]) * (HG_HEADDIM ** -0.5)
    vv = i_ref[...]
    if valid < q:
        tcol = lax.broadcasted_iota(jnp.int32, (q, 1), 0)
        lf = jnp.where(tcol < valid, lf, 0.0)
        kk = jnp.where(tcol < valid, kk, 0.0)

    gcum = _xdot(tl_ref[...], lf, NN, 1, lowp)

    att = [jnp.zeros((q, q), F32) for _ in range(HG_HEADS)]
    for lev in range(nlev):
        d = gcum - _xdot(sel_ref[lev], gcum, NN, 1, lowp)
        qt = qq * jnp.exp(d + penq_ref[lev])
        kt = kk * jnp.exp(penk_ref[lev] - d)
        pm = pm_ref[lev]
        for h in range(HG_HEADS):
            hs = slice(h * HG_HEADDIM, (h + 1) * HG_HEADDIM)
            att[h] = att[h] + pm * _mm(qt[:, hs], kt[:, hs], lowp, NT)

    eye = (lax.broadcasted_iota(jnp.int32, (q, q), 0) == lax.broadcasted_iota(jnp.int32, (q, q), 1))
    qg = qq * jnp.exp(gcum)
    kend = kk * jnp.exp(gcum[q - 1:q, :] - gcum)
    ones = jnp.ones((q, 128), F32)
    for h in range(HG_HEADS):
        hs = slice(h * HG_HEADDIM, (h + 1) * HG_HEADDIM)
        diag = jnp.sum(qq[:, hs] * kk[:, hs], axis=1, keepdims=True)
        a = att[h] + jnp.where(eye, diag, 0.0)
        sp = s_scr[h]
        oh = _mm(a, vv[:, hs], lowp) + _mm(qg[:, hs], sp, lowp)
        dec = jnp.exp(_xdot(lf[:, hs], ones, TN, 0, lowp))
        s_scr[h] = sp * dec + _mm(kend[:, hs], vv[:, hs], lowp, TN)
        o_scr[:, hs] = oh * lax.rsqrt(jnp.mean(oh * oh, -1, keepdims=True) + EPS)

    y_ref[...] = o_scr[...] * nw_ref[...] * _silu(g_ref[...])

    @pl.when(c == nc - 1)
    def _():
        so_ref[0] = s_scr[...]


def _hgrn(proj, s_state, layer, w, consts, *, b, q, nc, valid, lowp):
    t = b * nc * q
    nlev = int(math.log2(q))
    kern = functools.partial(_hgrn_kernel, q=q, nc=nc, valid=valid, lowp=lowp, nlev=nlev)
    return pl.pallas_call(
        kern,
        grid=(b, nc),
        in_specs=[
            _col_spec(q, nc, COL_D_F, BRANCH_W),
            _col_spec(q, nc, COL_D_I, BRANCH_W),
            _col_spec(q, nc, COL_D_Q, BRANCH_W),
            _col_spec(q, nc, COL_D_G, BRANCH_W),
            _state_in_spec(layer, (HG_HEADS, HG_HEADDIM, HG_HEADDIM)),
            _const_spec((1, BRANCH_W)),
            _const_spec((1, BRANCH_W)),
            _const_spec((q, q)),
            _const_spec((nlev, q, q)),
            _const_spec((nlev, q, q)),
            _const_spec((nlev, q, 1)),
            _const_spec((nlev, q, 1)),
        ],
        out_specs=[
            _y_spec(q, nc),
            _state_out_spec((HG_HEADS, HG_HEADDIM, HG_HEADDIM)),
        ],
        out_shape=[
            jax.ShapeDtypeStruct((t, BRANCH_W), F32),
            jax.ShapeDtypeStruct((b, HG_HEADS, HG_HEADDIM, HG_HEADDIM), F32),
        ],
        scratch_shapes=[
            pltpu.VMEM((HG_HEADS, HG_HEADDIM, HG_HEADDIM), F32),
            pltpu.VMEM((q, BRANCH_W), F32),
        ],
        compiler_params=_MIXER_PARAMS,
        name="hgrn",
    )(proj, proj, proj, proj, s_state, w['hg_lb'], w['hg_norm_w'],
      consts['tril'], consts['sel'], consts['pairmask'], consts['penq'], consts['penk'])


def _chunk_consts(q):
    r = np.arange(q)
    tril = (r[:, None] >= r[None, :]).astype(np.float32)
    nlev = int(math.log2(q))
    sel = np.zeros((nlev, q, q), np.float32)
    pm = np.zeros((nlev, q, q), np.float32)
    penq = np.full((nlev, q, 1), NEG, np.float32)
    penk = np.full((nlev, q, 1), NEG, np.float32)
    for lev in range(nlev):
        m = q >> (lev + 1)
        blk = r // (2 * m)
        second = (r % (2 * m)) >= m
        bnd = blk * 2 * m + m - 1
        sel[lev, r, bnd] = 1.0
        pm[lev] = ((blk[:, None] == blk[None, :]) & second[:, None] & ~second[None, :]).astype(np.float32)
        penq[lev, second, 0] = 0.0
        penk[lev, ~second, 0] = 0.0
    expand = np.zeros((128, BRANCH_W), np.float32)
    for h in range(SSD_HEADS):
        expand[h, h * SSD_HEADDIM:(h + 1) * SSD_HEADDIM] = 1.0
    return {
        'tril': jnp.asarray(tril), 'triu': jnp.asarray(tril.T), 'sel': jnp.asarray(sel),
        'pairmask': jnp.asarray(pm), 'penq': jnp.asarray(penq), 'penk': jnp.asarray(penk),
        'expand': jnp.asarray(expand),
    }


def _cmul(ar, ai, br, bi):
    return ar * br - ai * bi, ar * bi + ai * br


def _layer_params(l, p, hg_lb):
    w_in = p['w_in'][l]
    small = jnp.concatenate([w_in[:, 1280:1288], w_in[:, 3848:3856]], axis=1)
    w_perm = jnp.concatenate([
        w_in[:, 5904:10000],
        w_in[:, 0:512],
        w_in[:, 512:1024],
        w_in[:, 1288:2312],
        w_in[:, 2312:3848],
        w_in[:, 3856:5904],
        w_in[:, 1024:1280],
        small,
        jnp.zeros((D_MODEL, N_PROJ - COL_SMALL - N_SMALL), F32),
    ], axis=1).astype(BF16)

    def lane_pad(v, off):
        return jnp.zeros((1, 128), F32).at[0, off:off + v.shape[0]].set(v)

    bias_col = (lane_pad(p['ssd_dt_bias'][l], SMALL_DT) + lane_pad(p['ml_i_bias'][l], SMALL_I)
                + lane_pad(p['ml_f_bias'][l], SMALL_F))
    alog_col = lane_pad(p['ssd_A_log'][l], SMALL_DT)

    dt = jnp.exp(p['s5_log_dt'][l])[:, None]
    lr = p['s5_A_re'][l]
    li = p['s5_A_im'][l]
    mag = jnp.exp(lr * dt)
    abr, abi = mag * jnp.cos(li * dt), mag * jnp.sin(li * dt)
    den = lr * lr + li * li
    cr = ((abr - 1.0) * lr + abi * li) / den
    ci = (abi * lr - (abr - 1.0) * li) / den
    bbr, bbi = _cmul(cr[..., None], ci[..., None], p['s5_B_re'][l], p['s5_B_im'][l])
    eye8 = jnp.eye(8, dtype=F32)

    def pack_b(bb):
        return jnp.einsum('jgpc,gh->jgchp', bb.reshape(4, 8, S5_STATE, S5_GROUP), eye8).reshape(4, 128, 512)

    def pack_c(cc):
        return jnp.einsum('jgcp,gh->jgphc', cc.reshape(4, 8, S5_GROUP, S5_STATE), eye8).reshape(4, 512, 128)

    ar1, ai1 = abr.reshape(1, S5_LANES), abi.reshape(1, S5_LANES)
    pows = [(ar1, ai1)]
    for _ in range(7):
        pows.append(_cmul(pows[-1][0], pows[-1][1], ar1, ai1))
    rows = jnp.arange(8)[:, None]
    pr = jnp.stack([jnp.where(rows >= d, pows[d - 1][0], 0.0) for d in (1, 2, 4)])
    pi = jnp.stack([jnp.where(rows >= d, pows[d - 1][1], 0.0) for d in (1, 2, 4)])
    tr = jnp.concatenate([pw[0] for pw in pows], axis=0)
    ti = jnp.concatenate([pw[1] for pw in pows], axis=0)

    return {
        'norm_w': p['norm_w'][l].reshape(1, D_MODEL),
        'w_in': w_perm,
        'w_small_t': small.T.astype(BF16),
        'bias_col': bias_col, 'bias_row': bias_col[0, :N_SMALL].reshape(N_SMALL, 1),
        'alog_col': alog_col, 'alog_row': alog_col[0, :N_SMALL].reshape(N_SMALL, 1),
        'ssd_conv_w': p['ssd_conv_w'][l], 'ssd_conv_b': p['ssd_conv_b'][l].reshape(1, SSD_CONV_CH),
        'ssd_D_exp': jnp.repeat(p['ssd_D'][l], SSD_HEADDIM).reshape(1, BRANCH_W),
        'ssd_norm_w': p['ssd_norm_w'][l].reshape(1, BRANCH_W),
        's5_wbr': pack_b(bbr), 's5_wbi': pack_b(bbi),
        's5_wcr': pack_c(p['s5_C_re'][l]), 's5_wci': pack_c(p['s5_C_im'][l]),
        's5_pr': pr, 's5_pi': pi, 's5_tr': tr, 's5_ti': ti,
        's5_D': p['s5_D'][l].reshape(1, BRANCH_W), 's5_glu_w': p['s5_glu_w'][l],
        'ml_conv_w': p['ml_conv_w'][l], 'ml_conv_b': p['ml_conv_b'][l].reshape(1, BRANCH_W),
        'ml_wq': p['ml_wq'][l], 'ml_wk': p['ml_wk'][l], 'ml_wv': p['ml_wv'][l],
        'ml_norm_w': p['ml_norm_w'][l].reshape(1, BRANCH_W), 'ml_skip': p['ml_skip'][l].reshape(1, BRANCH_W),
        'hg_lb': hg_lb[l].reshape(1, BRANCH_W), 'hg_norm_w': p['hg_norm_w'][l].reshape(1, BRANCH_W),
        'w_branch': p['w_branch'][l].astype(BF16), 'w_out': p['w_out'][l].astype(BF16),
    }


_MATMUL_WEIGHTS = ('s5_wbr', 's5_wbi', 's5_wcr', 's5_wci', 's5_glu_w', 'ml_wq', 'ml_wk', 'ml_wv')


def _run_group(x3, states, layers, fnw, *, q, valid):
    b, lp, _ = x3.shape
    nc = lp // q
    t = b * lp
    lowp = q >= 16
    consts = _chunk_consts(q)
    ssd_conv, ssd_h, s5_re, s5_im, ml_conv, ml_c, ml_n, ml_m, hg_s = states
    s5_re = s5_re.reshape(DEPTH, b, 1, S5_LANES)
    s5_im = s5_im.reshape(DEPTH, b, 1, S5_LANES)
    ml_m = jnp.broadcast_to(jnp.pad(ml_m, ((0, 0), (0, 0), (0, 8 - ML_HEADS)))[..., None], (DEPTH, b, 8, 128))
    kw = dict(b=b, q=q, nc=nc, valid=valid, lowp=lowp)
    x = x3.reshape(t, D_MODEL)
    new = []
    for l, w in enumerate(layers):
        if lowp:
            w = dict(w, **{k: w[k].astype(BF16) for k in _MATMUL_WEIGHTS})
        proj, small_t = _inproj(x, w['norm_w'], w['w_in'], w['w_small_t'])
        srow = small_t.reshape(N_SMALL, b, lp).transpose(1, 0, 2)
        ya, n_ssd_conv, n_ssd_h = _ssd(proj, srow, ssd_conv, ssd_h, l, w, consts, **kw)
        yb, n_re, n_im = _s5(proj, s5_re, s5_im, l, w, **kw)
        yc, n_ml_conv, n_c, n_n, n_m = _mlstm(proj, srow, ml_conv, ml_c, ml_n, ml_m, l, w, consts, **kw)
        yd, n_s = _hgrn(proj, hg_s, l, w, consts, **kw)
        x = _merge(proj, (ya, yb, yc, yd), x, w['w_branch'], w['w_out'], fnw, final=(l == DEPTH - 1))
        new.append((n_ssd_conv, n_ssd_h,
                    n_re.reshape(b, S5_GROUPS, S5_STATE), n_im.reshape(b, S5_GROUPS, S5_STATE),
                    n_ml_conv, n_c, n_n, n_m[:, :ML_HEADS, 0], n_s))
    new_states = tuple(jnp.stack(ss, axis=0) for ss in zip(*new))
    return x.reshape(b, lp, D_MODEL), new_states


SAMPLE_PAD = 8


def kernel(x_prompt, x_sample, state_ssd_conv, state_ssd, state_s5_re, state_s5_im, state_mlstm_conv, state_mlstm_C, state_mlstm_n, state_mlstm_m, state_hgrn, norm_w, w_in, ssd_conv_w, ssd_conv_b, ssd_dt_bias, ssd_A_log, ssd_D, ssd_norm_w, s5_A_re, s5_A_im, s5_B_re, s5_B_im, s5_C_re, s5_C_im, s5_D, s5_log_dt, s5_glu_w, ml_conv_w, ml_conv_b, ml_wq, ml_wk, ml_wv, ml_i_bias, ml_f_bias, ml_norm_w, ml_skip, hg_lb_logits, hg_norm_w, w_branch, w_out, final_norm_w):
    p = {'norm_w': norm_w, 'w_in': w_in,
         'ssd_conv_w': ssd_conv_w, 'ssd_conv_b': ssd_conv_b, 'ssd_dt_bias': ssd_dt_bias,
         'ssd_A_log': ssd_A_log, 'ssd_D': ssd_D, 'ssd_norm_w': ssd_norm_w,
         's5_A_re': s5_A_re, 's5_A_im': s5_A_im, 's5_B_re': s5_B_re, 's5_B_im': s5_B_im,
         's5_C_re': s5_C_re, 's5_C_im': s5_C_im, 's5_D': s5_D, 's5_log_dt': s5_log_dt, 's5_glu_w': s5_glu_w,
         'ml_conv_w': ml_conv_w, 'ml_conv_b': ml_conv_b, 'ml_wq': ml_wq, 'ml_wk': ml_wk, 'ml_wv': ml_wv,
         'ml_i_bias': ml_i_bias, 'ml_f_bias': ml_f_bias, 'ml_norm_w': ml_norm_w, 'ml_skip': ml_skip,
         'hg_norm_w': hg_norm_w, 'w_branch': w_branch, 'w_out': w_out}
    lb_cum = jnp.cumsum(jax.nn.softmax(hg_lb_logits, axis=0), axis=0)
    hg_lb = lb_cum - lb_cum[0]
    layers = [_layer_params(l, p, hg_lb) for l in range(DEPTH)]
    fnw = final_norm_w.reshape(1, D_MODEL)

    sample_states = (state_ssd_conv, state_ssd, state_s5_re, state_s5_im, state_mlstm_conv,
                     state_mlstm_C, state_mlstm_n, state_mlstm_m, state_hgrn)
    bp, lp_, _ = x_prompt.shape
    prompt_states = tuple(jnp.zeros((DEPTH, bp) + s.shape[2:], F32) for s in sample_states)
    q_prompt = math.gcd(lp_, 128)
    y_prompt, new_p = _run_group(x_prompt, prompt_states, layers, fnw, q=q_prompt, valid=q_prompt)

    ls = x_sample.shape[1]
    xs_pad = jnp.pad(x_sample, ((0, 0), (0, SAMPLE_PAD - ls), (0, 0)))
    y_sample, new_s = _run_group(xs_pad, sample_states, layers, fnw, q=SAMPLE_PAD, valid=ls)
    y_sample = y_sample[:, :ls]

    out = [y_prompt, y_sample]
    for ps, ss in zip(new_p, new_s):
        out += [ps, ss]
    return tuple(out)
```

```python
import functools
import math

import numpy as np
import jax
import jax.numpy as jnp
from jax import lax
from jax.experimental import pallas as pl
from jax.experimental.pallas import tpu as pltpu

F32 = jnp.float32
BF16 = jnp.bfloat16

D_MODEL = 1024
DEPTH = 2
BRANCH_W = 512
CONV_W = 4
EPS = 1e-6
SSD_HEADS = 8
SSD_HEADDIM = 64
SSD_STATE = 64
SSD_GROUPS = 2
SSD_CONV_CH = 768
S5_GROUPS = 32
S5_GROUP = 16
S5_STATE = 64
S5_LANES = S5_GROUPS * S5_STATE
ML_HEADS = 4
ML_HEADDIM = 128
HG_HEADS = 4
HG_HEADDIM = 128
N_BRANCH = 4

NEG = -1e30

N_PROJ = 10240
COL_MERGE = 0
COL_A_Z = 4096
COL_A_X = 4608
COL_B_U = 5120
COL_B_GATE = 5632
COL_C_X = 6144
COL_C_Z = 6656
COL_C_O = 7168
COL_D_F = 7680
COL_D_I = 8192
COL_D_Q = 8704
COL_D_G = 9216
COL_A_BC = 9728
COL_SMALL = 9984
N_SMALL = 16
SMALL_DT = 0
SMALL_I = 8
SMALL_F = 12

NN = (((1,), (0,)), ((), ()))
NT = (((1,), (1,)), ((), ()))
TN = (((0,), (0,)), ((), ()))


def _mm(a, b, lowp, dims=NN):
    if lowp:
        a = a.astype(BF16)
        b = b.astype(BF16)
    return lax.dot_general(a, b, dims, preferred_element_type=F32)


def _split3(a):
    hi = a.astype(BF16)
    r = a - hi.astype(F32)
    mid = r.astype(BF16)
    lo = (r - mid.astype(F32)).astype(BF16)
    return hi, mid, lo


def _xdot(a, b, dims, split, lowp):
    if not lowp:
        return lax.dot_general(a, b, dims, precision=lax.Precision.HIGHEST,
                               preferred_element_type=F32)
    if split == 0:
        other = b.astype(BF16)
        return sum(lax.dot_general(p, other, dims, preferred_element_type=F32) for p in _split3(a))
    other = a.astype(BF16)
    return sum(lax.dot_general(other, p, dims, preferred_element_type=F32) for p in _split3(b))


def _sigmoid(x):
    return jax.nn.sigmoid(x)


def _silu(x):
    return x * jax.nn.sigmoid(x)


def _softplus(x):
    return jnp.maximum(x, 0.0) + jnp.log1p(jnp.exp(-jnp.abs(x)))


def _conv_chunk(xp_scr, cw_ref, cb_ref, q, valid):
    cw = cw_ref[...]
    conv = (cb_ref[...]
            + cw[0:1] * xp_scr[5:5 + q, :]
            + cw[1:2] * xp_scr[6:6 + q, :]
            + cw[2:3] * xp_scr[7:7 + q, :]
            + cw[3:4] * xp_scr[8:8 + q, :])
    carry = xp_scr[5 + valid:8 + valid, :]
    xp_scr[5:8, :] = carry
    return conv, carry


def _inproj_kernel(x_ref, nw_ref, w_ref, wst_ref, o_ref, st_ref, xn_scr):
    @pl.when(pl.program_id(1) == 0)
    def _():
        x = x_ref[...]
        xn = x * lax.rsqrt(jnp.mean(x * x, -1, keepdims=True) + EPS) * nw_ref[...]
        xb = xn.astype(BF16)
        xn_scr[...] = xb
        st_ref[...] = lax.dot_general(wst_ref[...], xb, NT, preferred_element_type=F32)

    o_ref[...] = jnp.dot(xn_scr[...], w_ref[...], preferred_element_type=F32)


def _inproj(x, nw, w, wst):
    t = x.shape[0]
    tm = min(1024, t)
    tn = 1024
    return pl.pallas_call(
        _inproj_kernel,
        grid=(t // tm, N_PROJ // tn),
        in_specs=[
            pl.BlockSpec((tm, D_MODEL), lambda i, j: (i, 0)),
            pl.BlockSpec((1, D_MODEL), lambda i, j: (0, 0)),
            pl.BlockSpec((D_MODEL, tn), lambda i, j: (0, j)),
            pl.BlockSpec((N_SMALL, D_MODEL), lambda i, j: (0, 0)),
        ],
        out_specs=[
            pl.BlockSpec((tm, tn), lambda i, j: (i, j)),
            pl.BlockSpec((N_SMALL, tm), lambda i, j: (0, i)),
        ],
        out_shape=[
            jax.ShapeDtypeStruct((t, N_PROJ), F32),
            jax.ShapeDtypeStruct((N_SMALL, t), F32),
        ],
        scratch_shapes=[pltpu.VMEM((tm, D_MODEL), BF16)],
        compiler_params=pltpu.CompilerParams(
            dimension_semantics=("arbitrary", "arbitrary"),
            vmem_limit_bytes=48 * 1024 * 1024),
        name="inproj",
    )(x, nw, w, wst)


def _merge_kernel(g_ref, ya_ref, yb_ref, yc_ref, yd_ref, x_ref, wb_ref, wo_ref, fnw_ref, o_ref, *, final):
    acc = None
    for i, y_ref in enumerate((ya_ref, yb_ref, yc_ref, yd_ref)):
        gate = _sigmoid(g_ref[:, i * D_MODEL:(i + 1) * D_MODEL])
        term = gate * jnp.dot(y_ref[...].astype(BF16), wb_ref[i], preferred_element_type=F32)
        acc = term if acc is None else acc + term
    out = x_ref[...] + jnp.dot(acc.astype(BF16), wo_ref[...], preferred_element_type=F32)
    if final:
        out = out * lax.rsqrt(jnp.mean(out * out, -1, keepdims=True) + EPS) * fnw_ref[...]
    o_ref[...] = out


def _merge(proj, ys, x, wb, wo, fnw, final):
    t = x.shape[0]
    tm = min(256, t)
    yspec = pl.BlockSpec((tm, BRANCH_W), lambda i: (i, 0))
    return pl.pallas_call(
        functools.partial(_merge_kernel, final=final),
        grid=(t // tm,),
        in_specs=[
            pl.BlockSpec((tm, N_BRANCH * D_MODEL), lambda i: (i, COL_MERGE // (N_BRANCH * D_MODEL))),
            yspec, yspec, yspec, yspec,
            pl.BlockSpec((tm, D_MODEL), lambda i: (i, 0)),
            pl.BlockSpec((N_BRANCH, BRANCH_W, D_MODEL), lambda i: (0, 0, 0)),
            pl.BlockSpec((D_MODEL, D_MODEL), lambda i: (0, 0)),
            pl.BlockSpec((1, D_MODEL), lambda i: (0, 0)),
        ],
        out_specs=pl.BlockSpec((tm, D_MODEL), lambda i: (i, 0)),
        out_shape=jax.ShapeDtypeStruct((t, D_MODEL), F32),
        compiler_params=pltpu.CompilerParams(
            dimension_semantics=("arbitrary",),
            vmem_limit_bytes=48 * 1024 * 1024),
        name="merge",
    )(proj, *ys, x, wb, wo, fnw)


def _col_spec(q, nc, col, width):
    return pl.BlockSpec((q, width), lambda b, c: (b * nc + c, col // width))


def _row_spec(q):
    return pl.BlockSpec((1, N_SMALL, q), lambda b, c: (b, 0, c))


def _const_spec(shape):
    nd = len(shape)
    return pl.BlockSpec(shape, lambda b, c: (0,) * nd)


def _state_in_spec(layer, tail):
    nd = len(tail)
    return pl.BlockSpec((1, 1) + tail, lambda b, c: (layer, b) + (0,) * nd)


def _state_out_spec(tail):
    nd = len(tail)
    return pl.BlockSpec((1,) + tail, lambda b, c: (b,) + (0,) * nd)


def _y_spec(q, nc):
    return pl.BlockSpec((q, BRANCH_W), lambda b, c: (b * nc + c, 0))


_MIXER_PARAMS = pltpu.CompilerParams(
    dimension_semantics=("arbitrary", "arbitrary"),
    vmem_limit_bytes=48 * 1024 * 1024)


def _ssd_kernel(z_ref, x_ref, bc_ref, sc_ref, sr_ref, conv0_ref, h0_ref,
                cw_ref, cb_ref, bcol_ref, brow_ref, alc_ref, alr_ref, dexp_ref, nw_ref,
                e_ref, tl_ref, tu_ref,
                y_ref, convo_ref, ho_ref,
                xp_scr, h_scr, y_scr, *, q, nc, valid, lowp):
    c = pl.program_id(1)

    @pl.when(c == 0)
    def _():
        xp_scr[5:8, :] = conv0_ref[0, 0]
        h_scr[...] = h0_ref[0, 0]

    xp_scr[8:8 + q, 0:BRANCH_W] = x_ref[...]
    xp_scr[8:8 + q, BRANCH_W:SSD_CONV_CH] = bc_ref[...]
    conv, carry = _conv_chunk(xp_scr, cw_ref, cb_ref, q, valid)
    convo_ref[0] = carry
    xbc = _silu(conv)
    xs = xbc[:, 0:BRANCH_W]
    bm = xbc[:, BRANCH_W:BRANCH_W + 128]
    cm = xbc[:, BRANCH_W + 128:BRANCH_W + 256]

    dt_c = _softplus(sc_ref[...] + bcol_ref[...])
    a_c = dt_c * (-jnp.exp(alc_ref[...]))
    dt_r = _softplus(sr_ref[0] + brow_ref[...])
    a_r = dt_r * (-jnp.exp(alr_ref[...]))
    if valid < q:
        tcol = lax.broadcasted_iota(jnp.int32, (q, 1), 0)
        trow = lax.broadcasted_iota(jnp.int32, (1, q), 1)
        a_c = jnp.where(tcol < valid, a_c, 0.0)
        dt_c = jnp.where(tcol < valid, dt_c, 0.0)
        a_r = jnp.where(trow < valid, a_r, 0.0)

    tl = tl_ref[...]
    acum_c = _xdot(tl, a_c, NN, 1, lowp)
    acum_r = _xdot(a_r, tu_ref[...], NN, 0, lowp)
    e = e_ref[...]
    dt_e = _xdot(dt_c, e, NN, 0, lowp)
    acum_e = _xdot(acum_c, e, NN, 0, lowp)
    xdt = xs * dt_e
    xend = xdt * jnp.exp(acum_e[q - 1:q, :] - acum_e)
    eac = jnp.exp(acum_e)
    dec_last = jnp.exp(acum_c[q - 1:q, :])

    tri = (lax.broadcasted_iota(jnp.int32, (q, q), 0) >= lax.broadcasted_iota(jnp.int32, (q, q), 1))
    rpg = SSD_HEADS // SSD_GROUPS
    for g in range(SSD_GROUPS):
        bg = bm[:, g * SSD_STATE:(g + 1) * SSD_STATE]
        cg = cm[:, g * SSD_STATE:(g + 1) * SSD_STATE]
        cb = _mm(cg, bg, lowp, NT)
        for r in range(rpg):
            h = g * rpg + r
            hs = slice(h * SSD_HEADDIM, (h + 1) * SSD_HEADDIM)
            seg = acum_c[:, h:h + 1] - acum_r[h:h + 1, :]
            m = cb * jnp.exp(jnp.where(tri, seg, NEG))
            hp = h_scr[h]
            yh = _mm(m, xdt[:, hs], lowp) + _mm(cg, hp, lowp, NT) * eac[:, hs]
            y_scr[:, hs] = yh
            h_scr[h] = hp * dec_last[:, h:h + 1] + _mm(xend[:, hs], bg, lowp, TN)

    y = y_scr[...] + xs * dexp_ref[...]
    y = y * _silu(z_ref[...])
    y = y * lax.rsqrt(jnp.mean(y * y, -1, keepdims=True) + EPS) * nw_ref[...]
    y_ref[...] = y

    @pl.when(c == nc - 1)
    def _():
        ho_ref[0] = h_scr[...]


def _ssd(proj, srow, conv_state, h_state, layer, w, consts, *, b, q, nc, valid, lowp):
    t = b * nc * q
    kern = functools.partial(_ssd_kernel, q=q, nc=nc, valid=valid, lowp=lowp)
    return pl.pallas_call(
        kern,
        grid=(b, nc),
        in_specs=[
            _col_spec(q, nc, COL_A_Z, BRANCH_W),
            _col_spec(q, nc, COL_A_X, BRANCH_W),
            _col_spec(q, nc, COL_A_BC, 256),
            _col_spec(q, nc, COL_SMALL, 128),
            _row_spec(q),
            _state_in_spec(layer, (CONV_W - 1, SSD_CONV_CH)),
            _state_in_spec(layer, (SSD_HEADS, SSD_HEADDIM, SSD_STATE)),
            _const_spec((CONV_W, SSD_CONV_CH)),
            _const_spec((1, SSD_CONV_CH)),
            _const_spec((1, 128)),
            _const_spec((N_SMALL, 1)),
            _const_spec((1, 128)),
            _const_spec((N_SMALL, 1)),
            _const_spec((1, BRANCH_W)),
            _const_spec((1, BRANCH_W)),
            _const_spec((128, BRANCH_W)),
            _const_spec((q, q)),
            _const_spec((q, q)),
        ],
        out_specs=[
            _y_spec(q, nc),
            _state_out_spec((CONV_W - 1, SSD_CONV_CH)),
            _state_out_spec((SSD_HEADS, SSD_HEADDIM, SSD_STATE)),
        ],
        out_shape=[
            jax.ShapeDtypeStruct((t, BRANCH_W), F32),
            jax.ShapeDtypeStruct((b, CONV_W - 1, SSD_CONV_CH), F32),
            jax.ShapeDtypeStruct((b, SSD_HEADS, SSD_HEADDIM, SSD_STATE), F32),
        ],
        scratch_shapes=[
            pltpu.VMEM((q + 8, SSD_CONV_CH), F32),
            pltpu.VMEM((SSD_HEADS, SSD_HEADDIM, SSD_STATE), F32),
            pltpu.VMEM((q, BRANCH_W), F32),
        ],
        compiler_params=_MIXER_PARAMS,
        name="ssd",
    )(proj, proj, proj, proj, srow, conv_state, h_state,
      w['ssd_conv_w'], w['ssd_conv_b'], w['bias_col'], w['bias_row'], w['alog_col'], w['alog_row'],
      w['ssd_D_exp'], w['ssd_norm_w'], consts['expand'], consts['tril'], consts['triu'])


def _s5_kernel(u_ref, gate_ref, hr0_ref, hi0_ref, perm_ref, permt_ref, ar_ref, ai_ref,
               wbr_ref, wbi_ref, wcr_ref, wci_ref, d_ref, glu_ref,
               y_ref, hro_ref, hio_ref,
               hr_scr, hi_scr, cr_scr, ci_scr, *, q, nc, ngrp):
    c = pl.program_id(1)

    @pl.when(c == 0)
    def _():
        cr_scr[...] = hr0_ref[0]
        ci_scr[...] = hi0_ref[0]

    rows_in = perm_ref.shape[1]
    u = u_ref[...].reshape(rows_in, BRANCH_W)
    u_tm = jnp.dot(perm_ref[...], u.astype(BF16), preferred_element_type=F32).astype(BF16)
    nblk = S5_LANES // 512
    for j in range(nblk):
        uj = u_tm[:, j * 128:(j + 1) * 128]
        hr_scr[:, j * 512:(j + 1) * 512] = jnp.dot(uj, wbr_ref[j], preferred_element_type=F32)
        hi_scr[:, j * 512:(j + 1) * 512] = jnp.dot(uj, wbi_ref[j], preferred_element_type=F32)

    half = S5_LANES // 2
    for hf in range(2):
        sl = slice(hf * half, (hf + 1) * half)
        ar = ar_ref[:, sl]
        ai = ai_ref[:, sl]

        def grp_body(g, carry, sl=sl, ar=ar, ai=ai):
            s0 = pl.multiple_of(g * 8, 8)

            def t_body(t, h):
                r0 = pl.multiple_of(g * (8 * q) + t * 8, 8)
                hr, hi = h
                nr = ar * hr - ai * hi + hr_scr[pl.ds(r0, 8), sl]
                ni = ar * hi + ai * hr + hi_scr[pl.ds(r0, 8), sl]
                hr_scr[pl.ds(r0, 8), sl] = nr
                hi_scr[pl.ds(r0, 8), sl] = ni
                return nr, ni

            h = lax.fori_loop(0, q, t_body, (cr_scr[pl.ds(s0, 8), sl], ci_scr[pl.ds(s0, 8), sl]),
                              unroll=min(q, 4))
            cr_scr[pl.ds(s0, 8), sl] = h[0]
            ci_scr[pl.ds(s0, 8), sl] = h[1]
            return carry

        lax.fori_loop(0, ngrp, grp_body, 0)

    parts = []
    for j in range(nblk):
        sl = slice(j * 512, (j + 1) * 512)
        parts.append(jnp.dot(hr_scr[:, sl].astype(BF16), wcr_ref[j], preferred_element_type=F32)
                     - jnp.dot(hi_scr[:, sl].astype(BF16), wci_ref[j], preferred_element_type=F32))
    y_tm = jnp.concatenate(parts, axis=1)
    y = _xdot(permt_ref[...], y_tm, NN, 1, True) + u * d_ref[...]
    g = jax.nn.gelu(y)
    y = g * _sigmoid(jnp.dot(g.astype(BF16), glu_ref[...], preferred_element_type=F32))
    y = y * _silu(gate_ref[...].reshape(rows_in, BRANCH_W))
    y_ref[...] = y.reshape(y_ref.shape)

    @pl.when(c == nc - 1)
    def _():
        hro_ref[0] = cr_scr[...]
        hio_ref[0] = ci_scr[...]


def _s5_perm(bb, lp, valid):
    perm = np.zeros((bb * valid, bb * lp), np.float32)
    for g in range(bb // 8):
        for t in range(valid):
            for s in range(8):
                perm[g * 8 * valid + t * 8 + s, (g * 8 + s) * lp + t] = 1.0
    return jnp.asarray(perm, BF16), jnp.asarray(perm.T, BF16)


def _s5(proj, hr_state, hi_state, layer, w, *, b, lp, q, valid, bb):
    nc = lp // q
    perm, permt = _s5_perm(bb, q, valid)
    rows = bb * valid
    if nc == 1:
        def col_spec(col):
            return pl.BlockSpec((bb * q, BRANCH_W), lambda i, c: (i, col // BRANCH_W))
        src = proj
        y_spec = pl.BlockSpec((bb * q, BRANCH_W), lambda i, c: (i, 0))
        y_shape = jax.ShapeDtypeStruct((b * lp, BRANCH_W), F32)
    else:
        def col_spec(col):
            return pl.BlockSpec((bb, q, BRANCH_W), lambda i, c: (i, c, col // BRANCH_W))
        src = proj.reshape(b, lp, N_PROJ)
        y_spec = pl.BlockSpec((bb, q, BRANCH_W), lambda i, c: (i, c, 0))
        y_shape = jax.ShapeDtypeStruct((b, lp, BRANCH_W), F32)
    st_in = pl.BlockSpec((1, bb, S5_LANES), lambda i, c: (layer, i, 0))
    st_out = pl.BlockSpec((1, bb, S5_LANES), lambda i, c: (0, i, 0))
    kern = functools.partial(_s5_kernel, q=valid, nc=nc, ngrp=bb // 8)
    y, hr, hi = pl.pallas_call(
        kern,
        grid=(b // bb, nc),
        in_specs=[
            col_spec(COL_B_U),
            col_spec(COL_B_GATE),
            st_in, st_in,
            _const_spec(perm.shape),
            _const_spec(permt.shape),
            _const_spec((8, S5_LANES)),
            _const_spec((8, S5_LANES)),
            _const_spec((4, 128, 512)),
            _const_spec((4, 128, 512)),
            _const_spec((4, 512, 128)),
            _const_spec((4, 512, 128)),
            _const_spec((1, BRANCH_W)),
            _const_spec((BRANCH_W, BRANCH_W)),
        ],
        out_specs=[y_spec, st_out, st_out],
        out_shape=[
            y_shape,
            jax.ShapeDtypeStruct((1, b, S5_LANES), F32),
            jax.ShapeDtypeStruct((1, b, S5_LANES), F32),
        ],
        scratch_shapes=[
            pltpu.VMEM((rows, S5_LANES), F32),
            pltpu.VMEM((rows, S5_LANES), F32),
            pltpu.VMEM((bb, S5_LANES), F32),
            pltpu.VMEM((bb, S5_LANES), F32),
        ],
        compiler_params=_MIXER_PARAMS,
        name="s5",
    )(src, src, hr_state, hi_state, perm, permt, w['s5_ar'], w['s5_ai'],
      w['s5_wbr'], w['s5_wbi'], w['s5_wcr'], w['s5_wci'], w['s5_D'], w['s5_glu_w'])
    return y.reshape(b * lp, BRANCH_W), hr[0], hi[0]


def _mlstm_kernel(x_ref, z_ref, o_ref, sc_ref, sr_ref, conv0_ref, c0_ref, n0_ref, m0_ref,
                  cw_ref, cb_ref, wq_ref, wk_ref, wv_ref, bcol_ref, brow_ref, nw_ref, skip_ref,
                  tl_ref, tu_ref,
                  y_ref, convo_ref, co_ref, no_ref, mo_ref,
                  xp_scr, c_scr, n_scr, m_scr, h_scr, *, q, nc, valid, lowp):
    c = pl.program_id(1)

    @pl.when(c == 0)
    def _():
        xp_scr[5:8, :] = conv0_ref[0, 0]
        c_scr[...] = c0_ref[0, 0]
        n_scr[...] = n0_ref[0, 0]
        m_scr[...] = m0_ref[0, 0]

    x = x_ref[...]
    xp_scr[8:8 + q, :] = x
    conv, carry = _conv_chunk(xp_scr, cw_ref, cb_ref, q, valid)
    convo_ref[0] = carry
    xc = _silu(conv)

    pre_c = sc_ref[...] + bcol_ref[...]
    pre_r = sr_ref[0] + brow_ref[...]
    ig_c = pre_c
    lf_c = -_softplus(-pre_c)
    ig_r = pre_r
    lf_r = -_softplus(-pre_r)
    if valid < q:
        tcol = lax.broadcasted_iota(jnp.int32, (q, 1), 0)
        trow = lax.broadcasted_iota(jnp.int32, (1, q), 1)
        lf_c = jnp.where(tcol < valid, lf_c, 0.0)
        ig_c = jnp.where(tcol < valid, ig_c, NEG)
        lf_r = jnp.where(trow < valid, lf_r, 0.0)
        ig_r = jnp.where(trow < valid, ig_r, NEG)
    b_c = _xdot(tl_ref[...], lf_c, NN, 1, lowp)
    b_r = _xdot(lf_r, tu_ref[...], NN, 0, lowp)

    tri = (lax.broadcasted_iota(jnp.int32, (q, q), 0) >= lax.broadcasted_iota(jnp.int32, (q, q), 1))
    o_all = o_ref[...]
    for h in range(ML_HEADS):
        hs = slice(h * ML_HEADDIM, (h + 1) * ML_HEADDIM)
        xh = xc[:, hs]
        qh = _mm(xh, wq_ref[h], lowp)
        kh = _mm(xh, wk_ref[h], lowp) * (ML_HEADDIM ** -0.5)
        vh = _mm(x[:, hs], wv_ref[h], lowp)
        bc = b_c[:, SMALL_F + h:SMALL_F + h + 1]
        br = b_r[SMALL_F + h:SMALL_F + h + 1, :]
        ic = ig_c[:, SMALL_I + h:SMALL_I + h + 1]
        ir = ig_r[SMALL_I + h:SMALL_I + h + 1, :]
        mp = m_scr[h:h + 1, 0:1]
        dlog = jnp.where(tri, bc - br + ir, NEG)
        inter = bc + mp
        m_t = jnp.maximum(inter, jnp.max(dlog, axis=1, keepdims=True))
        wgt = jnp.exp(dlog - m_t)
        s = _mm(qh, kh, lowp, NT) * wgt
        scale = jnp.exp(inter - m_t)
        cp = c_scr[h]
        npv = n_scr[h:h + 1, :]
        num = _mm(s, vh, lowp) + scale * _mm(qh, cp, lowp)
        dot = jnp.sum(s, axis=1, keepdims=True) + scale * jnp.sum(qh * npv, axis=1, keepdims=True)
        hh = num / jnp.maximum(jnp.abs(dot), jnp.exp(-m_t))
        m_new = m_t[q - 1:q, :]
        b_last = bc[q - 1:q, :]
        w_end = jnp.exp(b_last - bc + ic - m_new)
        cs = jnp.exp(b_last + mp - m_new)
        kw = kh * w_end
        c_scr[h] = cs * cp + _mm(kw, vh, lowp, TN)
        n_scr[h:h + 1, :] = cs * npv + jnp.sum(kw, axis=0, keepdims=True)
        m_scr[h:h + 1, :] = jnp.broadcast_to(m_new, (1, 128))
        hh = hh * _sigmoid(o_all[:, hs])
        hc = hh - jnp.mean(hh, -1, keepdims=True)
        h_scr[:, hs] = hc * lax.rsqrt(jnp.mean(hc * hc, -1, keepdims=True) + EPS)

    y = h_scr[...] * nw_ref[...] + skip_ref[...] * xc
    y_ref[...] = y * _silu(z_ref[...])

    @pl.when(c == nc - 1)
    def _():
        co_ref[0] = c_scr[...]
        no_ref[0] = n_scr[...]
        mo_ref[0] = m_scr[...]


def _mlstm(proj, srow, conv_state, c_state, n_state, m_state, layer, w, consts, *, b, q, nc, valid, lowp):
    t = b * nc * q
    kern = functools.partial(_mlstm_kernel, q=q, nc=nc, valid=valid, lowp=lowp)
    return pl.pallas_call(
        kern,
        grid=(b, nc),
        in_specs=[
            _col_spec(q, nc, COL_C_X, BRANCH_W),
            _col_spec(q, nc, COL_C_Z, BRANCH_W),
            _col_spec(q, nc, COL_C_O, BRANCH_W),
            _col_spec(q, nc, COL_SMALL, 128),
            _row_spec(q),
            _state_in_spec(layer, (CONV_W - 1, BRANCH_W)),
            _state_in_spec(layer, (ML_HEADS, ML_HEADDIM, ML_HEADDIM)),
            _state_in_spec(layer, (ML_HEADS, ML_HEADDIM)),
            _state_in_spec(layer, (8, 128)),
            _const_spec((CONV_W, BRANCH_W)),
            _const_spec((1, BRANCH_W)),
            _const_spec((ML_HEADS, ML_HEADDIM, ML_HEADDIM)),
            _const_spec((ML_HEADS, ML_HEADDIM, ML_HEADDIM)),
            _const_spec((ML_HEADS, ML_HEADDIM, ML_HEADDIM)),
            _const_spec((1, 128)),
            _const_spec((N_SMALL, 1)),
            _const_spec((1, BRANCH_W)),
            _const_spec((1, BRANCH_W)),
            _const_spec((q, q)),
            _const_spec((q, q)),
        ],
        out_specs=[
            _y_spec(q, nc),
            _state_out_spec((CONV_W - 1, BRANCH_W)),
            _state_out_spec((ML_HEADS, ML_HEADDIM, ML_HEADDIM)),
            _state_out_spec((ML_HEADS, ML_HEADDIM)),
            _state_out_spec((8, 128)),
        ],
        out_shape=[
            jax.ShapeDtypeStruct((t, BRANCH_W), F32),
            jax.ShapeDtypeStruct((b, CONV_W - 1, BRANCH_W), F32),
            jax.ShapeDtypeStruct((b, ML_HEADS, ML_HEADDIM, ML_HEADDIM), F32),
            jax.ShapeDtypeStruct((b, ML_HEADS, ML_HEADDIM), F32),
            jax.ShapeDtypeStruct((b, 8, 128), F32),
        ],
        scratch_shapes=[
            pltpu.VMEM((q + 8, BRANCH_W), F32),
            pltpu.VMEM((ML_HEADS, ML_HEADDIM, ML_HEADDIM), F32),
            pltpu.VMEM((ML_HEADS, ML_HEADDIM), F32),
            pltpu.VMEM((8, 128), F32),
            pltpu.VMEM((q, BRANCH_W), F32),
        ],
        compiler_params=_MIXER_PARAMS,
        name="mlstm",
    )(proj, proj, proj, proj, srow, conv_state, c_state, n_state, m_state,
      w['ml_conv_w'], w['ml_conv_b'], w['ml_wq'], w['ml_wk'], w['ml_wv'], w['bias_col'], w['bias_row'],
      w['ml_norm_w'], w['ml_skip'], consts['tril'], consts['triu'])


def _hgrn_kernel(f_ref, i_ref, q_ref, g_ref, s0_ref, lb_ref, nw_ref, tl_ref,
                 y_ref, so_ref,
                 s_scr, o_scr, *, q, nc, valid, lowp, nlev):
    c = pl.program_id(1)

    @pl.when(c == 0)
    def _():
        s_scr[...] = s0_ref[0, 0]

    lb = lb_ref[...]
    fg = lb + (1.0 - lb) * _sigmoid(f_ref[...])
    kk = 1.0 - fg
    qq = _silu(q_ref[...]) * (HG_HEADDIM ** -0.5)
    vv = i_ref[...]
    rcol = lax.broadcasted_iota(jnp.int32, (q, 1), 0)
    if valid < q:
        fg = jnp.where(rcol < valid, fg, 1.0)
        kk = jnp.where(rcol < valid, kk, 0.0)
    lf = jnp.log(fg)

    gcum = _xdot(tl_ref[...], lf, NN, 1, lowp)

    rt = lax.broadcasted_iota(jnp.int32, (q, q), 0)
    cs = lax.broadcasted_iota(jnp.int32, (q, q), 1)
    rx = rt ^ cs
    att = [None] * HG_HEADS
    for lev in range(nlev):
        m = q >> (lev + 1)
        second = (rcol & m) != 0
        if m >= 4:
            nb = q // (2 * m)
            gb = jnp.broadcast_to(gcum.reshape(nb, 2 * m, BRANCH_W)[:, m - 1:m, :], (nb, 2 * m, BRANCH_W))
            e = jnp.exp(-jnp.abs(gcum - gb.reshape(q, BRANCH_W)))
        elif m == 2:
            r4 = rcol & 3
            e = jnp.where(r4 == 0, pltpu.roll(fg, q - 1, 0),
                          jnp.where(r4 == 1, 1.0, jnp.where(r4 == 2, fg, fg * pltpu.roll(fg, 1, 0))))
        else:
            e = jnp.where(second, fg, 1.0)
        u = jnp.where(second, qq, kk) * e
        if lowp:
            u = u.astype(BF16)
        pair = (rx >> int(math.log2(m))) == 1
        for h in range(HG_HEADS):
            hs = slice(h * HG_HEADDIM, (h + 1) * HG_HEADDIM)
            s = lax.dot_general(u[:, hs], u[:, hs], NT, preferred_element_type=F32)
            att[h] = jnp.where(pair, s, 0.0 if att[h] is None else att[h])

    qg = qq * jnp.exp(gcum)
    kend = kk * jnp.exp(gcum[q - 1:q, :] - gcum)
    ones = jnp.ones((q, 128), F32)
    for h in range(HG_HEADS):
        hs = slice(h * HG_HEADDIM, (h + 1) * HG_HEADDIM)
        diag = jnp.sum(qq[:, hs] * kk[:, hs], axis=1, keepdims=True)
        a = jnp.where(rt > cs, att[h], jnp.where(rt == cs, diag, 0.0))
        sp = s_scr[h]
        oh = _mm(a, vv[:, hs], lowp) + _mm(qg[:, hs], sp, lowp)
        dec = jnp.exp(_xdot(lf[:, hs], ones, TN, 0, lowp))
        s_scr[h] = sp * dec + _mm(kend[:, hs], vv[:, hs], lowp, TN)
        o_scr[:, hs] = oh * lax.rsqrt(jnp.mean(oh * oh, -1, keepdims=True) + EPS)

    y_ref[...] = o_scr[...] * nw_ref[...] * _silu(g_ref[...])

    @pl.when(c == nc - 1)
    def _():
        so_ref[0] = s_scr[...]


def _hgrn(proj, s_state, layer, w, consts, *, b, q, nc, valid, lowp):
    t = b * nc * q
    nlev = int(math.log2(q))
    kern = functools.partial(_hgrn_kernel, q=q, nc=nc, valid=valid, lowp=lowp, nlev=nlev)
    return pl.pallas_call(
        kern,
        grid=(b, nc),
        in_specs=[
            _col_spec(q, nc, COL_D_F, BRANCH_W),
            _col_spec(q, nc, COL_D_I, BRANCH_W),
            _col_spec(q, nc, COL_D_Q, BRANCH_W),
            _col_spec(q, nc, COL_D_G, BRANCH_W),
            _state_in_spec(layer, (HG_HEADS, HG_HEADDIM, HG_HEADDIM)),
            _const_spec((1, BRANCH_W)),
            _const_spec((1, BRANCH_W)),
            _const_spec((q, q)),
        ],
        out_specs=[
            _y_spec(q, nc),
            _state_out_spec((HG_HEADS, HG_HEADDIM, HG_HEADDIM)),
        ],
        out_shape=[
            jax.ShapeDtypeStruct((t, BRANCH_W), F32),
            jax.ShapeDtypeStruct((b, HG_HEADS, HG_HEADDIM, HG_HEADDIM), F32),
        ],
        scratch_shapes=[
            pltpu.VMEM((HG_HEADS, HG_HEADDIM, HG_HEADDIM), F32),
            pltpu.VMEM((q, BRANCH_W), F32),
        ],
        compiler_params=_MIXER_PARAMS,
        name="hgrn",
    )(proj, proj, proj, proj, s_state, w['hg_lb'], w['hg_norm_w'], consts['tril'])


def _chunk_consts(q):
    r = np.arange(q)
    tril = (r[:, None] >= r[None, :]).astype(np.float32)
    expand = np.zeros((128, BRANCH_W), np.float32)
    for h in range(SSD_HEADS):
        expand[h, h * SSD_HEADDIM:(h + 1) * SSD_HEADDIM] = 1.0
    return {'tril': jnp.asarray(tril), 'triu': jnp.asarray(tril.T), 'expand': jnp.asarray(expand)}


def _cmul(ar, ai, br, bi):
    return ar * br - ai * bi, ar * bi + ai * br


def _layer_params(l, p, hg_lb):
    w_in = p['w_in'][l]
    small = jnp.concatenate([w_in[:, 1280:1288], w_in[:, 3848:3856]], axis=1)
    w_perm = jnp.concatenate([
        w_in[:, 5904:10000],
        w_in[:, 0:512],
        w_in[:, 512:1024],
        w_in[:, 1288:2312],
        w_in[:, 2312:3848],
        w_in[:, 3856:5904],
        w_in[:, 1024:1280],
        small,
        jnp.zeros((D_MODEL, N_PROJ - COL_SMALL - N_SMALL), F32),
    ], axis=1).astype(BF16)

    def lane_pad(v, off):
        return jnp.zeros((1, 128), F32).at[0, off:off + v.shape[0]].set(v)

    bias_col = (lane_pad(p['ssd_dt_bias'][l], SMALL_DT) + lane_pad(p['ml_i_bias'][l], SMALL_I)
                + lane_pad(p['ml_f_bias'][l], SMALL_F))
    alog_col = lane_pad(p['ssd_A_log'][l], SMALL_DT)

    dt = jnp.exp(p['s5_log_dt'][l])[:, None]
    lr = p['s5_A_re'][l]
    li = p['s5_A_im'][l]
    mag = jnp.exp(lr * dt)
    abr, abi = mag * jnp.cos(li * dt), mag * jnp.sin(li * dt)
    den = lr * lr + li * li
    cr = ((abr - 1.0) * lr + abi * li) / den
    ci = (abi * lr - (abr - 1.0) * li) / den
    bbr, bbi = _cmul(cr[..., None], ci[..., None], p['s5_B_re'][l], p['s5_B_im'][l])
    eye8 = jnp.eye(8, dtype=F32)

    def pack_b(bb):
        return jnp.einsum('jgpc,gh->jgchp', bb.reshape(4, 8, S5_STATE, S5_GROUP), eye8).reshape(4, 128, 512)

    def pack_c(cc):
        return jnp.einsum('jgcp,gh->jgphc', cc.reshape(4, 8, S5_GROUP, S5_STATE), eye8).reshape(4, 512, 128)

    return {
        'norm_w': p['norm_w'][l].reshape(1, D_MODEL),
        'w_in': w_perm,
        'w_small_t': small.T.astype(BF16),
        'bias_col': bias_col, 'bias_row': bias_col[0, :N_SMALL].reshape(N_SMALL, 1),
        'alog_col': alog_col, 'alog_row': alog_col[0, :N_SMALL].reshape(N_SMALL, 1),
        'ssd_conv_w': p['ssd_conv_w'][l], 'ssd_conv_b': p['ssd_conv_b'][l].reshape(1, SSD_CONV_CH),
        'ssd_D_exp': jnp.repeat(p['ssd_D'][l], SSD_HEADDIM).reshape(1, BRANCH_W),
        'ssd_norm_w': p['ssd_norm_w'][l].reshape(1, BRANCH_W),
        's5_wbr': pack_b(bbr).astype(BF16), 's5_wbi': pack_b(bbi).astype(BF16),
        's5_wcr': pack_c(p['s5_C_re'][l]).astype(BF16), 's5_wci': pack_c(p['s5_C_im'][l]).astype(BF16),
        's5_ar': jnp.broadcast_to(abr.reshape(1, S5_LANES), (8, S5_LANES)),
        's5_ai': jnp.broadcast_to(abi.reshape(1, S5_LANES), (8, S5_LANES)),
        's5_D': p['s5_D'][l].reshape(1, BRANCH_W), 's5_glu_w': p['s5_glu_w'][l].astype(BF16),
        'ml_conv_w': p['ml_conv_w'][l], 'ml_conv_b': p['ml_conv_b'][l].reshape(1, BRANCH_W),
        'ml_wq': p['ml_wq'][l], 'ml_wk': p['ml_wk'][l], 'ml_wv': p['ml_wv'][l],
        'ml_norm_w': p['ml_norm_w'][l].reshape(1, BRANCH_W), 'ml_skip': p['ml_skip'][l].reshape(1, BRANCH_W),
        'hg_lb': hg_lb[l].reshape(1, BRANCH_W), 'hg_norm_w': p['hg_norm_w'][l].reshape(1, BRANCH_W),
        'w_branch': p['w_branch'][l].astype(BF16), 'w_out': p['w_out'][l].astype(BF16),
    }


_MATMUL_WEIGHTS = ('ml_wq', 'ml_wk', 'ml_wv')
S5_CHUNK = 32


def _run_group(x3, states, layers, fnw, *, q, valid):
    b, lp, _ = x3.shape
    nc = lp // q
    t = b * lp
    lowp = q >= 16
    consts = _chunk_consts(q)
    ssd_conv, ssd_h, s5_re, s5_im, ml_conv, ml_c, ml_n, ml_m, hg_s = states
    s5_re = s5_re.reshape(DEPTH, b, S5_LANES)
    s5_im = s5_im.reshape(DEPTH, b, S5_LANES)
    if valid < q:
        s5_kw = dict(b=b, lp=lp, q=q, valid=valid, bb=min(b, 128))
    else:
        qs = math.gcd(lp, S5_CHUNK)
        s5_kw = dict(b=b, lp=lp, q=qs, valid=qs, bb=8)
    ml_m = jnp.broadcast_to(jnp.pad(ml_m, ((0, 0), (0, 0), (0, 8 - ML_HEADS)))[..., None], (DEPTH, b, 8, 128))
    kw = dict(b=b, q=q, nc=nc, valid=valid, lowp=lowp)
    x = x3.reshape(t, D_MODEL)
    new = []
    for l, w in enumerate(layers):
        if lowp:
            w = dict(w, **{k: w[k].astype(BF16) for k in _MATMUL_WEIGHTS})
        proj, small_t = _inproj(x, w['norm_w'], w['w_in'], w['w_small_t'])
        srow = small_t.reshape(N_SMALL, b, lp).transpose(1, 0, 2)
        ya, n_ssd_conv, n_ssd_h = _ssd(proj, srow, ssd_conv, ssd_h, l, w, consts, **kw)
        yb, n_re, n_im = _s5(proj, s5_re, s5_im, l, w, **s5_kw)
        yc, n_ml_conv, n_c, n_n, n_m = _mlstm(proj, srow, ml_conv, ml_c, ml_n, ml_m, l, w, consts, **kw)
        yd, n_s = _hgrn(proj, hg_s, l, w, consts, **kw)
        x = _merge(proj, (ya, yb, yc, yd), x, w['w_branch'], w['w_out'], fnw, final=(l == DEPTH - 1))
        new.append((n_ssd_conv, n_ssd_h,
                    n_re.reshape(b, S5_GROUPS, S5_STATE), n_im.reshape(b, S5_GROUPS, S5_STATE),
                    n_ml_conv, n_c, n_n, n_m[:, :ML_HEADS, 0], n_s))
    new_states = tuple(jnp.stack(ss, axis=0) for ss in zip(*new))
    return x.reshape(b, lp, D_MODEL), new_states


SAMPLE_PAD = 8


def kernel(x_prompt, x_sample, state_ssd_conv, state_ssd, state_s5_re, state_s5_im, state_mlstm_conv, state_mlstm_C, state_mlstm_n, state_mlstm_m, state_hgrn, norm_w, w_in, ssd_conv_w, ssd_conv_b, ssd_dt_bias, ssd_A_log, ssd_D, ssd_norm_w, s5_A_re, s5_A_im, s5_B_re, s5_B_im, s5_C_re, s5_C_im, s5_D, s5_log_dt, s5_glu_w, ml_conv_w, ml_conv_b, ml_wq, ml_wk, ml_wv, ml_i_bias, ml_f_bias, ml_norm_w, ml_skip, hg_lb_logits, hg_norm_w, w_branch, w_out, final_norm_w):
    p = {'norm_w': norm_w, 'w_in': w_in,
         'ssd_conv_w': ssd_conv_w, 'ssd_conv_b': ssd_conv_b, 'ssd_dt_bias': ssd_dt_bias,
         'ssd_A_log': ssd_A_log, 'ssd_D': ssd_D, 'ssd_norm_w': ssd_norm_w,
         's5_A_re': s5_A_re, 's5_A_im': s5_A_im, 's5_B_re': s5_B_re, 's5_B_im': s5_B_im,
         's5_C_re': s5_C_re, 's5_C_im': s5_C_im, 's5_D': s5_D, 's5_log_dt': s5_log_dt, 's5_glu_w': s5_glu_w,
         'ml_conv_w': ml_conv_w, 'ml_conv_b': ml_conv_b, 'ml_wq': ml_wq, 'ml_wk': ml_wk, 'ml_wv': ml_wv,
         'ml_i_bias': ml_i_bias, 'ml_f_bias': ml_f_bias, 'ml_norm_w': ml_norm_w, 'ml_skip': ml_skip,
         'hg_norm_w': hg_norm_w, 'w_branch': w_branch, 'w_out': w_out}
    lb_cum = jnp.cumsum(jax.nn.softmax(hg_lb_logits, axis=0), axis=0)
    hg_lb = lb_cum - lb_cum[0]
    layers = [_layer_params(l, p, hg_lb) for l in range(DEPTH)]
    fnw = final_norm_w.reshape(1, D_MODEL)

    sample_states = (state_ssd_conv, state_ssd, state_s5_re, state_s5_im, state_mlstm_conv,
                     state_mlstm_C, state_mlstm_n, state_mlstm_m, state_hgrn)
    bp, lp_, _ = x_prompt.shape
    prompt_states = tuple(jnp.zeros((DEPTH, bp) + s.shape[2:], F32) for s in sample_states)
    q_prompt = math.gcd(lp_, 128)
    y_prompt, new_p = _run_group(x_prompt, prompt_states, layers, fnw, q=q_prompt, valid=q_prompt)

    ls = x_sample.shape[1]
    xs_pad = jnp.pad(x_sample, ((0, 0), (0, SAMPLE_PAD - ls), (0, 0)))
    y_sample, new_s = _run_group(xs_pad, sample_states, layers, fnw, q=SAMPLE_PAD, valid=ls)
    y_sample = y_sample[:, :ls]

    out = [y_prompt, y_sample]
    for ps, ss in zip(new_p, new_s):
        out += [ps, ss]
    return tuple(out)
```

```python
import functools
import math

import numpy as np
import jax
import jax.numpy as jnp
from jax import lax
from jax.experimental import pallas as pl
from jax.experimental.pallas import tpu as pltpu

F32 = jnp.float32
BF16 = jnp.bfloat16

D_MODEL = 1024
DEPTH = 2
BRANCH_W = 512
CONV_W = 4
EPS = 1e-6
SSD_HEADS = 8
SSD_HEADDIM = 64
SSD_STATE = 64
SSD_GROUPS = 2
SSD_CONV_CH = 768
S5_GROUPS = 32
S5_GROUP = 16
S5_STATE = 64
S5_LANES = S5_GROUPS * S5_STATE
ML_HEADS = 4
ML_HEADDIM = 128
HG_HEADS = 4
HG_HEADDIM = 128
N_BRANCH = 4

NEG = -1e30

N_PROJ = 10240
COL_MERGE = 0
COL_A_Z = 4096
COL_A_X = 4608
COL_B_U = 5120
COL_B_GATE = 5632
COL_C_X = 6144
COL_C_Z = 6656
COL_C_O = 7168
COL_D_F = 7680
COL_D_I = 8192
COL_D_Q = 8704
COL_D_G = 9216
COL_A_BC = 9728
COL_SMALL = 9984
N_SMALL = 16
SMALL_DT = 0
SMALL_I = 8
SMALL_F = 12

NN = (((1,), (0,)), ((), ()))
NT = (((1,), (1,)), ((), ()))
TN = (((0,), (0,)), ((), ()))


def _mm(a, b, lowp, dims=NN):
    if lowp:
        a = a.astype(BF16)
        b = b.astype(BF16)
    return lax.dot_general(a, b, dims, preferred_element_type=F32)


def _split3(a):
    hi = a.astype(BF16)
    r = a - hi.astype(F32)
    mid = r.astype(BF16)
    lo = (r - mid.astype(F32)).astype(BF16)
    return hi, mid, lo


def _xdot(a, b, dims, split, lowp):
    if not lowp:
        return lax.dot_general(a, b, dims, precision=lax.Precision.HIGHEST,
                               preferred_element_type=F32)
    if split == 0:
        other = b.astype(BF16)
        return sum(lax.dot_general(p, other, dims, preferred_element_type=F32) for p in _split3(a))
    other = a.astype(BF16)
    return sum(lax.dot_general(other, p, dims, preferred_element_type=F32) for p in _split3(b))


def _sigmoid(x):
    return jax.nn.sigmoid(x)


def _silu(x):
    return x * jax.nn.sigmoid(x)


def _softplus(x):
    return jnp.maximum(x, 0.0) + jnp.log1p(jnp.exp(-jnp.abs(x)))


def _conv_chunk(xp_scr, cw_ref, cb_ref, q, valid):
    cw = cw_ref[...]
    conv = (cb_ref[...]
            + cw[0:1] * xp_scr[5:5 + q, :]
            + cw[1:2] * xp_scr[6:6 + q, :]
            + cw[2:3] * xp_scr[7:7 + q, :]
            + cw[3:4] * xp_scr[8:8 + q, :])
    carry = xp_scr[5 + valid:8 + valid, :]
    xp_scr[5:8, :] = carry
    return conv, carry


def _inproj_kernel(x_ref, nw_ref, w_ref, wst_ref, o_ref, st_ref, xn_scr):
    @pl.when(pl.program_id(1) == 0)
    def _():
        x = x_ref[...]
        xn = x * lax.rsqrt(jnp.mean(x * x, -1, keepdims=True) + EPS) * nw_ref[...]
        xb = xn.astype(BF16)
        xn_scr[...] = xb
        st_ref[...] = lax.dot_general(wst_ref[...], xb, NT, preferred_element_type=F32)

    o_ref[...] = jnp.dot(xn_scr[...], w_ref[...], preferred_element_type=F32)


def _inproj(x, nw, w, wst):
    t = x.shape[0]
    tm = min(1024, t)
    tn = 1024
    return pl.pallas_call(
        _inproj_kernel,
        grid=(t // tm, N_PROJ // tn),
        in_specs=[
            pl.BlockSpec((tm, D_MODEL), lambda i, j: (i, 0)),
            pl.BlockSpec((1, D_MODEL), lambda i, j: (0, 0)),
            pl.BlockSpec((D_MODEL, tn), lambda i, j: (0, j)),
            pl.BlockSpec((N_SMALL, D_MODEL), lambda i, j: (0, 0)),
        ],
        out_specs=[
            pl.BlockSpec((tm, tn), lambda i, j: (i, j)),
            pl.BlockSpec((N_SMALL, tm), lambda i, j: (0, i)),
        ],
        out_shape=[
            jax.ShapeDtypeStruct((t, N_PROJ), F32),
            jax.ShapeDtypeStruct((N_SMALL, t), F32),
        ],
        scratch_shapes=[pltpu.VMEM((tm, D_MODEL), BF16)],
        compiler_params=pltpu.CompilerParams(
            dimension_semantics=("arbitrary", "arbitrary"),
            vmem_limit_bytes=48 * 1024 * 1024),
        name="inproj",
    )(x, nw, w, wst)


def _merge_kernel(g_ref, ya_ref, yb_ref, yc_ref, yd_ref, x_ref, wb_ref, wo_ref, fnw_ref, o_ref, *, final):
    acc = None
    for i, y_ref in enumerate((ya_ref, yb_ref, yc_ref, yd_ref)):
        gate = _sigmoid(g_ref[:, i * D_MODEL:(i + 1) * D_MODEL])
        term = gate * jnp.dot(y_ref[...].astype(BF16), wb_ref[i], preferred_element_type=F32)
        acc = term if acc is None else acc + term
    out = x_ref[...] + jnp.dot(acc.astype(BF16), wo_ref[...], preferred_element_type=F32)
    if final:
        out = out * lax.rsqrt(jnp.mean(out * out, -1, keepdims=True) + EPS) * fnw_ref[...]
    o_ref[...] = out


def _merge(proj, ys, x, wb, wo, fnw, final):
    t = x.shape[0]
    tm = min(256, t)
    yspec = pl.BlockSpec((tm, BRANCH_W), lambda i: (i, 0))
    return pl.pallas_call(
        functools.partial(_merge_kernel, final=final),
        grid=(t // tm,),
        in_specs=[
            pl.BlockSpec((tm, N_BRANCH * D_MODEL), lambda i: (i, COL_MERGE // (N_BRANCH * D_MODEL))),
            yspec, yspec, yspec, yspec,
            pl.BlockSpec((tm, D_MODEL), lambda i: (i, 0)),
            pl.BlockSpec((N_BRANCH, BRANCH_W, D_MODEL), lambda i: (0, 0, 0)),
            pl.BlockSpec((D_MODEL, D_MODEL), lambda i: (0, 0)),
            pl.BlockSpec((1, D_MODEL), lambda i: (0, 0)),
        ],
        out_specs=pl.BlockSpec((tm, D_MODEL), lambda i: (i, 0)),
        out_shape=jax.ShapeDtypeStruct((t, D_MODEL), F32),
        compiler_params=pltpu.CompilerParams(
            dimension_semantics=("arbitrary",),
            vmem_limit_bytes=48 * 1024 * 1024),
        name="merge",
    )(proj, *ys, x, wb, wo, fnw)


def _col_spec(bs, q, col, width):
    return pl.BlockSpec((bs, q, width), lambda b, c: (b, c, col // width))


def _row_spec(bs, q):
    return pl.BlockSpec((bs, N_SMALL, q), lambda b, c: (b, 0, c))


def _const_spec(shape):
    nd = len(shape)
    return pl.BlockSpec(shape, lambda b, c: (0,) * nd)


def _state_in_spec(layer, bs, tail):
    nd = len(tail)
    return pl.BlockSpec((1, bs) + tail, lambda b, c: (layer, b) + (0,) * nd)


def _state_out_spec(bs, tail):
    nd = len(tail)
    return pl.BlockSpec((bs,) + tail, lambda b, c: (b,) + (0,) * nd)


def _y_spec(bs, q):
    return pl.BlockSpec((bs, q, BRANCH_W), lambda b, c: (b, c, 0))


_DONE = object()


def _per_seq(kern_init, kern_main, kern_final, seq_refs, const_refs, bs, nc):
    c = pl.program_id(1)
    views = [[r.at[s] for r in seq_refs] for s in range(bs)]

    @pl.when(c == 0)
    def _():
        for s in range(bs):
            kern_init(views[s])

    active = [kern_main(views[s], const_refs) for s in range(bs)]
    while active:
        active = [g for g in active if next(g, _DONE) is not _DONE]

    @pl.when(c == nc - 1)
    def _():
        for s in range(bs):
            kern_final(views[s])


_MIXER_PARAMS = pltpu.CompilerParams(
    dimension_semantics=("arbitrary", "arbitrary"),
    vmem_limit_bytes=48 * 1024 * 1024)


def _ssd_kernel(z_ref, x_ref, bc_ref, sc_ref, sr_ref, conv0_ref, h0_ref,
                cw_ref, cb_ref, bcol_ref, brow_ref, alc_ref, alr_ref, dexp_ref, nw_ref,
                e_ref, tl_ref, tu_ref,
                y_ref, convo_ref, ho_ref,
                xp_scr, h_scr, y_scr, *, q, nc, valid, lowp, bs):
    def init(v):
        v[9][5:8, :] = v[5][...]
        v[10][...] = v[6][...]

    def final(v):
        v[8][...] = v[10][...]

    _per_seq(init, functools.partial(_ssd_main, q=q, valid=valid, lowp=lowp), final,
             (z_ref, x_ref, bc_ref, sc_ref, sr_ref, conv0_ref.at[0], h0_ref.at[0],
              y_ref, ho_ref, xp_scr, h_scr, y_scr, convo_ref),
             (cw_ref, cb_ref, bcol_ref, brow_ref, alc_ref, alr_ref, dexp_ref, nw_ref, e_ref, tl_ref, tu_ref),
             bs, nc)


def _ssd_main(v, k, *, q, valid, lowp):
    z_ref, x_ref, bc_ref, sc_ref, sr_ref, _, _, y_ref, _, xp_scr, h_scr, y_scr, convo_ref = v
    cw_ref, cb_ref, bcol_ref, brow_ref, alc_ref, alr_ref, dexp_ref, nw_ref, e_ref, tl_ref, tu_ref = k
    xp_scr[8:8 + q, 0:BRANCH_W] = x_ref[...]
    xp_scr[8:8 + q, BRANCH_W:SSD_CONV_CH] = bc_ref[...]
    conv, carry = _conv_chunk(xp_scr, cw_ref, cb_ref, q, valid)
    convo_ref[...] = carry
    yield
    xbc = _silu(conv)
    xs = xbc[:, 0:BRANCH_W]
    bm = xbc[:, BRANCH_W:BRANCH_W + 128]
    cm = xbc[:, BRANCH_W + 128:BRANCH_W + 256]

    dt_c = _softplus(sc_ref[...] + bcol_ref[...])
    a_c = dt_c * (-jnp.exp(alc_ref[...]))
    dt_r = _softplus(sr_ref[...] + brow_ref[...])
    a_r = dt_r * (-jnp.exp(alr_ref[...]))
    if valid < q:
        tcol = lax.broadcasted_iota(jnp.int32, (q, 1), 0)
        trow = lax.broadcasted_iota(jnp.int32, (1, q), 1)
        a_c = jnp.where(tcol < valid, a_c, 0.0)
        dt_c = jnp.where(tcol < valid, dt_c, 0.0)
        a_r = jnp.where(trow < valid, a_r, 0.0)

    tl = tl_ref[...]
    acum_c = _xdot(tl, a_c, NN, 1, lowp)
    acum_r = _xdot(a_r, tu_ref[...], NN, 0, lowp)
    e = e_ref[...]
    dt_e = _xdot(dt_c, e, NN, 0, lowp)
    acum_e = _xdot(acum_c, e, NN, 0, lowp)
    xdt = xs * dt_e
    xend = xdt * jnp.exp(acum_e[q - 1:q, :] - acum_e)
    eac = jnp.exp(acum_e)
    dec_last = jnp.exp(acum_c[q - 1:q, :])
    yield

    tri = (lax.broadcasted_iota(jnp.int32, (q, q), 0) >= lax.broadcasted_iota(jnp.int32, (q, q), 1))
    rpg = SSD_HEADS // SSD_GROUPS
    for g in range(SSD_GROUPS):
        bg = bm[:, g * SSD_STATE:(g + 1) * SSD_STATE]
        cg = cm[:, g * SSD_STATE:(g + 1) * SSD_STATE]
        cb = _mm(cg, bg, lowp, NT)
        for r in range(rpg):
            h = g * rpg + r
            hs = slice(h * SSD_HEADDIM, (h + 1) * SSD_HEADDIM)
            seg = acum_c[:, h:h + 1] - acum_r[h:h + 1, :]
            m = cb * jnp.exp(jnp.where(tri, seg, NEG))
            hp = h_scr[h]
            yh = _mm(m, xdt[:, hs], lowp) + _mm(cg, hp, lowp, NT) * eac[:, hs]
            y_scr[:, hs] = yh
            h_scr[h] = hp * dec_last[:, h:h + 1] + _mm(xend[:, hs], bg, lowp, TN)
            yield

    y = y_scr[...] + xs * dexp_ref[...]
    y = y * _silu(z_ref[...])
    y = y * lax.rsqrt(jnp.mean(y * y, -1, keepdims=True) + EPS) * nw_ref[...]
    y_ref[...] = y


def _ssd(proj, srow, conv_state, h_state, layer, w, consts, *, b, q, nc, valid, lowp, bs):
    kern = functools.partial(_ssd_kernel, q=q, nc=nc, valid=valid, lowp=lowp, bs=bs)
    return pl.pallas_call(
        kern,
        grid=(b // bs, nc),
        in_specs=[
            _col_spec(bs, q, COL_A_Z, BRANCH_W),
            _col_spec(bs, q, COL_A_X, BRANCH_W),
            _col_spec(bs, q, COL_A_BC, 256),
            _col_spec(bs, q, COL_SMALL, 128),
            _row_spec(bs, q),
            _state_in_spec(layer, bs, (CONV_W - 1, SSD_CONV_CH)),
            _state_in_spec(layer, bs, (SSD_HEADS, SSD_HEADDIM, SSD_STATE)),
            _const_spec((CONV_W, SSD_CONV_CH)),
            _const_spec((1, SSD_CONV_CH)),
            _const_spec((1, 128)),
            _const_spec((N_SMALL, 1)),
            _const_spec((1, 128)),
            _const_spec((N_SMALL, 1)),
            _const_spec((1, BRANCH_W)),
            _const_spec((1, BRANCH_W)),
            _const_spec((128, BRANCH_W)),
            _const_spec((q, q)),
            _const_spec((q, q)),
        ],
        out_specs=[
            _y_spec(bs, q),
            _state_out_spec(bs, (CONV_W - 1, SSD_CONV_CH)),
            _state_out_spec(bs, (SSD_HEADS, SSD_HEADDIM, SSD_STATE)),
        ],
        out_shape=[
            jax.ShapeDtypeStruct((b, nc * q, BRANCH_W), F32),
            jax.ShapeDtypeStruct((b, CONV_W - 1, SSD_CONV_CH), F32),
            jax.ShapeDtypeStruct((b, SSD_HEADS, SSD_HEADDIM, SSD_STATE), F32),
        ],
        scratch_shapes=[
            pltpu.VMEM((bs, q + 8, SSD_CONV_CH), F32),
            pltpu.VMEM((bs, SSD_HEADS, SSD_HEADDIM, SSD_STATE), F32),
            pltpu.VMEM((bs, q, BRANCH_W), F32),
        ],
        compiler_params=_MIXER_PARAMS,
        name="ssd",
    )(proj, proj, proj, proj, srow, conv_state, h_state,
      w['ssd_conv_w'], w['ssd_conv_b'], w['bias_col'], w['bias_row'], w['alog_col'], w['alog_row'],
      w['ssd_D_exp'], w['ssd_norm_w'], consts['expand'], consts['tril'], consts['triu'])


def _s5_kernel(u_ref, gate_ref, hr0_ref, hi0_ref, perm_ref, permt_ref, ar_ref, ai_ref,
               wbr_ref, wbi_ref, wcr_ref, wci_ref, d_ref, glu_ref,
               y_ref, hro_ref, hio_ref,
               hr_scr, hi_scr, cr_scr, ci_scr, *, q, nc, ngrp):
    c = pl.program_id(1)

    @pl.when(c == 0)
    def _():
        cr_scr[...] = hr0_ref[0]
        ci_scr[...] = hi0_ref[0]

    rows_in = perm_ref.shape[1]
    u = u_ref[...].reshape(rows_in, BRANCH_W)
    u_tm = jnp.dot(perm_ref[...], u.astype(BF16), preferred_element_type=F32).astype(BF16)
    nblk = S5_LANES // 512
    for j in range(nblk):
        uj = u_tm[:, j * 128:(j + 1) * 128]
        hr_scr[:, j * 512:(j + 1) * 512] = jnp.dot(uj, wbr_ref[j], preferred_element_type=F32)
        hi_scr[:, j * 512:(j + 1) * 512] = jnp.dot(uj, wbi_ref[j], preferred_element_type=F32)

    half = S5_LANES // 2
    for hf in range(2):
        sl = slice(hf * half, (hf + 1) * half)
        ar = ar_ref[:, sl]
        ai = ai_ref[:, sl]

        def grp_body(g, carry, sl=sl, ar=ar, ai=ai):
            s0 = pl.multiple_of(g * 8, 8)

            def t_body(t, h):
                r0 = pl.multiple_of(g * (8 * q) + t * 8, 8)
                hr, hi = h
                nr = ar * hr - ai * hi + hr_scr[pl.ds(r0, 8), sl]
                ni = ar * hi + ai * hr + hi_scr[pl.ds(r0, 8), sl]
                hr_scr[pl.ds(r0, 8), sl] = nr
                hi_scr[pl.ds(r0, 8), sl] = ni
                return nr, ni

            h = lax.fori_loop(0, q, t_body, (cr_scr[pl.ds(s0, 8), sl], ci_scr[pl.ds(s0, 8), sl]),
                              unroll=min(q, 4))
            cr_scr[pl.ds(s0, 8), sl] = h[0]
            ci_scr[pl.ds(s0, 8), sl] = h[1]
            return carry

        lax.fori_loop(0, ngrp, grp_body, 0)

    parts = []
    for j in range(nblk):
        sl = slice(j * 512, (j + 1) * 512)
        parts.append(jnp.dot(hr_scr[:, sl].astype(BF16), wcr_ref[j], preferred_element_type=F32)
                     - jnp.dot(hi_scr[:, sl].astype(BF16), wci_ref[j], preferred_element_type=F32))
    y_tm = jnp.concatenate(parts, axis=1)
    y = _xdot(permt_ref[...], y_tm, NN, 1, True) + u * d_ref[...]
    g = jax.nn.gelu(y)
    y = g * _sigmoid(jnp.dot(g.astype(BF16), glu_ref[...], preferred_element_type=F32))
    y = y * _silu(gate_ref[...].reshape(rows_in, BRANCH_W))
    y_ref[...] = y.reshape(y_ref.shape)

    @pl.when(c == nc - 1)
    def _():
        hro_ref[0] = cr_scr[...]
        hio_ref[0] = ci_scr[...]


def _s5_perm(bb, lp, valid):
    perm = np.zeros((bb * valid, bb * lp), np.float32)
    for g in range(bb // 8):
        for t in range(valid):
            for s in range(8):
                perm[g * 8 * valid + t * 8 + s, (g * 8 + s) * lp + t] = 1.0
    return jnp.asarray(perm, BF16), jnp.asarray(perm.T, BF16)


def _s5(proj, hr_state, hi_state, layer, w, *, b, lp, q, valid, bb):
    nc = lp // q
    perm, permt = _s5_perm(bb, q, valid)
    rows = bb * valid
    if nc == 1:
        def col_spec(col):
            return pl.BlockSpec((bb * q, BRANCH_W), lambda i, c: (i, col // BRANCH_W))
        src = proj
        y_spec = pl.BlockSpec((bb * q, BRANCH_W), lambda i, c: (i, 0))
        y_shape = jax.ShapeDtypeStruct((b * lp, BRANCH_W), F32)
    else:
        def col_spec(col):
            return pl.BlockSpec((bb, q, BRANCH_W), lambda i, c: (i, c, col // BRANCH_W))
        src = proj.reshape(b, lp, N_PROJ)
        y_spec = pl.BlockSpec((bb, q, BRANCH_W), lambda i, c: (i, c, 0))
        y_shape = jax.ShapeDtypeStruct((b, lp, BRANCH_W), F32)
    st_in = pl.BlockSpec((1, bb, S5_LANES), lambda i, c: (layer, i, 0))
    st_out = pl.BlockSpec((1, bb, S5_LANES), lambda i, c: (0, i, 0))
    kern = functools.partial(_s5_kernel, q=valid, nc=nc, ngrp=bb // 8)
    y, hr, hi = pl.pallas_call(
        kern,
        grid=(b // bb, nc),
        in_specs=[
            col_spec(COL_B_U),
            col_spec(COL_B_GATE),
            st_in, st_in,
            _const_spec(perm.shape),
            _const_spec(permt.shape),
            _const_spec((8, S5_LANES)),
            _const_spec((8, S5_LANES)),
            _const_spec((4, 128, 512)),
            _const_spec((4, 128, 512)),
            _const_spec((4, 512, 128)),
            _const_spec((4, 512, 128)),
            _const_spec((1, BRANCH_W)),
            _const_spec((BRANCH_W, BRANCH_W)),
        ],
        out_specs=[y_spec, st_out, st_out],
        out_shape=[
            y_shape,
            jax.ShapeDtypeStruct((1, b, S5_LANES), F32),
            jax.ShapeDtypeStruct((1, b, S5_LANES), F32),
        ],
        scratch_shapes=[
            pltpu.VMEM((rows, S5_LANES), F32),
            pltpu.VMEM((rows, S5_LANES), F32),
            pltpu.VMEM((bb, S5_LANES), F32),
            pltpu.VMEM((bb, S5_LANES), F32),
        ],
        compiler_params=_MIXER_PARAMS,
        name="s5",
    )(src, src, hr_state, hi_state, perm, permt, w['s5_ar'], w['s5_ai'],
      w['s5_wbr'], w['s5_wbi'], w['s5_wcr'], w['s5_wci'], w['s5_D'], w['s5_glu_w'])
    return y.reshape(b * lp, BRANCH_W), hr[0], hi[0]


def _mlstm_kernel(x_ref, z_ref, o_ref, sc_ref, sr_ref, conv0_ref, c0_ref, n0_ref, m0_ref,
                  cw_ref, cb_ref, wq_ref, wk_ref, wv_ref, bcol_ref, brow_ref, nw_ref, skip_ref,
                  tl_ref, tu_ref,
                  y_ref, convo_ref, co_ref, no_ref, mo_ref,
                  xp_scr, c_scr, n_scr, m_scr, h_scr, *, q, nc, valid, lowp, bs):
    def init(v):
        v[13][5:8, :] = v[5][...]
        v[14][...] = v[6][...]
        v[15][...] = v[7][...]
        v[16][...] = v[8][...]

    def final(v):
        v[10][...] = v[14][...]
        v[11][...] = v[15][...]
        v[12][...] = v[16][...]

    _per_seq(init, functools.partial(_mlstm_main, q=q, valid=valid, lowp=lowp), final,
             (x_ref, z_ref, o_ref, sc_ref, sr_ref, conv0_ref.at[0], c0_ref.at[0], n0_ref.at[0], m0_ref.at[0],
              y_ref, co_ref, no_ref, mo_ref, xp_scr, c_scr, n_scr, m_scr, h_scr, convo_ref),
             (cw_ref, cb_ref, wq_ref, wk_ref, wv_ref, bcol_ref, brow_ref, nw_ref, skip_ref, tl_ref, tu_ref),
             bs, nc)


def _mlstm_main(v, k, *, q, valid, lowp):
    (x_ref, z_ref, o_ref, sc_ref, sr_ref, _, _, _, _, y_ref, _, _, _,
     xp_scr, c_scr, n_scr, m_scr, h_scr, convo_ref) = v
    cw_ref, cb_ref, wq_ref, wk_ref, wv_ref, bcol_ref, brow_ref, nw_ref, skip_ref, tl_ref, tu_ref = k
    x = x_ref[...]
    xp_scr[8:8 + q, :] = x
    conv, carry = _conv_chunk(xp_scr, cw_ref, cb_ref, q, valid)
    convo_ref[...] = carry
    yield
    xc = _silu(conv)

    pre_c = sc_ref[...] + bcol_ref[...]
    pre_r = sr_ref[...] + brow_ref[...]
    ig_c = pre_c
    lf_c = -_softplus(-pre_c)
    ig_r = pre_r
    lf_r = -_softplus(-pre_r)
    if valid < q:
        tcol = lax.broadcasted_iota(jnp.int32, (q, 1), 0)
        trow = lax.broadcasted_iota(jnp.int32, (1, q), 1)
        lf_c = jnp.where(tcol < valid, lf_c, 0.0)
        ig_c = jnp.where(tcol < valid, ig_c, NEG)
        lf_r = jnp.where(trow < valid, lf_r, 0.0)
        ig_r = jnp.where(trow < valid, ig_r, NEG)
    b_c = _xdot(tl_ref[...], lf_c, NN, 1, lowp)
    b_r = _xdot(lf_r, tu_ref[...], NN, 0, lowp)
    yield

    tri = (lax.broadcasted_iota(jnp.int32, (q, q), 0) >= lax.broadcasted_iota(jnp.int32, (q, q), 1))
    o_all = o_ref[...]
    for h in range(ML_HEADS):
        hs = slice(h * ML_HEADDIM, (h + 1) * ML_HEADDIM)
        xh = xc[:, hs]
        qh = _mm(xh, wq_ref[h], lowp)
        kh = _mm(xh, wk_ref[h], lowp) * (ML_HEADDIM ** -0.5)
        vh = _mm(x[:, hs], wv_ref[h], lowp)
        yield
        bc = b_c[:, SMALL_F + h:SMALL_F + h + 1]
        br = b_r[SMALL_F + h:SMALL_F + h + 1, :]
        ic = ig_c[:, SMALL_I + h:SMALL_I + h + 1]
        ir = ig_r[SMALL_I + h:SMALL_I + h + 1, :]
        mp = m_scr[h:h + 1, 0:1]
        dlog = jnp.where(tri, bc - br + ir, NEG)
        inter = bc + mp
        m_t = jnp.maximum(inter, jnp.max(dlog, axis=1, keepdims=True))
        wgt = jnp.exp(dlog - m_t)
        yield
        s = _mm(qh, kh, lowp, NT) * wgt
        scale = jnp.exp(inter - m_t)
        cp = c_scr[h]
        npv = n_scr[h:h + 1, :]
        num = _mm(s, vh, lowp) + scale * _mm(qh, cp, lowp)
        dot = jnp.sum(s, axis=1, keepdims=True) + scale * jnp.sum(qh * npv, axis=1, keepdims=True)
        hh = num / jnp.maximum(jnp.abs(dot), jnp.exp(-m_t))
        yield
        m_new = m_t[q - 1:q, :]
        b_last = bc[q - 1:q, :]
        w_end = jnp.exp(b_last - bc + ic - m_new)
        cs = jnp.exp(b_last + mp - m_new)
        kw = kh * w_end
        c_scr[h] = cs * cp + _mm(kw, vh, lowp, TN)
        n_scr[h:h + 1, :] = cs * npv + jnp.sum(kw, axis=0, keepdims=True)
        m_scr[h:h + 1, :] = jnp.broadcast_to(m_new, (1, 128))
        yield
        hh = hh * _sigmoid(o_all[:, hs])
        hc = hh - jnp.mean(hh, -1, keepdims=True)
        h_scr[:, hs] = hc * lax.rsqrt(jnp.mean(hc * hc, -1, keepdims=True) + EPS)
        yield

    y = h_scr[...] * nw_ref[...] + skip_ref[...] * xc
    y_ref[...] = y * _silu(z_ref[...])


def _mlstm(proj, srow, conv_state, c_state, n_state, m_state, layer, w, consts, *, b, q, nc, valid, lowp, bs):
    kern = functools.partial(_mlstm_kernel, q=q, nc=nc, valid=valid, lowp=lowp, bs=bs)
    return pl.pallas_call(
        kern,
        grid=(b // bs, nc),
        in_specs=[
            _col_spec(bs, q, COL_C_X, BRANCH_W),
            _col_spec(bs, q, COL_C_Z, BRANCH_W),
            _col_spec(bs, q, COL_C_O, BRANCH_W),
            _col_spec(bs, q, COL_SMALL, 128),
            _row_spec(bs, q),
            _state_in_spec(layer, bs, (CONV_W - 1, BRANCH_W)),
            _state_in_spec(layer, bs, (ML_HEADS, ML_HEADDIM, ML_HEADDIM)),
            _state_in_spec(layer, bs, (ML_HEADS, ML_HEADDIM)),
            _state_in_spec(layer, bs, (8, 128)),
            _const_spec((CONV_W, BRANCH_W)),
            _const_spec((1, BRANCH_W)),
            _const_spec((ML_HEADS, ML_HEADDIM, ML_HEADDIM)),
            _const_spec((ML_HEADS, ML_HEADDIM, ML_HEADDIM)),
            _const_spec((ML_HEADS, ML_HEADDIM, ML_HEADDIM)),
            _const_spec((1, 128)),
            _const_spec((N_SMALL, 1)),
            _const_spec((1, BRANCH_W)),
            _const_spec((1, BRANCH_W)),
            _const_spec((q, q)),
            _const_spec((q, q)),
        ],
        out_specs=[
            _y_spec(bs, q),
            _state_out_spec(bs, (CONV_W - 1, BRANCH_W)),
            _state_out_spec(bs, (ML_HEADS, ML_HEADDIM, ML_HEADDIM)),
            _state_out_spec(bs, (ML_HEADS, ML_HEADDIM)),
            _state_out_spec(bs, (8, 128)),
        ],
        out_shape=[
            jax.ShapeDtypeStruct((b, nc * q, BRANCH_W), F32),
            jax.ShapeDtypeStruct((b, CONV_W - 1, BRANCH_W), F32),
            jax.ShapeDtypeStruct((b, ML_HEADS, ML_HEADDIM, ML_HEADDIM), F32),
            jax.ShapeDtypeStruct((b, ML_HEADS, ML_HEADDIM), F32),
            jax.ShapeDtypeStruct((b, 8, 128), F32),
        ],
        scratch_shapes=[
            pltpu.VMEM((bs, q + 8, BRANCH_W), F32),
            pltpu.VMEM((bs, ML_HEADS, ML_HEADDIM, ML_HEADDIM), F32),
            pltpu.VMEM((bs, ML_HEADS, ML_HEADDIM), F32),
            pltpu.VMEM((bs, 8, 128), F32),
            pltpu.VMEM((bs, q, BRANCH_W), F32),
        ],
        compiler_params=_MIXER_PARAMS,
        name="mlstm",
    )(proj, proj, proj, proj, srow, conv_state, c_state, n_state, m_state,
      w['ml_conv_w'], w['ml_conv_b'], w['ml_wq'], w['ml_wk'], w['ml_wv'], w['bias_col'], w['bias_row'],
      w['ml_norm_w'], w['ml_skip'], consts['tril'], consts['triu'])


def _hgrn_kernel(f_ref, i_ref, q_ref, g_ref, s0_ref, lb_ref, nw_ref, tl_ref,
                 y_ref, so_ref,
                 s_scr, o_scr, *, q, nc, valid, lowp, nlev, bs):
    def init(v):
        v[7][...] = v[4][...]

    def final(v):
        v[6][...] = v[7][...]

    _per_seq(init, functools.partial(_hgrn_main, q=q, valid=valid, lowp=lowp, nlev=nlev), final,
             (f_ref, i_ref, q_ref, g_ref, s0_ref.at[0], y_ref, so_ref, s_scr, o_scr),
             (lb_ref, nw_ref, tl_ref), bs, nc)


def _hgrn_main(v, k, *, q, valid, lowp, nlev):
    f_ref, i_ref, q_ref, g_ref, _, y_ref, _, s_scr, o_scr = v
    lb_ref, nw_ref, tl_ref = k
    lb = lb_ref[...]
    fg = lb + (1.0 - lb) * _sigmoid(f_ref[...])
    kk = 1.0 - fg
    qq = _silu(q_ref[...]) * (HG_HEADDIM ** -0.5)
    vv = i_ref[...]
    rcol = lax.broadcasted_iota(jnp.int32, (q, 1), 0)
    if valid < q:
        fg = jnp.where(rcol < valid, fg, 1.0)
        kk = jnp.where(rcol < valid, kk, 0.0)
    lf = jnp.log(fg)

    gcum = _xdot(tl_ref[...], lf, NN, 1, lowp)
    yield

    rt = lax.broadcasted_iota(jnp.int32, (q, q), 0)
    cs = lax.broadcasted_iota(jnp.int32, (q, q), 1)
    rx = rt ^ cs
    att = [None] * HG_HEADS
    for lev in range(nlev):
        m = q >> (lev + 1)
        second = (rcol & m) != 0
        if m >= 4:
            nb = q // (2 * m)
            gb = jnp.broadcast_to(gcum.reshape(nb, 2 * m, BRANCH_W)[:, m - 1:m, :], (nb, 2 * m, BRANCH_W))
            e = jnp.exp(-jnp.abs(gcum - gb.reshape(q, BRANCH_W)))
        elif m == 2:
            r4 = rcol & 3
            e = jnp.where(r4 == 0, pltpu.roll(fg, q - 1, 0),
                          jnp.where(r4 == 1, 1.0, jnp.where(r4 == 2, fg, fg * pltpu.roll(fg, 1, 0))))
        else:
            e = jnp.where(second, fg, 1.0)
        u = jnp.where(second, qq, kk) * e
        if lowp:
            u = u.astype(BF16)
        pair = (rx >> int(math.log2(m))) == 1
        for h in range(HG_HEADS):
            hs = slice(h * HG_HEADDIM, (h + 1) * HG_HEADDIM)
            s = lax.dot_general(u[:, hs], u[:, hs], NT, preferred_element_type=F32)
            att[h] = jnp.where(pair, s, 0.0 if att[h] is None else att[h])
        yield

    qg = qq * jnp.exp(gcum)
    kend = kk * jnp.exp(gcum[q - 1:q, :] - gcum)
    ones = jnp.ones((q, 128), F32)
    for h in range(HG_HEADS):
        hs = slice(h * HG_HEADDIM, (h + 1) * HG_HEADDIM)
        diag = jnp.sum(qq[:, hs] * kk[:, hs], axis=1, keepdims=True)
        a = jnp.where(rt > cs, att[h], jnp.where(rt == cs, diag, 0.0))
        sp = s_scr[h]
        oh = _mm(a, vv[:, hs], lowp) + _mm(qg[:, hs], sp, lowp)
        dec = jnp.exp(_xdot(lf[:, hs], ones, TN, 0, lowp))
        s_scr[h] = sp * dec + _mm(kend[:, hs], vv[:, hs], lowp, TN)
        o_scr[:, hs] = oh * lax.rsqrt(jnp.mean(oh * oh, -1, keepdims=True) + EPS)
        yield

    y_ref[...] = o_scr[...] * nw_ref[...] * _silu(g_ref[...])


def _hgrn(proj, s_state, layer, w, consts, *, b, q, nc, valid, lowp, bs):
    nlev = int(math.log2(q))
    kern = functools.partial(_hgrn_kernel, q=q, nc=nc, valid=valid, lowp=lowp, nlev=nlev, bs=bs)
    return pl.pallas_call(
        kern,
        grid=(b // bs, nc),
        in_specs=[
            _col_spec(bs, q, COL_D_F, BRANCH_W),
            _col_spec(bs, q, COL_D_I, BRANCH_W),
            _col_spec(bs, q, COL_D_Q, BRANCH_W),
            _col_spec(bs, q, COL_D_G, BRANCH_W),
            _state_in_spec(layer, bs, (HG_HEADS, HG_HEADDIM, HG_HEADDIM)),
            _const_spec((1, BRANCH_W)),
            _const_spec((1, BRANCH_W)),
            _const_spec((q, q)),
        ],
        out_specs=[
            _y_spec(bs, q),
            _state_out_spec(bs, (HG_HEADS, HG_HEADDIM, HG_HEADDIM)),
        ],
        out_shape=[
            jax.ShapeDtypeStruct((b, nc * q, BRANCH_W), F32),
            jax.ShapeDtypeStruct((b, HG_HEADS, HG_HEADDIM, HG_HEADDIM), F32),
        ],
        scratch_shapes=[
            pltpu.VMEM((bs, HG_HEADS, HG_HEADDIM, HG_HEADDIM), F32),
            pltpu.VMEM((bs, q, BRANCH_W), F32),
        ],
        compiler_params=_MIXER_PARAMS,
        name="hgrn",
    )(proj, proj, proj, proj, s_state, w['hg_lb'], w['hg_norm_w'], consts['tril'])


def _chunk_consts(q):
    r = np.arange(q)
    tril = (r[:, None] >= r[None, :]).astype(np.float32)
    expand = np.zeros((128, BRANCH_W), np.float32)
    for h in range(SSD_HEADS):
        expand[h, h * SSD_HEADDIM:(h + 1) * SSD_HEADDIM] = 1.0
    return {'tril': jnp.asarray(tril), 'triu': jnp.asarray(tril.T), 'expand': jnp.asarray(expand)}


def _cmul(ar, ai, br, bi):
    return ar * br - ai * bi, ar * bi + ai * br


def _layer_params(l, p, hg_lb):
    w_in = p['w_in'][l]
    small = jnp.concatenate([w_in[:, 1280:1288], w_in[:, 3848:3856]], axis=1)
    w_perm = jnp.concatenate([
        w_in[:, 5904:10000],
        w_in[:, 0:512],
        w_in[:, 512:1024],
        w_in[:, 1288:2312],
        w_in[:, 2312:3848],
        w_in[:, 3856:5904],
        w_in[:, 1024:1280],
        small,
        jnp.zeros((D_MODEL, N_PROJ - COL_SMALL - N_SMALL), F32),
    ], axis=1).astype(BF16)

    def lane_pad(v, off):
        return jnp.zeros((1, 128), F32).at[0, off:off + v.shape[0]].set(v)

    bias_col = (lane_pad(p['ssd_dt_bias'][l], SMALL_DT) + lane_pad(p['ml_i_bias'][l], SMALL_I)
                + lane_pad(p['ml_f_bias'][l], SMALL_F))
    alog_col = lane_pad(p['ssd_A_log'][l], SMALL_DT)

    dt = jnp.exp(p['s5_log_dt'][l])[:, None]
    lr = p['s5_A_re'][l]
    li = p['s5_A_im'][l]
    mag = jnp.exp(lr * dt)
    abr, abi = mag * jnp.cos(li * dt), mag * jnp.sin(li * dt)
    den = lr * lr + li * li
    cr = ((abr - 1.0) * lr + abi * li) / den
    ci = (abi * lr - (abr - 1.0) * li) / den
    bbr, bbi = _cmul(cr[..., None], ci[..., None], p['s5_B_re'][l], p['s5_B_im'][l])
    eye8 = jnp.eye(8, dtype=F32)

    def pack_b(bb):
        return jnp.einsum('jgpc,gh->jgchp', bb.reshape(4, 8, S5_STATE, S5_GROUP), eye8).reshape(4, 128, 512)

    def pack_c(cc):
        return jnp.einsum('jgcp,gh->jgphc', cc.reshape(4, 8, S5_GROUP, S5_STATE), eye8).reshape(4, 512, 128)

    return {
        'norm_w': p['norm_w'][l].reshape(1, D_MODEL),
        'w_in': w_perm,
        'w_small_t': small.T.astype(BF16),
        'bias_col': bias_col, 'bias_row': bias_col[0, :N_SMALL].reshape(N_SMALL, 1),
        'alog_col': alog_col, 'alog_row': alog_col[0, :N_SMALL].reshape(N_SMALL, 1),
        'ssd_conv_w': p['ssd_conv_w'][l], 'ssd_conv_b': p['ssd_conv_b'][l].reshape(1, SSD_CONV_CH),
        'ssd_D_exp': jnp.repeat(p['ssd_D'][l], SSD_HEADDIM).reshape(1, BRANCH_W),
        'ssd_norm_w': p['ssd_norm_w'][l].reshape(1, BRANCH_W),
        's5_wbr': pack_b(bbr).astype(BF16), 's5_wbi': pack_b(bbi).astype(BF16),
        's5_wcr': pack_c(p['s5_C_re'][l]).astype(BF16), 's5_wci': pack_c(p['s5_C_im'][l]).astype(BF16),
        's5_ar': jnp.broadcast_to(abr.reshape(1, S5_LANES), (8, S5_LANES)),
        's5_ai': jnp.broadcast_to(abi.reshape(1, S5_LANES), (8, S5_LANES)),
        's5_D': p['s5_D'][l].reshape(1, BRANCH_W), 's5_glu_w': p['s5_glu_w'][l].astype(BF16),
        'ml_conv_w': p['ml_conv_w'][l], 'ml_conv_b': p['ml_conv_b'][l].reshape(1, BRANCH_W),
        'ml_wq': p['ml_wq'][l], 'ml_wk': p['ml_wk'][l], 'ml_wv': p['ml_wv'][l],
        'ml_norm_w': p['ml_norm_w'][l].reshape(1, BRANCH_W), 'ml_skip': p['ml_skip'][l].reshape(1, BRANCH_W),
        'hg_lb': hg_lb[l].reshape(1, BRANCH_W), 'hg_norm_w': p['hg_norm_w'][l].reshape(1, BRANCH_W),
        'w_branch': p['w_branch'][l].astype(BF16), 'w_out': p['w_out'][l].astype(BF16),
    }


_MATMUL_WEIGHTS = ('ml_wq', 'ml_wk', 'ml_wv')
S5_CHUNK = 32
SEQS_PER_STEP = {'ssd': 4, 'mlstm': 2, 'hgrn': 4}
SEQS_PER_STEP_PADDED = {'ssd': 8, 'mlstm': 8, 'hgrn': 8}


def _run_group(x3, states, layers, fnw, *, q, valid):
    b, lp, _ = x3.shape
    nc = lp // q
    t = b * lp
    lowp = q >= 16
    consts = _chunk_consts(q)
    ssd_conv, ssd_h, s5_re, s5_im, ml_conv, ml_c, ml_n, ml_m, hg_s = states
    s5_re = s5_re.reshape(DEPTH, b, S5_LANES)
    s5_im = s5_im.reshape(DEPTH, b, S5_LANES)
    if valid < q:
        s5_kw = dict(b=b, lp=lp, q=q, valid=valid, bb=min(b, 128))
    else:
        qs = math.gcd(lp, S5_CHUNK)
        s5_kw = dict(b=b, lp=lp, q=qs, valid=qs, bb=8)
    ml_m = jnp.broadcast_to(jnp.pad(ml_m, ((0, 0), (0, 0), (0, 8 - ML_HEADS)))[..., None], (DEPTH, b, 8, 128))
    bs = {k: math.gcd(b, v) for k, v in (SEQS_PER_STEP_PADDED if valid < q else SEQS_PER_STEP).items()}
    kw = dict(b=b, q=q, nc=nc, valid=valid, lowp=lowp)
    x = x3.reshape(t, D_MODEL)
    new = []
    for l, w in enumerate(layers):
        if lowp:
            w = dict(w, **{k: w[k].astype(BF16) for k in _MATMUL_WEIGHTS})
        proj, small_t = _inproj(x, w['norm_w'], w['w_in'], w['w_small_t'])
        proj3 = proj.reshape(b, lp, N_PROJ)
        srow = small_t.reshape(N_SMALL, b, lp).transpose(1, 0, 2)
        ya, n_ssd_conv, n_ssd_h = _ssd(proj3, srow, ssd_conv, ssd_h, l, w, consts, bs=bs['ssd'], **kw)
        yb, n_re, n_im = _s5(proj, s5_re, s5_im, l, w, **s5_kw)
        yc, n_ml_conv, n_c, n_n, n_m = _mlstm(proj3, srow, ml_conv, ml_c, ml_n, ml_m, l, w, consts,
                                              bs=bs['mlstm'], **kw)
        yd, n_s = _hgrn(proj3, hg_s, l, w, consts, bs=bs['hgrn'], **kw)
        ya, yc, yd = (y.reshape(t, BRANCH_W) for y in (ya, yc, yd))
        x = _merge(proj, (ya, yb, yc, yd), x, w['w_branch'], w['w_out'], fnw, final=(l == DEPTH - 1))
        new.append((n_ssd_conv, n_ssd_h,
                    n_re.reshape(b, S5_GROUPS, S5_STATE), n_im.reshape(b, S5_GROUPS, S5_STATE),
                    n_ml_conv, n_c, n_n, n_m[:, :ML_HEADS, 0], n_s))
    new_states = tuple(jnp.stack(ss, axis=0) for ss in zip(*new))
    return x.reshape(b, lp, D_MODEL), new_states


SAMPLE_PAD = 8


def kernel(x_prompt, x_sample, state_ssd_conv, state_ssd, state_s5_re, state_s5_im, state_mlstm_conv, state_mlstm_C, state_mlstm_n, state_mlstm_m, state_hgrn, norm_w, w_in, ssd_conv_w, ssd_conv_b, ssd_dt_bias, ssd_A_log, ssd_D, ssd_norm_w, s5_A_re, s5_A_im, s5_B_re, s5_B_im, s5_C_re, s5_C_im, s5_D, s5_log_dt, s5_glu_w, ml_conv_w, ml_conv_b, ml_wq, ml_wk, ml_wv, ml_i_bias, ml_f_bias, ml_norm_w, ml_skip, hg_lb_logits, hg_norm_w, w_branch, w_out, final_norm_w):
    p = {'norm_w': norm_w, 'w_in': w_in,
         'ssd_conv_w': ssd_conv_w, 'ssd_conv_b': ssd_conv_b, 'ssd_dt_bias': ssd_dt_bias,
         'ssd_A_log': ssd_A_log, 'ssd_D': ssd_D, 'ssd_norm_w': ssd_norm_w,
         's5_A_re': s5_A_re, 's5_A_im': s5_A_im, 's5_B_re': s5_B_re, 's5_B_im': s5_B_im,
         's5_C_re': s5_C_re, 's5_C_im': s5_C_im, 's5_D': s5_D, 's5_log_dt': s5_log_dt, 's5_glu_w': s5_glu_w,
         'ml_conv_w': ml_conv_w, 'ml_conv_b': ml_conv_b, 'ml_wq': ml_wq, 'ml_wk': ml_wk, 'ml_wv': ml_wv,
         'ml_i_bias': ml_i_bias, 'ml_f_bias': ml_f_bias, 'ml_norm_w': ml_norm_w, 'ml_skip': ml_skip,
         'hg_norm_w': hg_norm_w, 'w_branch': w_branch, 'w_out': w_out}
    lb_cum = jnp.cumsum(jax.nn.softmax(hg_lb_logits, axis=0), axis=0)
    hg_lb = lb_cum - lb_cum[0]
    layers = [_layer_params(l, p, hg_lb) for l in range(DEPTH)]
    fnw = final_norm_w.reshape(1, D_MODEL)

    sample_states = (state_ssd_conv, state_ssd, state_s5_re, state_s5_im, state_mlstm_conv,
                     state_mlstm_C, state_mlstm_n, state_mlstm_m, state_hgrn)
    bp, lp_, _ = x_prompt.shape
    prompt_states = tuple(jnp.zeros((DEPTH, bp) + s.shape[2:], F32) for s in sample_states)
    q_prompt = math.gcd(lp_, 128)
    y_prompt, new_p = _run_group(x_prompt, prompt_states, layers, fnw, q=q_prompt, valid=q_prompt)

    ls = x_sample.shape[1]
    xs_pad = jnp.pad(x_sample, ((0, 0), (0, SAMPLE_PAD - ls), (0, 0)))
    y_sample, new_s = _run_group(xs_pad, sample_states, layers, fnw, q=SAMPLE_PAD, valid=ls)
    y_sample = y_sample[:, :ls]

    out = [y_prompt, y_sample]
    for ps, ss in zip(new_p, new_s):
        out += [ps, ss]
    return tuple(out)
```

```python
import functools
import math

import numpy as np
import jax
import jax.numpy as jnp
from jax import lax
from jax.experimental import pallas as pl
from jax.experimental.pallas import tpu as pltpu

F32 = jnp.float32
BF16 = jnp.bfloat16

D_MODEL = 1024
DEPTH = 2
BRANCH_W = 512
CONV_W = 4
EPS = 1e-6
SSD_HEADS = 8
SSD_HEADDIM = 64
SSD_STATE = 64
SSD_GROUPS = 2
SSD_CONV_CH = 768
S5_GROUPS = 32
S5_GROUP = 16
S5_STATE = 64
S5_LANES = S5_GROUPS * S5_STATE
ML_HEADS = 4
ML_HEADDIM = 128
HG_HEADS = 4
HG_HEADDIM = 128
N_BRANCH = 4

NEG = -1e30

N_PROJ = 9216
COL_MERGE = 0
COL_A_Z = 4096
COL_A_X = 4608
COL_B_U = 5120
COL_B_GATE = 5632
COL_C_X = 6144
COL_C_Z = 6656
COL_C_O = 7168
COL_D_I = 7680
COL_D_Q = 8192
COL_D_G = 8704
N_PROJ32 = 896
COL32_D_F = 0
COL32_A_BC = 512
COL32_SMALL = 768
N_SMALL = 16
SMALL_DT = 0
SMALL_I = 8
SMALL_F = 12

NN = (((1,), (0,)), ((), ()))
NT = (((1,), (1,)), ((), ()))
TN = (((0,), (0,)), ((), ()))


def _mm(a, b, lowp, dims=NN):
    if lowp:
        a = a.astype(BF16)
        b = b.astype(BF16)
    return lax.dot_general(a, b, dims, preferred_element_type=F32)


def _split3(a):
    hi = a.astype(BF16)
    r = a - hi.astype(F32)
    mid = r.astype(BF16)
    lo = (r - mid.astype(F32)).astype(BF16)
    return hi, mid, lo


def _xdot(a, b, dims, split, lowp):
    if not lowp:
        return lax.dot_general(a, b, dims, precision=lax.Precision.HIGHEST,
                               preferred_element_type=F32)
    if split == 0:
        other = b.astype(BF16)
        return sum(lax.dot_general(p, other, dims, preferred_element_type=F32) for p in _split3(a))
    other = a.astype(BF16)
    return sum(lax.dot_general(other, p, dims, preferred_element_type=F32) for p in _split3(b))


def _sigmoid(x):
    return jax.nn.sigmoid(x)


def _silu(x):
    return x * jax.nn.sigmoid(x)


def _softplus(x):
    return jnp.maximum(x, 0.0) + jnp.log1p(jnp.exp(-jnp.abs(x)))


def _conv_chunk(xp_scr, cw_ref, cb_ref, q, valid):
    cw = cw_ref[...]
    conv = (cb_ref[...]
            + cw[0:1] * xp_scr[5:5 + q, :]
            + cw[1:2] * xp_scr[6:6 + q, :]
            + cw[2:3] * xp_scr[7:7 + q, :]
            + cw[3:4] * xp_scr[8:8 + q, :])
    carry = xp_scr[5 + valid:8 + valid, :]
    xp_scr[5:8, :] = carry
    return conv, carry


def _inproj_kernel(x_ref, nw_ref, w_ref, w32_ref, wst_ref, o_ref, o32_ref, st_ref, xn_scr):
    @pl.when(pl.program_id(1) == 0)
    def _():
        x = x_ref[...]
        xn = x * lax.rsqrt(jnp.mean(x * x, -1, keepdims=True) + EPS) * nw_ref[...]
        xb = xn.astype(BF16)
        xn_scr[...] = xb
        o32_ref[...] = jnp.dot(xb, w32_ref[...], preferred_element_type=F32)
        st_ref[...] = lax.dot_general(wst_ref[...], xb, NT, preferred_element_type=F32)

    o_ref[...] = jnp.dot(xn_scr[...], w_ref[...], preferred_element_type=F32).astype(BF16)


def _inproj(x, nw, w, w32, wst):
    t = x.shape[0]
    tm = min(1024, t)
    tn = 1024
    return pl.pallas_call(
        _inproj_kernel,
        grid=(t // tm, N_PROJ // tn),
        in_specs=[
            pl.BlockSpec((tm, D_MODEL), lambda i, j: (i, 0)),
            pl.BlockSpec((1, D_MODEL), lambda i, j: (0, 0)),
            pl.BlockSpec((D_MODEL, tn), lambda i, j: (0, j)),
            pl.BlockSpec((D_MODEL, N_PROJ32), lambda i, j: (0, 0)),
            pl.BlockSpec((N_SMALL, D_MODEL), lambda i, j: (0, 0)),
        ],
        out_specs=[
            pl.BlockSpec((tm, tn), lambda i, j: (i, j)),
            pl.BlockSpec((tm, N_PROJ32), lambda i, j: (i, 0)),
            pl.BlockSpec((N_SMALL, tm), lambda i, j: (0, i)),
        ],
        out_shape=[
            jax.ShapeDtypeStruct((t, N_PROJ), BF16),
            jax.ShapeDtypeStruct((t, N_PROJ32), F32),
            jax.ShapeDtypeStruct((N_SMALL, t), F32),
        ],
        scratch_shapes=[pltpu.VMEM((tm, D_MODEL), BF16)],
        compiler_params=pltpu.CompilerParams(
            dimension_semantics=("arbitrary", "arbitrary"),
            vmem_limit_bytes=48 * 1024 * 1024),
        name="inproj",
    )(x, nw, w, w32, wst)


def _merge_kernel(g_ref, ya_ref, yb_ref, yc_ref, yd_ref, x_ref, wb_ref, wo_ref, fnw_ref, o_ref, *, final):
    acc = None
    for i, y_ref in enumerate((ya_ref, yb_ref, yc_ref, yd_ref)):
        gate = _sigmoid(g_ref[:, i * D_MODEL:(i + 1) * D_MODEL].astype(F32))
        term = gate * jnp.dot(y_ref[...], wb_ref[i], preferred_element_type=F32)
        acc = term if acc is None else acc + term
    out = x_ref[...] + jnp.dot(acc.astype(BF16), wo_ref[...], preferred_element_type=F32)
    if final:
        out = out * lax.rsqrt(jnp.mean(out * out, -1, keepdims=True) + EPS) * fnw_ref[...]
    o_ref[...] = out


def _merge(proj, ys, x, wb, wo, fnw, final):
    t = x.shape[0]
    tm = min(256, t)
    yspec = pl.BlockSpec((tm, BRANCH_W), lambda i: (i, 0))
    return pl.pallas_call(
        functools.partial(_merge_kernel, final=final),
        grid=(t // tm,),
        in_specs=[
            pl.BlockSpec((tm, N_BRANCH * D_MODEL), lambda i: (i, COL_MERGE // (N_BRANCH * D_MODEL))),
            yspec, yspec, yspec, yspec,
            pl.BlockSpec((tm, D_MODEL), lambda i: (i, 0)),
            pl.BlockSpec((N_BRANCH, BRANCH_W, D_MODEL), lambda i: (0, 0, 0)),
            pl.BlockSpec((D_MODEL, D_MODEL), lambda i: (0, 0)),
            pl.BlockSpec((1, D_MODEL), lambda i: (0, 0)),
        ],
        out_specs=pl.BlockSpec((tm, D_MODEL), lambda i: (i, 0)),
        out_shape=jax.ShapeDtypeStruct((t, D_MODEL), F32),
        compiler_params=pltpu.CompilerParams(
            dimension_semantics=("arbitrary",),
            vmem_limit_bytes=48 * 1024 * 1024),
        name="merge",
    )(proj, *ys, x, wb, wo, fnw)


def _col_spec(bs, q, col, width):
    return pl.BlockSpec((bs, q, width), lambda b, c: (b, c, col // width))


def _row_spec(bs, q):
    return pl.BlockSpec((bs, N_SMALL, q), lambda b, c: (b, 0, c))


def _const_spec(shape):
    nd = len(shape)
    return pl.BlockSpec(shape, lambda b, c: (0,) * nd)


def _state_in_spec(layer, bs, tail):
    nd = len(tail)
    return pl.BlockSpec((1, bs) + tail, lambda b, c: (layer, b) + (0,) * nd)


def _state_out_spec(bs, tail):
    nd = len(tail)
    return pl.BlockSpec((bs,) + tail, lambda b, c: (b,) + (0,) * nd)


def _y_spec(bs, q):
    return pl.BlockSpec((bs, q, BRANCH_W), lambda b, c: (b, c, 0))


_DONE = object()


def _per_seq(kern_init, kern_main, kern_final, seq_refs, const_refs, bs, nc):
    c = pl.program_id(1)
    views = [[r.at[s] for r in seq_refs] for s in range(bs)]

    @pl.when(c == 0)
    def _():
        for s in range(bs):
            kern_init(views[s])

    active = [kern_main(views[s], const_refs) for s in range(bs)]
    while active:
        active = [g for g in active if next(g, _DONE) is not _DONE]

    @pl.when(c == nc - 1)
    def _():
        for s in range(bs):
            kern_final(views[s])


_MIXER_PARAMS = pltpu.CompilerParams(
    dimension_semantics=("arbitrary", "arbitrary"),
    vmem_limit_bytes=48 * 1024 * 1024)


def _ssd_kernel(z_ref, x_ref, bc_ref, sc_ref, sr_ref, conv0_ref, h0_ref,
                cw_ref, cb_ref, bcol_ref, brow_ref, alc_ref, alr_ref, dexp_ref, nw_ref,
                e_ref, tl_ref, tu_ref,
                y_ref, convo_ref, ho_ref,
                xp_scr, h_scr, y_scr, *, q, nc, valid, lowp, bs):
    def init(v):
        v[9][5:8, :] = v[5][...]
        v[10][...] = v[6][...]

    def final(v):
        v[8][...] = v[10][...]

    _per_seq(init, functools.partial(_ssd_main, q=q, valid=valid, lowp=lowp), final,
             (z_ref, x_ref, bc_ref, sc_ref, sr_ref, conv0_ref.at[0], h0_ref.at[0],
              y_ref, ho_ref, xp_scr, h_scr, y_scr, convo_ref),
             (cw_ref, cb_ref, bcol_ref, brow_ref, alc_ref, alr_ref, dexp_ref, nw_ref, e_ref, tl_ref, tu_ref),
             bs, nc)


def _ssd_main(v, k, *, q, valid, lowp):
    z_ref, x_ref, bc_ref, sc_ref, sr_ref, _, _, y_ref, _, xp_scr, h_scr, y_scr, convo_ref = v
    cw_ref, cb_ref, bcol_ref, brow_ref, alc_ref, alr_ref, dexp_ref, nw_ref, e_ref, tl_ref, tu_ref = k
    xp_scr[8:8 + q, 0:BRANCH_W] = x_ref[...].astype(F32)
    xp_scr[8:8 + q, BRANCH_W:SSD_CONV_CH] = bc_ref[...]
    conv, carry = _conv_chunk(xp_scr, cw_ref, cb_ref, q, valid)
    convo_ref[...] = carry
    yield
    xbc = _silu(conv)
    xs = xbc[:, 0:BRANCH_W]
    bm = xbc[:, BRANCH_W:BRANCH_W + 128]
    cm = xbc[:, BRANCH_W + 128:BRANCH_W + 256]

    dt_c = _softplus(sc_ref[...] + bcol_ref[...])
    a_c = dt_c * (-jnp.exp(alc_ref[...]))
    dt_r = _softplus(sr_ref[...] + brow_ref[...])
    a_r = dt_r * (-jnp.exp(alr_ref[...]))
    if valid < q:
        tcol = lax.broadcasted_iota(jnp.int32, (q, 1), 0)
        trow = lax.broadcasted_iota(jnp.int32, (1, q), 1)
        a_c = jnp.where(tcol < valid, a_c, 0.0)
        dt_c = jnp.where(tcol < valid, dt_c, 0.0)
        a_r = jnp.where(trow < valid, a_r, 0.0)

    tl = tl_ref[...]
    acum_c = _xdot(tl, a_c, NN, 1, lowp)
    acum_r = _xdot(a_r, tu_ref[...], NN, 0, lowp)
    e = e_ref[...]
    dt_e = _xdot(dt_c, e, NN, 0, lowp)
    acum_e = _xdot(acum_c, e, NN, 0, lowp)
    xdt = xs * dt_e
    xend = xdt * jnp.exp(acum_e[q - 1:q, :] - acum_e)
    eac = jnp.exp(acum_e)
    dec_last = jnp.exp(acum_c[q - 1:q, :])
    yield

    tri = (lax.broadcasted_iota(jnp.int32, (q, q), 0) >= lax.broadcasted_iota(jnp.int32, (q, q), 1))
    rpg = SSD_HEADS // SSD_GROUPS
    for g in range(SSD_GROUPS):
        bg = bm[:, g * SSD_STATE:(g + 1) * SSD_STATE]
        cg = cm[:, g * SSD_STATE:(g + 1) * SSD_STATE]
        cb = _mm(cg, bg, lowp, NT)
        for r in range(rpg):
            h = g * rpg + r
            hs = slice(h * SSD_HEADDIM, (h + 1) * SSD_HEADDIM)
            seg = acum_c[:, h:h + 1] - acum_r[h:h + 1, :]
            m = cb * jnp.exp(jnp.where(tri, seg, NEG))
            hp = h_scr[h]
            yh = _mm(m, xdt[:, hs], lowp) + _mm(cg, hp, lowp, NT) * eac[:, hs]
            y_scr[:, hs] = yh
            h_scr[h] = hp * dec_last[:, h:h + 1] + _mm(xend[:, hs], bg, lowp, TN)
            yield

    y = y_scr[...] + xs * dexp_ref[...]
    y = y * _silu(z_ref[...].astype(F32))
    y = y * lax.rsqrt(jnp.mean(y * y, -1, keepdims=True) + EPS) * nw_ref[...]
    y_ref[...] = y.astype(BF16)


def _ssd(proj, proj32, srow, conv_state, h_state, layer, w, consts, *, b, q, nc, valid, lowp, bs):
    kern = functools.partial(_ssd_kernel, q=q, nc=nc, valid=valid, lowp=lowp, bs=bs)
    return pl.pallas_call(
        kern,
        grid=(b // bs, nc),
        in_specs=[
            _col_spec(bs, q, COL_A_Z, BRANCH_W),
            _col_spec(bs, q, COL_A_X, BRANCH_W),
            _col_spec(bs, q, COL32_A_BC, 256),
            _col_spec(bs, q, COL32_SMALL, 128),
            _row_spec(bs, q),
            _state_in_spec(layer, bs, (CONV_W - 1, SSD_CONV_CH)),
            _state_in_spec(layer, bs, (SSD_HEADS, SSD_HEADDIM, SSD_STATE)),
            _const_spec((CONV_W, SSD_CONV_CH)),
            _const_spec((1, SSD_CONV_CH)),
            _const_spec((1, 128)),
            _const_spec((N_SMALL, 1)),
            _const_spec((1, 128)),
            _const_spec((N_SMALL, 1)),
            _const_spec((1, BRANCH_W)),
            _const_spec((1, BRANCH_W)),
            _const_spec((128, BRANCH_W)),
            _const_spec((q, q)),
            _const_spec((q, q)),
        ],
        out_specs=[
            _y_spec(bs, q),
            _state_out_spec(bs, (CONV_W - 1, SSD_CONV_CH)),
            _state_out_spec(bs, (SSD_HEADS, SSD_HEADDIM, SSD_STATE)),
        ],
        out_shape=[
            jax.ShapeDtypeStruct((b, nc * q, BRANCH_W), BF16),
            jax.ShapeDtypeStruct((b, CONV_W - 1, SSD_CONV_CH), F32),
            jax.ShapeDtypeStruct((b, SSD_HEADS, SSD_HEADDIM, SSD_STATE), F32),
        ],
        scratch_shapes=[
            pltpu.VMEM((bs, q + 8, SSD_CONV_CH), F32),
            pltpu.VMEM((bs, SSD_HEADS, SSD_HEADDIM, SSD_STATE), F32),
            pltpu.VMEM((bs, q, BRANCH_W), F32),
        ],
        compiler_params=_MIXER_PARAMS,
        name="ssd",
    )(proj, proj, proj32, proj32, srow, conv_state, h_state,
      w['ssd_conv_w'], w['ssd_conv_b'], w['bias_col'], w['bias_row'], w['alog_col'], w['alog_row'],
      w['ssd_D_exp'], w['ssd_norm_w'], consts['expand'], consts['tril'], consts['triu'])


def _s5_kernel(u_ref, gate_ref, hr0_ref, hi0_ref, perm_ref, permt_ref, ar_ref, ai_ref,
               wbr_ref, wbi_ref, wcr_ref, wci_ref, d_ref, glu_ref,
               y_ref, hro_ref, hio_ref,
               hr_scr, hi_scr, cr_scr, ci_scr, *, q, nc, ngrp):
    c = pl.program_id(1)

    @pl.when(c == 0)
    def _():
        cr_scr[...] = hr0_ref[0]
        ci_scr[...] = hi0_ref[0]

    rows_in = perm_ref.shape[1]
    u = u_ref[...].reshape(rows_in, BRANCH_W)
    u_tm = jnp.dot(perm_ref[...], u, preferred_element_type=F32).astype(BF16)
    nblk = S5_LANES // 512
    for j in range(nblk):
        uj = u_tm[:, j * 128:(j + 1) * 128]
        hr_scr[:, j * 512:(j + 1) * 512] = jnp.dot(uj, wbr_ref[j], preferred_element_type=F32)
        hi_scr[:, j * 512:(j + 1) * 512] = jnp.dot(uj, wbi_ref[j], preferred_element_type=F32)

    half = S5_LANES // 2
    for hf in range(2):
        sl = slice(hf * half, (hf + 1) * half)
        ar = ar_ref[:, sl]
        ai = ai_ref[:, sl]

        def grp_body(g, carry, sl=sl, ar=ar, ai=ai):
            s0 = pl.multiple_of(g * 8, 8)

            def t_body(t, h):
                r0 = pl.multiple_of(g * (8 * q) + t * 8, 8)
                hr, hi = h
                nr = ar * hr - ai * hi + hr_scr[pl.ds(r0, 8), sl]
                ni = ar * hi + ai * hr + hi_scr[pl.ds(r0, 8), sl]
                hr_scr[pl.ds(r0, 8), sl] = nr
                hi_scr[pl.ds(r0, 8), sl] = ni
                return nr, ni

            h = lax.fori_loop(0, q, t_body, (cr_scr[pl.ds(s0, 8), sl], ci_scr[pl.ds(s0, 8), sl]),
                              unroll=min(q, 4))
            cr_scr[pl.ds(s0, 8), sl] = h[0]
            ci_scr[pl.ds(s0, 8), sl] = h[1]
            return carry

        lax.fori_loop(0, ngrp, grp_body, 0)

    parts = []
    for j in range(nblk):
        sl = slice(j * 512, (j + 1) * 512)
        parts.append(jnp.dot(hr_scr[:, sl].astype(BF16), wcr_ref[j], preferred_element_type=F32)
                     - jnp.dot(hi_scr[:, sl].astype(BF16), wci_ref[j], preferred_element_type=F32))
    y_tm = jnp.concatenate(parts, axis=1)
    y = _xdot(permt_ref[...], y_tm, NN, 1, True) + u.astype(F32) * d_ref[...]
    g = jax.nn.gelu(y)
    y = g * _sigmoid(jnp.dot(g.astype(BF16), glu_ref[...], preferred_element_type=F32))
    y = y * _silu(gate_ref[...].reshape(rows_in, BRANCH_W).astype(F32))
    y_ref[...] = y.reshape(y_ref.shape).astype(BF16)

    @pl.when(c == nc - 1)
    def _():
        hro_ref[0] = cr_scr[...]
        hio_ref[0] = ci_scr[...]


def _s5_perm(bb, lp, valid):
    perm = np.zeros((bb * valid, bb * lp), np.float32)
    for g in range(bb // 8):
        for t in range(valid):
            for s in range(8):
                perm[g * 8 * valid + t * 8 + s, (g * 8 + s) * lp + t] = 1.0
    return jnp.asarray(perm, BF16), jnp.asarray(perm.T, BF16)


def _s5(proj, hr_state, hi_state, layer, w, *, b, lp, q, valid, bb):
    nc = lp // q
    perm, permt = _s5_perm(bb, q, valid)
    rows = bb * valid
    if nc == 1:
        def col_spec(col):
            return pl.BlockSpec((bb * q, BRANCH_W), lambda i, c: (i, col // BRANCH_W))
        src = proj
        y_spec = pl.BlockSpec((bb * q, BRANCH_W), lambda i, c: (i, 0))
        y_shape = jax.ShapeDtypeStruct((b * lp, BRANCH_W), BF16)
    else:
        def col_spec(col):
            return pl.BlockSpec((bb, q, BRANCH_W), lambda i, c: (i, c, col // BRANCH_W))
        src = proj.reshape(b, lp, N_PROJ)
        y_spec = pl.BlockSpec((bb, q, BRANCH_W), lambda i, c: (i, c, 0))
        y_shape = jax.ShapeDtypeStruct((b, lp, BRANCH_W), BF16)
    st_in = pl.BlockSpec((1, bb, S5_LANES), lambda i, c: (layer, i, 0))
    st_out = pl.BlockSpec((1, bb, S5_LANES), lambda i, c: (0, i, 0))
    kern = functools.partial(_s5_kernel, q=valid, nc=nc, ngrp=bb // 8)
    y, hr, hi = pl.pallas_call(
        kern,
        grid=(b // bb, nc),
        in_specs=[
            col_spec(COL_B_U),
            col_spec(COL_B_GATE),
            st_in, st_in,
            _const_spec(perm.shape),
            _const_spec(permt.shape),
            _const_spec((8, S5_LANES)),
            _const_spec((8, S5_LANES)),
            _const_spec((4, 128, 512)),
            _const_spec((4, 128, 512)),
            _const_spec((4, 512, 128)),
            _const_spec((4, 512, 128)),
            _const_spec((1, BRANCH_W)),
            _const_spec((BRANCH_W, BRANCH_W)),
        ],
        out_specs=[y_spec, st_out, st_out],
        out_shape=[
            y_shape,
            jax.ShapeDtypeStruct((1, b, S5_LANES), F32),
            jax.ShapeDtypeStruct((1, b, S5_LANES), F32),
        ],
        scratch_shapes=[
            pltpu.VMEM((rows, S5_LANES), F32),
            pltpu.VMEM((rows, S5_LANES), F32),
            pltpu.VMEM((bb, S5_LANES), F32),
            pltpu.VMEM((bb, S5_LANES), F32),
        ],
        compiler_params=_MIXER_PARAMS,
        name="s5",
    )(src, src, hr_state, hi_state, perm, permt, w['s5_ar'], w['s5_ai'],
      w['s5_wbr'], w['s5_wbi'], w['s5_wcr'], w['s5_wci'], w['s5_D'], w['s5_glu_w'])
    return y.reshape(b * lp, BRANCH_W), hr[0], hi[0]


def _mlstm_kernel(x_ref, z_ref, o_ref, sc_ref, sr_ref, conv0_ref, c0_ref, n0_ref, m0_ref,
                  cw_ref, cb_ref, wq_ref, wk_ref, wv_ref, bcol_ref, brow_ref, nw_ref, skip_ref,
                  tl_ref, tu_ref,
                  y_ref, convo_ref, co_ref, no_ref, mo_ref,
                  xp_scr, c_scr, n_scr, m_scr, h_scr, *, q, nc, valid, lowp, bs):
    def init(v):
        v[13][5:8, :] = v[5][...]
        v[14][...] = v[6][...]
        v[15][...] = v[7][...]
        v[16][...] = v[8][...]

    def final(v):
        v[10][...] = v[14][...]
        v[11][...] = v[15][...]
        v[12][...] = v[16][...]

    _per_seq(init, functools.partial(_mlstm_main, q=q, valid=valid, lowp=lowp), final,
             (x_ref, z_ref, o_ref, sc_ref, sr_ref, conv0_ref.at[0], c0_ref.at[0], n0_ref.at[0], m0_ref.at[0],
              y_ref, co_ref, no_ref, mo_ref, xp_scr, c_scr, n_scr, m_scr, h_scr, convo_ref),
             (cw_ref, cb_ref, wq_ref, wk_ref, wv_ref, bcol_ref, brow_ref, nw_ref, skip_ref, tl_ref, tu_ref),
             bs, nc)


def _mlstm_main(v, k, *, q, valid, lowp):
    (x_ref, z_ref, o_ref, sc_ref, sr_ref, _, _, _, _, y_ref, _, _, _,
     xp_scr, c_scr, n_scr, m_scr, h_scr, convo_ref) = v
    cw_ref, cb_ref, wq_ref, wk_ref, wv_ref, bcol_ref, brow_ref, nw_ref, skip_ref, tl_ref, tu_ref = k
    x = x_ref[...].astype(F32)
    xp_scr[8:8 + q, :] = x
    conv, carry = _conv_chunk(xp_scr, cw_ref, cb_ref, q, valid)
    convo_ref[...] = carry
    yield
    xc = _silu(conv)

    pre_c = sc_ref[...] + bcol_ref[...]
    pre_r = sr_ref[...] + brow_ref[...]
    ig_c = pre_c
    lf_c = -_softplus(-pre_c)
    ig_r = pre_r
    lf_r = -_softplus(-pre_r)
    if valid < q:
        tcol = lax.broadcasted_iota(jnp.int32, (q, 1), 0)
        trow = lax.broadcasted_iota(jnp.int32, (1, q), 1)
        lf_c = jnp.where(tcol < valid, lf_c, 0.0)
        ig_c = jnp.where(tcol < valid, ig_c, NEG)
        lf_r = jnp.where(trow < valid, lf_r, 0.0)
        ig_r = jnp.where(trow < valid, ig_r, NEG)
    b_c = _xdot(tl_ref[...], lf_c, NN, 1, lowp)
    b_r = _xdot(lf_r, tu_ref[...], NN, 0, lowp)
    yield

    tri = (lax.broadcasted_iota(jnp.int32, (q, q), 0) >= lax.broadcasted_iota(jnp.int32, (q, q), 1))
    o_all = o_ref[...].astype(F32)
    for h in range(ML_HEADS):
        hs = slice(h * ML_HEADDIM, (h + 1) * ML_HEADDIM)
        xh = xc[:, hs]
        qh = _mm(xh, wq_ref[h], lowp)
        kh = _mm(xh, wk_ref[h], lowp) * (ML_HEADDIM ** -0.5)
        vh = _mm(x[:, hs], wv_ref[h], lowp)
        yield
        bc = b_c[:, SMALL_F + h:SMALL_F + h + 1]
        br = b_r[SMALL_F + h:SMALL_F + h + 1, :]
        ic = ig_c[:, SMALL_I + h:SMALL_I + h + 1]
        ir = ig_r[SMALL_I + h:SMALL_I + h + 1, :]
        mp = m_scr[h:h + 1, 0:1]
        dlog = jnp.where(tri, bc - br + ir, NEG)
        inter = bc + mp
        m_t = jnp.maximum(inter, jnp.max(dlog, axis=1, keepdims=True))
        wgt = jnp.exp(dlog - m_t)
        yield
        s = _mm(qh, kh, lowp, NT) * wgt
        scale = jnp.exp(inter - m_t)
        cp = c_scr[h]
        npv = n_scr[h:h + 1, :]
        num = _mm(s, vh, lowp) + scale * _mm(qh, cp, lowp)
        dot = jnp.sum(s, axis=1, keepdims=True) + scale * jnp.sum(qh * npv, axis=1, keepdims=True)
        hh = num / jnp.maximum(jnp.abs(dot), jnp.exp(-m_t))
        yield
        m_new = m_t[q - 1:q, :]
        b_last = bc[q - 1:q, :]
        w_end = jnp.exp(b_last - bc + ic - m_new)
        cs = jnp.exp(b_last + mp - m_new)
        kw = kh * w_end
        c_scr[h] = cs * cp + _mm(kw, vh, lowp, TN)
        n_scr[h:h + 1, :] = cs * npv + jnp.sum(kw, axis=0, keepdims=True)
        m_scr[h:h + 1, :] = jnp.broadcast_to(m_new, (1, 128))
        yield
        hh = hh * _sigmoid(o_all[:, hs])
        hc = hh - jnp.mean(hh, -1, keepdims=True)
        h_scr[:, hs] = hc * lax.rsqrt(jnp.mean(hc * hc, -1, keepdims=True) + EPS)
        yield

    y = h_scr[...] * nw_ref[...] + skip_ref[...] * xc
    y_ref[...] = (y * _silu(z_ref[...].astype(F32))).astype(BF16)


def _mlstm(proj, proj32, srow, conv_state, c_state, n_state, m_state, layer, w, consts,
           *, b, q, nc, valid, lowp, bs):
    kern = functools.partial(_mlstm_kernel, q=q, nc=nc, valid=valid, lowp=lowp, bs=bs)
    return pl.pallas_call(
        kern,
        grid=(b // bs, nc),
        in_specs=[
            _col_spec(bs, q, COL_C_X, BRANCH_W),
            _col_spec(bs, q, COL_C_Z, BRANCH_W),
            _col_spec(bs, q, COL_C_O, BRANCH_W),
            _col_spec(bs, q, COL32_SMALL, 128),
            _row_spec(bs, q),
            _state_in_spec(layer, bs, (CONV_W - 1, BRANCH_W)),
            _state_in_spec(layer, bs, (ML_HEADS, ML_HEADDIM, ML_HEADDIM)),
            _state_in_spec(layer, bs, (ML_HEADS, ML_HEADDIM)),
            _state_in_spec(layer, bs, (8, 128)),
            _const_spec((CONV_W, BRANCH_W)),
            _const_spec((1, BRANCH_W)),
            _const_spec((ML_HEADS, ML_HEADDIM, ML_HEADDIM)),
            _const_spec((ML_HEADS, ML_HEADDIM, ML_HEADDIM)),
            _const_spec((ML_HEADS, ML_HEADDIM, ML_HEADDIM)),
            _const_spec((1, 128)),
            _const_spec((N_SMALL, 1)),
            _const_spec((1, BRANCH_W)),
            _const_spec((1, BRANCH_W)),
            _const_spec((q, q)),
            _const_spec((q, q)),
        ],
        out_specs=[
            _y_spec(bs, q),
            _state_out_spec(bs, (CONV_W - 1, BRANCH_W)),
            _state_out_spec(bs, (ML_HEADS, ML_HEADDIM, ML_HEADDIM)),
            _state_out_spec(bs, (ML_HEADS, ML_HEADDIM)),
            _state_out_spec(bs, (8, 128)),
        ],
        out_shape=[
            jax.ShapeDtypeStruct((b, nc * q, BRANCH_W), BF16),
            jax.ShapeDtypeStruct((b, CONV_W - 1, BRANCH_W), F32),
            jax.ShapeDtypeStruct((b, ML_HEADS, ML_HEADDIM, ML_HEADDIM), F32),
            jax.ShapeDtypeStruct((b, ML_HEADS, ML_HEADDIM), F32),
            jax.ShapeDtypeStruct((b, 8, 128), F32),
        ],
        scratch_shapes=[
            pltpu.VMEM((bs, q + 8, BRANCH_W), F32),
            pltpu.VMEM((bs, ML_HEADS, ML_HEADDIM, ML_HEADDIM), F32),
            pltpu.VMEM((bs, ML_HEADS, ML_HEADDIM), F32),
            pltpu.VMEM((bs, 8, 128), F32),
            pltpu.VMEM((bs, q, BRANCH_W), F32),
        ],
        compiler_params=_MIXER_PARAMS,
        name="mlstm",
    )(proj, proj, proj, proj32, srow, conv_state, c_state, n_state, m_state,
      w['ml_conv_w'], w['ml_conv_b'], w['ml_wq'], w['ml_wk'], w['ml_wv'], w['bias_col'], w['bias_row'],
      w['ml_norm_w'], w['ml_skip'], consts['tril'], consts['triu'])


def _hgrn_kernel(f_ref, i_ref, q_ref, g_ref, s0_ref, lb_ref, nw_ref, tl_ref,
                 y_ref, so_ref,
                 s_scr, o_scr, *, q, nc, valid, lowp, nlev, bs):
    def init(v):
        v[7][...] = v[4][...]

    def final(v):
        v[6][...] = v[7][...]

    _per_seq(init, functools.partial(_hgrn_main, q=q, valid=valid, lowp=lowp, nlev=nlev), final,
             (f_ref, i_ref, q_ref, g_ref, s0_ref.at[0], y_ref, so_ref, s_scr, o_scr),
             (lb_ref, nw_ref, tl_ref), bs, nc)


def _hgrn_main(v, k, *, q, valid, lowp, nlev):
    f_ref, i_ref, q_ref, g_ref, _, y_ref, _, s_scr, o_scr = v
    lb_ref, nw_ref, tl_ref = k
    lb = lb_ref[...]
    fg = lb + (1.0 - lb) * _sigmoid(f_ref[...])
    kk = 1.0 - fg
    qq = _silu(q_ref[...].astype(F32)) * (HG_HEADDIM ** -0.5)
    vv = i_ref[...] if lowp else i_ref[...].astype(F32)
    rcol =lax.broadcasted_iota(jnp.int32, (q, 1), 0)
    if valid < q:
        fg = jnp.where(rcol < valid, fg, 1.0)
        kk = jnp.where(rcol < valid, kk, 0.0)
    lf = jnp.log(fg)

    gcum = _xdot(tl_ref[...], lf, NN, 1, lowp)
    yield

    rt = lax.broadcasted_iota(jnp.int32, (q, q), 0)
    cs = lax.broadcasted_iota(jnp.int32, (q, q), 1)
    rx = rt ^ cs
    att = [None] * HG_HEADS
    for lev in range(nlev):
        m = q >> (lev + 1)
        second = (rcol & m) != 0
        if m >= 4:
            nb = q // (2 * m)
            gb = jnp.broadcast_to(gcum.reshape(nb, 2 * m, BRANCH_W)[:, m - 1:m, :], (nb, 2 * m, BRANCH_W))
            e = jnp.exp(-jnp.abs(gcum - gb.reshape(q, BRANCH_W)))
        elif m == 2:
            r4 = rcol & 3
            e = jnp.where(r4 == 0, pltpu.roll(fg, q - 1, 0),
                          jnp.where(r4 == 1, 1.0, jnp.where(r4 == 2, fg, fg * pltpu.roll(fg, 1, 0))))
        else:
            e = jnp.where(second, fg, 1.0)
        u = jnp.where(second, qq, kk) * e
        if lowp:
            u = u.astype(BF16)
        pair = (rx >> int(math.log2(m))) == 1
        for h in range(HG_HEADS):
            hs = slice(h * HG_HEADDIM, (h + 1) * HG_HEADDIM)
            s = lax.dot_general(u[:, hs], u[:, hs], NT, preferred_element_type=F32)
            att[h] = jnp.where(pair, s, 0.0 if att[h] is None else att[h])
        yield

    qg = qq * jnp.exp(gcum)
    kend = kk * jnp.exp(gcum[q - 1:q, :] - gcum)
    ones = jnp.ones((q, 128), F32)
    for h in range(HG_HEADS):
        hs = slice(h * HG_HEADDIM, (h + 1) * HG_HEADDIM)
        diag = jnp.sum(qq[:, hs] * kk[:, hs], axis=1, keepdims=True)
        a = jnp.where(rt > cs, att[h], jnp.where(rt == cs, diag, 0.0))
        sp = s_scr[h]
        oh = _mm(a, vv[:, hs], lowp) + _mm(qg[:, hs], sp, lowp)
        dec = jnp.exp(_xdot(lf[:, hs], ones, TN, 0, lowp))
        s_scr[h] = sp * dec + _mm(kend[:, hs], vv[:, hs], lowp, TN)
        o_scr[:, hs] = oh * lax.rsqrt(jnp.mean(oh * oh, -1, keepdims=True) + EPS)
        yield

    y_ref[...] = (o_scr[...] * nw_ref[...] * _silu(g_ref[...].astype(F32))).astype(BF16)


def _hgrn(proj, proj32, s_state, layer, w, consts, *, b, q, nc, valid, lowp, bs):
    nlev = int(math.log2(q))
    kern = functools.partial(_hgrn_kernel, q=q, nc=nc, valid=valid, lowp=lowp, nlev=nlev, bs=bs)
    return pl.pallas_call(
        kern,
        grid=(b // bs, nc),
        in_specs=[
            _col_spec(bs, q, COL32_D_F, BRANCH_W),
            _col_spec(bs, q, COL_D_I, BRANCH_W),
            _col_spec(bs, q, COL_D_Q, BRANCH_W),
            _col_spec(bs, q, COL_D_G, BRANCH_W),
            _state_in_spec(layer, bs, (HG_HEADS, HG_HEADDIM, HG_HEADDIM)),
            _const_spec((1, BRANCH_W)),
            _const_spec((1, BRANCH_W)),
            _const_spec((q, q)),
        ],
        out_specs=[
            _y_spec(bs, q),
            _state_out_spec(bs, (HG_HEADS, HG_HEADDIM, HG_HEADDIM)),
        ],
        out_shape=[
            jax.ShapeDtypeStruct((b, nc * q, BRANCH_W), BF16),
            jax.ShapeDtypeStruct((b, HG_HEADS, HG_HEADDIM, HG_HEADDIM), F32),
        ],
        scratch_shapes=[
            pltpu.VMEM((bs, HG_HEADS, HG_HEADDIM, HG_HEADDIM), F32),
            pltpu.VMEM((bs, q, BRANCH_W), F32),
        ],
        compiler_params=_MIXER_PARAMS,
        name="hgrn",
    )(proj32, proj, proj, proj, s_state, w['hg_lb'], w['hg_norm_w'], consts['tril'])


def _chunk_consts(q):
    r = np.arange(q)
    tril = (r[:, None] >= r[None, :]).astype(np.float32)
    expand = np.zeros((128, BRANCH_W), np.float32)
    for h in range(SSD_HEADS):
        expand[h, h * SSD_HEADDIM:(h + 1) * SSD_HEADDIM] = 1.0
    return {'tril': jnp.asarray(tril), 'triu': jnp.asarray(tril.T), 'expand': jnp.asarray(expand)}


def _cmul(ar, ai, br, bi):
    return ar * br - ai * bi, ar * bi + ai * br


def _layer_params(l, p, hg_lb):
    w_in = p['w_in'][l]
    small = jnp.concatenate([w_in[:, 1280:1288], w_in[:, 3848:3856]], axis=1)
    w_perm = jnp.concatenate([
        w_in[:, 5904:10000],
        w_in[:, 0:512],
        w_in[:, 512:1024],
        w_in[:, 1288:2312],
        w_in[:, 2312:3848],
        w_in[:, 4368:5904],
    ], axis=1).astype(BF16)
    w_perm32 = jnp.concatenate([
        w_in[:, 3856:4368],
        w_in[:, 1024:1280],
        small,
        jnp.zeros((D_MODEL, N_PROJ32 - COL32_SMALL - N_SMALL), F32),
    ], axis=1).astype(BF16)

    def lane_pad(v, off):
        return jnp.zeros((1, 128), F32).at[0, off:off + v.shape[0]].set(v)

    bias_col = (lane_pad(p['ssd_dt_bias'][l], SMALL_DT) + lane_pad(p['ml_i_bias'][l], SMALL_I)
                + lane_pad(p['ml_f_bias'][l], SMALL_F))
    alog_col = lane_pad(p['ssd_A_log'][l], SMALL_DT)

    dt = jnp.exp(p['s5_log_dt'][l])[:, None]
    lr = p['s5_A_re'][l]
    li = p['s5_A_im'][l]
    mag = jnp.exp(lr * dt)
    abr, abi = mag * jnp.cos(li * dt), mag * jnp.sin(li * dt)
    den = lr * lr + li * li
    cr = ((abr - 1.0) * lr + abi * li) / den
    ci = (abi * lr - (abr - 1.0) * li) / den
    bbr, bbi = _cmul(cr[..., None], ci[..., None], p['s5_B_re'][l], p['s5_B_im'][l])
    eye8 = jnp.eye(8, dtype=F32)

    def pack_b(bb):
        return jnp.einsum('jgpc,gh->jgchp', bb.reshape(4, 8, S5_STATE, S5_GROUP), eye8).reshape(4, 128, 512)

    def pack_c(cc):
        return jnp.einsum('jgcp,gh->jgphc', cc.reshape(4, 8, S5_GROUP, S5_STATE), eye8).reshape(4, 512, 128)

    return {
        'norm_w': p['norm_w'][l].reshape(1, D_MODEL),
        'w_in': w_perm, 'w_in32': w_perm32,
        'w_small_t': small.T.astype(BF16),
        'bias_col': bias_col, 'bias_row': bias_col[0, :N_SMALL].reshape(N_SMALL, 1),
        'alog_col': alog_col, 'alog_row': alog_col[0, :N_SMALL].reshape(N_SMALL, 1),
        'ssd_conv_w': p['ssd_conv_w'][l], 'ssd_conv_b': p['ssd_conv_b'][l].reshape(1, SSD_CONV_CH),
        'ssd_D_exp': jnp.repeat(p['ssd_D'][l], SSD_HEADDIM).reshape(1, BRANCH_W),
        'ssd_norm_w': p['ssd_norm_w'][l].reshape(1, BRANCH_W),
        's5_wbr': pack_b(bbr).astype(BF16), 's5_wbi': pack_b(bbi).astype(BF16),
        's5_wcr': pack_c(p['s5_C_re'][l]).astype(BF16), 's5_wci': pack_c(p['s5_C_im'][l]).astype(BF16),
        's5_ar': jnp.broadcast_to(abr.reshape(1, S5_LANES), (8, S5_LANES)),
        's5_ai': jnp.broadcast_to(abi.reshape(1, S5_LANES), (8, S5_LANES)),
        's5_D': p['s5_D'][l].reshape(1, BRANCH_W), 's5_glu_w': p['s5_glu_w'][l].astype(BF16),
        'ml_conv_w': p['ml_conv_w'][l], 'ml_conv_b': p['ml_conv_b'][l].reshape(1, BRANCH_W),
        'ml_wq': p['ml_wq'][l], 'ml_wk': p['ml_wk'][l], 'ml_wv': p['ml_wv'][l],
        'ml_norm_w': p['ml_norm_w'][l].reshape(1, BRANCH_W), 'ml_skip': p['ml_skip'][l].reshape(1, BRANCH_W),
        'hg_lb': hg_lb[l].reshape(1, BRANCH_W), 'hg_norm_w': p['hg_norm_w'][l].reshape(1, BRANCH_W),
        'w_branch': p['w_branch'][l].astype(BF16), 'w_out': p['w_out'][l].astype(BF16),
    }


_MATMUL_WEIGHTS = ('ml_wq', 'ml_wk', 'ml_wv')
S5_CHUNK = 32
SEQS_PER_STEP = {'ssd': 4, 'mlstm': 2, 'hgrn': 4}
SEQS_PER_STEP_PADDED = {'ssd': 8, 'mlstm': 8, 'hgrn': 8}


def _run_group(x3, states, layers, fnw, *, q, valid):
    b, lp, _ = x3.shape
    nc = lp // q
    t = b * lp
    lowp = q >= 16
    consts = _chunk_consts(q)
    ssd_conv, ssd_h, s5_re, s5_im, ml_conv, ml_c, ml_n, ml_m, hg_s = states
    s5_re = s5_re.reshape(DEPTH, b, S5_LANES)
    s5_im = s5_im.reshape(DEPTH, b, S5_LANES)
    if valid < q:
        s5_kw = dict(b=b, lp=lp, q=q, valid=valid, bb=min(b, 128))
    else:
        qs = math.gcd(lp, S5_CHUNK)
        s5_kw = dict(b=b, lp=lp, q=qs, valid=qs, bb=8)
    ml_m = jnp.broadcast_to(jnp.pad(ml_m, ((0, 0), (0, 0), (0, 8 - ML_HEADS)))[..., None], (DEPTH, b, 8, 128))
    bs = {k: math.gcd(b, v) for k, v in (SEQS_PER_STEP_PADDED if valid < q else SEQS_PER_STEP).items()}
    kw = dict(b=b, q=q, nc=nc, valid=valid, lowp=lowp)
    x = x3.reshape(t, D_MODEL)
    new = []
    for l, w in enumerate(layers):
        if lowp:
            w = dict(w, **{k: w[k].astype(BF16) for k in _MATMUL_WEIGHTS})
        proj, proj32, small_t = _inproj(x, w['norm_w'], w['w_in'], w['w_in32'], w['w_small_t'])
        proj3 = proj.reshape(b, lp, N_PROJ)
        proj32 = proj32.reshape(b, lp, N_PROJ32)
        srow = small_t.reshape(N_SMALL, b, lp).transpose(1, 0, 2)
        ya, n_ssd_conv, n_ssd_h = _ssd(proj3, proj32, srow, ssd_conv, ssd_h, l, w, consts, bs=bs['ssd'], **kw)
        yb, n_re, n_im = _s5(proj, s5_re, s5_im, l, w, **s5_kw)
        yc, n_ml_conv, n_c, n_n, n_m = _mlstm(proj3, proj32, srow, ml_conv, ml_c, ml_n, ml_m, l, w, consts,
                                              bs=bs['mlstm'], **kw)
        yd, n_s = _hgrn(proj3, proj32, hg_s, l, w, consts, bs=bs['hgrn'], **kw)
        ya, yc, yd = (y.reshape(t, BRANCH_W) for y in (ya, yc, yd))
        x = _merge(proj, (ya, yb, yc, yd), x, w['w_branch'], w['w_out'], fnw, final=(l == DEPTH - 1))
        new.append((n_ssd_conv, n_ssd_h,
                    n_re.reshape(b, S5_GROUPS, S5_STATE), n_im.reshape(b, S5_GROUPS, S5_STATE),
                    n_ml_conv, n_c, n_n, n_m[:, :ML_HEADS, 0], n_s))
    new_states = tuple(jnp.stack(ss, axis=0) for ss in zip(*new))
    return x.reshape(b, lp, D_MODEL), new_states


SAMPLE_PAD = 8


def kernel(x_prompt, x_sample, state_ssd_conv, state_ssd, state_s5_re, state_s5_im, state_mlstm_conv, state_mlstm_C, state_mlstm_n, state_mlstm_m, state_hgrn, norm_w, w_in, ssd_conv_w, ssd_conv_b, ssd_dt_bias, ssd_A_log, ssd_D, ssd_norm_w, s5_A_re, s5_A_im, s5_B_re, s5_B_im, s5_C_re, s5_C_im, s5_D, s5_log_dt, s5_glu_w, ml_conv_w, ml_conv_b, ml_wq, ml_wk, ml_wv, ml_i_bias, ml_f_bias, ml_norm_w, ml_skip, hg_lb_logits, hg_norm_w, w_branch, w_out, final_norm_w):
    p = {'norm_w': norm_w, 'w_in': w_in,
         'ssd_conv_w': ssd_conv_w, 'ssd_conv_b': ssd_conv_b, 'ssd_dt_bias': ssd_dt_bias,
         'ssd_A_log': ssd_A_log, 'ssd_D': ssd_D, 'ssd_norm_w': ssd_norm_w,
         's5_A_re': s5_A_re, 's5_A_im': s5_A_im, 's5_B_re': s5_B_re, 's5_B_im': s5_B_im,
         's5_C_re': s5_C_re, 's5_C_im': s5_C_im, 's5_D': s5_D, 's5_log_dt': s5_log_dt, 's5_glu_w': s5_glu_w,
         'ml_conv_w': ml_conv_w, 'ml_conv_b': ml_conv_b, 'ml_wq': ml_wq, 'ml_wk': ml_wk, 'ml_wv': ml_wv,
         'ml_i_bias': ml_i_bias, 'ml_f_bias': ml_f_bias, 'ml_norm_w': ml_norm_w, 'ml_skip': ml_skip,
         'hg_norm_w': hg_norm_w, 'w_branch': w_branch, 'w_out': w_out}
    lb_cum = jnp.cumsum(jax.nn.softmax(hg_lb_logits, axis=0), axis=0)
    hg_lb = lb_cum - lb_cum[0]
    layers = [_layer_params(l, p, hg_lb) for l in range(DEPTH)]
    fnw = final_norm_w.reshape(1, D_MODEL)

    sample_states = (state_ssd_conv, state_ssd, state_s5_re, state_s5_im, state_mlstm_conv,
                     state_mlstm_C, state_mlstm_n, state_mlstm_m, state_hgrn)
    bp, lp_, _ = x_prompt.shape
    prompt_states = tuple(jnp.zeros((DEPTH, bp) + s.shape[2:], F32) for s in sample_states)
    q_prompt = math.gcd(lp_, 128)
    y_prompt, new_p = _run_group(x_prompt, prompt_states, layers, fnw, q=q_prompt, valid=q_prompt)

    ls = x_sample.shape[1]
    xs_pad = jnp.pad(x_sample, ((0, 0), (0, SAMPLE_PAD - ls), (0, 0)))
    y_sample, new_s = _run_group(xs_pad, sample_states, layers, fnw, q=SAMPLE_PAD, valid=ls)
    y_sample = y_sample[:, :ls]

    out = [y_prompt, y_sample]
    for ps, ss in zip(new_p, new_s):
        out += [ps, ss]
    return tuple(out)
```

```python
import functools
import math
from typing import Callable, NamedTuple

import numpy as np
import jax
import jax.numpy as jnp
from jax import lax
from jax.experimental import pallas as pl
from jax.experimental.pallas import tpu as pltpu

F32 = jnp.float32
BF16 = jnp.bfloat16

D_MODEL = 1024
DEPTH = 2
BRANCH_W = 512
CONV_W = 4
EPS = 1e-6
SSD_HEADS = 8
SSD_HEADDIM = 64
SSD_STATE = 64
SSD_GROUPS = 2
SSD_CONV_CH = 768
S5_GROUPS = 32
S5_GROUP = 16
S5_STATE = 64
S5_LANES = S5_GROUPS * S5_STATE
ML_HEADS = 4
ML_HEADDIM = 128
HG_HEADS = 4
HG_HEADDIM = 128
N_BRANCH = 4

NEG = -1e30

N_PROJ = 9216
COL_MERGE = 0
COL_A_Z = 4096
COL_A_X = 4608
COL_B_U = 5120
COL_B_GATE = 5632
COL_C_X = 6144
COL_C_Z = 6656
COL_C_O = 7168
COL_D_I = 7680
COL_D_Q = 8192
COL_D_G = 8704
N_PROJ32 = 896
COL32_D_F = 0
COL32_A_BC = 512
COL32_SMALL = 768
N_SMALL = 16
SMALL_DT = 0
SMALL_I = 8
SMALL_F = 12

NN = (((1,), (0,)), ((), ()))
NT = (((1,), (1,)), ((), ()))
TN = (((0,), (0,)), ((), ()))


def _mm(a, b, lowp, dims=NN):
    if lowp:
        a = a.astype(BF16)
        b = b.astype(BF16)
    return lax.dot_general(a, b, dims, preferred_element_type=F32)


def _split3(a):
    hi = a.astype(BF16)
    r = a - hi.astype(F32)
    mid = r.astype(BF16)
    lo = (r - mid.astype(F32)).astype(BF16)
    return hi, mid, lo


def _xdot(a, b, dims, split, lowp):
    if not lowp:
        return lax.dot_general(a, b, dims, precision=lax.Precision.HIGHEST,
                               preferred_element_type=F32)
    if split == 0:
        other = b.astype(BF16)
        return sum(lax.dot_general(p, other, dims, preferred_element_type=F32) for p in _split3(a))
    other = a.astype(BF16)
    return sum(lax.dot_general(other, p, dims, preferred_element_type=F32) for p in _split3(b))


def _sigmoid(x):
    return jax.nn.sigmoid(x)


def _silu(x):
    return x * jax.nn.sigmoid(x)


def _softplus(x):
    return jnp.maximum(x, 0.0) + jnp.log1p(jnp.exp(-jnp.abs(x)))


def _conv_chunk(xp_scr, cw_ref, cb_ref, q, valid):
    cw = cw_ref[...]
    conv = (cb_ref[...]
            + cw[0:1] * xp_scr[5:5 + q, :]
            + cw[1:2] * xp_scr[6:6 + q, :]
            + cw[2:3] * xp_scr[7:7 + q, :]
            + cw[3:4] * xp_scr[8:8 + q, :])
    carry = xp_scr[5 + valid:8 + valid, :]
    xp_scr[5:8, :] = carry
    return conv, carry


def _inproj_kernel(x_ref, nw_ref, w_ref, w32_ref, wst_ref, o_ref, o32_ref, st_ref, xn_scr):
    @pl.when(pl.program_id(1) == 0)
    def _():
        x = x_ref[...]
        xn = x * lax.rsqrt(jnp.mean(x * x, -1, keepdims=True) + EPS) * nw_ref[...]
        xb = xn.astype(BF16)
        xn_scr[...] = xb
        o32_ref[...] = jnp.dot(xb, w32_ref[...], preferred_element_type=F32)
        st_ref[...] = lax.dot_general(wst_ref[...], xb, NT, preferred_element_type=F32)

    o_ref[...] = jnp.dot(xn_scr[...], w_ref[...], preferred_element_type=F32).astype(BF16)


def _inproj(x, nw, w, w32, wst):
    t = x.shape[0]
    tm = min(1024, t)
    tn = 1024
    return pl.pallas_call(
        _inproj_kernel,
        grid=(t // tm, N_PROJ // tn),
        in_specs=[
            pl.BlockSpec((tm, D_MODEL), lambda i, j: (i, 0)),
            pl.BlockSpec((1, D_MODEL), lambda i, j: (0, 0)),
            pl.BlockSpec((D_MODEL, tn), lambda i, j: (0, j)),
            pl.BlockSpec((D_MODEL, N_PROJ32), lambda i, j: (0, 0)),
            pl.BlockSpec((N_SMALL, D_MODEL), lambda i, j: (0, 0)),
        ],
        out_specs=[
            pl.BlockSpec((tm, tn), lambda i, j: (i, j)),
            pl.BlockSpec((tm, N_PROJ32), lambda i, j: (i, 0)),
            pl.BlockSpec((N_SMALL, tm), lambda i, j: (0, i)),
        ],
        out_shape=[
            jax.ShapeDtypeStruct((t, N_PROJ), BF16),
            jax.ShapeDtypeStruct((t, N_PROJ32), F32),
            jax.ShapeDtypeStruct((N_SMALL, t), F32),
        ],
        scratch_shapes=[pltpu.VMEM((tm, D_MODEL), BF16)],
        compiler_params=pltpu.CompilerParams(
            dimension_semantics=("arbitrary", "arbitrary"),
            vmem_limit_bytes=48 * 1024 * 1024),
        name="inproj",
    )(x, nw, w, w32, wst)


def _merge_kernel(g_ref, ya_ref, yb_ref, yc_ref, yd_ref, x_ref, wb_ref, wo_ref, fnw_ref, o_ref, *, final):
    acc = None
    for i, y_ref in enumerate((ya_ref, yb_ref, yc_ref, yd_ref)):
        gate = _sigmoid(g_ref[:, i * D_MODEL:(i + 1) * D_MODEL].astype(F32))
        term = gate * jnp.dot(y_ref[...], wb_ref[i], preferred_element_type=F32)
        acc = term if acc is None else acc + term
    out = x_ref[...] + jnp.dot(acc.astype(BF16), wo_ref[...], preferred_element_type=F32)
    if final:
        out = out * lax.rsqrt(jnp.mean(out * out, -1, keepdims=True) + EPS) * fnw_ref[...]
    o_ref[...] = out


def _merge(proj, ys, x, wb, wo, fnw, final):
    t = x.shape[0]
    tm = min(256, t)
    yspec = pl.BlockSpec((tm, BRANCH_W), lambda i: (i, 0))
    return pl.pallas_call(
        functools.partial(_merge_kernel, final=final),
        grid=(t // tm,),
        in_specs=[
            pl.BlockSpec((tm, N_BRANCH * D_MODEL), lambda i: (i, COL_MERGE // (N_BRANCH * D_MODEL))),
            yspec, yspec, yspec, yspec,
            pl.BlockSpec((tm, D_MODEL), lambda i: (i, 0)),
            pl.BlockSpec((N_BRANCH, BRANCH_W, D_MODEL), lambda i: (0, 0, 0)),
            pl.BlockSpec((D_MODEL, D_MODEL), lambda i: (0, 0)),
            pl.BlockSpec((1, D_MODEL), lambda i: (0, 0)),
        ],
        out_specs=pl.BlockSpec((tm, D_MODEL), lambda i: (i, 0)),
        out_shape=jax.ShapeDtypeStruct((t, D_MODEL), F32),
        compiler_params=pltpu.CompilerParams(
            dimension_semantics=("arbitrary",),
            vmem_limit_bytes=48 * 1024 * 1024),
        name="merge",
    )(proj, *ys, x, wb, wo, fnw)


def _col_spec(bs, q, col, width):
    return pl.BlockSpec((bs, q, width), lambda b, c: (b, c, col // width))


def _row_spec(bs, q):
    return pl.BlockSpec((bs, N_SMALL, q), lambda b, c: (b, 0, c))


def _const_spec(shape):
    nd = len(shape)
    return pl.BlockSpec(shape, lambda b, c: (0,) * nd)


def _state_in_spec(layer, bs, tail):
    nd = len(tail)
    return pl.BlockSpec((1, bs) + tail, lambda b, c: (layer, b) + (0,) * nd)


def _state_out_spec(layer, bs, tail):
    nd = len(tail)
    return pl.BlockSpec((1, bs) + tail, lambda b, c: (layer, b) + (0,) * nd)


def _y_spec(bs, q):
    return pl.BlockSpec((bs, q, BRANCH_W), lambda b, c: (b, c, 0))


class _Mixer(NamedTuple):
    in_specs: list
    args: list
    out_specs: list
    out_shapes: list
    scratch: list
    group: Callable


_DONE = object()
_MIXER_PARAMS = pltpu.CompilerParams(
    dimension_semantics=("arbitrary", "arbitrary"),
    vmem_limit_bytes=48 * 1024 * 1024)


def _launch(mixers, prevs, *, grid, bs, nc, name):
    n_in = [len(m.in_specs) for m in mixers]
    n_out = [len(m.out_specs) for m in mixers]
    n_scr = [len(m.scratch) for m in mixers]
    tot_in, tot_out = sum(n_in), sum(n_out)
    state_idx = [sum(n_out[:i]) + k for i in range(len(mixers)) for k in range(1, n_out[i])]
    n_alias = 0 if prevs is None else len(state_idx)

    def kern(*refs):
        ins, outs, scr = refs[:tot_in], refs[tot_in + n_alias:tot_in + n_alias + tot_out], refs[tot_in + n_alias + tot_out:]
        groups = []
        for i, m in enumerate(mixers):
            groups.append(m.group(ins[sum(n_in[:i]):sum(n_in[:i + 1])],
                                  outs[sum(n_out[:i]):sum(n_out[:i + 1])],
                                  scr[sum(n_scr[:i]):sum(n_scr[:i + 1])]))
        c = pl.program_id(1)
        views = [[[r.at[s] for r in g[3]] for s in range(bs)] for g in groups]

        @pl.when(c == 0)
        def _():
            for g, v in zip(groups, views):
                for s in range(bs):
                    g[0](v[s])

        active = [g[1](v[s], g[4]) for s in range(bs) for g, v in zip(groups, views)]
        while active:
            active = [gen for gen in active if next(gen, _DONE) is not _DONE]

        @pl.when(c == nc - 1)
        def _():
            for g, v in zip(groups, views):
                for s in range(bs):
                    g[2](v[s])

    in_specs = [sp for m in mixers for sp in m.in_specs] + [pl.BlockSpec(memory_space=pl.ANY)] * n_alias
    args = [a for m in mixers for a in m.args]
    if prevs is not None:
        args += [a for p in prevs for a in p]
    outs = pl.pallas_call(
        kern,
        grid=grid,
        in_specs=in_specs,
        out_specs=[sp for m in mixers for sp in m.out_specs],
        out_shape=[sh for m in mixers for sh in m.out_shapes],
        scratch_shapes=[sc for m in mixers for sc in m.scratch],
        input_output_aliases={tot_in + k: idx for k, idx in enumerate(state_idx)} if n_alias else {},
        compiler_params=_MIXER_PARAMS,
        name=name,
    )(*args)
    return [outs[sum(n_out[:i]):sum(n_out[:i + 1])] for i in range(len(mixers))]


def _ssd_group(ins, outs, scr, *, q, valid, lowp):
    z_ref, x_ref, bc_ref, sc_ref, sr_ref, conv0_ref, h0_ref, *consts = ins
    y_ref, convo_ref, ho_ref = outs
    xp_scr, h_scr, y_scr = scr

    def init(v):
        v[9][5:8, :] = v[5][...]
        v[10][...] = v[6][...]

    def final(v):
        v[8][...] = v[10][...]

    return (init, functools.partial(_ssd_main, q=q, valid=valid, lowp=lowp), final,
            (z_ref, x_ref, bc_ref, sc_ref, sr_ref, conv0_ref.at[0], h0_ref.at[0],
             y_ref, ho_ref.at[0], xp_scr, h_scr, y_scr, convo_ref.at[0]),
            tuple(consts))


def _ssd_main(v, k, *, q, valid, lowp):
    z_ref, x_ref, bc_ref, sc_ref, sr_ref, _, _, y_ref, _, xp_scr, h_scr, y_scr, convo_ref = v
    cw_ref, cb_ref, bcol_ref, brow_ref, alc_ref, alr_ref, dexp_ref, nw_ref, e_ref, tl_ref, tu_ref = k
    xp_scr[8:8 + q, 0:BRANCH_W] = x_ref[...].astype(F32)
    xp_scr[8:8 + q, BRANCH_W:SSD_CONV_CH] = bc_ref[...]
    conv, carry = _conv_chunk(xp_scr, cw_ref, cb_ref, q, valid)
    convo_ref[...] = carry
    yield
    xbc = _silu(conv)
    xs = xbc[:, 0:BRANCH_W]
    bm = xbc[:, BRANCH_W:BRANCH_W + 128]
    cm = xbc[:, BRANCH_W + 128:BRANCH_W + 256]

    dt_c = _softplus(sc_ref[...] + bcol_ref[...])
    a_c = dt_c * (-jnp.exp(alc_ref[...]))
    dt_r = _softplus(sr_ref[...] + brow_ref[...])
    a_r = dt_r * (-jnp.exp(alr_ref[...]))
    if valid < q:
        tcol = lax.broadcasted_iota(jnp.int32, (q, 1), 0)
        trow = lax.broadcasted_iota(jnp.int32, (1, q), 1)
        a_c = jnp.where(tcol < valid, a_c, 0.0)
        dt_c = jnp.where(tcol < valid, dt_c, 0.0)
        a_r = jnp.where(trow < valid, a_r, 0.0)

    tl = tl_ref[...]
    acum_c = _xdot(tl, a_c, NN, 1, lowp)
    acum_r = _xdot(a_r, tu_ref[...], NN, 0, lowp)
    e = e_ref[...]
    dt_e = _xdot(dt_c, e, NN, 0, lowp)
    acum_e = _xdot(acum_c, e, NN, 0, lowp)
    xdt = xs * dt_e
    xend = xdt * jnp.exp(acum_e[q - 1:q, :] - acum_e)
    eac = jnp.exp(acum_e)
    dec_last = jnp.exp(acum_c[q - 1:q, :])
    yield

    tri = (lax.broadcasted_iota(jnp.int32, (q, q), 0) >= lax.broadcasted_iota(jnp.int32, (q, q), 1))
    rpg = SSD_HEADS // SSD_GROUPS
    for g in range(SSD_GROUPS):
        bg = bm[:, g * SSD_STATE:(g + 1) * SSD_STATE]
        cg = cm[:, g * SSD_STATE:(g + 1) * SSD_STATE]
        cb = _mm(cg, bg, lowp, NT)
        for r in range(rpg):
            h = g * rpg + r
            hs = slice(h * SSD_HEADDIM, (h + 1) * SSD_HEADDIM)
            seg = acum_c[:, h:h + 1] - acum_r[h:h + 1, :]
            m = cb * jnp.exp(jnp.where(tri, seg, NEG))
            hp = h_scr[h]
            yh = _mm(m, xdt[:, hs], lowp) + _mm(cg, hp, lowp, NT) * eac[:, hs]
            y_scr[:, hs] = yh
            h_scr[h] = hp * dec_last[:, h:h + 1] + _mm(xend[:, hs], bg, lowp, TN)
            yield

    y = y_scr[...] + xs * dexp_ref[...]
    y = y * _silu(z_ref[...].astype(F32))
    y = y * lax.rsqrt(jnp.mean(y * y, -1, keepdims=True) + EPS) * nw_ref[...]
    y_ref[...] = y.astype(BF16)


def _ssd(proj, proj32, srow, conv_state, h_state, layer, w, consts, *, b, q, nc, valid, lowp, bs):
    return _Mixer(
        in_specs=[
            _col_spec(bs, q, COL_A_Z, BRANCH_W),
            _col_spec(bs, q, COL_A_X, BRANCH_W),
            _col_spec(bs, q, COL32_A_BC, 256),
            _col_spec(bs, q, COL32_SMALL, 128),
            _row_spec(bs, q),
            _state_in_spec(layer, bs, (CONV_W - 1, SSD_CONV_CH)),
            _state_in_spec(layer, bs, (SSD_HEADS, SSD_HEADDIM, SSD_STATE)),
            _const_spec((CONV_W, SSD_CONV_CH)),
            _const_spec((1, SSD_CONV_CH)),
            _const_spec((1, 128)),
            _const_spec((N_SMALL, 1)),
            _const_spec((1, 128)),
            _const_spec((N_SMALL, 1)),
            _const_spec((1, BRANCH_W)),
            _const_spec((1, BRANCH_W)),
            _const_spec((128, BRANCH_W)),
            _const_spec((q, q)),
            _const_spec((q, q)),
        ],
        out_specs=[
            _y_spec(bs, q),
            _state_out_spec(layer, bs, (CONV_W - 1, SSD_CONV_CH)),
            _state_out_spec(layer, bs, (SSD_HEADS, SSD_HEADDIM, SSD_STATE)),
        ],
        out_shapes=[
            jax.ShapeDtypeStruct((b, nc * q, BRANCH_W), BF16),
            jax.ShapeDtypeStruct((DEPTH, b, CONV_W - 1, SSD_CONV_CH), F32),
            jax.ShapeDtypeStruct((DEPTH, b, SSD_HEADS, SSD_HEADDIM, SSD_STATE), F32),
        ],
        scratch=[
            pltpu.VMEM((bs, q + 8, SSD_CONV_CH), F32),
            pltpu.VMEM((bs, SSD_HEADS, SSD_HEADDIM, SSD_STATE), F32),
            pltpu.VMEM((bs, q, BRANCH_W), F32),
        ],
        args=[proj, proj, proj32, proj32, srow, conv_state, h_state,
              w['ssd_conv_w'], w['ssd_conv_b'], w['bias_col'], w['bias_row'], w['alog_col'], w['alog_row'],
              w['ssd_D_exp'], w['ssd_norm_w'], consts['expand'], consts['tril'], consts['triu']],
        group=functools.partial(_ssd_group, q=q, valid=valid, lowp=lowp))


def _s5_kernel(u_ref, gate_ref, hr0_ref, hi0_ref, perm_ref, permt_ref, ar_ref, ai_ref,
               wbr_ref, wbi_ref, wcr_ref, wci_ref, d_ref, glu_ref,
               y_ref, hro_ref, hio_ref,
               hr_scr, hi_scr, cr_scr, ci_scr, *, q, nc, ngrp):
    c = pl.program_id(1)

    @pl.when(c == 0)
    def _():
        cr_scr[...] = hr0_ref[0]
        ci_scr[...] = hi0_ref[0]

    rows_in = perm_ref.shape[1]
    u = u_ref[...].reshape(rows_in, BRANCH_W)
    u_tm = jnp.dot(perm_ref[...], u, preferred_element_type=F32).astype(BF16)
    nblk = S5_LANES // 512
    for j in range(nblk):
        uj = u_tm[:, j * 128:(j + 1) * 128]
        hr_scr[:, j * 512:(j + 1) * 512] = jnp.dot(uj, wbr_ref[j], preferred_element_type=F32)
        hi_scr[:, j * 512:(j + 1) * 512] = jnp.dot(uj, wbi_ref[j], preferred_element_type=F32)

    half = S5_LANES // 2
    for hf in range(2):
        sl = slice(hf * half, (hf + 1) * half)
        ar = ar_ref[:, sl]
        ai = ai_ref[:, sl]

        def grp_body(g, carry, sl=sl, ar=ar, ai=ai):
            s0 = pl.multiple_of(g * 8, 8)

            def t_body(t, h):
                r0 = pl.multiple_of(g * (8 * q) + t * 8, 8)
                hr, hi = h
                nr = ar * hr - ai * hi + hr_scr[pl.ds(r0, 8), sl]
                ni = ar * hi + ai * hr + hi_scr[pl.ds(r0, 8), sl]
                hr_scr[pl.ds(r0, 8), sl] = nr
                hi_scr[pl.ds(r0, 8), sl] = ni
                return nr, ni

            h = lax.fori_loop(0, q, t_body, (cr_scr[pl.ds(s0, 8), sl], ci_scr[pl.ds(s0, 8), sl]),
                              unroll=min(q, 4))
            cr_scr[pl.ds(s0, 8), sl] = h[0]
            ci_scr[pl.ds(s0, 8), sl] = h[1]
            return carry

        lax.fori_loop(0, ngrp, grp_body, 0)

    parts = []
    for j in range(nblk):
        sl = slice(j * 512, (j + 1) * 512)
        parts.append(jnp.dot(hr_scr[:, sl].astype(BF16), wcr_ref[j], preferred_element_type=F32)
                     - jnp.dot(hi_scr[:, sl].astype(BF16), wci_ref[j], preferred_element_type=F32))
    y_tm = jnp.concatenate(parts, axis=1)
    y = _xdot(permt_ref[...], y_tm, NN, 1, True) + u.astype(F32) * d_ref[...]
    g = jax.nn.gelu(y)
    y = g * _sigmoid(jnp.dot(g.astype(BF16), glu_ref[...], preferred_element_type=F32))
    y = y * _silu(gate_ref[...].reshape(rows_in, BRANCH_W).astype(F32))
    y_ref[...] = y.reshape(y_ref.shape).astype(BF16)

    @pl.when(c == nc - 1)
    def _():
        hro_ref[0] = cr_scr[...]
        hio_ref[0] = ci_scr[...]


def _s5_perm(bb, lp, valid):
    perm = np.zeros((bb * valid, bb * lp), np.float32)
    for g in range(bb // 8):
        for t in range(valid):
            for s in range(8):
                perm[g * 8 * valid + t * 8 + s, (g * 8 + s) * lp + t] = 1.0
    return jnp.asarray(perm, BF16), jnp.asarray(perm.T, BF16)


def _s5(proj, hr_state, hi_state, layer, w, prev, *, b, lp, q, valid, bb):
    nc = lp // q
    perm, permt = _s5_perm(bb, q, valid)
    rows = bb * valid
    if nc == 1:
        def col_spec(col):
            return pl.BlockSpec((bb * q, BRANCH_W), lambda i, c: (i, col // BRANCH_W))
        src = proj
        y_spec = pl.BlockSpec((bb * q, BRANCH_W), lambda i, c: (i, 0))
        y_shape = jax.ShapeDtypeStruct((b * lp, BRANCH_W), BF16)
    else:
        def col_spec(col):
            return pl.BlockSpec((bb, q, BRANCH_W), lambda i, c: (i, c, col // BRANCH_W))
        src = proj.reshape(b, lp, N_PROJ)
        y_spec = pl.BlockSpec((bb, q, BRANCH_W), lambda i, c: (i, c, 0))
        y_shape = jax.ShapeDtypeStruct((b, lp, BRANCH_W), BF16)
    st_in = pl.BlockSpec((1, bb, S5_LANES), lambda i, c: (layer, i, 0))
    st_out = st_in
    kern = functools.partial(_s5_kernel, q=valid, nc=nc, ngrp=bb // 8)
    n_in = 14
    alias_kw = {}
    if prev is not None:
        base = kern

        def kern(*refs):
            return base(*refs[:n_in], *refs[n_in + 2:])

        alias_kw = dict(input_output_aliases={n_in: 1, n_in + 1: 2})
    y, hr, hi = pl.pallas_call(
        kern,
        grid=(b // bb, nc),
        **alias_kw,
        in_specs=[
            col_spec(COL_B_U),
            col_spec(COL_B_GATE),
            st_in, st_in,
            _const_spec(perm.shape),
            _const_spec(permt.shape),
            _const_spec((8, S5_LANES)),
            _const_spec((8, S5_LANES)),
            _const_spec((4, 128, 512)),
            _const_spec((4, 128, 512)),
            _const_spec((4, 512, 128)),
            _const_spec((4, 512, 128)),
            _const_spec((1, BRANCH_W)),
            _const_spec((BRANCH_W, BRANCH_W)),
        ] + [pl.BlockSpec(memory_space=pl.ANY)] * (0 if prev is None else 2),
        out_specs=[y_spec, st_out, st_out],
        out_shape=[
            y_shape,
            jax.ShapeDtypeStruct((DEPTH, b, S5_LANES), F32),
            jax.ShapeDtypeStruct((DEPTH, b, S5_LANES), F32),
        ],
        scratch_shapes=[
            pltpu.VMEM((rows, S5_LANES), F32),
            pltpu.VMEM((rows, S5_LANES), F32),
            pltpu.VMEM((bb, S5_LANES), F32),
            pltpu.VMEM((bb, S5_LANES), F32),
        ],
        compiler_params=_MIXER_PARAMS,
        name="s5",
    )(src, src, hr_state, hi_state, perm, permt, w['s5_ar'], w['s5_ai'],
      w['s5_wbr'], w['s5_wbi'], w['s5_wcr'], w['s5_wci'], w['s5_D'], w['s5_glu_w'], *(prev or ()))
    return y.reshape(b * lp, BRANCH_W), hr, hi


def _mlstm_group(ins, outs, scr, *, q, valid, lowp):
    x_ref, z_ref, o_ref, sc_ref, sr_ref, conv0_ref, c0_ref, n0_ref, m0_ref, *consts = ins
    y_ref, convo_ref, co_ref, no_ref, mo_ref = outs
    xp_scr, c_scr, n_scr, m_scr, h_scr = scr

    def init(v):
        v[13][5:8, :] = v[5][...]
        v[14][...] = v[6][...]
        v[15][...] = v[7][...]
        v[16][...] = v[8][...]

    def final(v):
        v[10][...] = v[14][...]
        v[11][...] = v[15][...]
        v[12][...] = v[16][...]

    return (init, functools.partial(_mlstm_main, q=q, valid=valid, lowp=lowp), final,
            (x_ref, z_ref, o_ref, sc_ref, sr_ref, conv0_ref.at[0], c0_ref.at[0], n0_ref.at[0], m0_ref.at[0],
             y_ref, co_ref.at[0], no_ref.at[0], mo_ref.at[0], xp_scr, c_scr, n_scr, m_scr, h_scr,
             convo_ref.at[0]),
            tuple(consts))


def _mlstm_main(v, k, *, q, valid, lowp):
    (x_ref, z_ref, o_ref, sc_ref, sr_ref, _, _, _, _, y_ref, _, _, _,
     xp_scr, c_scr, n_scr, m_scr, h_scr, convo_ref) = v
    cw_ref, cb_ref, wq_ref, wk_ref, wv_ref, bcol_ref, brow_ref, nw_ref, skip_ref, tl_ref, tu_ref = k
    x = x_ref[...].astype(F32)
    xp_scr[8:8 + q, :] = x
    conv, carry = _conv_chunk(xp_scr, cw_ref, cb_ref, q, valid)
    convo_ref[...] = carry
    yield
    xc = _silu(conv)

    pre_c = sc_ref[...] + bcol_ref[...]
    pre_r = sr_ref[...] + brow_ref[...]
    ig_c = pre_c
    lf_c = -_softplus(-pre_c)
    ig_r = pre_r
    lf_r = -_softplus(-pre_r)
    if valid < q:
        tcol = lax.broadcasted_iota(jnp.int32, (q, 1), 0)
        trow = lax.broadcasted_iota(jnp.int32, (1, q), 1)
        lf_c = jnp.where(tcol < valid, lf_c, 0.0)
        ig_c = jnp.where(tcol < valid, ig_c, NEG)
        lf_r = jnp.where(trow < valid, lf_r, 0.0)
        ig_r = jnp.where(trow < valid, ig_r, NEG)
    b_c = _xdot(tl_ref[...], lf_c, NN, 1, lowp)
    b_r = _xdot(lf_r, tu_ref[...], NN, 0, lowp)
    yield

    tri = (lax.broadcasted_iota(jnp.int32, (q, q), 0) >= lax.broadcasted_iota(jnp.int32, (q, q), 1))
    o_all = o_ref[...].astype(F32)
    for h in range(ML_HEADS):
        hs = slice(h * ML_HEADDIM, (h + 1) * ML_HEADDIM)
        xh = xc[:, hs]
        qh = _mm(xh, wq_ref[h], lowp)
        kh = _mm(xh, wk_ref[h], lowp) * (ML_HEADDIM ** -0.5)
        vh = _mm(x[:, hs], wv_ref[h], lowp)
        yield
        bc = b_c[:, SMALL_F + h:SMALL_F + h + 1]
        br = b_r[SMALL_F + h:SMALL_F + h + 1, :]
        ic = ig_c[:, SMALL_I + h:SMALL_I + h + 1]
        ir = ig_r[SMALL_I + h:SMALL_I + h + 1, :]
        mp = m_scr[h:h + 1, 0:1]
        dlog = jnp.where(tri, bc - br + ir, NEG)
        inter = bc + mp
        m_t = jnp.maximum(inter, jnp.max(dlog, axis=1, keepdims=True))
        wgt = jnp.exp(dlog - m_t)
        yield
        s = _mm(qh, kh, lowp, NT) * wgt
        scale = jnp.exp(inter - m_t)
        cp = c_scr[h]
        npv = n_scr[h:h + 1, :]
        num = _mm(s, vh, lowp) + scale * _mm(qh, cp, lowp)
        dot = jnp.sum(s, axis=1, keepdims=True) + scale * jnp.sum(qh * npv, axis=1, keepdims=True)
        hh = num / jnp.maximum(jnp.abs(dot), jnp.exp(-m_t))
        yield
        m_new = m_t[q - 1:q, :]
        b_last = bc[q - 1:q, :]
        w_end = jnp.exp(b_last - bc + ic - m_new)
        cs = jnp.exp(b_last + mp - m_new)
        kw = kh * w_end
        c_scr[h] = cs * cp + _mm(kw, vh, lowp, TN)
        n_scr[h:h + 1, :] = cs * npv + jnp.sum(kw, axis=0, keepdims=True)
        m_scr[h:h + 1, :] = jnp.broadcast_to(m_new, (1, 128))
        yield
        hh = hh * _sigmoid(o_all[:, hs])
        hc = hh - jnp.mean(hh, -1, keepdims=True)
        h_scr[:, hs] = hc * lax.rsqrt(jnp.mean(hc * hc, -1, keepdims=True) + EPS)
        yield

    y = h_scr[...] * nw_ref[...] + skip_ref[...] * xc
    y_ref[...] = (y * _silu(z_ref[...].astype(F32))).astype(BF16)


def _mlstm(proj, proj32, srow, conv_state, c_state, n_state, m_state, layer, w, consts,
           *, b, q, nc, valid, lowp, bs):
    return _Mixer(
        in_specs=[
            _col_spec(bs, q, COL_C_X, BRANCH_W),
            _col_spec(bs, q, COL_C_Z, BRANCH_W),
            _col_spec(bs, q, COL_C_O, BRANCH_W),
            _col_spec(bs, q, COL32_SMALL, 128),
            _row_spec(bs, q),
            _state_in_spec(layer, bs, (CONV_W - 1, BRANCH_W)),
            _state_in_spec(layer, bs, (ML_HEADS, ML_HEADDIM, ML_HEADDIM)),
            _state_in_spec(layer, bs, (ML_HEADS, ML_HEADDIM)),
            _state_in_spec(layer, bs, (8, 128)),
            _const_spec((CONV_W, BRANCH_W)),
            _const_spec((1, BRANCH_W)),
            _const_spec((ML_HEADS, ML_HEADDIM, ML_HEADDIM)),
            _const_spec((ML_HEADS, ML_HEADDIM, ML_HEADDIM)),
            _const_spec((ML_HEADS, ML_HEADDIM, ML_HEADDIM)),
            _const_spec((1, 128)),
            _const_spec((N_SMALL, 1)),
            _const_spec((1, BRANCH_W)),
            _const_spec((1, BRANCH_W)),
            _const_spec((q, q)),
            _const_spec((q, q)),
        ],
        out_specs=[
            _y_spec(bs, q),
            _state_out_spec(layer, bs, (CONV_W - 1, BRANCH_W)),
            _state_out_spec(layer, bs, (ML_HEADS, ML_HEADDIM, ML_HEADDIM)),
            _state_out_spec(layer, bs, (ML_HEADS, ML_HEADDIM)),
            _state_out_spec(layer, bs, (8, 128)),
        ],
        out_shapes=[
            jax.ShapeDtypeStruct((b, nc * q, BRANCH_W), BF16),
            jax.ShapeDtypeStruct((DEPTH, b, CONV_W - 1, BRANCH_W), F32),
            jax.ShapeDtypeStruct((DEPTH, b, ML_HEADS, ML_HEADDIM, ML_HEADDIM), F32),
            jax.ShapeDtypeStruct((DEPTH, b, ML_HEADS, ML_HEADDIM), F32),
            jax.ShapeDtypeStruct((DEPTH, b, 8, 128), F32),
        ],
        scratch=[
            pltpu.VMEM((bs, q + 8, BRANCH_W), F32),
            pltpu.VMEM((bs, ML_HEADS, ML_HEADDIM, ML_HEADDIM), F32),
            pltpu.VMEM((bs, ML_HEADS, ML_HEADDIM), F32),
            pltpu.VMEM((bs, 8, 128), F32),
            pltpu.VMEM((bs, q, BRANCH_W), F32),
        ],
        args=[proj, proj, proj, proj32, srow, conv_state, c_state, n_state, m_state,
              w['ml_conv_w'], w['ml_conv_b'], w['ml_wq'], w['ml_wk'], w['ml_wv'], w['bias_col'], w['bias_row'],
              w['ml_norm_w'], w['ml_skip'], consts['tril'], consts['triu']],
        group=functools.partial(_mlstm_group, q=q, valid=valid, lowp=lowp))


def _hgrn_group(ins, outs, scr, *, q, valid, lowp, nlev):
    f_ref, i_ref, q_ref, g_ref, s0_ref, *consts = ins
    y_ref, so_ref = outs
    s_scr, o_scr = scr

    def init(v):
        v[7][...] = v[4][...]

    def final(v):
        v[6][...] = v[7][...]

    return (init, functools.partial(_hgrn_main, q=q, valid=valid, lowp=lowp, nlev=nlev), final,
            (f_ref, i_ref, q_ref, g_ref, s0_ref.at[0], y_ref, so_ref.at[0], s_scr, o_scr),
            tuple(consts))


def _hgrn_main(v, k, *, q, valid, lowp, nlev):
    f_ref, i_ref, q_ref, g_ref, _, y_ref, _, s_scr, o_scr = v
    lb_ref, nw_ref, tl_ref = k
    lb = lb_ref[...]
    fg = lb + (1.0 - lb) * _sigmoid(f_ref[...])
    kk = 1.0 - fg
    qq = _silu(q_ref[...].astype(F32)) * (HG_HEADDIM ** -0.5)
    vv = i_ref[...] if lowp else i_ref[...].astype(F32)
    rcol =lax.broadcasted_iota(jnp.int32, (q, 1), 0)
    if valid < q:
        fg = jnp.where(rcol < valid, fg, 1.0)
        kk = jnp.where(rcol < valid, kk, 0.0)
    lf = jnp.log(fg)

    gcum = _xdot(tl_ref[...], lf, NN, 1, lowp)
    yield

    rt = lax.broadcasted_iota(jnp.int32, (q, q), 0)
    cs = lax.broadcasted_iota(jnp.int32, (q, q), 1)
    rx = rt ^ cs
    att = [None] * HG_HEADS
    for lev in range(nlev):
        m = q >> (lev + 1)
        second = (rcol & m) != 0
        if m >= 4:
            nb = q // (2 * m)
            gb = jnp.broadcast_to(gcum.reshape(nb, 2 * m, BRANCH_W)[:, m - 1:m, :], (nb, 2 * m, BRANCH_W))
            e = jnp.exp(-jnp.abs(gcum - gb.reshape(q, BRANCH_W)))
        elif m == 2:
            r4 = rcol & 3
            e = jnp.where(r4 == 0, pltpu.roll(fg, q - 1, 0),
                          jnp.where(r4 == 1, 1.0, jnp.where(r4 == 2, fg, fg * pltpu.roll(fg, 1, 0))))
        else:
            e = jnp.where(second, fg, 1.0)
        u = jnp.where(second, qq, kk) * e
        if lowp:
            u = u.astype(BF16)
        pair = (rx >> int(math.log2(m))) == 1
        for h in range(HG_HEADS):
            hs = slice(h * HG_HEADDIM, (h + 1) * HG_HEADDIM)
            s = lax.dot_general(u[:, hs], u[:, hs], NT, preferred_element_type=F32)
            att[h] = jnp.where(pair, s, 0.0 if att[h] is None else att[h])
        yield

    qg = qq * jnp.exp(gcum)
    kend = kk * jnp.exp(gcum[q - 1:q, :] - gcum)
    ones = jnp.ones((q, 128), F32)
    for h in range(HG_HEADS):
        hs = slice(h * HG_HEADDIM, (h + 1) * HG_HEADDIM)
        diag = jnp.sum(qq[:, hs] * kk[:, hs], axis=1, keepdims=True)
        a = jnp.where(rt > cs, att[h], jnp.where(rt == cs, diag, 0.0))
        sp = s_scr[h]
        oh = _mm(a, vv[:, hs], lowp) + _mm(qg[:, hs], sp, lowp)
        dec = jnp.exp(_xdot(lf[:, hs], ones, TN, 0, lowp))
        s_scr[h] = sp * dec + _mm(kend[:, hs], vv[:, hs], lowp, TN)
        o_scr[:, hs] = oh * lax.rsqrt(jnp.mean(oh * oh, -1, keepdims=True) + EPS)
        yield

    y_ref[...] = (o_scr[...] * nw_ref[...] * _silu(g_ref[...].astype(F32))).astype(BF16)


def _hgrn(proj, proj32, s_state, layer, w, consts, *, b, q, nc, valid, lowp, bs):
    nlev = int(math.log2(q))
    return _Mixer(
        in_specs=[
            _col_spec(bs, q, COL32_D_F, BRANCH_W),
            _col_spec(bs, q, COL_D_I, BRANCH_W),
            _col_spec(bs, q, COL_D_Q, BRANCH_W),
            _col_spec(bs, q, COL_D_G, BRANCH_W),
            _state_in_spec(layer, bs, (HG_HEADS, HG_HEADDIM, HG_HEADDIM)),
            _const_spec((1, BRANCH_W)),
            _const_spec((1, BRANCH_W)),
            _const_spec((q, q)),
        ],
        out_specs=[
            _y_spec(bs, q),
            _state_out_spec(layer, bs, (HG_HEADS, HG_HEADDIM, HG_HEADDIM)),
        ],
        out_shapes=[
            jax.ShapeDtypeStruct((b, nc * q, BRANCH_W), BF16),
            jax.ShapeDtypeStruct((DEPTH, b, HG_HEADS, HG_HEADDIM, HG_HEADDIM), F32),
        ],
        scratch=[
            pltpu.VMEM((bs, HG_HEADS, HG_HEADDIM, HG_HEADDIM), F32),
            pltpu.VMEM((bs, q, BRANCH_W), F32),
        ],
        args=[proj32, proj, proj, proj, s_state, w['hg_lb'], w['hg_norm_w'], consts['tril']],
        group=functools.partial(_hgrn_group, q=q, valid=valid, lowp=lowp, nlev=nlev))


def _chunk_consts(q):
    r = np.arange(q)
    tril = (r[:, None] >= r[None, :]).astype(np.float32)
    expand = np.zeros((128, BRANCH_W), np.float32)
    for h in range(SSD_HEADS):
        expand[h, h * SSD_HEADDIM:(h + 1) * SSD_HEADDIM] = 1.0
    return {'tril': jnp.asarray(tril), 'triu': jnp.asarray(tril.T), 'expand': jnp.asarray(expand)}


def _cmul(ar, ai, br, bi):
    return ar * br - ai * bi, ar * bi + ai * br


def _layer_params(l, p, hg_lb):
    w_in = p['w_in'][l]
    small = jnp.concatenate([w_in[:, 1280:1288], w_in[:, 3848:3856]], axis=1)
    w_perm = jnp.concatenate([
        w_in[:, 5904:10000],
        w_in[:, 0:512],
        w_in[:, 512:1024],
        w_in[:, 1288:2312],
        w_in[:, 2312:3848],
        w_in[:, 4368:5904],
    ], axis=1).astype(BF16)
    w_perm32 = jnp.concatenate([
        w_in[:, 3856:4368],
        w_in[:, 1024:1280],
        small,
        jnp.zeros((D_MODEL, N_PROJ32 - COL32_SMALL - N_SMALL), F32),
    ], axis=1).astype(BF16)

    def lane_pad(v, off):
        return jnp.zeros((1, 128), F32).at[0, off:off + v.shape[0]].set(v)

    bias_col = (lane_pad(p['ssd_dt_bias'][l], SMALL_DT) + lane_pad(p['ml_i_bias'][l], SMALL_I)
                + lane_pad(p['ml_f_bias'][l], SMALL_F))
    alog_col = lane_pad(p['ssd_A_log'][l], SMALL_DT)

    dt = jnp.exp(p['s5_log_dt'][l])[:, None]
    lr = p['s5_A_re'][l]
    li = p['s5_A_im'][l]
    mag = jnp.exp(lr * dt)
    abr, abi = mag * jnp.cos(li * dt), mag * jnp.sin(li * dt)
    den = lr * lr + li * li
    cr = ((abr - 1.0) * lr + abi * li) / den
    ci = (abi * lr - (abr - 1.0) * li) / den
    bbr, bbi = _cmul(cr[..., None], ci[..., None], p['s5_B_re'][l], p['s5_B_im'][l])
    eye8 = jnp.eye(8, dtype=F32)

    def pack_b(bb):
        return jnp.einsum('jgpc,gh->jgchp', bb.reshape(4, 8, S5_STATE, S5_GROUP), eye8).reshape(4, 128, 512)

    def pack_c(cc):
        return jnp.einsum('jgcp,gh->jgphc', cc.reshape(4, 8, S5_GROUP, S5_STATE), eye8).reshape(4, 512, 128)

    return {
        'norm_w': p['norm_w'][l].reshape(1, D_MODEL),
        'w_in': w_perm, 'w_in32': w_perm32,
        'w_small_t': small.T.astype(BF16),
        'bias_col': bias_col, 'bias_row': bias_col[0, :N_SMALL].reshape(N_SMALL, 1),
        'alog_col': alog_col, 'alog_row': alog_col[0, :N_SMALL].reshape(N_SMALL, 1),
        'ssd_conv_w': p['ssd_conv_w'][l], 'ssd_conv_b': p['ssd_conv_b'][l].reshape(1, SSD_CONV_CH),
        'ssd_D_exp': jnp.repeat(p['ssd_D'][l], SSD_HEADDIM).reshape(1, BRANCH_W),
        'ssd_norm_w': p['ssd_norm_w'][l].reshape(1, BRANCH_W),
        's5_wbr': pack_b(bbr).astype(BF16), 's5_wbi': pack_b(bbi).astype(BF16),
        's5_wcr': pack_c(p['s5_C_re'][l]).astype(BF16), 's5_wci': pack_c(p['s5_C_im'][l]).astype(BF16),
        's5_ar': jnp.broadcast_to(abr.reshape(1, S5_LANES), (8, S5_LANES)),
        's5_ai': jnp.broadcast_to(abi.reshape(1, S5_LANES), (8, S5_LANES)),
        's5_D': p['s5_D'][l].reshape(1, BRANCH_W), 's5_glu_w': p['s5_glu_w'][l].astype(BF16),
        'ml_conv_w': p['ml_conv_w'][l], 'ml_conv_b': p['ml_conv_b'][l].reshape(1, BRANCH_W),
        'ml_wq': p['ml_wq'][l], 'ml_wk': p['ml_wk'][l], 'ml_wv': p['ml_wv'][l],
        'ml_norm_w': p['ml_norm_w'][l].reshape(1, BRANCH_W), 'ml_skip': p['ml_skip'][l].reshape(1, BRANCH_W),
        'hg_lb': hg_lb[l].reshape(1, BRANCH_W), 'hg_norm_w': p['hg_norm_w'][l].reshape(1, BRANCH_W),
        'w_branch': p['w_branch'][l].astype(BF16), 'w_out': p['w_out'][l].astype(BF16),
    }


_MATMUL_WEIGHTS = ('ml_wq', 'ml_wk', 'ml_wv')
S5_CHUNK = 32
MIXER_PLAN = [(('ssd', 'mlstm', 'hgrn'), 2)]
MIXER_PLAN_PADDED = [(('ssd', 'mlstm', 'hgrn'), 8)]


def _run_group(x3, states, layers, fnw, *, q, valid):
    b, lp, _ = x3.shape
    nc = lp // q
    t = b * lp
    lowp = q >= 16
    consts = _chunk_consts(q)
    ssd_conv, ssd_h, s5_re, s5_im, ml_conv, ml_c, ml_n, ml_m, hg_s = states
    s5_re = s5_re.reshape(DEPTH, b, S5_LANES)
    s5_im = s5_im.reshape(DEPTH, b, S5_LANES)
    if valid < q:
        s5_kw = dict(b=b, lp=lp, q=q, valid=valid, bb=min(b, 128))
    else:
        qs = math.gcd(lp, S5_CHUNK)
        s5_kw = dict(b=b, lp=lp, q=qs, valid=qs, bb=8)
    ml_m = jnp.broadcast_to(jnp.pad(ml_m, ((0, 0), (0, 0), (0, 8 - ML_HEADS)))[..., None], (DEPTH, b, 8, 128))
    plan = [(names, math.gcd(b, n)) for names, n in (MIXER_PLAN_PADDED if valid < q else MIXER_PLAN)]
    kw = dict(b=b, q=q, nc=nc, valid=valid, lowp=lowp)
    x = x3.reshape(t, D_MODEL)
    ys, st, st_s5 = {}, {}, None
    for l, w in enumerate(layers):
        if lowp:
            w = dict(w, **{k: w[k].astype(BF16) for k in _MATMUL_WEIGHTS})
        proj, proj32, small_t = _inproj(x, w['norm_w'], w['w_in'], w['w_in32'], w['w_small_t'])
        proj3 = proj.reshape(b, lp, N_PROJ)
        proj32 = proj32.reshape(b, lp, N_PROJ32)
        srow = small_t.reshape(N_SMALL, b, lp).transpose(1, 0, 2)
        mixers = {
            'ssd': lambda n: _ssd(proj3, proj32, srow, ssd_conv, ssd_h, l, w, consts, bs=n, **kw),
            'mlstm': lambda n: _mlstm(proj3, proj32, srow, ml_conv, ml_c, ml_n, ml_m, l, w, consts, bs=n, **kw),
            'hgrn': lambda n: _hgrn(proj3, proj32, hg_s, l, w, consts, bs=n, **kw),
        }
        for names, n in plan:
            outs = _launch([mixers[k](n) for k in names], None if l == 0 else [st[k] for k in names],
                           grid=(b // n, nc), bs=n, nc=nc, name='_'.join(names))
            for k, o in zip(names, outs):
                ys[k], st[k] = o[0], list(o[1:])
        yb, *st_s5 = _s5(proj, s5_re, s5_im, l, w, st_s5, **s5_kw)
        ya, yc, yd = (ys[k].reshape(t, BRANCH_W) for k in ('ssd', 'mlstm', 'hgrn'))
        x = _merge(proj, (ya, yb, yc, yd), x, w['w_branch'], w['w_out'], fnw, final=(l == DEPTH - 1))
    n_ssd_conv, n_ssd_h = st['ssd']
    n_re, n_im = (s.reshape(DEPTH, b, S5_GROUPS, S5_STATE) for s in st_s5)
    n_ml_conv, n_c, n_n, n_m = st['mlstm']
    new_states = (n_ssd_conv, n_ssd_h, n_re, n_im, n_ml_conv, n_c, n_n, n_m[:, :, :ML_HEADS, 0], st['hgrn'][0])
    return x.reshape(b, lp, D_MODEL), new_states


SAMPLE_PAD = 8


def kernel(x_prompt, x_sample, state_ssd_conv, state_ssd, state_s5_re, state_s5_im, state_mlstm_conv, state_mlstm_C, state_mlstm_n, state_mlstm_m, state_hgrn, norm_w, w_in, ssd_conv_w, ssd_conv_b, ssd_dt_bias, ssd_A_log, ssd_D, ssd_norm_w, s5_A_re, s5_A_im, s5_B_re, s5_B_im, s5_C_re, s5_C_im, s5_D, s5_log_dt, s5_glu_w, ml_conv_w, ml_conv_b, ml_wq, ml_wk, ml_wv, ml_i_bias, ml_f_bias, ml_norm_w, ml_skip, hg_lb_logits, hg_norm_w, w_branch, w_out, final_norm_w):
    p = {'norm_w': norm_w, 'w_in': w_in,
         'ssd_conv_w': ssd_conv_w, 'ssd_conv_b': ssd_conv_b, 'ssd_dt_bias': ssd_dt_bias,
         'ssd_A_log': ssd_A_log, 'ssd_D': ssd_D, 'ssd_norm_w': ssd_norm_w,
         's5_A_re': s5_A_re, 's5_A_im': s5_A_im, 's5_B_re': s5_B_re, 's5_B_im': s5_B_im,
         's5_C_re': s5_C_re, 's5_C_im': s5_C_im, 's5_D': s5_D, 's5_log_dt': s5_log_dt, 's5_glu_w': s5_glu_w,
         'ml_conv_w': ml_conv_w, 'ml_conv_b': ml_conv_b, 'ml_wq': ml_wq, 'ml_wk': ml_wk, 'ml_wv': ml_wv,
         'ml_i_bias': ml_i_bias, 'ml_f_bias': ml_f_bias, 'ml_norm_w': ml_norm_w, 'ml_skip': ml_skip,
         'hg_norm_w': hg_norm_w, 'w_branch': w_branch, 'w_out': w_out}
    lb_cum = jnp.cumsum(jax.nn.softmax(hg_lb_logits, axis=0), axis=0)
    hg_lb = lb_cum - lb_cum[0]
    layers = [_layer_params(l, p, hg_lb) for l in range(DEPTH)]
    fnw = final_norm_w.reshape(1, D_MODEL)

    sample_states = (state_ssd_conv, state_ssd, state_s5_re, state_s5_im, state_mlstm_conv,
                     state_mlstm_C, state_mlstm_n, state_mlstm_m, state_hgrn)
    bp, lp_, _ = x_prompt.shape
    prompt_states = tuple(jnp.zeros((DEPTH, bp) + s.shape[2:], F32) for s in sample_states)
    q_prompt = math.gcd(lp_, 128)
    y_prompt, new_p = _run_group(x_prompt, prompt_states, layers, fnw, q=q_prompt, valid=q_prompt)

    ls = x_sample.shape[1]
    xs_pad = jnp.pad(x_sample, ((0, 0), (0, SAMPLE_PAD - ls), (0, 0)))
    y_sample, new_s = _run_group(xs_pad, sample_states, layers, fnw, q=SAMPLE_PAD, valid=ls)
    y_sample = y_sample[:, :ls]

    out = [y_prompt, y_sample]
    for ps, ss in zip(new_p, new_s):
        out += [ps, ss]
    return tuple(out)
```

```python
import functools
import math
from typing import Callable, NamedTuple

import numpy as np
import jax
import jax.numpy as jnp
from jax import lax
from jax.experimental import pallas as pl
from jax.experimental.pallas import tpu as pltpu

F32 = jnp.float32
BF16 = jnp.bfloat16

D_MODEL = 1024
DEPTH = 2
BRANCH_W = 512
CONV_W = 4
EPS = 1e-6
SSD_HEADS = 8
SSD_HEADDIM = 64
SSD_STATE = 64
SSD_GROUPS = 2
SSD_CONV_CH = 768
S5_GROUPS = 32
S5_GROUP = 16
S5_STATE = 64
S5_LANES = S5_GROUPS * S5_STATE
ML_HEADS = 4
ML_HEADDIM = 128
HG_HEADS = 4
HG_HEADDIM = 128
N_BRANCH = 4

NEG = -1e30

N_PROJ = 9216
COL_MERGE = 0
COL_A_Z = 4096
COL_A_X = 4608
COL_B_U = 5120
COL_B_GATE = 5632
COL_C_X = 6144
COL_C_Z = 6656
COL_C_O = 7168
COL_D_I = 7680
COL_D_Q = 8192
COL_D_G = 8704
N_PROJ32 = 896
COL32_D_F = 0
COL32_A_BC = 512
COL32_SMALL = 768
N_SMALL = 16
SMALL_DT = 0
SMALL_I = 8
SMALL_F = 12

NN = (((1,), (0,)), ((), ()))
NT = (((1,), (1,)), ((), ()))
TN = (((0,), (0,)), ((), ()))


def _mm(a, b, lowp, dims=NN):
    if lowp:
        a = a.astype(BF16)
        b = b.astype(BF16)
    return lax.dot_general(a, b, dims, preferred_element_type=F32)


def _split3(a):
    hi = a.astype(BF16)
    r = a - hi.astype(F32)
    mid = r.astype(BF16)
    lo = (r - mid.astype(F32)).astype(BF16)
    return hi, mid, lo


def _xdot(a, b, dims, split, lowp):
    if not lowp and min((b if split == 0 else a).shape) < 16:
        return lax.dot_general(a, b, dims, precision=lax.Precision.HIGHEST,
                               preferred_element_type=F32)
    if split == 0:
        other = b.astype(BF16)
        return sum(lax.dot_general(p, other, dims, preferred_element_type=F32) for p in _split3(a))
    other = a.astype(BF16)
    return sum(lax.dot_general(other, p, dims, preferred_element_type=F32) for p in _split3(b))


def _sigmoid(x):
    return jax.nn.sigmoid(x)


def _silu(x):
    return x * jax.nn.sigmoid(x)


def _softplus(x):
    return jnp.maximum(x, 0.0) + jnp.log1p(jnp.exp(-jnp.abs(x)))


def _conv_chunk(xp_scr, cw_ref, cb_ref, q, valid):
    cw = cw_ref[...]
    conv = (cb_ref[...]
            + cw[0:1] * xp_scr[5:5 + q, :]
            + cw[1:2] * xp_scr[6:6 + q, :]
            + cw[2:3] * xp_scr[7:7 + q, :]
            + cw[3:4] * xp_scr[8:8 + q, :])
    carry = xp_scr[5 + valid:8 + valid, :]
    xp_scr[5:8, :] = carry
    return conv, carry


def _inproj_kernel(x_ref, nw_ref, w_ref, w32_ref, wst_ref, o_ref, o32_ref, st_ref, xn_scr):
    @pl.when(pl.program_id(1) == 0)
    def _():
        x = x_ref[...]
        xn = x * lax.rsqrt(jnp.mean(x * x, -1, keepdims=True) + EPS) * nw_ref[...]
        xb = xn.astype(BF16)
        xn_scr[...] = xb
        o32_ref[...] = jnp.dot(xb, w32_ref[...], preferred_element_type=F32)
        st_ref[...] = lax.dot_general(wst_ref[...], xb, NT, preferred_element_type=F32)

    o_ref[...] = jnp.dot(xn_scr[...], w_ref[...], preferred_element_type=F32).astype(BF16)


def _inproj(x, nw, w, w32, wst):
    t = x.shape[0]
    tm = min(1024, t)
    tn = N_PROJ // 4
    return pl.pallas_call(
        _inproj_kernel,
        grid=(t // tm, N_PROJ // tn),
        in_specs=[
            pl.BlockSpec((tm, D_MODEL), lambda i, j: (i, 0)),
            pl.BlockSpec((1, D_MODEL), lambda i, j: (0, 0)),
            pl.BlockSpec((D_MODEL, tn), lambda i, j: (0, j)),
            pl.BlockSpec((D_MODEL, N_PROJ32), lambda i, j: (0, 0)),
            pl.BlockSpec((N_SMALL, D_MODEL), lambda i, j: (0, 0)),
        ],
        out_specs=[
            pl.BlockSpec((tm, tn), lambda i, j: (i, j)),
            pl.BlockSpec((tm, N_PROJ32), lambda i, j: (i, 0)),
            pl.BlockSpec((N_SMALL, tm), lambda i, j: (0, i)),
        ],
        out_shape=[
            jax.ShapeDtypeStruct((t, N_PROJ), BF16),
            jax.ShapeDtypeStruct((t, N_PROJ32), F32),
            jax.ShapeDtypeStruct((N_SMALL, t), F32),
        ],
        scratch_shapes=[pltpu.VMEM((tm, D_MODEL), BF16)],
        compiler_params=pltpu.CompilerParams(
            dimension_semantics=("arbitrary", "arbitrary"),
            vmem_limit_bytes=48 * 1024 * 1024),
        name="inproj",
    )(x, nw, w, w32, wst)


def _merge_kernel(g_ref, ya_ref, yb_ref, yc_ref, yd_ref, x_ref, wb_ref, wo_ref, fnw_ref, o_ref, *, final):
    acc = None
    for i, y_ref in enumerate((ya_ref, yb_ref, yc_ref, yd_ref)):
        gate = _sigmoid(g_ref[:, i * D_MODEL:(i + 1) * D_MODEL].astype(F32))
        term = gate * jnp.dot(y_ref[...], wb_ref[i], preferred_element_type=F32)
        acc = term if acc is None else acc + term
    out = x_ref[...] + jnp.dot(acc.astype(BF16), wo_ref[...], preferred_element_type=F32)
    if final:
        out = out * lax.rsqrt(jnp.mean(out * out, -1, keepdims=True) + EPS) * fnw_ref[...]
    o_ref[...] = out


def _merge(proj, ys, x, wb, wo, fnw, final):
    t = x.shape[0]
    tm = min(512, t)
    yspec = pl.BlockSpec((tm, BRANCH_W), lambda i: (i, 0))
    return pl.pallas_call(
        functools.partial(_merge_kernel, final=final),
        grid=(t // tm,),
        in_specs=[
            pl.BlockSpec((tm, N_BRANCH * D_MODEL), lambda i: (i, COL_MERGE // (N_BRANCH * D_MODEL))),
            yspec, yspec, yspec, yspec,
            pl.BlockSpec((tm, D_MODEL), lambda i: (i, 0)),
            pl.BlockSpec((N_BRANCH, BRANCH_W, D_MODEL), lambda i: (0, 0, 0)),
            pl.BlockSpec((D_MODEL, D_MODEL), lambda i: (0, 0)),
            pl.BlockSpec((1, D_MODEL), lambda i: (0, 0)),
        ],
        out_specs=pl.BlockSpec((tm, D_MODEL), lambda i: (i, 0)),
        out_shape=jax.ShapeDtypeStruct((t, D_MODEL), F32),
        compiler_params=pltpu.CompilerParams(
            dimension_semantics=("arbitrary",),
            vmem_limit_bytes=48 * 1024 * 1024),
        name="merge",
    )(proj, *ys, x, wb, wo, fnw)


def _col_spec(bs, q, col, width):
    return pl.BlockSpec((bs, q, width), lambda b, c: (b, c, col // width))


def _row_spec(bs, q):
    return pl.BlockSpec((bs, N_SMALL, q), lambda b, c: (b, 0, c))


def _const_spec(shape):
    nd = len(shape)
    return pl.BlockSpec(shape, lambda b, c: (0,) * nd)


def _state_in_spec(layer, bs, tail):
    nd = len(tail)
    return pl.BlockSpec((1, bs) + tail, lambda b, c: (layer, b) + (0,) * nd)


def _state_out_spec(layer, bs, tail):
    nd = len(tail)
    return pl.BlockSpec((1, bs) + tail, lambda b, c: (layer, b) + (0,) * nd)


def _y_spec(bs, q):
    return pl.BlockSpec((bs, q, BRANCH_W), lambda b, c: (b, c, 0))


class _Mixer(NamedTuple):
    in_specs: list
    args: list
    out_specs: list
    out_shapes: list
    scratch: list
    group: Callable


_DONE = object()
_MIXER_PARAMS = pltpu.CompilerParams(
    dimension_semantics=("arbitrary", "arbitrary"),
    vmem_limit_bytes=48 * 1024 * 1024)


def _launch(mixers, prevs, *, grid, bs, nc, name):
    n_in = [len(m.in_specs) for m in mixers]
    n_out = [len(m.out_specs) for m in mixers]
    n_scr = [len(m.scratch) for m in mixers]
    tot_in, tot_out = sum(n_in), sum(n_out)
    state_idx = [sum(n_out[:i]) + k for i in range(len(mixers)) for k in range(1, n_out[i])]
    n_alias = 0 if prevs is None else len(state_idx)

    def kern(*refs):
        ins, outs, scr = refs[:tot_in], refs[tot_in + n_alias:tot_in + n_alias + tot_out], refs[tot_in + n_alias + tot_out:]
        groups = []
        for i, m in enumerate(mixers):
            groups.append(m.group(ins[sum(n_in[:i]):sum(n_in[:i + 1])],
                                  outs[sum(n_out[:i]):sum(n_out[:i + 1])],
                                  scr[sum(n_scr[:i]):sum(n_scr[:i + 1])]))
        c = pl.program_id(1)
        views = [[[r.at[s] for r in g[3]] for s in range(bs)] for g in groups]

        @pl.when(c == 0)
        def _():
            for g, v in zip(groups, views):
                for s in range(bs):
                    g[0](v[s])

        active = [g[1](v[s], g[4]) for s in range(bs) for g, v in zip(groups, views)]
        while active:
            active = [gen for gen in active if next(gen, _DONE) is not _DONE]

        @pl.when(c == nc - 1)
        def _():
            for g, v in zip(groups, views):
                for s in range(bs):
                    g[2](v[s])

    in_specs = [sp for m in mixers for sp in m.in_specs] + [pl.BlockSpec(memory_space=pl.ANY)] * n_alias
    args = [a for m in mixers for a in m.args]
    if prevs is not None:
        args += [a for p in prevs for a in p]
    outs = pl.pallas_call(
        kern,
        grid=grid,
        in_specs=in_specs,
        out_specs=[sp for m in mixers for sp in m.out_specs],
        out_shape=[sh for m in mixers for sh in m.out_shapes],
        scratch_shapes=[sc for m in mixers for sc in m.scratch],
        input_output_aliases={tot_in + k: idx for k, idx in enumerate(state_idx)} if n_alias else {},
        compiler_params=_MIXER_PARAMS,
        name=name,
    )(*args)
    return [outs[sum(n_out[:i]):sum(n_out[:i + 1])] for i in range(len(mixers))]


def _ssd_group(ins, outs, scr, *, q, valid, lowp):
    z_ref, x_ref, bc_ref, sc_ref, sr_ref, conv0_ref, h0_ref, *consts = ins
    y_ref, convo_ref, ho_ref = outs
    xp_scr, h_scr, y_scr = scr

    def init(v):
        v[9][5:8, :] = v[5][...]
        v[10][...] = v[6][...]

    def final(v):
        v[8][...] = v[10][...]

    return (init, functools.partial(_ssd_main, q=q, valid=valid, lowp=lowp), final,
            (z_ref, x_ref, bc_ref, sc_ref, sr_ref, conv0_ref.at[0], h0_ref.at[0],
             y_ref, ho_ref.at[0], xp_scr, h_scr, y_scr, convo_ref.at[0]),
            tuple(consts))


def _ssd_main(v, k, *, q, valid, lowp):
    z_ref, x_ref, bc_ref, sc_ref, sr_ref, _, _, y_ref, _, xp_scr, h_scr, y_scr, convo_ref = v
    cw_ref, cb_ref, bcol_ref, brow_ref, alc_ref, alr_ref, dexp_ref, nw_ref, e_ref, tl_ref, tu_ref = k
    xp_scr[8:8 + q, 0:BRANCH_W] = x_ref[...].astype(F32)
    xp_scr[8:8 + q, BRANCH_W:SSD_CONV_CH] = bc_ref[...]
    conv, carry = _conv_chunk(xp_scr, cw_ref, cb_ref, q, valid)
    convo_ref[...] = carry
    yield
    xbc = _silu(conv)
    xs = xbc[:, 0:BRANCH_W]
    bm = xbc[:, BRANCH_W:BRANCH_W + 128]
    cm = xbc[:, BRANCH_W + 128:BRANCH_W + 256]

    dt_c = _softplus(sc_ref[...] + bcol_ref[...])
    a_c = dt_c * (-jnp.exp(alc_ref[...]))
    dt_r = _softplus(sr_ref[...] + brow_ref[...])
    a_r = dt_r * (-jnp.exp(alr_ref[...]))
    if valid < q:
        tcol = lax.broadcasted_iota(jnp.int32, (q, 1), 0)
        trow = lax.broadcasted_iota(jnp.int32, (1, q), 1)
        a_c = jnp.where(tcol < valid, a_c, 0.0)
        dt_c = jnp.where(tcol < valid, dt_c, 0.0)
        a_r = jnp.where(trow < valid, a_r, 0.0)

    tl = tl_ref[...]
    acum_c = _xdot(tl, a_c, NN, 1, lowp)
    acum_r = _xdot(a_r, tu_ref[...], NN, 0, lowp)
    e = e_ref[...]
    dt_e = _xdot(dt_c, e, NN, 0, lowp)
    acum_e = _xdot(acum_c, e, NN, 0, lowp)
    xdt = xs * dt_e
    xend = xdt * jnp.exp(acum_e[q - 1:q, :] - acum_e)
    eac = jnp.exp(acum_e)
    dec_last = jnp.exp(acum_c[q - 1:q, :])
    yield

    tri = (lax.broadcasted_iota(jnp.int32, (q, q), 0) >= lax.broadcasted_iota(jnp.int32, (q, q), 1))
    rpg = SSD_HEADS // SSD_GROUPS
    for g in range(SSD_GROUPS):
        bg = bm[:, g * SSD_STATE:(g + 1) * SSD_STATE]
        cg = cm[:, g * SSD_STATE:(g + 1) * SSD_STATE]
        cb = _mm(cg, bg, lowp, NT)
        for r in range(rpg):
            h = g * rpg + r
            hs = slice(h * SSD_HEADDIM, (h + 1) * SSD_HEADDIM)
            seg = acum_c[:, h:h + 1] - acum_r[h:h + 1, :]
            m = cb * jnp.exp(jnp.where(tri, seg, NEG))
            hp = h_scr[h]
            yh = _mm(m, xdt[:, hs], lowp) + _mm(cg, hp, lowp, NT) * eac[:, hs]
            y_scr[:, hs] = yh
            h_scr[h] = hp * dec_last[:, h:h + 1] + _mm(xend[:, hs], bg, lowp, TN)
            yield

    y = y_scr[...] + xs * dexp_ref[...]
    y = y * _silu(z_ref[...].astype(F32))
    y = y * lax.rsqrt(jnp.mean(y * y, -1, keepdims=True) + EPS) * nw_ref[...]
    y_ref[...] = y.astype(BF16)


def _ssd(proj, proj32, srow, conv_state, h_state, layer, w, consts, *, b, q, nc, valid, lowp, bs):
    return _Mixer(
        in_specs=[
            _col_spec(bs, q, COL_A_Z, BRANCH_W),
            _col_spec(bs, q, COL_A_X, BRANCH_W),
            _col_spec(bs, q, COL32_A_BC, 256),
            _col_spec(bs, q, COL32_SMALL, 128),
            _row_spec(bs, q),
            _state_in_spec(layer, bs, (CONV_W - 1, SSD_CONV_CH)),
            _state_in_spec(layer, bs, (SSD_HEADS, SSD_HEADDIM, SSD_STATE)),
            _const_spec((CONV_W, SSD_CONV_CH)),
            _const_spec((1, SSD_CONV_CH)),
            _const_spec((1, 128)),
            _const_spec((N_SMALL, 1)),
            _const_spec((1, 128)),
            _const_spec((N_SMALL, 1)),
            _const_spec((1, BRANCH_W)),
            _const_spec((1, BRANCH_W)),
            _const_spec((128, BRANCH_W)),
            _const_spec((q, q)),
            _const_spec((q, q)),
        ],
        out_specs=[
            _y_spec(bs, q),
            _state_out_spec(layer, bs, (CONV_W - 1, SSD_CONV_CH)),
            _state_out_spec(layer, bs, (SSD_HEADS, SSD_HEADDIM, SSD_STATE)),
        ],
        out_shapes=[
            jax.ShapeDtypeStruct((b, nc * q, BRANCH_W), BF16),
            jax.ShapeDtypeStruct((DEPTH, b, CONV_W - 1, SSD_CONV_CH), F32),
            jax.ShapeDtypeStruct((DEPTH, b, SSD_HEADS, SSD_HEADDIM, SSD_STATE), F32),
        ],
        scratch=[
            pltpu.VMEM((bs, q + 8, SSD_CONV_CH), F32),
            pltpu.VMEM((bs, SSD_HEADS, SSD_HEADDIM, SSD_STATE), F32),
            pltpu.VMEM((bs, q, BRANCH_W), F32),
        ],
        args=[proj, proj, proj32, proj32, srow, conv_state, h_state,
              w['ssd_conv_w'], w['ssd_conv_b'], w['bias_col'], w['bias_row'], w['alog_col'], w['alog_row'],
              w['ssd_D_exp'], w['ssd_norm_w'], consts['expand'], consts['tril'], consts['triu']],
        group=functools.partial(_ssd_group, q=q, valid=valid, lowp=lowp))


def _s5_kernel(u_ref, gate_ref, hr0_ref, hi0_ref, perm_ref, permt_ref, ar_ref, ai_ref,
               wbr_ref, wbi_ref, wcr_ref, wci_ref, d_ref, glu_ref,
               y_ref, hro_ref, hio_ref,
               hr_scr, hi_scr, cr_scr, ci_scr, *, q, nc, ngrp):
    c = pl.program_id(1)

    @pl.when(c == 0)
    def _():
        cr_scr[...] = hr0_ref[0]
        ci_scr[...] = hi0_ref[0]

    rows_in = perm_ref.shape[1]
    u = u_ref[...].reshape(rows_in, BRANCH_W)
    u_tm = jnp.dot(perm_ref[...], u, preferred_element_type=F32).astype(BF16)
    nblk = S5_LANES // 512
    for j in range(nblk):
        uj = u_tm[:, j * 128:(j + 1) * 128]
        hr_scr[:, j * 512:(j + 1) * 512] = jnp.dot(uj, wbr_ref[j], preferred_element_type=F32)
        hi_scr[:, j * 512:(j + 1) * 512] = jnp.dot(uj, wbi_ref[j], preferred_element_type=F32)

    half = S5_LANES // 2
    for hf in range(2):
        sl = slice(hf * half, (hf + 1) * half)
        ar = ar_ref[:, sl]
        ai = ai_ref[:, sl]

        def grp_body(g, carry, sl=sl, ar=ar, ai=ai):
            s0 = pl.multiple_of(g * 8, 8)

            def t_body(t, h):
                r0 = pl.multiple_of(g * (8 * q) + t * 8, 8)
                hr, hi = h
                nr = ar * hr - ai * hi + hr_scr[pl.ds(r0, 8), sl]
                ni = ar * hi + ai * hr + hi_scr[pl.ds(r0, 8), sl]
                hr_scr[pl.ds(r0, 8), sl] = nr
                hi_scr[pl.ds(r0, 8), sl] = ni
                return nr, ni

            h = lax.fori_loop(0, q, t_body, (cr_scr[pl.ds(s0, 8), sl], ci_scr[pl.ds(s0, 8), sl]),
                              unroll=min(q, 4))
            cr_scr[pl.ds(s0, 8), sl] = h[0]
            ci_scr[pl.ds(s0, 8), sl] = h[1]
            return carry

        lax.fori_loop(0, ngrp, grp_body, 0)

    parts = []
    for j in range(nblk):
        sl = slice(j * 512, (j + 1) * 512)
        parts.append(jnp.dot(hr_scr[:, sl].astype(BF16), wcr_ref[j], preferred_element_type=F32)
                     - jnp.dot(hi_scr[:, sl].astype(BF16), wci_ref[j], preferred_element_type=F32))
    y_tm = jnp.concatenate(parts, axis=1)
    y = _xdot(permt_ref[...], y_tm, NN, 1, True) + u.astype(F32) * d_ref[...]
    g = jax.nn.gelu(y)
    y = g * _sigmoid(jnp.dot(g.astype(BF16), glu_ref[...], preferred_element_type=F32))
    y = y * _silu(gate_ref[...].reshape(rows_in, BRANCH_W).astype(F32))
    y_ref[...] = y.reshape(y_ref.shape).astype(BF16)

    @pl.when(c == nc - 1)
    def _():
        hro_ref[0] = cr_scr[...]
        hio_ref[0] = ci_scr[...]


def _s5_perm(bb, lp, valid):
    perm = np.zeros((bb * valid, bb * lp), np.float32)
    for g in range(bb // 8):
        for t in range(valid):
            for s in range(8):
                perm[g * 8 * valid + t * 8 + s, (g * 8 + s) * lp + t] = 1.0
    return jnp.asarray(perm, BF16), jnp.asarray(perm.T, BF16)


def _s5(proj, hr_state, hi_state, layer, w, prev, *, b, lp, q, valid, bb):
    nc = lp // q
    perm, permt = _s5_perm(bb, q, valid)
    rows = bb * valid
    if nc == 1:
        def col_spec(col):
            return pl.BlockSpec((bb * q, BRANCH_W), lambda i, c: (i, col // BRANCH_W))
        src = proj
        y_spec = pl.BlockSpec((bb * q, BRANCH_W), lambda i, c: (i, 0))
        y_shape = jax.ShapeDtypeStruct((b * lp, BRANCH_W), BF16)
    else:
        def col_spec(col):
            return pl.BlockSpec((bb, q, BRANCH_W), lambda i, c: (i, c, col // BRANCH_W))
        src = proj.reshape(b, lp, N_PROJ)
        y_spec = pl.BlockSpec((bb, q, BRANCH_W), lambda i, c: (i, c, 0))
        y_shape = jax.ShapeDtypeStruct((b, lp, BRANCH_W), BF16)
    st_in = pl.BlockSpec((1, bb, S5_LANES), lambda i, c: (layer, i, 0))
    st_out = st_in
    kern = functools.partial(_s5_kernel, q=valid, nc=nc, ngrp=bb // 8)
    n_in = 14
    alias_kw = {}
    if prev is not None:
        base = kern

        def kern(*refs):
            return base(*refs[:n_in], *refs[n_in + 2:])

        alias_kw = dict(input_output_aliases={n_in: 1, n_in + 1: 2})
    y, hr, hi = pl.pallas_call(
        kern,
        grid=(b // bb, nc),
        **alias_kw,
        in_specs=[
            col_spec(COL_B_U),
            col_spec(COL_B_GATE),
            st_in, st_in,
            _const_spec(perm.shape),
            _const_spec(permt.shape),
            _const_spec((8, S5_LANES)),
            _const_spec((8, S5_LANES)),
            _const_spec((4, 128, 512)),
            _const_spec((4, 128, 512)),
            _const_spec((4, 512, 128)),
            _const_spec((4, 512, 128)),
            _const_spec((1, BRANCH_W)),
            _const_spec((BRANCH_W, BRANCH_W)),
        ] + [pl.BlockSpec(memory_space=pl.ANY)] * (0 if prev is None else 2),
        out_specs=[y_spec, st_out, st_out],
        out_shape=[
            y_shape,
            jax.ShapeDtypeStruct((DEPTH, b, S5_LANES), F32),
            jax.ShapeDtypeStruct((DEPTH, b, S5_LANES), F32),
        ],
        scratch_shapes=[
            pltpu.VMEM((rows, S5_LANES), F32),
            pltpu.VMEM((rows, S5_LANES), F32),
            pltpu.VMEM((bb, S5_LANES), F32),
            pltpu.VMEM((bb, S5_LANES), F32),
        ],
        compiler_params=_MIXER_PARAMS,
        name="s5",
    )(src, src, hr_state, hi_state, perm, permt, w['s5_ar'], w['s5_ai'],
      w['s5_wbr'], w['s5_wbi'], w['s5_wcr'], w['s5_wci'], w['s5_D'], w['s5_glu_w'], *(prev or ()))
    return y.reshape(b * lp, BRANCH_W), hr, hi


def _mlstm_group(ins, outs, scr, *, q, valid, lowp):
    x_ref, z_ref, o_ref, sc_ref, sr_ref, conv0_ref, c0_ref, n0_ref, m0_ref, *consts = ins
    y_ref, convo_ref, co_ref, no_ref, mo_ref = outs
    xp_scr, c_scr, n_scr, m_scr, h_scr = scr

    def init(v):
        v[13][5:8, :] = v[5][...]
        v[14][...] = v[6][...]
        v[15][...] = v[7][...]
        v[16][...] = v[8][...]

    def final(v):
        v[10][...] = v[14][...]
        v[11][...] = v[15][...]
        v[12][...] = v[16][...]

    return (init, functools.partial(_mlstm_main, q=q, valid=valid, lowp=lowp), final,
            (x_ref, z_ref, o_ref, sc_ref, sr_ref, conv0_ref.at[0], c0_ref.at[0], n0_ref.at[0], m0_ref.at[0],
             y_ref, co_ref.at[0], no_ref.at[0], mo_ref.at[0], xp_scr, c_scr, n_scr, m_scr, h_scr,
             convo_ref.at[0]),
            tuple(consts))


def _mlstm_main(v, k, *, q, valid, lowp):
    (x_ref, z_ref, o_ref, sc_ref, sr_ref, _, _, _, _, y_ref, _, _, _,
     xp_scr, c_scr, n_scr, m_scr, h_scr, convo_ref) = v
    cw_ref, cb_ref, wq_ref, wk_ref, wv_ref, bcol_ref, brow_ref, nw_ref, skip_ref, tl_ref, tu_ref = k
    x = x_ref[...].astype(F32)
    xp_scr[8:8 + q, :] = x
    conv, carry = _conv_chunk(xp_scr, cw_ref, cb_ref, q, valid)
    convo_ref[...] = carry
    yield
    xc = _silu(conv)

    pre_c = sc_ref[...] + bcol_ref[...]
    pre_r = sr_ref[...] + brow_ref[...]
    ig_c = pre_c
    lf_c = -_softplus(-pre_c)
    ig_r = pre_r
    lf_r = -_softplus(-pre_r)
    if valid < q:
        tcol = lax.broadcasted_iota(jnp.int32, (q, 1), 0)
        trow = lax.broadcasted_iota(jnp.int32, (1, q), 1)
        lf_c = jnp.where(tcol < valid, lf_c, 0.0)
        ig_c = jnp.where(tcol < valid, ig_c, NEG)
        lf_r = jnp.where(trow < valid, lf_r, 0.0)
        ig_r = jnp.where(trow < valid, ig_r, NEG)
    b_c = _xdot(tl_ref[...], lf_c, NN, 1, lowp)
    b_r = _xdot(lf_r, tu_ref[...], NN, 0, lowp)
    yield

    tri = (lax.broadcasted_iota(jnp.int32, (q, q), 0) >= lax.broadcasted_iota(jnp.int32, (q, q), 1))
    o_all = o_ref[...].astype(F32)
    for h in range(ML_HEADS):
        hs = slice(h * ML_HEADDIM, (h + 1) * ML_HEADDIM)
        xh = xc[:, hs]
        qh = _mm(xh, wq_ref[h], lowp)
        kh = _mm(xh, wk_ref[h], lowp) * (ML_HEADDIM ** -0.5)
        vh = _mm(x[:, hs], wv_ref[h], lowp)
        yield
        bc = b_c[:, SMALL_F + h:SMALL_F + h + 1]
        br = b_r[SMALL_F + h:SMALL_F + h + 1, :]
        ic = ig_c[:, SMALL_I + h:SMALL_I + h + 1]
        ir = ig_r[SMALL_I + h:SMALL_I + h + 1, :]
        mp = m_scr[h:h + 1, 0:1]
        dlog = jnp.where(tri, bc - br + ir, NEG)
        inter = bc + mp
        m_t = jnp.maximum(inter, jnp.max(dlog, axis=1, keepdims=True))
        wgt = jnp.exp(dlog - m_t)
        yield
        s = _mm(qh, kh, lowp, NT) * wgt
        scale = jnp.exp(inter - m_t)
        cp = c_scr[h]
        npv = n_scr[h:h + 1, :]
        num = _mm(s, vh, lowp) + scale * _mm(qh, cp, lowp)
        dot = jnp.sum(s, axis=1, keepdims=True) + scale * jnp.sum(qh * npv, axis=1, keepdims=True)
        hh = num / jnp.maximum(jnp.abs(dot), jnp.exp(-m_t))
        yield
        m_new = m_t[q - 1:q, :]
        b_last = bc[q - 1:q, :]
        w_end = jnp.exp(b_last - bc + ic - m_new)
        cs = jnp.exp(b_last + mp - m_new)
        kw = kh * w_end
        c_scr[h] = cs * cp + _mm(kw, vh, lowp, TN)
        n_scr[h:h + 1, :] = cs * npv + jnp.sum(kw, axis=0, keepdims=True)
        m_scr[h:h + 1, :] = jnp.broadcast_to(m_new, (1, 128))
        yield
        hh = hh * _sigmoid(o_all[:, hs])
        hc = hh - jnp.mean(hh, -1, keepdims=True)
        h_scr[:, hs] = hc * lax.rsqrt(jnp.mean(hc * hc, -1, keepdims=True) + EPS)
        yield

    y = h_scr[...] * nw_ref[...] + skip_ref[...] * xc
    y_ref[...] = (y * _silu(z_ref[...].astype(F32))).astype(BF16)


def _mlstm(proj, proj32, srow, conv_state, c_state, n_state, m_state, layer, w, consts,
           *, b, q, nc, valid, lowp, bs):
    return _Mixer(
        in_specs=[
            _col_spec(bs, q, COL_C_X, BRANCH_W),
            _col_spec(bs, q, COL_C_Z, BRANCH_W),
            _col_spec(bs, q, COL_C_O, BRANCH_W),
            _col_spec(bs, q, COL32_SMALL, 128),
            _row_spec(bs, q),
            _state_in_spec(layer, bs, (CONV_W - 1, BRANCH_W)),
            _state_in_spec(layer, bs, (ML_HEADS, ML_HEADDIM, ML_HEADDIM)),
            _state_in_spec(layer, bs, (ML_HEADS, ML_HEADDIM)),
            _state_in_spec(layer, bs, (8, 128)),
            _const_spec((CONV_W, BRANCH_W)),
            _const_spec((1, BRANCH_W)),
            _const_spec((ML_HEADS, ML_HEADDIM, ML_HEADDIM)),
            _const_spec((ML_HEADS, ML_HEADDIM, ML_HEADDIM)),
            _const_spec((ML_HEADS, ML_HEADDIM, ML_HEADDIM)),
            _const_spec((1, 128)),
            _const_spec((N_SMALL, 1)),
            _const_spec((1, BRANCH_W)),
            _const_spec((1, BRANCH_W)),
            _const_spec((q, q)),
            _const_spec((q, q)),
        ],
        out_specs=[
            _y_spec(bs, q),
            _state_out_spec(layer, bs, (CONV_W - 1, BRANCH_W)),
            _state_out_spec(layer, bs, (ML_HEADS, ML_HEADDIM, ML_HEADDIM)),
            _state_out_spec(layer, bs, (ML_HEADS, ML_HEADDIM)),
            _state_out_spec(layer, bs, (8, 128)),
        ],
        out_shapes=[
            jax.ShapeDtypeStruct((b, nc * q, BRANCH_W), BF16),
            jax.ShapeDtypeStruct((DEPTH, b, CONV_W - 1, BRANCH_W), F32),
            jax.ShapeDtypeStruct((DEPTH, b, ML_HEADS, ML_HEADDIM, ML_HEADDIM), F32),
            jax.ShapeDtypeStruct((DEPTH, b, ML_HEADS, ML_HEADDIM), F32),
            jax.ShapeDtypeStruct((DEPTH, b, 8, 128), F32),
        ],
        scratch=[
            pltpu.VMEM((bs, q + 8, BRANCH_W), F32),
            pltpu.VMEM((bs, ML_HEADS, ML_HEADDIM, ML_HEADDIM), F32),
            pltpu.VMEM((bs, ML_HEADS, ML_HEADDIM), F32),
            pltpu.VMEM((bs, 8, 128), F32),
            pltpu.VMEM((bs, q, BRANCH_W), F32),
        ],
        args=[proj, proj, proj, proj32, srow, conv_state, c_state, n_state, m_state,
              w['ml_conv_w'], w['ml_conv_b'], w['ml_wq'], w['ml_wk'], w['ml_wv'], w['bias_col'], w['bias_row'],
              w['ml_norm_w'], w['ml_skip'], consts['tril'], consts['triu']],
        group=functools.partial(_mlstm_group, q=q, valid=valid, lowp=lowp))


def _hgrn_group(ins, outs, scr, *, q, valid, lowp, nlev):
    f_ref, i_ref, q_ref, g_ref, s0_ref, *consts = ins
    y_ref, so_ref = outs
    s_scr, o_scr = scr

    def init(v):
        v[7][...] = v[4][...]

    def final(v):
        v[6][...] = v[7][...]

    return (init, functools.partial(_hgrn_main, q=q, valid=valid, lowp=lowp, nlev=nlev), final,
            (f_ref, i_ref, q_ref, g_ref, s0_ref.at[0], y_ref, so_ref.at[0], s_scr, o_scr),
            tuple(consts))


def _hgrn_main(v, k, *, q, valid, lowp, nlev):
    f_ref, i_ref, q_ref, g_ref, _, y_ref, _, s_scr, o_scr = v
    lb_ref, nw_ref, tl_ref = k
    lb = lb_ref[...]
    fg = lb + (1.0 - lb) * _sigmoid(f_ref[...])
    kk = 1.0 - fg
    qq = _silu(q_ref[...].astype(F32)) * (HG_HEADDIM ** -0.5)
    vv = i_ref[...] if lowp else i_ref[...].astype(F32)
    rcol =lax.broadcasted_iota(jnp.int32, (q, 1), 0)
    if valid < q:
        fg = jnp.where(rcol < valid, fg, 1.0)
        kk = jnp.where(rcol < valid, kk, 0.0)
    lf = jnp.log(fg)

    gcum = _xdot(tl_ref[...], lf, NN, 1, lowp)
    yield

    rt = lax.broadcasted_iota(jnp.int32, (q, q), 0)
    cs = lax.broadcasted_iota(jnp.int32, (q, q), 1)
    rx = rt ^ cs
    att = [None] * HG_HEADS
    for lev in range(nlev):
        m = q >> (lev + 1)
        second = (rcol & m) != 0
        if m >= 4:
            nb = q // (2 * m)
            gb = jnp.broadcast_to(gcum.reshape(nb, 2 * m, BRANCH_W)[:, m - 1:m, :], (nb, 2 * m, BRANCH_W))
            e = jnp.exp(-jnp.abs(gcum - gb.reshape(q, BRANCH_W)))
        elif m == 2:
            r4 = rcol & 3
            e = jnp.where(r4 == 0, pltpu.roll(fg, q - 1, 0),
                          jnp.where(r4 == 1, 1.0, jnp.where(r4 == 2, fg, fg * pltpu.roll(fg, 1, 0))))
        else:
            e = jnp.where(second, fg, 1.0)
        u = jnp.where(second, qq, kk) * e
        if lowp:
            u = u.astype(BF16)
        pair = (rx >> int(math.log2(m))) == 1
        for h in range(HG_HEADS):
            hs = slice(h * HG_HEADDIM, (h + 1) * HG_HEADDIM)
            s = lax.dot_general(u[:, hs], u[:, hs], NT, preferred_element_type=F32)
            att[h] = jnp.where(pair, s, 0.0 if att[h] is None else att[h])
        yield

    qg = qq * jnp.exp(gcum)
    kend = kk * jnp.exp(gcum[q - 1:q, :] - gcum)
    ones = jnp.ones((q, 128), F32)
    for h in range(HG_HEADS):
        hs = slice(h * HG_HEADDIM, (h + 1) * HG_HEADDIM)
        diag = jnp.sum(qq[:, hs] * kk[:, hs], axis=1, keepdims=True)
        a = jnp.where(rt > cs, att[h], jnp.where(rt == cs, diag, 0.0))
        sp = s_scr[h]
        oh = _mm(a, vv[:, hs], lowp) + _mm(qg[:, hs], sp, lowp)
        dec = jnp.exp(_xdot(lf[:, hs], ones, TN, 0, lowp))
        s_scr[h] = sp * dec + _mm(kend[:, hs], vv[:, hs], lowp, TN)
        o_scr[:, hs] = oh * lax.rsqrt(jnp.mean(oh * oh, -1, keepdims=True) + EPS)
        yield

    y_ref[...] = (o_scr[...] * nw_ref[...] * _silu(g_ref[...].astype(F32))).astype(BF16)


def _hgrn(proj, proj32, s_state, layer, w, consts, *, b, q, nc, valid, lowp, bs):
    nlev = int(math.log2(q))
    return _Mixer(
        in_specs=[
            _col_spec(bs, q, COL32_D_F, BRANCH_W),
            _col_spec(bs, q, COL_D_I, BRANCH_W),
            _col_spec(bs, q, COL_D_Q, BRANCH_W),
            _col_spec(bs, q, COL_D_G, BRANCH_W),
            _state_in_spec(layer, bs, (HG_HEADS, HG_HEADDIM, HG_HEADDIM)),
            _const_spec((1, BRANCH_W)),
            _const_spec((1, BRANCH_W)),
            _const_spec((q, q)),
        ],
        out_specs=[
            _y_spec(bs, q),
            _state_out_spec(layer, bs, (HG_HEADS, HG_HEADDIM, HG_HEADDIM)),
        ],
        out_shapes=[
            jax.ShapeDtypeStruct((b, nc * q, BRANCH_W), BF16),
            jax.ShapeDtypeStruct((DEPTH, b, HG_HEADS, HG_HEADDIM, HG_HEADDIM), F32),
        ],
        scratch=[
            pltpu.VMEM((bs, HG_HEADS, HG_HEADDIM, HG_HEADDIM), F32),
            pltpu.VMEM((bs, q, BRANCH_W), F32),
        ],
        args=[proj32, proj, proj, proj, s_state, w['hg_lb'], w['hg_norm_w'], consts['tril']],
        group=functools.partial(_hgrn_group, q=q, valid=valid, lowp=lowp, nlev=nlev))


def _chunk_consts(q):
    r = np.arange(q)
    tril = (r[:, None] >= r[None, :]).astype(np.float32)
    expand = np.zeros((128, BRANCH_W), np.float32)
    for h in range(SSD_HEADS):
        expand[h, h * SSD_HEADDIM:(h + 1) * SSD_HEADDIM] = 1.0
    return {'tril': jnp.asarray(tril), 'triu': jnp.asarray(tril.T), 'expand': jnp.asarray(expand)}


def _cmul(ar, ai, br, bi):
    return ar * br - ai * bi, ar * bi + ai * br


def _layer_params(l, p, hg_lb):
    w_in = p['w_in'][l]
    small = jnp.concatenate([w_in[:, 1280:1288], w_in[:, 3848:3856]], axis=1)
    w_perm = jnp.concatenate([
        w_in[:, 5904:10000],
        w_in[:, 0:512],
        w_in[:, 512:1024],
        w_in[:, 1288:2312],
        w_in[:, 2312:3848],
        w_in[:, 4368:5904],
    ], axis=1).astype(BF16)
    w_perm32 = jnp.concatenate([
        w_in[:, 3856:4368],
        w_in[:, 1024:1280],
        small,
        jnp.zeros((D_MODEL, N_PROJ32 - COL32_SMALL - N_SMALL), F32),
    ], axis=1).astype(BF16)

    def lane_pad(v, off):
        return jnp.zeros((1, 128), F32).at[0, off:off + v.shape[0]].set(v)

    bias_col = (lane_pad(p['ssd_dt_bias'][l], SMALL_DT) + lane_pad(p['ml_i_bias'][l], SMALL_I)
                + lane_pad(p['ml_f_bias'][l], SMALL_F))
    alog_col = lane_pad(p['ssd_A_log'][l], SMALL_DT)

    dt = jnp.exp(p['s5_log_dt'][l])[:, None]
    lr = p['s5_A_re'][l]
    li = p['s5_A_im'][l]
    mag = jnp.exp(lr * dt)
    abr, abi = mag * jnp.cos(li * dt), mag * jnp.sin(li * dt)
    den = lr * lr + li * li
    cr = ((abr - 1.0) * lr + abi * li) / den
    ci = (abi * lr - (abr - 1.0) * li) / den
    bbr, bbi = _cmul(cr[..., None], ci[..., None], p['s5_B_re'][l], p['s5_B_im'][l])
    eye8 = jnp.eye(8, dtype=F32)

    def pack_b(bb):
        return jnp.einsum('jgpc,gh->jgchp', bb.reshape(4, 8, S5_STATE, S5_GROUP), eye8).reshape(4, 128, 512)

    def pack_c(cc):
        return jnp.einsum('jgcp,gh->jgphc', cc.reshape(4, 8, S5_GROUP, S5_STATE), eye8).reshape(4, 512, 128)

    return {
        'norm_w': p['norm_w'][l].reshape(1, D_MODEL),
        'w_in': w_perm, 'w_in32': w_perm32,
        'w_small_t': small.T.astype(BF16),
        'bias_col': bias_col, 'bias_row': bias_col[0, :N_SMALL].reshape(N_SMALL, 1),
        'alog_col': alog_col, 'alog_row': alog_col[0, :N_SMALL].reshape(N_SMALL, 1),
        'ssd_conv_w': p['ssd_conv_w'][l], 'ssd_conv_b': p['ssd_conv_b'][l].reshape(1, SSD_CONV_CH),
        'ssd_D_exp': jnp.repeat(p['ssd_D'][l], SSD_HEADDIM).reshape(1, BRANCH_W),
        'ssd_norm_w': p['ssd_norm_w'][l].reshape(1, BRANCH_W),
        's5_wbr': pack_b(bbr).astype(BF16), 's5_wbi': pack_b(bbi).astype(BF16),
        's5_wcr': pack_c(p['s5_C_re'][l]).astype(BF16), 's5_wci': pack_c(p['s5_C_im'][l]).astype(BF16),
        's5_ar': jnp.broadcast_to(abr.reshape(1, S5_LANES), (8, S5_LANES)),
        's5_ai': jnp.broadcast_to(abi.reshape(1, S5_LANES), (8, S5_LANES)),
        's5_D': p['s5_D'][l].reshape(1, BRANCH_W), 's5_glu_w': p['s5_glu_w'][l].astype(BF16),
        'ml_conv_w': p['ml_conv_w'][l], 'ml_conv_b': p['ml_conv_b'][l].reshape(1, BRANCH_W),
        'ml_wq': p['ml_wq'][l], 'ml_wk': p['ml_wk'][l], 'ml_wv': p['ml_wv'][l],
        'ml_norm_w': p['ml_norm_w'][l].reshape(1, BRANCH_W), 'ml_skip': p['ml_skip'][l].reshape(1, BRANCH_W),
        'hg_lb': hg_lb[l].reshape(1, BRANCH_W), 'hg_norm_w': p['hg_norm_w'][l].reshape(1, BRANCH_W),
        'w_branch': p['w_branch'][l].astype(BF16), 'w_out': p['w_out'][l].astype(BF16),
    }


_MATMUL_WEIGHTS = ('ml_wq', 'ml_wk', 'ml_wv')
S5_CHUNK = 64
MIXER_PLAN = [(('ssd', 'mlstm', 'hgrn'), 2)]
MIXER_PLAN_PADDED = [(('ssd', 'mlstm', 'hgrn'), 8)]


def _run_group(x3, states, layers, fnw, *, q, valid):
    b, lp, _ = x3.shape
    nc = lp // q
    t = b * lp
    lowp = q >= 16
    consts = _chunk_consts(q)
    ssd_conv, ssd_h, s5_re, s5_im, ml_conv, ml_c, ml_n, ml_m, hg_s = states
    s5_re = s5_re.reshape(DEPTH, b, S5_LANES)
    s5_im = s5_im.reshape(DEPTH, b, S5_LANES)
    if valid < q:
        s5_kw = dict(b=b, lp=lp, q=q, valid=valid, bb=min(b, 128))
    else:
        qs = math.gcd(lp, S5_CHUNK)
        s5_kw = dict(b=b, lp=lp, q=qs, valid=qs, bb=8)
    ml_m = jnp.broadcast_to(jnp.pad(ml_m, ((0, 0), (0, 0), (0, 8 - ML_HEADS)))[..., None], (DEPTH, b, 8, 128))
    plan = [(names, math.gcd(b, n)) for names, n in (MIXER_PLAN_PADDED if valid < q else MIXER_PLAN)]
    kw = dict(b=b, q=q, nc=nc, valid=valid, lowp=lowp)
    x = x3.reshape(t, D_MODEL)
    ys, st, st_s5 = {}, {}, None
    for l, w in enumerate(layers):
        if lowp:
            w = dict(w, **{k: w[k].astype(BF16) for k in _MATMUL_WEIGHTS})
        proj, proj32, small_t = _inproj(x, w['norm_w'], w['w_in'], w['w_in32'], w['w_small_t'])
        proj3 = proj.reshape(b, lp, N_PROJ)
        proj32 = proj32.reshape(b, lp, N_PROJ32)
        srow = small_t.reshape(N_SMALL, b, lp).transpose(1, 0, 2)
        mixers = {
            'ssd': lambda n: _ssd(proj3, proj32, srow, ssd_conv, ssd_h, l, w, consts, bs=n, **kw),
            'mlstm': lambda n: _mlstm(proj3, proj32, srow, ml_conv, ml_c, ml_n, ml_m, l, w, consts, bs=n, **kw),
            'hgrn': lambda n: _hgrn(proj3, proj32, hg_s, l, w, consts, bs=n, **kw),
        }
        for names, n in plan:
            outs = _launch([mixers[k](n) for k in names], None if l == 0 else [st[k] for k in names],
                           grid=(b // n, nc), bs=n, nc=nc, name='_'.join(names))
            for k, o in zip(names, outs):
                ys[k], st[k] = o[0], list(o[1:])
        yb, *st_s5 = _s5(proj, s5_re, s5_im, l, w, st_s5, **s5_kw)
        ya, yc, yd = (ys[k].reshape(t, BRANCH_W) for k in ('ssd', 'mlstm', 'hgrn'))
        x = _merge(proj, (ya, yb, yc, yd), x, w['w_branch'], w['w_out'], fnw, final=(l == DEPTH - 1))
    n_ssd_conv, n_ssd_h = st['ssd']
    n_re, n_im = (s.reshape(DEPTH, b, S5_GROUPS, S5_STATE) for s in st_s5)
    n_ml_conv, n_c, n_n, n_m = st['mlstm']
    new_states = (n_ssd_conv, n_ssd_h, n_re, n_im, n_ml_conv, n_c, n_n, n_m[:, :, :ML_HEADS, 0], st['hgrn'][0])
    return x.reshape(b, lp, D_MODEL), new_states


SAMPLE_PAD = 8


def kernel(x_prompt, x_sample, state_ssd_conv, state_ssd, state_s5_re, state_s5_im, state_mlstm_conv, state_mlstm_C, state_mlstm_n, state_mlstm_m, state_hgrn, norm_w, w_in, ssd_conv_w, ssd_conv_b, ssd_dt_bias, ssd_A_log, ssd_D, ssd_norm_w, s5_A_re, s5_A_im, s5_B_re, s5_B_im, s5_C_re, s5_C_im, s5_D, s5_log_dt, s5_glu_w, ml_conv_w, ml_conv_b, ml_wq, ml_wk, ml_wv, ml_i_bias, ml_f_bias, ml_norm_w, ml_skip, hg_lb_logits, hg_norm_w, w_branch, w_out, final_norm_w):
    p = {'norm_w': norm_w, 'w_in': w_in,
         'ssd_conv_w': ssd_conv_w, 'ssd_conv_b': ssd_conv_b, 'ssd_dt_bias': ssd_dt_bias,
         'ssd_A_log': ssd_A_log, 'ssd_D': ssd_D, 'ssd_norm_w': ssd_norm_w,
         's5_A_re': s5_A_re, 's5_A_im': s5_A_im, 's5_B_re': s5_B_re, 's5_B_im': s5_B_im,
         's5_C_re': s5_C_re, 's5_C_im': s5_C_im, 's5_D': s5_D, 's5_log_dt': s5_log_dt, 's5_glu_w': s5_glu_w,
         'ml_conv_w': ml_conv_w, 'ml_conv_b': ml_conv_b, 'ml_wq': ml_wq, 'ml_wk': ml_wk, 'ml_wv': ml_wv,
         'ml_i_bias': ml_i_bias, 'ml_f_bias': ml_f_bias, 'ml_norm_w': ml_norm_w, 'ml_skip': ml_skip,
         'hg_norm_w': hg_norm_w, 'w_branch': w_branch, 'w_out': w_out}
    lb_cum = jnp.cumsum(jax.nn.softmax(hg_lb_logits, axis=0), axis=0)
    hg_lb = lb_cum - lb_cum[0]
    layers = [_layer_params(l, p, hg_lb) for l in range(DEPTH)]
    fnw = final_norm_w.reshape(1, D_MODEL)

    sample_states = (state_ssd_conv, state_ssd, state_s5_re, state_s5_im, state_mlstm_conv,
                     state_mlstm_C, state_mlstm_n, state_mlstm_m, state_hgrn)
    bp, lp_, _ = x_prompt.shape
    prompt_states = tuple(jnp.zeros((DEPTH, bp) + s.shape[2:], F32) for s in sample_states)
    q_prompt = math.gcd(lp_, 128)
    y_prompt, new_p = _run_group(x_prompt, prompt_states, layers, fnw, q=q_prompt, valid=q_prompt)

    ls = x_sample.shape[1]
    xs_pad = jnp.pad(x_sample, ((0, 0), (0, SAMPLE_PAD - ls), (0, 0)))
    y_sample, new_s = _run_group(xs_pad, sample_states, layers, fnw, q=SAMPLE_PAD, valid=ls)
    y_sample = y_sample[:, :ls]

    out = [y_prompt, y_sample]
    for ps, ss in zip(new_p, new_s):
        out += [ps, ss]
    return tuple(out)
```

```python
import functools
import math
from typing import Callable, NamedTuple

import numpy as np
import jax
import jax.numpy as jnp
from jax import lax
from jax.experimental import pallas as pl
from jax.experimental.pallas import tpu as pltpu

F32 = jnp.float32
BF16 = jnp.bfloat16

D_MODEL = 1024
DEPTH = 2
BRANCH_W = 512
CONV_W = 4
EPS = 1e-6
SSD_HEADS = 8
SSD_HEADDIM = 64
SSD_STATE = 64
SSD_GROUPS = 2
SSD_CONV_CH = 768
S5_GROUPS = 32
S5_GROUP = 16
S5_STATE = 64
S5_LANES = S5_GROUPS * S5_STATE
ML_HEADS = 4
ML_HEADDIM = 128
HG_HEADS = 4
HG_HEADDIM = 128
N_BRANCH = 4

NEG = -1e30

N_PROJ = 9216
COL_MERGE = 0
COL_A_Z = 4096
COL_A_X = 4608
COL_B_U = 5120
COL_B_GATE = 5632
COL_C_X = 6144
COL_C_Z = 6656
COL_C_O = 7168
COL_D_I = 7680
COL_D_Q = 8192
COL_D_G = 8704
N_PROJ32 = 896
COL32_D_F = 0
COL32_A_BC = 512
COL32_SMALL = 768
N_SMALL = 16
SMALL_DT = 0
SMALL_I = 8
SMALL_F = 12

NN = (((1,), (0,)), ((), ()))
NT = (((1,), (1,)), ((), ()))
TN = (((0,), (0,)), ((), ()))


def _mm(a, b, lowp, dims=NN):
    if lowp:
        a = a.astype(BF16)
        b = b.astype(BF16)
    return lax.dot_general(a, b, dims, preferred_element_type=F32)


def _split3(a):
    hi = a.astype(BF16)
    r = a - hi.astype(F32)
    mid = r.astype(BF16)
    lo = (r - mid.astype(F32)).astype(BF16)
    return hi, mid, lo


def _xdot(a, b, dims, split, lowp):
    if not lowp and min((b if split == 0 else a).shape) < 16:
        return lax.dot_general(a, b, dims, precision=lax.Precision.HIGHEST,
                               preferred_element_type=F32)
    if split == 0:
        other = b.astype(BF16)
        return sum(lax.dot_general(p, other, dims, preferred_element_type=F32) for p in _split3(a))
    other = a.astype(BF16)
    return sum(lax.dot_general(other, p, dims, preferred_element_type=F32) for p in _split3(b))


def _sigmoid(x):
    return 0.5 * jnp.tanh(0.5 * x) + 0.5


def _silu(x):
    h = 0.5 * x
    return h * jnp.tanh(h) + h


def _softplus(x):
    return jnp.maximum(x, 0.0) + jnp.log1p(jnp.exp(-jnp.abs(x)))


def _conv_chunk(xp_scr, cw_ref, cb_ref, q, valid):
    cw = cw_ref[...]
    conv = (cb_ref[...]
            + cw[0:1] * xp_scr[5:5 + q, :]
            + cw[1:2] * xp_scr[6:6 + q, :]
            + cw[2:3] * xp_scr[7:7 + q, :]
            + cw[3:4] * xp_scr[8:8 + q, :])
    carry = xp_scr[5 + valid:8 + valid, :]
    xp_scr[5:8, :] = carry
    return conv, carry


def _inproj_kernel(x_ref, nw_ref, w_ref, w32_ref, wst_ref, o_ref, o32_ref, st_ref, xn_scr):
    @pl.when(pl.program_id(1) == 0)
    def _():
        x = x_ref[...]
        xn = x * lax.rsqrt(jnp.mean(x * x, -1, keepdims=True) + EPS) * nw_ref[...]
        xb = xn.astype(BF16)
        xn_scr[...] = xb
        o32_ref[...] = lax.dot_general(xb, w32_ref[...], NT, preferred_element_type=F32)
        st_ref[...] = lax.dot_general(wst_ref[...], xb, NT, preferred_element_type=F32)

    o_ref[...] = lax.dot_general(xn_scr[...], w_ref[...], NT, preferred_element_type=F32).astype(BF16)


def _inproj(x, nw, w, w32, wst):
    t = x.shape[0]
    tm = min(1024, t)
    tn = N_PROJ // 4
    return pl.pallas_call(
        _inproj_kernel,
        grid=(t // tm, N_PROJ // tn),
        in_specs=[
            pl.BlockSpec((tm, D_MODEL), lambda i, j: (i, 0)),
            pl.BlockSpec((1, D_MODEL), lambda i, j: (0, 0)),
            pl.BlockSpec((tn, D_MODEL), lambda i, j: (j, 0)),
            pl.BlockSpec((N_PROJ32, D_MODEL), lambda i, j: (0, 0)),
            pl.BlockSpec((N_SMALL, D_MODEL), lambda i, j: (0, 0)),
        ],
        out_specs=[
            pl.BlockSpec((tm, tn), lambda i, j: (i, j)),
            pl.BlockSpec((tm, N_PROJ32), lambda i, j: (i, 0)),
            pl.BlockSpec((N_SMALL, tm), lambda i, j: (0, i)),
        ],
        out_shape=[
            jax.ShapeDtypeStruct((t, N_PROJ), BF16),
            jax.ShapeDtypeStruct((t, N_PROJ32), F32),
            jax.ShapeDtypeStruct((N_SMALL, t), F32),
        ],
        scratch_shapes=[pltpu.VMEM((tm, D_MODEL), BF16)],
        compiler_params=pltpu.CompilerParams(
            dimension_semantics=("arbitrary", "arbitrary"),
            vmem_limit_bytes=48 * 1024 * 1024),
        name="inproj",
    )(x, nw, w, w32, wst)


def _merge_kernel(g_ref, ya_ref, yb_ref, yc_ref, yd_ref, x_ref, wb_ref, wo_ref, fnw_ref, o_ref, *, final):
    acc = None
    for i, y_ref in enumerate((ya_ref, yb_ref, yc_ref, yd_ref)):
        gate = _sigmoid(g_ref[:, i * D_MODEL:(i + 1) * D_MODEL].astype(F32))
        term = gate * jnp.dot(y_ref[...], wb_ref[i], preferred_element_type=F32)
        acc = term if acc is None else acc + term
    out = x_ref[...] + jnp.dot(acc.astype(BF16), wo_ref[...], preferred_element_type=F32)
    if final:
        out = out * lax.rsqrt(jnp.mean(out * out, -1, keepdims=True) + EPS) * fnw_ref[...]
    o_ref[...] = out


def _merge(proj, ys, x, wb, wo, fnw, final):
    t = x.shape[0]
    tm = min(512, t)
    yspec = pl.BlockSpec((tm, BRANCH_W), lambda i: (i, 0))
    return pl.pallas_call(
        functools.partial(_merge_kernel, final=final),
        grid=(t // tm,),
        in_specs=[
            pl.BlockSpec((tm, N_BRANCH * D_MODEL), lambda i: (i, COL_MERGE // (N_BRANCH * D_MODEL))),
            yspec, yspec, yspec, yspec,
            pl.BlockSpec((tm, D_MODEL), lambda i: (i, 0)),
            pl.BlockSpec((N_BRANCH, BRANCH_W, D_MODEL), lambda i: (0, 0, 0)),
            pl.BlockSpec((D_MODEL, D_MODEL), lambda i: (0, 0)),
            pl.BlockSpec((1, D_MODEL), lambda i: (0, 0)),
        ],
        out_specs=pl.BlockSpec((tm, D_MODEL), lambda i: (i, 0)),
        out_shape=jax.ShapeDtypeStruct((t, D_MODEL), F32),
        compiler_params=pltpu.CompilerParams(
            dimension_semantics=("arbitrary",),
            vmem_limit_bytes=48 * 1024 * 1024),
        name="merge",
    )(proj, *ys, x, wb, wo, fnw)


def _col_spec(bs, q, col, width):
    return pl.BlockSpec((bs, q, width), lambda b, c: (b, c, col // width))


def _row_spec(bs, q):
    return pl.BlockSpec((bs, N_SMALL, q), lambda b, c: (b, 0, c))


def _const_spec(shape):
    nd = len(shape)
    return pl.BlockSpec(shape, lambda b, c: (0,) * nd)


def _state_in_spec(layer, bs, tail):
    nd = len(tail)
    return pl.BlockSpec((1, bs) + tail, lambda b, c: (layer, b) + (0,) * nd)


def _state_out_spec(layer, bs, tail):
    nd = len(tail)
    return pl.BlockSpec((1, bs) + tail, lambda b, c: (layer, b) + (0,) * nd)


def _y_spec(bs, q):
    return pl.BlockSpec((bs, q, BRANCH_W), lambda b, c: (b, c, 0))


class _Mixer(NamedTuple):
    in_specs: list
    args: list
    out_specs: list
    out_shapes: list
    scratch: list
    group: Callable


_DONE = object()
_MIXER_PARAMS = pltpu.CompilerParams(
    dimension_semantics=("arbitrary", "arbitrary"),
    vmem_limit_bytes=48 * 1024 * 1024)


def _launch(mixers, prevs, *, grid, bs, nc, name):
    n_in = [len(m.in_specs) for m in mixers]
    n_out = [len(m.out_specs) for m in mixers]
    n_scr = [len(m.scratch) for m in mixers]
    tot_in, tot_out = sum(n_in), sum(n_out)
    state_idx = [sum(n_out[:i]) + k for i in range(len(mixers)) for k in range(1, n_out[i])]
    n_alias = 0 if prevs is None else len(state_idx)

    def kern(*refs):
        ins, outs, scr = refs[:tot_in], refs[tot_in + n_alias:tot_in + n_alias + tot_out], refs[tot_in + n_alias + tot_out:]
        groups = []
        for i, m in enumerate(mixers):
            groups.append(m.group(ins[sum(n_in[:i]):sum(n_in[:i + 1])],
                                  outs[sum(n_out[:i]):sum(n_out[:i + 1])],
                                  scr[sum(n_scr[:i]):sum(n_scr[:i + 1])]))
        c = pl.program_id(1)
        views = [[[r.at[s] for r in g[3]] for s in range(bs)] for g in groups]

        @pl.when(c == 0)
        def _():
            for g, v in zip(groups, views):
                for s in range(bs):
                    g[0](v[s])

        active = [g[1](v[s], g[4]) for s in range(bs) for g, v in zip(groups, views)]
        while active:
            active = [gen for gen in active if next(gen, _DONE) is not _DONE]

        @pl.when(c == nc - 1)
        def _():
            for g, v in zip(groups, views):
                for s in range(bs):
                    g[2](v[s])

    in_specs = [sp for m in mixers for sp in m.in_specs] + [pl.BlockSpec(memory_space=pl.ANY)] * n_alias
    args = [a for m in mixers for a in m.args]
    if prevs is not None:
        args += [a for p in prevs for a in p]
    outs = pl.pallas_call(
        kern,
        grid=grid,
        in_specs=in_specs,
        out_specs=[sp for m in mixers for sp in m.out_specs],
        out_shape=[sh for m in mixers for sh in m.out_shapes],
        scratch_shapes=[sc for m in mixers for sc in m.scratch],
        input_output_aliases={tot_in + k: idx for k, idx in enumerate(state_idx)} if n_alias else {},
        compiler_params=_MIXER_PARAMS,
        name=name,
    )(*args)
    return [outs[sum(n_out[:i]):sum(n_out[:i + 1])] for i in range(len(mixers))]


def _ssd_group(ins, outs, scr, *, q, valid, lowp):
    z_ref, x_ref, bc_ref, sc_ref, sr_ref, conv0_ref, h0_ref, *consts = ins
    y_ref, convo_ref, ho_ref = outs
    xp_scr, h_scr, y_scr = scr

    def init(v):
        v[9][5:8, :] = v[5][...]
        v[10][...] = v[6][...]

    def final(v):
        v[8][...] = v[10][...]

    return (init, functools.partial(_ssd_main, q=q, valid=valid, lowp=lowp), final,
            (z_ref, x_ref, bc_ref, sc_ref, sr_ref, conv0_ref.at[0], h0_ref.at[0],
             y_ref, ho_ref.at[0], xp_scr, h_scr, y_scr, convo_ref.at[0]),
            tuple(consts))


def _ssd_main(v, k, *, q, valid, lowp):
    z_ref, x_ref, bc_ref, sc_ref, sr_ref, _, _, y_ref, _, xp_scr, h_scr, y_scr, convo_ref = v
    cw_ref, cb_ref, bcol_ref, brow_ref, alc_ref, alr_ref, dexp_ref, nw_ref, e_ref, tl_ref, tu_ref = k
    xp_scr[8:8 + q, 0:BRANCH_W] = x_ref[...].astype(F32)
    xp_scr[8:8 + q, BRANCH_W:SSD_CONV_CH] = bc_ref[...]
    conv, carry = _conv_chunk(xp_scr, cw_ref, cb_ref, q, valid)
    convo_ref[...] = carry
    yield
    xbc = _silu(conv)
    xs = xbc[:, 0:BRANCH_W]
    bm = xbc[:, BRANCH_W:BRANCH_W + 128]
    cm = xbc[:, BRANCH_W + 128:BRANCH_W + 256]

    dt_c = _softplus(sc_ref[...] + bcol_ref[...])
    a_c = dt_c * (-jnp.exp(alc_ref[...]))
    dt_r = _softplus(sr_ref[...] + brow_ref[...])
    a_r = dt_r * (-jnp.exp(alr_ref[...]))
    if valid < q:
        tcol = lax.broadcasted_iota(jnp.int32, (q, 1), 0)
        trow = lax.broadcasted_iota(jnp.int32, (1, q), 1)
        a_c = jnp.where(tcol < valid, a_c, 0.0)
        dt_c = jnp.where(tcol < valid, dt_c, 0.0)
        a_r = jnp.where(trow < valid, a_r, 0.0)

    tl = tl_ref[...]
    acum_c = _xdot(tl, a_c, NN, 1, lowp)
    acum_r = _xdot(a_r, tu_ref[...], NN, 0, lowp)
    e = e_ref[...]
    dt_e = _xdot(dt_c, e, NN, 0, lowp)
    acum_e = _xdot(acum_c, e, NN, 0, lowp)
    xdt = xs * dt_e
    xend = xdt * jnp.exp(acum_e[q - 1:q, :] - acum_e)
    eac = jnp.exp(acum_e)
    dec_last = jnp.exp(acum_c[q - 1:q, :])
    yield

    tri = (lax.broadcasted_iota(jnp.int32, (q, q), 0) >= lax.broadcasted_iota(jnp.int32, (q, q), 1))
    rpg = SSD_HEADS // SSD_GROUPS
    for g in range(SSD_GROUPS):
        bg = bm[:, g * SSD_STATE:(g + 1) * SSD_STATE]
        cg = cm[:, g * SSD_STATE:(g + 1) * SSD_STATE]
        cb = _mm(cg, bg, lowp, NT)
        for r in range(rpg):
            h = g * rpg + r
            hs = slice(h * SSD_HEADDIM, (h + 1) * SSD_HEADDIM)
            seg = acum_c[:, h:h + 1] - acum_r[h:h + 1, :]
            m = cb * jnp.exp(jnp.where(tri, seg, NEG))
            hp = h_scr[h]
            yh = _mm(m, xdt[:, hs], lowp) + _mm(cg, hp, lowp, NT) * eac[:, hs]
            y_scr[:, hs] = yh
            h_scr[h] = hp * dec_last[:, h:h + 1] + _mm(xend[:, hs], bg, lowp, TN)
            yield

    y = y_scr[...] + xs * dexp_ref[...]
    y = y * _silu(z_ref[...].astype(F32))
    y = y * lax.rsqrt(jnp.mean(y * y, -1, keepdims=True) + EPS) * nw_ref[...]
    y_ref[...] = y.astype(BF16)


def _ssd(proj, proj32, srow, conv_state, h_state, layer, w, consts, *, b, q, nc, valid, lowp, bs):
    return _Mixer(
        in_specs=[
            _col_spec(bs, q, COL_A_Z, BRANCH_W),
            _col_spec(bs, q, COL_A_X, BRANCH_W),
            _col_spec(bs, q, COL32_A_BC, 256),
            _col_spec(bs, q, COL32_SMALL, 128),
            _row_spec(bs, q),
            _state_in_spec(layer, bs, (CONV_W - 1, SSD_CONV_CH)),
            _state_in_spec(layer, bs, (SSD_HEADS, SSD_HEADDIM, SSD_STATE)),
            _const_spec((CONV_W, SSD_CONV_CH)),
            _const_spec((1, SSD_CONV_CH)),
            _const_spec((1, 128)),
            _const_spec((N_SMALL, 1)),
            _const_spec((1, 128)),
            _const_spec((N_SMALL, 1)),
            _const_spec((1, BRANCH_W)),
            _const_spec((1, BRANCH_W)),
            _const_spec((128, BRANCH_W)),
            _const_spec((q, q)),
            _const_spec((q, q)),
        ],
        out_specs=[
            _y_spec(bs, q),
            _state_out_spec(layer, bs, (CONV_W - 1, SSD_CONV_CH)),
            _state_out_spec(layer, bs, (SSD_HEADS, SSD_HEADDIM, SSD_STATE)),
        ],
        out_shapes=[
            jax.ShapeDtypeStruct((b, nc * q, BRANCH_W), BF16),
            jax.ShapeDtypeStruct((DEPTH, b, CONV_W - 1, SSD_CONV_CH), F32),
            jax.ShapeDtypeStruct((DEPTH, b, SSD_HEADS, SSD_HEADDIM, SSD_STATE), F32),
        ],
        scratch=[
            pltpu.VMEM((bs, q + 8, SSD_CONV_CH), F32),
            pltpu.VMEM((bs, SSD_HEADS, SSD_HEADDIM, SSD_STATE), F32),
            pltpu.VMEM((bs, q, BRANCH_W), F32),
        ],
        args=[proj, proj, proj32, proj32, srow, conv_state, h_state,
              w['ssd_conv_w'], w['ssd_conv_b'], w['bias_col'], w['bias_row'], w['alog_col'], w['alog_row'],
              w['ssd_D_exp'], w['ssd_norm_w'], consts['expand'], consts['tril'], consts['triu']],
        group=functools.partial(_ssd_group, q=q, valid=valid, lowp=lowp))


def _s5_kernel(u_ref, gate_ref, hr0_ref, hi0_ref, perm_ref, permt_ref, ar_ref, ai_ref,
               wbr_ref, wbi_ref, wcr_ref, wci_ref, d_ref, glu_ref,
               y_ref, hro_ref, hio_ref,
               hr_scr, hi_scr, cr_scr, ci_scr, *, q, nc, ngrp):
    c = pl.program_id(1)

    @pl.when(c == 0)
    def _():
        cr_scr[...] = hr0_ref[0]
        ci_scr[...] = hi0_ref[0]

    rows_in = perm_ref.shape[1]
    u = u_ref[...].reshape(rows_in, BRANCH_W)
    u_tm = jnp.dot(perm_ref[...], u, preferred_element_type=F32).astype(BF16)
    nblk = S5_LANES // 512
    for j in range(nblk):
        uj = u_tm[:, j * 128:(j + 1) * 128]
        hr_scr[:, j * 512:(j + 1) * 512] = jnp.dot(uj, wbr_ref[j], preferred_element_type=F32)
        hi_scr[:, j * 512:(j + 1) * 512] = jnp.dot(uj, wbi_ref[j], preferred_element_type=F32)

    half = S5_LANES // 2
    for hf in range(2):
        sl = slice(hf * half, (hf + 1) * half)
        ar = ar_ref[:, sl]
        ai = ai_ref[:, sl]

        def grp_body(g, carry, sl=sl, ar=ar, ai=ai):
            s0 = pl.multiple_of(g * 8, 8)

            def t_body(t, h):
                r0 = pl.multiple_of(g * (8 * q) + t * 8, 8)
                hr, hi = h
                nr = ar * hr - ai * hi + hr_scr[pl.ds(r0, 8), sl]
                ni = ar * hi + ai * hr + hi_scr[pl.ds(r0, 8), sl]
                hr_scr[pl.ds(r0, 8), sl] = nr
                hi_scr[pl.ds(r0, 8), sl] = ni
                return nr, ni

            h = lax.fori_loop(0, q, t_body, (cr_scr[pl.ds(s0, 8), sl], ci_scr[pl.ds(s0, 8), sl]),
                              unroll=min(q, 4))
            cr_scr[pl.ds(s0, 8), sl] = h[0]
            ci_scr[pl.ds(s0, 8), sl] = h[1]
            return carry

        lax.fori_loop(0, ngrp, grp_body, 0)

    parts = []
    for j in range(nblk):
        sl = slice(j * 512, (j + 1) * 512)
        parts.append(jnp.dot(hr_scr[:, sl].astype(BF16), wcr_ref[j], preferred_element_type=F32)
                     - jnp.dot(hi_scr[:, sl].astype(BF16), wci_ref[j], preferred_element_type=F32))
    y_tm = jnp.concatenate(parts, axis=1)
    y = _xdot(permt_ref[...], y_tm, NN, 1, True) + u.astype(F32) * d_ref[...]
    g = jax.nn.gelu(y)
    y = g * _sigmoid(jnp.dot(g.astype(BF16), glu_ref[...], preferred_element_type=F32))
    y = y * _silu(gate_ref[...].reshape(rows_in, BRANCH_W).astype(F32))
    y_ref[...] = y.reshape(y_ref.shape).astype(BF16)

    @pl.when(c == nc - 1)
    def _():
        hro_ref[0] = cr_scr[...]
        hio_ref[0] = ci_scr[...]


def _s5_perm(bb, lp, valid):
    perm = np.zeros((bb * valid, bb * lp), np.float32)
    for g in range(bb // 8):
        for t in range(valid):
            for s in range(8):
                perm[g * 8 * valid + t * 8 + s, (g * 8 + s) * lp + t] = 1.0
    return jnp.asarray(perm, BF16), jnp.asarray(perm.T, BF16)


def _s5(proj, hr_state, hi_state, layer, w, prev, *, b, lp, q, valid, bb):
    nc = lp // q
    perm, permt = _s5_perm(bb, q, valid)
    rows = bb * valid
    if nc == 1:
        def col_spec(col):
            return pl.BlockSpec((bb * q, BRANCH_W), lambda i, c: (i, col // BRANCH_W))
        src = proj
        y_spec = pl.BlockSpec((bb * q, BRANCH_W), lambda i, c: (i, 0))
        y_shape = jax.ShapeDtypeStruct((b * lp, BRANCH_W), BF16)
    else:
        def col_spec(col):
            return pl.BlockSpec((bb, q, BRANCH_W), lambda i, c: (i, c, col // BRANCH_W))
        src = proj.reshape(b, lp, N_PROJ)
        y_spec = pl.BlockSpec((bb, q, BRANCH_W), lambda i, c: (i, c, 0))
        y_shape = jax.ShapeDtypeStruct((b, lp, BRANCH_W), BF16)
    st_in = pl.BlockSpec((1, bb, S5_LANES), lambda i, c: (layer, i, 0))
    st_out = st_in
    kern = functools.partial(_s5_kernel, q=valid, nc=nc, ngrp=bb // 8)
    n_in = 14
    alias_kw = {}
    if prev is not None:
        base = kern

        def kern(*refs):
            return base(*refs[:n_in], *refs[n_in + 2:])

        alias_kw = dict(input_output_aliases={n_in: 1, n_in + 1: 2})
    y, hr, hi = pl.pallas_call(
        kern,
        grid=(b // bb, nc),
        **alias_kw,
        in_specs=[
            col_spec(COL_B_U),
            col_spec(COL_B_GATE),
            st_in, st_in,
            _const_spec(perm.shape),
            _const_spec(permt.shape),
            _const_spec((8, S5_LANES)),
            _const_spec((8, S5_LANES)),
            _const_spec((4, 128, 512)),
            _const_spec((4, 128, 512)),
            _const_spec((4, 512, 128)),
            _const_spec((4, 512, 128)),
            _const_spec((1, BRANCH_W)),
            _const_spec((BRANCH_W, BRANCH_W)),
        ] + [pl.BlockSpec(memory_space=pl.ANY)] * (0 if prev is None else 2),
        out_specs=[y_spec, st_out, st_out],
        out_shape=[
            y_shape,
            jax.ShapeDtypeStruct((DEPTH, b, S5_LANES), F32),
            jax.ShapeDtypeStruct((DEPTH, b, S5_LANES), F32),
        ],
        scratch_shapes=[
            pltpu.VMEM((rows, S5_LANES), F32),
            pltpu.VMEM((rows, S5_LANES), F32),
            pltpu.VMEM((bb, S5_LANES), F32),
            pltpu.VMEM((bb, S5_LANES), F32),
        ],
        compiler_params=_MIXER_PARAMS,
        name="s5",
    )(src, src, hr_state, hi_state, perm, permt, w['s5_ar'], w['s5_ai'],
      w['s5_wbr'], w['s5_wbi'], w['s5_wcr'], w['s5_wci'], w['s5_D'], w['s5_glu_w'], *(prev or ()))
    return y.reshape(b * lp, BRANCH_W), hr, hi


def _mlstm_group(ins, outs, scr, *, q, valid, lowp):
    x_ref, z_ref, o_ref, sc_ref, sr_ref, conv0_ref, c0_ref, n0_ref, m0_ref, *consts = ins
    y_ref, convo_ref, co_ref, no_ref, mo_ref = outs
    xp_scr, c_scr, n_scr, m_scr, h_scr = scr

    def init(v):
        v[13][5:8, :] = v[5][...]
        v[14][...] = v[6][...]
        v[15][...] = v[7][...]
        v[16][...] = v[8][...]

    def final(v):
        v[10][...] = v[14][...]
        v[11][...] = v[15][...]
        v[12][...] = v[16][...]

    return (init, functools.partial(_mlstm_main, q=q, valid=valid, lowp=lowp), final,
            (x_ref, z_ref, o_ref, sc_ref, sr_ref, conv0_ref.at[0], c0_ref.at[0], n0_ref.at[0], m0_ref.at[0],
             y_ref, co_ref.at[0], no_ref.at[0], mo_ref.at[0], xp_scr, c_scr, n_scr, m_scr, h_scr,
             convo_ref.at[0]),
            tuple(consts))


def _mlstm_main(v, k, *, q, valid, lowp):
    (x_ref, z_ref, o_ref, sc_ref, sr_ref, _, _, _, _, y_ref, _, _, _,
     xp_scr, c_scr, n_scr, m_scr, h_scr, convo_ref) = v
    cw_ref, cb_ref, wq_ref, wk_ref, wv_ref, bcol_ref, brow_ref, nw_ref, skip_ref, tl_ref, tu_ref = k
    x = x_ref[...].astype(F32)
    xp_scr[8:8 + q, :] = x
    conv, carry = _conv_chunk(xp_scr, cw_ref, cb_ref, q, valid)
    convo_ref[...] = carry
    yield
    xc = _silu(conv)

    pre_c = sc_ref[...] + bcol_ref[...]
    pre_r = sr_ref[...] + brow_ref[...]
    ig_c = pre_c
    lf_c = -_softplus(-pre_c)
    ig_r = pre_r
    lf_r = -_softplus(-pre_r)
    if valid < q:
        tcol = lax.broadcasted_iota(jnp.int32, (q, 1), 0)
        trow = lax.broadcasted_iota(jnp.int32, (1, q), 1)
        lf_c = jnp.where(tcol < valid, lf_c, 0.0)
        ig_c = jnp.where(tcol < valid, ig_c, NEG)
        lf_r = jnp.where(trow < valid, lf_r, 0.0)
        ig_r = jnp.where(trow < valid, ig_r, NEG)
    b_c = _xdot(tl_ref[...], lf_c, NN, 1, lowp)
    b_r = _xdot(lf_r, tu_ref[...], NN, 0, lowp)
    yield

    tri = (lax.broadcasted_iota(jnp.int32, (q, q), 0) >= lax.broadcasted_iota(jnp.int32, (q, q), 1))
    o_all = o_ref[...].astype(F32)
    for h in range(ML_HEADS):
        hs = slice(h * ML_HEADDIM, (h + 1) * ML_HEADDIM)
        xh = xc[:, hs]
        qh = _mm(xh, wq_ref[h], lowp)
        kh = _mm(xh, wk_ref[h], lowp) * (ML_HEADDIM ** -0.5)
        vh = _mm(x[:, hs], wv_ref[h], lowp)
        yield
        bc = b_c[:, SMALL_F + h:SMALL_F + h + 1]
        br = b_r[SMALL_F + h:SMALL_F + h + 1, :]
        ic = ig_c[:, SMALL_I + h:SMALL_I + h + 1]
        ir = ig_r[SMALL_I + h:SMALL_I + h + 1, :]
        mp = m_scr[h:h + 1, 0:1]
        dlog = jnp.where(tri, bc - br + ir, NEG)
        inter = bc + mp
        m_t = jnp.maximum(inter, jnp.max(dlog, axis=1, keepdims=True))
        wgt = jnp.exp(dlog - m_t)
        yield
        s = _mm(qh, kh, lowp, NT) * wgt
        scale = jnp.exp(inter - m_t)
        cp = c_scr[h]
        npv = n_scr[h:h + 1, :]
        num = _mm(s, vh, lowp) + scale * _mm(qh, cp, lowp)
        dot = jnp.sum(s, axis=1, keepdims=True) + scale * jnp.sum(qh * npv, axis=1, keepdims=True)
        hh = num / jnp.maximum(jnp.abs(dot), jnp.exp(-m_t))
        yield
        m_new = m_t[q - 1:q, :]
        b_last = bc[q - 1:q, :]
        w_end = jnp.exp(b_last - bc + ic - m_new)
        cs = jnp.exp(b_last + mp - m_new)
        kw = kh * w_end
        c_scr[h] = cs * cp + _mm(kw, vh, lowp, TN)
        n_scr[h:h + 1, :] = cs * npv + jnp.sum(kw, axis=0, keepdims=True)
        m_scr[h:h + 1, :] = jnp.broadcast_to(m_new, (1, 128))
        yield
        hh = hh * _sigmoid(o_all[:, hs])
        hc = hh - jnp.mean(hh, -1, keepdims=True)
        h_scr[:, hs] = hc * lax.rsqrt(jnp.mean(hc * hc, -1, keepdims=True) + EPS)
        yield

    y = h_scr[...] * nw_ref[...] + skip_ref[...] * xc
    y_ref[...] = (y * _silu(z_ref[...].astype(F32))).astype(BF16)


def _mlstm(proj, proj32, srow, conv_state, c_state, n_state, m_state, layer, w, consts,
           *, b, q, nc, valid, lowp, bs):
    return _Mixer(
        in_specs=[
            _col_spec(bs, q, COL_C_X, BRANCH_W),
            _col_spec(bs, q, COL_C_Z, BRANCH_W),
            _col_spec(bs, q, COL_C_O, BRANCH_W),
            _col_spec(bs, q, COL32_SMALL, 128),
            _row_spec(bs, q),
            _state_in_spec(layer, bs, (CONV_W - 1, BRANCH_W)),
            _state_in_spec(layer, bs, (ML_HEADS, ML_HEADDIM, ML_HEADDIM)),
            _state_in_spec(layer, bs, (ML_HEADS, ML_HEADDIM)),
            _state_in_spec(layer, bs, (8, 128)),
            _const_spec((CONV_W, BRANCH_W)),
            _const_spec((1, BRANCH_W)),
            _const_spec((ML_HEADS, ML_HEADDIM, ML_HEADDIM)),
            _const_spec((ML_HEADS, ML_HEADDIM, ML_HEADDIM)),
            _const_spec((ML_HEADS, ML_HEADDIM, ML_HEADDIM)),
            _const_spec((1, 128)),
            _const_spec((N_SMALL, 1)),
            _const_spec((1, BRANCH_W)),
            _const_spec((1, BRANCH_W)),
            _const_spec((q, q)),
            _const_spec((q, q)),
        ],
        out_specs=[
            _y_spec(bs, q),
            _state_out_spec(layer, bs, (CONV_W - 1, BRANCH_W)),
            _state_out_spec(layer, bs, (ML_HEADS, ML_HEADDIM, ML_HEADDIM)),
            _state_out_spec(layer, bs, (ML_HEADS, ML_HEADDIM)),
            _state_out_spec(layer, bs, (8, 128)),
        ],
        out_shapes=[
            jax.ShapeDtypeStruct((b, nc * q, BRANCH_W), BF16),
            jax.ShapeDtypeStruct((DEPTH, b, CONV_W - 1, BRANCH_W), F32),
            jax.ShapeDtypeStruct((DEPTH, b, ML_HEADS, ML_HEADDIM, ML_HEADDIM), F32),
            jax.ShapeDtypeStruct((DEPTH, b, ML_HEADS, ML_HEADDIM), F32),
            jax.ShapeDtypeStruct((DEPTH, b, 8, 128), F32),
        ],
        scratch=[
            pltpu.VMEM((bs, q + 8, BRANCH_W), F32),
            pltpu.VMEM((bs, ML_HEADS, ML_HEADDIM, ML_HEADDIM), F32),
            pltpu.VMEM((bs, ML_HEADS, ML_HEADDIM), F32),
            pltpu.VMEM((bs, 8, 128), F32),
            pltpu.VMEM((bs, q, BRANCH_W), F32),
        ],
        args=[proj, proj, proj, proj32, srow, conv_state, c_state, n_state, m_state,
              w['ml_conv_w'], w['ml_conv_b'], w['ml_wq'], w['ml_wk'], w['ml_wv'], w['bias_col'], w['bias_row'],
              w['ml_norm_w'], w['ml_skip'], consts['tril'], consts['triu']],
        group=functools.partial(_mlstm_group, q=q, valid=valid, lowp=lowp))


def _hgrn_group(ins, outs, scr, *, q, valid, lowp, nlev):
    f_ref, i_ref, q_ref, g_ref, s0_ref, *consts = ins
    y_ref, so_ref = outs
    s_scr, o_scr = scr

    def init(v):
        v[7][...] = v[4][...]

    def final(v):
        v[6][...] = v[7][...]

    return (init, functools.partial(_hgrn_main, q=q, valid=valid, lowp=lowp, nlev=nlev), final,
            (f_ref, i_ref, q_ref, g_ref, s0_ref.at[0], y_ref, so_ref.at[0], s_scr, o_scr),
            tuple(consts))


def _hgrn_main(v, k, *, q, valid, lowp, nlev):
    f_ref, i_ref, q_ref, g_ref, _, y_ref, _, s_scr, o_scr = v
    lb_ref, nw_ref, tl_ref = k
    lb = lb_ref[...]
    fg = lb + (1.0 - lb) * _sigmoid(f_ref[...])
    kk = 1.0 - fg
    qq = _silu(q_ref[...].astype(F32)) * (HG_HEADDIM ** -0.5)
    vv = i_ref[...] if lowp else i_ref[...].astype(F32)
    rcol =lax.broadcasted_iota(jnp.int32, (q, 1), 0)
    if valid < q:
        fg = jnp.where(rcol < valid, fg, 1.0)
        kk = jnp.where(rcol < valid, kk, 0.0)
    lf = jnp.log(fg)

    gcum = _xdot(tl_ref[...], lf, NN, 1, lowp)
    yield

    rt = lax.broadcasted_iota(jnp.int32, (q, q), 0)
    cs = lax.broadcasted_iota(jnp.int32, (q, q), 1)
    rx = rt ^ cs
    att = [None] * HG_HEADS
    for lev in range(nlev):
        m = q >> (lev + 1)
        second = (rcol & m) != 0
        if m >= 4:
            nb = q // (2 * m)
            gb = jnp.broadcast_to(gcum.reshape(nb, 2 * m, BRANCH_W)[:, m - 1:m, :], (nb, 2 * m, BRANCH_W))
            e = jnp.exp(-jnp.abs(gcum - gb.reshape(q, BRANCH_W)))
        elif m == 2:
            r4 = rcol & 3
            e = jnp.where(r4 == 0, pltpu.roll(fg, q - 1, 0),
                          jnp.where(r4 == 1, 1.0, jnp.where(r4 == 2, fg, fg * pltpu.roll(fg, 1, 0))))
        else:
            e = jnp.where(second, fg, 1.0)
        u = jnp.where(second, qq, kk) * e
        if lowp:
            u = u.astype(BF16)
        pair = (rx >> int(math.log2(m))) == 1
        for h in range(HG_HEADS):
            hs = slice(h * HG_HEADDIM, (h + 1) * HG_HEADDIM)
            s = lax.dot_general(u[:, hs], u[:, hs], NT, preferred_element_type=F32)
            att[h] = jnp.where(pair, s, 0.0 if att[h] is None else att[h])
        yield

    qg = qq * jnp.exp(gcum)
    kend = kk * jnp.exp(gcum[q - 1:q, :] - gcum)
    ones = jnp.ones((q, 128), F32)
    for h in range(HG_HEADS):
        hs = slice(h * HG_HEADDIM, (h + 1) * HG_HEADDIM)
        diag = jnp.sum(qq[:, hs] * kk[:, hs], axis=1, keepdims=True)
        a = jnp.where(rt > cs, att[h], jnp.where(rt == cs, diag, 0.0))
        sp = s_scr[h]
        oh = _mm(a, vv[:, hs], lowp) + _mm(qg[:, hs], sp, lowp)
        dec = jnp.exp(_xdot(lf[:, hs], ones, TN, 0, lowp))
        s_scr[h] = sp * dec + _mm(kend[:, hs], vv[:, hs], lowp, TN)
        o_scr[:, hs] = oh * lax.rsqrt(jnp.mean(oh * oh, -1, keepdims=True) + EPS)
        yield

    y_ref[...] = (o_scr[...] * nw_ref[...] * _silu(g_ref[...].astype(F32))).astype(BF16)


def _hgrn(proj, proj32, s_state, layer, w, consts, *, b, q, nc, valid, lowp, bs):
    nlev = int(math.log2(q))
    return _Mixer(
        in_specs=[
            _col_spec(bs, q, COL32_D_F, BRANCH_W),
            _col_spec(bs, q, COL_D_I, BRANCH_W),
            _col_spec(bs, q, COL_D_Q, BRANCH_W),
            _col_spec(bs, q, COL_D_G, BRANCH_W),
            _state_in_spec(layer, bs, (HG_HEADS, HG_HEADDIM, HG_HEADDIM)),
            _const_spec((1, BRANCH_W)),
            _const_spec((1, BRANCH_W)),
            _const_spec((q, q)),
        ],
        out_specs=[
            _y_spec(bs, q),
            _state_out_spec(layer, bs, (HG_HEADS, HG_HEADDIM, HG_HEADDIM)),
        ],
        out_shapes=[
            jax.ShapeDtypeStruct((b, nc * q, BRANCH_W), BF16),
            jax.ShapeDtypeStruct((DEPTH, b, HG_HEADS, HG_HEADDIM, HG_HEADDIM), F32),
        ],
        scratch=[
            pltpu.VMEM((bs, HG_HEADS, HG_HEADDIM, HG_HEADDIM), F32),
            pltpu.VMEM((bs, q, BRANCH_W), F32),
        ],
        args=[proj32, proj, proj, proj, s_state, w['hg_lb'], w['hg_norm_w'], consts['tril']],
        group=functools.partial(_hgrn_group, q=q, valid=valid, lowp=lowp, nlev=nlev))


def _chunk_consts(q):
    r = np.arange(q)
    tril = (r[:, None] >= r[None, :]).astype(np.float32)
    expand = np.zeros((128, BRANCH_W), np.float32)
    for h in range(SSD_HEADS):
        expand[h, h * SSD_HEADDIM:(h + 1) * SSD_HEADDIM] = 1.0
    return {'tril': jnp.asarray(tril), 'triu': jnp.asarray(tril.T), 'expand': jnp.asarray(expand)}


def _cmul(ar, ai, br, bi):
    return ar * br - ai * bi, ar * bi + ai * br


def _layer_params(l, p, hg_lb):
    w_in_t = p['w_in'][l].T
    small = jnp.concatenate([w_in_t[1280:1288], w_in_t[3848:3856]], axis=0)
    w_perm = jnp.concatenate([
        w_in_t[5904:10000],
        w_in_t[0:512],
        w_in_t[512:1024],
        w_in_t[1288:2312],
        w_in_t[2312:3848],
        w_in_t[4368:5904],
    ], axis=0).astype(BF16)
    w_perm32 = jnp.concatenate([
        w_in_t[3856:4368],
        w_in_t[1024:1280],
        small,
        jnp.zeros((N_PROJ32 - COL32_SMALL - N_SMALL, D_MODEL), F32),
    ], axis=0).astype(BF16)

    def lane_pad(v, off):
        return jnp.zeros((1, 128), F32).at[0, off:off + v.shape[0]].set(v)

    bias_col = (lane_pad(p['ssd_dt_bias'][l], SMALL_DT) + lane_pad(p['ml_i_bias'][l], SMALL_I)
                + lane_pad(p['ml_f_bias'][l], SMALL_F))
    alog_col = lane_pad(p['ssd_A_log'][l], SMALL_DT)

    dt = jnp.exp(p['s5_log_dt'][l])[:, None]
    lr = p['s5_A_re'][l]
    li = p['s5_A_im'][l]
    mag = jnp.exp(lr * dt)
    abr, abi = mag * jnp.cos(li * dt), mag * jnp.sin(li * dt)
    den = lr * lr + li * li
    cr = ((abr - 1.0) * lr + abi * li) / den
    ci = (abi * lr - (abr - 1.0) * li) / den
    bbr, bbi = _cmul(cr[..., None], ci[..., None], p['s5_B_re'][l], p['s5_B_im'][l])
    eye8 = jnp.eye(8, dtype=F32)

    def pack_b(bb):
        return jnp.einsum('jgpc,gh->jgchp', bb.reshape(4, 8, S5_STATE, S5_GROUP), eye8).reshape(4, 128, 512)

    def pack_c(cc):
        return jnp.einsum('jgcp,gh->jgphc', cc.reshape(4, 8, S5_GROUP, S5_STATE), eye8).reshape(4, 512, 128)

    return {
        'norm_w': p['norm_w'][l].reshape(1, D_MODEL),
        'w_in': w_perm, 'w_in32': w_perm32,
        'w_small_t': small.astype(BF16),
        'bias_col': bias_col, 'bias_row': bias_col[0, :N_SMALL].reshape(N_SMALL, 1),
        'alog_col': alog_col, 'alog_row': alog_col[0, :N_SMALL].reshape(N_SMALL, 1),
        'ssd_conv_w': p['ssd_conv_w'][l], 'ssd_conv_b': p['ssd_conv_b'][l].reshape(1, SSD_CONV_CH),
        'ssd_D_exp': jnp.repeat(p['ssd_D'][l], SSD_HEADDIM).reshape(1, BRANCH_W),
        'ssd_norm_w': p['ssd_norm_w'][l].reshape(1, BRANCH_W),
        's5_wbr': pack_b(bbr).astype(BF16), 's5_wbi': pack_b(bbi).astype(BF16),
        's5_wcr': pack_c(p['s5_C_re'][l]).astype(BF16), 's5_wci': pack_c(p['s5_C_im'][l]).astype(BF16),
        's5_ar': jnp.broadcast_to(abr.reshape(1, S5_LANES), (8, S5_LANES)),
        's5_ai': jnp.broadcast_to(abi.reshape(1, S5_LANES), (8, S5_LANES)),
        's5_D': p['s5_D'][l].reshape(1, BRANCH_W), 's5_glu_w': p['s5_glu_w'][l].astype(BF16),
        'ml_conv_w': p['ml_conv_w'][l], 'ml_conv_b': p['ml_conv_b'][l].reshape(1, BRANCH_W),
        'ml_wq': p['ml_wq'][l], 'ml_wk': p['ml_wk'][l], 'ml_wv': p['ml_wv'][l],
        'ml_norm_w': p['ml_norm_w'][l].reshape(1, BRANCH_W), 'ml_skip': p['ml_skip'][l].reshape(1, BRANCH_W),
        'hg_lb': hg_lb[l].reshape(1, BRANCH_W), 'hg_norm_w': p['hg_norm_w'][l].reshape(1, BRANCH_W),
        'w_branch': p['w_branch'][l].astype(BF16), 'w_out': p['w_out'][l].astype(BF16),
    }


_MATMUL_WEIGHTS = ('ml_wq', 'ml_wk', 'ml_wv')
S5_CHUNK = 64
MIXER_PLAN = [(('ssd', 'mlstm', 'hgrn'), 2)]
MIXER_PLAN_PADDED = [(('ssd', 'mlstm', 'hgrn'), 8)]


def _run_group(x3, states, layers, fnw, *, q, valid):
    b, lp, _ = x3.shape
    nc = lp // q
    t = b * lp
    lowp = q >= 16
    consts = _chunk_consts(q)
    ssd_conv, ssd_h, s5_re, s5_im, ml_conv, ml_c, ml_n, ml_m, hg_s = states
    s5_re = s5_re.reshape(DEPTH, b, S5_LANES)
    s5_im = s5_im.reshape(DEPTH, b, S5_LANES)
    if valid < q:
        s5_kw = dict(b=b, lp=lp, q=q, valid=valid, bb=min(b, 128))
    else:
        qs = math.gcd(lp, S5_CHUNK)
        s5_kw = dict(b=b, lp=lp, q=qs, valid=qs, bb=8)
    ml_m = jnp.broadcast_to(jnp.pad(ml_m, ((0, 0), (0, 0), (0, 8 - ML_HEADS)))[..., None], (DEPTH, b, 8, 128))
    plan = [(names, math.gcd(b, n)) for names, n in (MIXER_PLAN_PADDED if valid < q else MIXER_PLAN)]
    kw = dict(b=b, q=q, nc=nc, valid=valid, lowp=lowp)
    x = x3.reshape(t, D_MODEL)
    ys, st, st_s5 = {}, {}, None
    for l, w in enumerate(layers):
        if lowp:
            w = dict(w, **{k: w[k].astype(BF16) for k in _MATMUL_WEIGHTS})
        proj, proj32, small_t = _inproj(x, w['norm_w'], w['w_in'], w['w_in32'], w['w_small_t'])
        proj3 = proj.reshape(b, lp, N_PROJ)
        proj32 = proj32.reshape(b, lp, N_PROJ32)
        srow = small_t.reshape(N_SMALL, b, lp).transpose(1, 0, 2)
        mixers = {
            'ssd': lambda n: _ssd(proj3, proj32, srow, ssd_conv, ssd_h, l, w, consts, bs=n, **kw),
            'mlstm': lambda n: _mlstm(proj3, proj32, srow, ml_conv, ml_c, ml_n, ml_m, l, w, consts, bs=n, **kw),
            'hgrn': lambda n: _hgrn(proj3, proj32, hg_s, l, w, consts, bs=n, **kw),
        }
        for names, n in plan:
            outs = _launch([mixers[k](n) for k in names], None if l == 0 else [st[k] for k in names],
                           grid=(b // n, nc), bs=n, nc=nc, name='_'.join(names))
            for k, o in zip(names, outs):
                ys[k], st[k] = o[0], list(o[1:])
        yb, *st_s5 = _s5(proj, s5_re, s5_im, l, w, st_s5, **s5_kw)
        ya, yc, yd = (ys[k].reshape(t, BRANCH_W) for k in ('ssd', 'mlstm', 'hgrn'))
        x = _merge(proj, (ya, yb, yc, yd), x, w['w_branch'], w['w_out'], fnw, final=(l == DEPTH - 1))
    n_ssd_conv, n_ssd_h = st['ssd']
    n_re, n_im = (s.reshape(DEPTH, b, S5_GROUPS, S5_STATE) for s in st_s5)
    n_ml_conv, n_c, n_n, n_m = st['mlstm']
    new_states = (n_ssd_conv, n_ssd_h, n_re, n_im, n_ml_conv, n_c, n_n, n_m[:, :, :ML_HEADS, 0], st['hgrn'][0])
    return x.reshape(b, lp, D_MODEL), new_states


SAMPLE_PAD = 8


def kernel(x_prompt, x_sample, state_ssd_conv, state_ssd, state_s5_re, state_s5_im, state_mlstm_conv, state_mlstm_C, state_mlstm_n, state_mlstm_m, state_hgrn, norm_w, w_in, ssd_conv_w, ssd_conv_b, ssd_dt_bias, ssd_A_log, ssd_D, ssd_norm_w, s5_A_re, s5_A_im, s5_B_re, s5_B_im, s5_C_re, s5_C_im, s5_D, s5_log_dt, s5_glu_w, ml_conv_w, ml_conv_b, ml_wq, ml_wk, ml_wv, ml_i_bias, ml_f_bias, ml_norm_w, ml_skip, hg_lb_logits, hg_norm_w, w_branch, w_out, final_norm_w):
    p = {'norm_w': norm_w, 'w_in': w_in,
         'ssd_conv_w': ssd_conv_w, 'ssd_conv_b': ssd_conv_b, 'ssd_dt_bias': ssd_dt_bias,
         'ssd_A_log': ssd_A_log, 'ssd_D': ssd_D, 'ssd_norm_w': ssd_norm_w,
         's5_A_re': s5_A_re, 's5_A_im': s5_A_im, 's5_B_re': s5_B_re, 's5_B_im': s5_B_im,
         's5_C_re': s5_C_re, 's5_C_im': s5_C_im, 's5_D': s5_D, 's5_log_dt': s5_log_dt, 's5_glu_w': s5_glu_w,
         'ml_conv_w': ml_conv_w, 'ml_conv_b': ml_conv_b, 'ml_wq': ml_wq, 'ml_wk': ml_wk, 'ml_wv': ml_wv,
         'ml_i_bias': ml_i_bias, 'ml_f_bias': ml_f_bias, 'ml_norm_w': ml_norm_w, 'ml_skip': ml_skip,
         'hg_norm_w': hg_norm_w, 'w_branch': w_branch, 'w_out': w_out}
    lb_cum = jnp.cumsum(jax.nn.softmax(hg_lb_logits, axis=0), axis=0)
    hg_lb = lb_cum - lb_cum[0]
    layers = [_layer_params(l, p, hg_lb) for l in range(DEPTH)]
    fnw = final_norm_w.reshape(1, D_MODEL)

    sample_states = (state_ssd_conv, state_ssd, state_s5_re, state_s5_im, state_mlstm_conv,
                     state_mlstm_C, state_mlstm_n, state_mlstm_m, state_hgrn)
    bp, lp_, _ = x_prompt.shape
    prompt_states = tuple(jnp.zeros((DEPTH, bp) + s.shape[2:], F32) for s in sample_states)
    q_prompt = math.gcd(lp_, 128)
    y_prompt, new_p = _run_group(x_prompt, prompt_states, layers, fnw, q=q_prompt, valid=q_prompt)

    ls = x_sample.shape[1]
    xs_pad = jnp.pad(x_sample, ((0, 0), (0, SAMPLE_PAD - ls), (0, 0)))
    y_sample, new_s = _run_group(xs_pad, sample_states, layers, fnw, q=SAMPLE_PAD, valid=ls)
    y_sample = y_sample[:, :ls]

    out = [y_prompt, y_sample]
    for ps, ss in zip(new_p, new_s):
        out += [ps, ss]
    return tuple(out)
```

```python
import functools
import math
from typing import Callable, NamedTuple

import numpy as np
import jax
import jax.numpy as jnp
from jax import lax
from jax.experimental import pallas as pl
from jax.experimental.pallas import tpu as pltpu

F32 = jnp.float32
BF16 = jnp.bfloat16

D_MODEL = 1024
DEPTH = 2
BRANCH_W = 512
CONV_W = 4
EPS = 1e-6
SSD_HEADS = 8
SSD_HEADDIM = 64
SSD_STATE = 64
SSD_GROUPS = 2
SSD_CONV_CH = 768
S5_GROUPS = 32
S5_GROUP = 16
S5_STATE = 64
S5_LANES = S5_GROUPS * S5_STATE
ML_HEADS = 4
ML_HEADDIM = 128
HG_HEADS = 4
HG_HEADDIM = 128
N_BRANCH = 4

NEG = -1e30

N_PROJ = 9216
COL_MERGE = 0
COL_A_Z = 4096
COL_A_X = 4608
COL_B_U = 5120
COL_B_GATE = 5632
COL_C_X = 6144
COL_C_Z = 6656
COL_C_O = 7168
COL_D_I = 7680
COL_D_Q = 8192
COL_D_G = 8704
N_PROJ32 = 896
COL32_D_F = 0
COL32_A_BC = 512
COL32_SMALL = 768
N_SMALL = 16
SMALL_DT = 0
SMALL_I = 8
SMALL_F = 12

NN = (((1,), (0,)), ((), ()))
NT = (((1,), (1,)), ((), ()))
TN = (((0,), (0,)), ((), ()))


def _mm(a, b, lowp, dims=NN):
    if lowp:
        a = a.astype(BF16)
        b = b.astype(BF16)
    return lax.dot_general(a, b, dims, preferred_element_type=F32)


def _split3(a):
    hi = a.astype(BF16)
    r = a - hi.astype(F32)
    mid = r.astype(BF16)
    lo = (r - mid.astype(F32)).astype(BF16)
    return hi, mid, lo


def _xdot(a, b, dims, split, lowp):
    if not lowp and min((b if split == 0 else a).shape) < 16:
        return lax.dot_general(a, b, dims, precision=lax.Precision.HIGHEST,
                               preferred_element_type=F32)
    if split == 0:
        other = b.astype(BF16)
        return sum(lax.dot_general(p, other, dims, preferred_element_type=F32) for p in _split3(a))
    other = a.astype(BF16)
    return sum(lax.dot_general(other, p, dims, preferred_element_type=F32) for p in _split3(b))


def _sigmoid(x):
    return 0.5 * jnp.tanh(0.5 * x) + 0.5


def _silu(x):
    h = 0.5 * x
    return h * jnp.tanh(h) + h


def _softplus(x):
    return jnp.maximum(x, 0.0) + jnp.log1p(jnp.exp(-jnp.abs(x)))


def _conv_chunk(xp_scr, cw_ref, cb_ref, q, valid):
    cw = cw_ref[...]
    conv = (cb_ref[...]
            + cw[0:1] * xp_scr[5:5 + q, :]
            + cw[1:2] * xp_scr[6:6 + q, :]
            + cw[2:3] * xp_scr[7:7 + q, :]
            + cw[3:4] * xp_scr[8:8 + q, :])
    carry = xp_scr[5 + valid:8 + valid, :]
    xp_scr[5:8, :] = carry
    return conv, carry


def _inproj_kernel(x_ref, nw_ref, w_ref, w32_ref, wst_ref, o_ref, o32_ref, st_ref, xn_scr):
    @pl.when(pl.program_id(1) == 0)
    def _():
        x = x_ref[...]
        xn = x * lax.rsqrt(jnp.mean(x * x, -1, keepdims=True) + EPS) * nw_ref[...]
        xb = xn.astype(BF16)
        xn_scr[...] = xb
        o32_ref[...] = lax.dot_general(xb, w32_ref[...], NT, preferred_element_type=F32)
        st_ref[...] = lax.dot_general(wst_ref[...], xb, NT, preferred_element_type=F32)

    o_ref[...] = lax.dot_general(xn_scr[...], w_ref[...], NT, preferred_element_type=F32).astype(BF16)


def _inproj(x, nw, w, w32, wst):
    t = x.shape[0]
    tm = min(1024, t)
    tn = N_PROJ // 4
    return pl.pallas_call(
        _inproj_kernel,
        grid=(t // tm, N_PROJ // tn),
        in_specs=[
            pl.BlockSpec((tm, D_MODEL), lambda i, j: (i, 0)),
            pl.BlockSpec((1, D_MODEL), lambda i, j: (0, 0)),
            pl.BlockSpec((tn, D_MODEL), lambda i, j: (j, 0)),
            pl.BlockSpec((N_PROJ32, D_MODEL), lambda i, j: (0, 0)),
            pl.BlockSpec((N_SMALL, D_MODEL), lambda i, j: (0, 0)),
        ],
        out_specs=[
            pl.BlockSpec((tm, tn), lambda i, j: (i, j)),
            pl.BlockSpec((tm, N_PROJ32), lambda i, j: (i, 0)),
            pl.BlockSpec((N_SMALL, tm), lambda i, j: (0, i)),
        ],
        out_shape=[
            jax.ShapeDtypeStruct((t, N_PROJ), BF16),
            jax.ShapeDtypeStruct((t, N_PROJ32), F32),
            jax.ShapeDtypeStruct((N_SMALL, t), F32),
        ],
        scratch_shapes=[pltpu.VMEM((tm, D_MODEL), BF16)],
        compiler_params=pltpu.CompilerParams(
            dimension_semantics=("arbitrary", "arbitrary"),
            vmem_limit_bytes=48 * 1024 * 1024),
        name="inproj",
    )(x, nw, w, w32, wst)


def _merge_kernel(g_ref, ya_ref, yb_ref, yc_ref, yd_ref, x_ref, wb_ref, wo_ref, fnw_ref, o_ref, *, final):
    acc = None
    for i, y_ref in enumerate((ya_ref, yb_ref, yc_ref, yd_ref)):
        gate = _sigmoid(g_ref[:, i * D_MODEL:(i + 1) * D_MODEL].astype(F32))
        term = gate * jnp.dot(y_ref[...], wb_ref[i], preferred_element_type=F32)
        acc = term if acc is None else acc + term
    out = x_ref[...] + jnp.dot(acc.astype(BF16), wo_ref[...], preferred_element_type=F32)
    if final:
        out = out * lax.rsqrt(jnp.mean(out * out, -1, keepdims=True) + EPS) * fnw_ref[...]
    o_ref[...] = out


def _merge(proj, ys, x, wb, wo, fnw, final):
    t = x.shape[0]
    tm = min(512, t)
    yspec = pl.BlockSpec((tm, BRANCH_W), lambda i: (i, 0))
    return pl.pallas_call(
        functools.partial(_merge_kernel, final=final),
        grid=(t // tm,),
        in_specs=[
            pl.BlockSpec((tm, N_BRANCH * D_MODEL), lambda i: (i, COL_MERGE // (N_BRANCH * D_MODEL))),
            yspec, yspec, yspec, yspec,
            pl.BlockSpec((tm, D_MODEL), lambda i: (i, 0)),
            pl.BlockSpec((N_BRANCH, BRANCH_W, D_MODEL), lambda i: (0, 0, 0)),
            pl.BlockSpec((D_MODEL, D_MODEL), lambda i: (0, 0)),
            pl.BlockSpec((1, D_MODEL), lambda i: (0, 0)),
        ],
        out_specs=pl.BlockSpec((tm, D_MODEL), lambda i: (i, 0)),
        out_shape=jax.ShapeDtypeStruct((t, D_MODEL), F32),
        compiler_params=pltpu.CompilerParams(
            dimension_semantics=("arbitrary",),
            vmem_limit_bytes=48 * 1024 * 1024),
        name="merge",
    )(proj, *ys, x, wb, wo, fnw)


def _col_spec(bs, q, col, width):
    return pl.BlockSpec((bs, q, width), lambda b, c: (b, c, col // width))


def _row_spec(bs, q):
    return pl.BlockSpec((bs, N_SMALL, q), lambda b, c: (b, 0, c))


def _const_spec(shape):
    nd = len(shape)
    return pl.BlockSpec(shape, lambda b, c: (0,) * nd)


def _state_in_spec(layer, bs, tail):
    nd = len(tail)
    return pl.BlockSpec((1, bs) + tail, lambda b, c: (layer, b) + (0,) * nd)


def _state_out_spec(layer, bs, tail):
    nd = len(tail)
    return pl.BlockSpec((1, bs) + tail, lambda b, c: (layer, b) + (0,) * nd)


def _y_spec(bs, q):
    return pl.BlockSpec((bs, q, BRANCH_W), lambda b, c: (b, c, 0))


class _Mixer(NamedTuple):
    in_specs: list
    args: list
    out_specs: list
    out_shapes: list
    scratch: list
    group: Callable


_DONE = object()
_MIXER_PARAMS = pltpu.CompilerParams(
    dimension_semantics=("arbitrary", "arbitrary"),
    vmem_limit_bytes=48 * 1024 * 1024)


def _launch(mixers, prevs, *, grid, bs, nc, name):
    n_in = [len(m.in_specs) for m in mixers]
    n_out = [len(m.out_specs) for m in mixers]
    n_scr = [len(m.scratch) for m in mixers]
    tot_in, tot_out = sum(n_in), sum(n_out)
    state_idx = [sum(n_out[:i]) + k for i in range(len(mixers)) for k in range(1, n_out[i])]
    n_alias = 0 if prevs is None else len(state_idx)

    def kern(*refs):
        ins, outs, scr = refs[:tot_in], refs[tot_in + n_alias:tot_in + n_alias + tot_out], refs[tot_in + n_alias + tot_out:]
        groups = []
        for i, m in enumerate(mixers):
            groups.append(m.group(ins[sum(n_in[:i]):sum(n_in[:i + 1])],
                                  outs[sum(n_out[:i]):sum(n_out[:i + 1])],
                                  scr[sum(n_scr[:i]):sum(n_scr[:i + 1])]))
        c = pl.program_id(1)
        views = [[[r.at[s] for r in g[3]] for s in range(bs)] for g in groups]

        @pl.when(c == 0)
        def _():
            for g, v in zip(groups, views):
                for s in range(bs):
                    g[0](v[s])

        active = [g[1](v[s], g[4]) for s in range(bs) for g, v in zip(groups, views)]
        while active:
            active = [gen for gen in active if next(gen, _DONE) is not _DONE]

        @pl.when(c == nc - 1)
        def _():
            for g, v in zip(groups, views):
                for s in range(bs):
                    g[2](v[s])

    in_specs = [sp for m in mixers for sp in m.in_specs] + [pl.BlockSpec(memory_space=pl.ANY)] * n_alias
    args = [a for m in mixers for a in m.args]
    if prevs is not None:
        args += [a for p in prevs for a in p]
    outs = pl.pallas_call(
        kern,
        grid=grid,
        in_specs=in_specs,
        out_specs=[sp for m in mixers for sp in m.out_specs],
        out_shape=[sh for m in mixers for sh in m.out_shapes],
        scratch_shapes=[sc for m in mixers for sc in m.scratch],
        input_output_aliases={tot_in + k: idx for k, idx in enumerate(state_idx)} if n_alias else {},
        compiler_params=_MIXER_PARAMS,
        name=name,
    )(*args)
    return [outs[sum(n_out[:i]):sum(n_out[:i + 1])] for i in range(len(mixers))]


def _ssd_group(ins, outs, scr, *, q, valid, lowp):
    z_ref, x_ref, bc_ref, sc_ref, sr_ref, conv0_ref, h0_ref, *consts = ins
    y_ref, convo_ref, ho_ref = outs
    xp_scr, h_scr, y_scr = scr

    def init(v):
        v[9][5:8, :] = v[5][...]
        v[10][...] = v[6][...]

    def final(v):
        v[8][...] = v[10][...]

    return (init, functools.partial(_ssd_main, q=q, valid=valid, lowp=lowp), final,
            (z_ref, x_ref, bc_ref, sc_ref, sr_ref, conv0_ref.at[0], h0_ref.at[0],
             y_ref, ho_ref.at[0], xp_scr, h_scr, y_scr, convo_ref.at[0]),
            tuple(consts))


def _ssd_main(v, k, *, q, valid, lowp):
    z_ref, x_ref, bc_ref, sc_ref, sr_ref, _, _, y_ref, _, xp_scr, h_scr, y_scr, convo_ref = v
    cw_ref, cb_ref, bcol_ref, brow_ref, alc_ref, alr_ref, dexp_ref, nw_ref, e_ref, tl_ref, tu_ref = k
    xp_scr[8:8 + q, 0:BRANCH_W] = x_ref[...].astype(F32)
    xp_scr[8:8 + q, BRANCH_W:SSD_CONV_CH] = bc_ref[...]
    conv, carry = _conv_chunk(xp_scr, cw_ref, cb_ref, q, valid)
    convo_ref[...] = carry
    yield
    xbc = _silu(conv)
    xs = xbc[:, 0:BRANCH_W]
    bm = xbc[:, BRANCH_W:BRANCH_W + 128]
    cm = xbc[:, BRANCH_W + 128:BRANCH_W + 256]

    dt_c = _softplus(sc_ref[...] + bcol_ref[...])
    a_c = dt_c * (-jnp.exp(alc_ref[...]))
    dt_r = _softplus(sr_ref[...] + brow_ref[...])
    a_r = dt_r * (-jnp.exp(alr_ref[...]))
    if valid < q:
        tcol = lax.broadcasted_iota(jnp.int32, (q, 1), 0)
        trow = lax.broadcasted_iota(jnp.int32, (1, q), 1)
        a_c = jnp.where(tcol < valid, a_c, 0.0)
        dt_c = jnp.where(tcol < valid, dt_c, 0.0)
        a_r = jnp.where(trow < valid, a_r, 0.0)

    tl = tl_ref[...]
    acum_c = _xdot(tl, a_c, NN, 1, lowp)
    acum_r = _xdot(a_r, tu_ref[...], NN, 0, lowp)
    e = e_ref[...]
    dt_e = _xdot(dt_c, e, NN, 0, lowp)
    acum_e = _xdot(acum_c, e, NN, 0, lowp)
    xdt = xs * dt_e
    xend = xdt * jnp.exp(acum_e[q - 1:q, :] - acum_e)
    eac = jnp.exp(acum_e)
    dec_last = jnp.exp(acum_c[q - 1:q, :])
    yield

    fine = q < 16
    tri = (lax.broadcasted_iota(jnp.int32, (q, q), 0) >= lax.broadcasted_iota(jnp.int32, (q, q), 1))
    rpg = SSD_HEADS // SSD_GROUPS
    for g in range(SSD_GROUPS):
        bg = bm[:, g * SSD_STATE:(g + 1) * SSD_STATE]
        cg = cm[:, g * SSD_STATE:(g + 1) * SSD_STATE]
        cb = _mm(cg, bg, lowp, NT)
        for r in range(rpg):
            h = g * rpg + r
            hs = slice(h * SSD_HEADDIM, (h + 1) * SSD_HEADDIM)
            seg = acum_c[:, h:h + 1] - acum_r[h:h + 1, :]
            m = cb * jnp.exp(jnp.where(tri, seg, NEG))
            hp = h_scr[h]
            yh = _mm(m, xdt[:, hs], lowp) + _mm(cg, hp, lowp, NT) * eac[:, hs]
            y_scr[:, hs] = yh
            if fine:
                yield
            h_scr[h] = hp * dec_last[:, h:h + 1] + _mm(xend[:, hs], bg, lowp, TN)
            yield

    y = y_scr[...] + xs * dexp_ref[...]
    y = y * _silu(z_ref[...].astype(F32))
    y = y * lax.rsqrt(jnp.mean(y * y, -1, keepdims=True) + EPS) * nw_ref[...]
    y_ref[...] = y.astype(BF16)


def _ssd(proj, proj32, srow, conv_state, h_state, layer, w, consts, *, b, q, nc, valid, lowp, bs):
    return _Mixer(
        in_specs=[
            _col_spec(bs, q, COL_A_Z, BRANCH_W),
            _col_spec(bs, q, COL_A_X, BRANCH_W),
            _col_spec(bs, q, COL32_A_BC, 256),
            _col_spec(bs, q, COL32_SMALL, 128),
            _row_spec(bs, q),
            _state_in_spec(layer, bs, (CONV_W - 1, SSD_CONV_CH)),
            _state_in_spec(layer, bs, (SSD_HEADS, SSD_HEADDIM, SSD_STATE)),
            _const_spec((CONV_W, SSD_CONV_CH)),
            _const_spec((1, SSD_CONV_CH)),
            _const_spec((1, 128)),
            _const_spec((N_SMALL, 1)),
            _const_spec((1, 128)),
            _const_spec((N_SMALL, 1)),
            _const_spec((1, BRANCH_W)),
            _const_spec((1, BRANCH_W)),
            _const_spec((128, BRANCH_W)),
            _const_spec((q, q)),
            _const_spec((q, q)),
        ],
        out_specs=[
            _y_spec(bs, q),
            _state_out_spec(layer, bs, (CONV_W - 1, SSD_CONV_CH)),
            _state_out_spec(layer, bs, (SSD_HEADS, SSD_HEADDIM, SSD_STATE)),
        ],
        out_shapes=[
            jax.ShapeDtypeStruct((b, nc * q, BRANCH_W), BF16),
            jax.ShapeDtypeStruct((DEPTH, b, CONV_W - 1, SSD_CONV_CH), F32),
            jax.ShapeDtypeStruct((DEPTH, b, SSD_HEADS, SSD_HEADDIM, SSD_STATE), F32),
        ],
        scratch=[
            pltpu.VMEM((bs, q + 8, SSD_CONV_CH), F32),
            pltpu.VMEM((bs, SSD_HEADS, SSD_HEADDIM, SSD_STATE), F32),
            pltpu.VMEM((bs, q, BRANCH_W), F32),
        ],
        args=[proj, proj, proj32, proj32, srow, conv_state, h_state,
              w['ssd_conv_w'], w['ssd_conv_b'], w['bias_col'], w['bias_row'], w['alog_col'], w['alog_row'],
              w['ssd_D_exp'], w['ssd_norm_w'], consts['expand'], consts['tril'], consts['triu']],
        group=functools.partial(_ssd_group, q=q, valid=valid, lowp=lowp))


def _s5_kernel(u_ref, gate_ref, hr0_ref, hi0_ref, perm_ref, permt_ref, ar_ref, ai_ref,
               wbr_ref, wbi_ref, wcr_ref, wci_ref, d_ref, glu_ref,
               y_ref, hro_ref, hio_ref,
               hr_scr, hi_scr, cr_scr, ci_scr, *, q, nc, ngrp):
    c = pl.program_id(1)

    @pl.when(c == 0)
    def _():
        cr_scr[...] = hr0_ref[0]
        ci_scr[...] = hi0_ref[0]

    rows_in = perm_ref.shape[1]
    u = u_ref[...].reshape(rows_in, BRANCH_W)
    u_tm = jnp.dot(perm_ref[...], u, preferred_element_type=F32).astype(BF16)
    nblk = S5_LANES // 512
    for j in range(nblk):
        uj = u_tm[:, j * 128:(j + 1) * 128]
        hr_scr[:, j * 512:(j + 1) * 512] = jnp.dot(uj, wbr_ref[j], preferred_element_type=F32)
        hi_scr[:, j * 512:(j + 1) * 512] = jnp.dot(uj, wbi_ref[j], preferred_element_type=F32)

    half = S5_LANES // 2
    for hf in range(2):
        sl = slice(hf * half, (hf + 1) * half)
        ar = ar_ref[:, sl]
        ai = ai_ref[:, sl]

        def grp_body(g, carry, sl=sl, ar=ar, ai=ai):
            s0 = pl.multiple_of(g * 8, 8)

            def t_body(t, h):
                r0 = pl.multiple_of(g * (8 * q) + t * 8, 8)
                hr, hi = h
                nr = ar * hr - ai * hi + hr_scr[pl.ds(r0, 8), sl]
                ni = ar * hi + ai * hr + hi_scr[pl.ds(r0, 8), sl]
                hr_scr[pl.ds(r0, 8), sl] = nr
                hi_scr[pl.ds(r0, 8), sl] = ni
                return nr, ni

            h = lax.fori_loop(0, q, t_body, (cr_scr[pl.ds(s0, 8), sl], ci_scr[pl.ds(s0, 8), sl]),
                              unroll=min(q, 4))
            cr_scr[pl.ds(s0, 8), sl] = h[0]
            ci_scr[pl.ds(s0, 8), sl] = h[1]
            return carry

        lax.fori_loop(0, ngrp, grp_body, 0)

    parts = []
    for j in range(nblk):
        sl = slice(j * 512, (j + 1) * 512)
        parts.append(jnp.dot(hr_scr[:, sl].astype(BF16), wcr_ref[j], preferred_element_type=F32)
                     - jnp.dot(hi_scr[:, sl].astype(BF16), wci_ref[j], preferred_element_type=F32))
    y_tm = jnp.concatenate(parts, axis=1)
    y = _xdot(permt_ref[...], y_tm, NN, 1, True) + u.astype(F32) * d_ref[...]
    g = jax.nn.gelu(y)
    y = g * _sigmoid(jnp.dot(g.astype(BF16), glu_ref[...], preferred_element_type=F32))
    y = y * _silu(gate_ref[...].reshape(rows_in, BRANCH_W).astype(F32))
    y_ref[...] = y.reshape(y_ref.shape).astype(BF16)

    @pl.when(c == nc - 1)
    def _():
        hro_ref[0] = cr_scr[...]
        hio_ref[0] = ci_scr[...]


def _s5_perm(bb, lp, valid):
    perm = np.zeros((bb * valid, bb * lp), np.float32)
    for g in range(bb // 8):
        for t in range(valid):
            for s in range(8):
                perm[g * 8 * valid + t * 8 + s, (g * 8 + s) * lp + t] = 1.0
    return jnp.asarray(perm, BF16), jnp.asarray(perm.T, BF16)


def _s5(proj, hr_state, hi_state, layer, w, prev, *, b, lp, q, valid, bb):
    nc = lp // q
    perm, permt = _s5_perm(bb, q, valid)
    rows = bb * valid
    if nc == 1:
        def col_spec(col):
            return pl.BlockSpec((bb * q, BRANCH_W), lambda i, c: (i, col // BRANCH_W))
        src = proj
        y_spec = pl.BlockSpec((bb * q, BRANCH_W), lambda i, c: (i, 0))
        y_shape = jax.ShapeDtypeStruct((b * lp, BRANCH_W), BF16)
    else:
        def col_spec(col):
            return pl.BlockSpec((bb, q, BRANCH_W), lambda i, c: (i, c, col // BRANCH_W))
        src = proj.reshape(b, lp, N_PROJ)
        y_spec = pl.BlockSpec((bb, q, BRANCH_W), lambda i, c: (i, c, 0))
        y_shape = jax.ShapeDtypeStruct((b, lp, BRANCH_W), BF16)
    st_in = pl.BlockSpec((1, bb, S5_LANES), lambda i, c: (layer, i, 0))
    st_out = st_in
    kern = functools.partial(_s5_kernel, q=valid, nc=nc, ngrp=bb // 8)
    n_in = 14
    alias_kw = {}
    if prev is not None:
        base = kern

        def kern(*refs):
            return base(*refs[:n_in], *refs[n_in + 2:])

        alias_kw = dict(input_output_aliases={n_in: 1, n_in + 1: 2})
    y, hr, hi = pl.pallas_call(
        kern,
        grid=(b // bb, nc),
        **alias_kw,
        in_specs=[
            col_spec(COL_B_U),
            col_spec(COL_B_GATE),
            st_in, st_in,
            _const_spec(perm.shape),
            _const_spec(permt.shape),
            _const_spec((8, S5_LANES)),
            _const_spec((8, S5_LANES)),
            _const_spec((4, 128, 512)),
            _const_spec((4, 128, 512)),
            _const_spec((4, 512, 128)),
            _const_spec((4, 512, 128)),
            _const_spec((1, BRANCH_W)),
            _const_spec((BRANCH_W, BRANCH_W)),
        ] + [pl.BlockSpec(memory_space=pl.ANY)] * (0 if prev is None else 2),
        out_specs=[y_spec, st_out, st_out],
        out_shape=[
            y_shape,
            jax.ShapeDtypeStruct((DEPTH, b, S5_LANES), F32),
            jax.ShapeDtypeStruct((DEPTH, b, S5_LANES), F32),
        ],
        scratch_shapes=[
            pltpu.VMEM((rows, S5_LANES), F32),
            pltpu.VMEM((rows, S5_LANES), F32),
            pltpu.VMEM((bb, S5_LANES), F32),
            pltpu.VMEM((bb, S5_LANES), F32),
        ],
        compiler_params=_MIXER_PARAMS,
        name="s5",
    )(src, src, hr_state, hi_state, perm, permt, w['s5_ar'], w['s5_ai'],
      w['s5_wbr'], w['s5_wbi'], w['s5_wcr'], w['s5_wci'], w['s5_D'], w['s5_glu_w'], *(prev or ()))
    return y.reshape(b * lp, BRANCH_W), hr, hi


def _mlstm_group(ins, outs, scr, *, q, valid, lowp):
    x_ref, z_ref, o_ref, sc_ref, sr_ref, conv0_ref, c0_ref, n0_ref, m0_ref, *consts = ins
    y_ref, convo_ref, co_ref, no_ref, mo_ref = outs
    xp_scr, c_scr, n_scr, m_scr, h_scr = scr

    def init(v):
        v[13][5:8, :] = v[5][...]
        v[14][...] = v[6][...]
        v[15][...] = v[7][...]
        v[16][...] = v[8][...]

    def final(v):
        v[10][...] = v[14][...]
        v[11][...] = v[15][...]
        v[12][...] = v[16][...]

    return (init, functools.partial(_mlstm_main, q=q, valid=valid, lowp=lowp), final,
            (x_ref, z_ref, o_ref, sc_ref, sr_ref, conv0_ref.at[0], c0_ref.at[0], n0_ref.at[0], m0_ref.at[0],
             y_ref, co_ref.at[0], no_ref.at[0], mo_ref.at[0], xp_scr, c_scr, n_scr, m_scr, h_scr,
             convo_ref.at[0]),
            tuple(consts))


def _mlstm_main(v, k, *, q, valid, lowp):
    (x_ref, z_ref, o_ref, sc_ref, sr_ref, _, _, _, _, y_ref, _, _, _,
     xp_scr, c_scr, n_scr, m_scr, h_scr, convo_ref) = v
    cw_ref, cb_ref, wq_ref, wk_ref, wv_ref, bcol_ref, brow_ref, nw_ref, skip_ref, tl_ref, tu_ref = k
    x = x_ref[...].astype(F32)
    xp_scr[8:8 + q, :] = x
    conv, carry = _conv_chunk(xp_scr, cw_ref, cb_ref, q, valid)
    convo_ref[...] = carry
    yield
    xc = _silu(conv)

    pre_c = sc_ref[...] + bcol_ref[...]
    pre_r = sr_ref[...] + brow_ref[...]
    ig_c = pre_c
    lf_c = -_softplus(-pre_c)
    ig_r = pre_r
    lf_r = -_softplus(-pre_r)
    if valid < q:
        tcol = lax.broadcasted_iota(jnp.int32, (q, 1), 0)
        trow = lax.broadcasted_iota(jnp.int32, (1, q), 1)
        lf_c = jnp.where(tcol < valid, lf_c, 0.0)
        ig_c = jnp.where(tcol < valid, ig_c, NEG)
        lf_r = jnp.where(trow < valid, lf_r, 0.0)
        ig_r = jnp.where(trow < valid, ig_r, NEG)
    b_c = _xdot(tl_ref[...], lf_c, NN, 1, lowp)
    b_r = _xdot(lf_r, tu_ref[...], NN, 0, lowp)
    yield

    fine = q < 16
    tri = (lax.broadcasted_iota(jnp.int32, (q, q), 0) >= lax.broadcasted_iota(jnp.int32, (q, q), 1))
    o_all = o_ref[...].astype(F32)
    for h in range(ML_HEADS):
        hs = slice(h * ML_HEADDIM, (h + 1) * ML_HEADDIM)
        xh = xc[:, hs]
        qh = _mm(xh, wq_ref[h], lowp)
        kh = _mm(xh, wk_ref[h], lowp) * (ML_HEADDIM ** -0.5)
        vh = _mm(x[:, hs], wv_ref[h], lowp)
        yield
        bc = b_c[:, SMALL_F + h:SMALL_F + h + 1]
        br = b_r[SMALL_F + h:SMALL_F + h + 1, :]
        ic = ig_c[:, SMALL_I + h:SMALL_I + h + 1]
        ir = ig_r[SMALL_I + h:SMALL_I + h + 1, :]
        mp = m_scr[h:h + 1, 0:1]
        dlog = jnp.where(tri, bc - br + ir, NEG)
        inter = bc + mp
        m_t = jnp.maximum(inter, jnp.max(dlog, axis=1, keepdims=True))
        wgt = jnp.exp(dlog - m_t)
        yield
        s = _mm(qh, kh, lowp, NT) * wgt
        if fine:
            yield
        scale = jnp.exp(inter - m_t)
        cp = c_scr[h]
        npv = n_scr[h:h + 1, :]
        num = _mm(s, vh, lowp) + scale * _mm(qh, cp, lowp)
        dot =jnp.sum(s, axis=1, keepdims=True) + scale * jnp.sum(qh * npv, axis=1, keepdims=True)
        hh = num / jnp.maximum(jnp.abs(dot), jnp.exp(-m_t))
        yield
        m_new = m_t[q - 1:q, :]
        b_last = bc[q - 1:q, :]
        w_end = jnp.exp(b_last - bc + ic - m_new)
        cs = jnp.exp(b_last + mp - m_new)
        if fine:
            yield
        kw = kh * w_end
        c_scr[h] = cs * cp + _mm(kw, vh, lowp, TN)
        n_scr[h:h + 1, :] = cs * npv + jnp.sum(kw, axis=0, keepdims=True)
        m_scr[h:h + 1, :] = jnp.broadcast_to(m_new, (1, 128))
        yield
        hh = hh * _sigmoid(o_all[:, hs])
        hc = hh - jnp.mean(hh, -1, keepdims=True)
        h_scr[:, hs] = hc * lax.rsqrt(jnp.mean(hc * hc, -1, keepdims=True) + EPS)
        yield

    y = h_scr[...] * nw_ref[...] + skip_ref[...] * xc
    y_ref[...] = (y * _silu(z_ref[...].astype(F32))).astype(BF16)


def _mlstm(proj, proj32, srow, conv_state, c_state, n_state, m_state, layer, w, consts,
           *, b, q, nc, valid, lowp, bs):
    return _Mixer(
        in_specs=[
            _col_spec(bs, q, COL_C_X, BRANCH_W),
            _col_spec(bs, q, COL_C_Z, BRANCH_W),
            _col_spec(bs, q, COL_C_O, BRANCH_W),
            _col_spec(bs, q, COL32_SMALL, 128),
            _row_spec(bs, q),
            _state_in_spec(layer, bs, (CONV_W - 1, BRANCH_W)),
            _state_in_spec(layer, bs, (ML_HEADS, ML_HEADDIM, ML_HEADDIM)),
            _state_in_spec(layer, bs, (ML_HEADS, ML_HEADDIM)),
            _state_in_spec(layer, bs, (8, 128)),
            _const_spec((CONV_W, BRANCH_W)),
            _const_spec((1, BRANCH_W)),
            _const_spec((ML_HEADS, ML_HEADDIM, ML_HEADDIM)),
            _const_spec((ML_HEADS, ML_HEADDIM, ML_HEADDIM)),
            _const_spec((ML_HEADS, ML_HEADDIM, ML_HEADDIM)),
            _const_spec((1, 128)),
            _const_spec((N_SMALL, 1)),
            _const_spec((1, BRANCH_W)),
            _const_spec((1, BRANCH_W)),
            _const_spec((q, q)),
            _const_spec((q, q)),
        ],
        out_specs=[
            _y_spec(bs, q),
            _state_out_spec(layer, bs, (CONV_W - 1, BRANCH_W)),
            _state_out_spec(layer, bs, (ML_HEADS, ML_HEADDIM, ML_HEADDIM)),
            _state_out_spec(layer, bs, (ML_HEADS, ML_HEADDIM)),
            _state_out_spec(layer, bs, (8, 128)),
        ],
        out_shapes=[
            jax.ShapeDtypeStruct((b, nc * q, BRANCH_W), BF16),
            jax.ShapeDtypeStruct((DEPTH, b, CONV_W - 1, BRANCH_W), F32),
            jax.ShapeDtypeStruct((DEPTH, b, ML_HEADS, ML_HEADDIM, ML_HEADDIM), F32),
            jax.ShapeDtypeStruct((DEPTH, b, ML_HEADS, ML_HEADDIM), F32),
            jax.ShapeDtypeStruct((DEPTH, b, 8, 128), F32),
        ],
        scratch=[
            pltpu.VMEM((bs, q + 8, BRANCH_W), F32),
            pltpu.VMEM((bs, ML_HEADS, ML_HEADDIM, ML_HEADDIM), F32),
            pltpu.VMEM((bs, ML_HEADS, ML_HEADDIM), F32),
            pltpu.VMEM((bs, 8, 128), F32),
            pltpu.VMEM((bs, q, BRANCH_W), F32),
        ],
        args=[proj, proj, proj, proj32, srow, conv_state, c_state, n_state, m_state,
              w['ml_conv_w'], w['ml_conv_b'], w['ml_wq'], w['ml_wk'], w['ml_wv'], w['bias_col'], w['bias_row'],
              w['ml_norm_w'], w['ml_skip'], consts['tril'], consts['triu']],
        group=functools.partial(_mlstm_group, q=q, valid=valid, lowp=lowp))


def _hgrn_group(ins, outs, scr, *, q, valid, lowp, nlev):
    f_ref, i_ref, q_ref, g_ref, s0_ref, *consts = ins
    y_ref, so_ref = outs
    s_scr, o_scr = scr

    def init(v):
        v[7][...] = v[4][...]

    def final(v):
        v[6][...] = v[7][...]

    return (init, functools.partial(_hgrn_main, q=q, valid=valid, lowp=lowp, nlev=nlev), final,
            (f_ref, i_ref, q_ref, g_ref, s0_ref.at[0], y_ref, so_ref.at[0], s_scr, o_scr),
            tuple(consts))


def _hgrn_main(v, k, *, q, valid, lowp, nlev):
    f_ref, i_ref, q_ref, g_ref, _, y_ref, _, s_scr, o_scr = v
    lb_ref, nw_ref, tl_ref = k
    lb = lb_ref[...]
    fg = lb + (1.0 - lb) * _sigmoid(f_ref[...])
    kk = 1.0 - fg
    qq = _silu(q_ref[...].astype(F32)) * (HG_HEADDIM ** -0.5)
    vv = i_ref[...] if lowp else i_ref[...].astype(F32)
    rcol =lax.broadcasted_iota(jnp.int32, (q, 1), 0)
    if valid < q:
        fg = jnp.where(rcol < valid, fg, 1.0)
        kk = jnp.where(rcol < valid, kk, 0.0)
    lf = jnp.log(fg)

    gcum = _xdot(tl_ref[...], lf, NN, 1, lowp)
    yield

    fine = q < 16
    rt = lax.broadcasted_iota(jnp.int32, (q, q), 0)
    cs = lax.broadcasted_iota(jnp.int32, (q, q), 1)
    rx = rt ^ cs
    att = [None] * HG_HEADS
    for lev in range(nlev):
        m = q >> (lev + 1)
        second = (rcol & m) != 0
        if m >= 4:
            nb = q // (2 * m)
            gb = jnp.broadcast_to(gcum.reshape(nb, 2 * m, BRANCH_W)[:, m - 1:m, :], (nb, 2 * m, BRANCH_W))
            e = jnp.exp(-jnp.abs(gcum - gb.reshape(q, BRANCH_W)))
        elif m == 2:
            r4 = rcol & 3
            e = jnp.where(r4 == 0, pltpu.roll(fg, q - 1, 0),
                          jnp.where(r4 == 1, 1.0, jnp.where(r4 == 2, fg, fg * pltpu.roll(fg, 1, 0))))
        else:
            e = jnp.where(second, fg, 1.0)
        u = jnp.where(second, qq, kk) * e
        if lowp:
            u = u.astype(BF16)
        pair = (rx >> int(math.log2(m))) == 1
        for h in range(HG_HEADS):
            hs = slice(h * HG_HEADDIM, (h + 1) * HG_HEADDIM)
            s = lax.dot_general(u[:, hs], u[:, hs], NT, preferred_element_type=F32)
            att[h] = jnp.where(pair, s, 0.0 if att[h] is None else att[h])
        yield

    qg = qq * jnp.exp(gcum)
    kend = kk * jnp.exp(gcum[q - 1:q, :] - gcum)
    ones = jnp.ones((q, 128), F32)
    for h in range(HG_HEADS):
        hs = slice(h * HG_HEADDIM, (h + 1) * HG_HEADDIM)
        diag = jnp.sum(qq[:, hs] * kk[:, hs], axis=1, keepdims=True)
        a = jnp.where(rt > cs, att[h], jnp.where(rt == cs, diag, 0.0))
        sp = s_scr[h]
        oh = _mm(a, vv[:, hs], lowp) + _mm(qg[:, hs], sp, lowp)
        if fine:
            yield
        dec = jnp.exp(_xdot(lf[:, hs], ones, TN, 0, lowp))
        s_scr[h] =sp * dec + _mm(kend[:, hs], vv[:, hs], lowp, TN)
        o_scr[:, hs] = oh * lax.rsqrt(jnp.mean(oh * oh, -1, keepdims=True) + EPS)
        yield

    y_ref[...] = (o_scr[...] * nw_ref[...] * _silu(g_ref[...].astype(F32))).astype(BF16)


def _hgrn(proj, proj32, s_state, layer, w, consts, *, b, q, nc, valid, lowp, bs):
    nlev = int(math.log2(q))
    return _Mixer(
        in_specs=[
            _col_spec(bs, q, COL32_D_F, BRANCH_W),
            _col_spec(bs, q, COL_D_I, BRANCH_W),
            _col_spec(bs, q, COL_D_Q, BRANCH_W),
            _col_spec(bs, q, COL_D_G, BRANCH_W),
            _state_in_spec(layer, bs, (HG_HEADS, HG_HEADDIM, HG_HEADDIM)),
            _const_spec((1, BRANCH_W)),
            _const_spec((1, BRANCH_W)),
            _const_spec((q, q)),
        ],
        out_specs=[
            _y_spec(bs, q),
            _state_out_spec(layer, bs, (HG_HEADS, HG_HEADDIM, HG_HEADDIM)),
        ],
        out_shapes=[
            jax.ShapeDtypeStruct((b, nc * q, BRANCH_W), BF16),
            jax.ShapeDtypeStruct((DEPTH, b, HG_HEADS, HG_HEADDIM, HG_HEADDIM), F32),
        ],
        scratch=[
            pltpu.VMEM((bs, HG_HEADS, HG_HEADDIM, HG_HEADDIM), F32),
            pltpu.VMEM((bs, q, BRANCH_W), F32),
        ],
        args=[proj32, proj, proj, proj, s_state, w['hg_lb'], w['hg_norm_w'], consts['tril']],
        group=functools.partial(_hgrn_group, q=q, valid=valid, lowp=lowp, nlev=nlev))


def _chunk_consts(q):
    r = np.arange(q)
    tril = (r[:, None] >= r[None, :]).astype(np.float32)
    expand = np.zeros((128, BRANCH_W), np.float32)
    for h in range(SSD_HEADS):
        expand[h, h * SSD_HEADDIM:(h + 1) * SSD_HEADDIM] = 1.0
    return {'tril': jnp.asarray(tril), 'triu': jnp.asarray(tril.T), 'expand': jnp.asarray(expand)}


def _cmul(ar, ai, br, bi):
    return ar * br - ai * bi, ar * bi + ai * br


def _layer_params(l, p, hg_lb):
    w_in_t = p['w_in'][l].T
    small = jnp.concatenate([w_in_t[1280:1288], w_in_t[3848:3856]], axis=0)
    w_perm = jnp.concatenate([
        w_in_t[5904:10000],
        w_in_t[0:512],
        w_in_t[512:1024],
        w_in_t[1288:2312],
        w_in_t[2312:3848],
        w_in_t[4368:5904],
    ], axis=0).astype(BF16)
    w_perm32 = jnp.concatenate([
        w_in_t[3856:4368],
        w_in_t[1024:1280],
        small,
        jnp.zeros((N_PROJ32 - COL32_SMALL - N_SMALL, D_MODEL), F32),
    ], axis=0).astype(BF16)

    def lane_pad(v, off):
        return jnp.zeros((1, 128), F32).at[0, off:off + v.shape[0]].set(v)

    bias_col = (lane_pad(p['ssd_dt_bias'][l], SMALL_DT) + lane_pad(p['ml_i_bias'][l], SMALL_I)
                + lane_pad(p['ml_f_bias'][l], SMALL_F))
    alog_col = lane_pad(p['ssd_A_log'][l], SMALL_DT)

    dt = jnp.exp(p['s5_log_dt'][l])[:, None]
    lr = p['s5_A_re'][l]
    li = p['s5_A_im'][l]
    mag = jnp.exp(lr * dt)
    abr, abi = mag * jnp.cos(li * dt), mag * jnp.sin(li * dt)
    den = lr * lr + li * li
    cr = ((abr - 1.0) * lr + abi * li) / den
    ci = (abi * lr - (abr - 1.0) * li) / den
    bbr, bbi = _cmul(cr[..., None], ci[..., None], p['s5_B_re'][l], p['s5_B_im'][l])
    eye8 = jnp.eye(8, dtype=F32)

    def pack_b(bb):
        return jnp.einsum('jgpc,gh->jgchp', bb.reshape(4, 8, S5_STATE, S5_GROUP), eye8).reshape(4, 128, 512)

    def pack_c(cc):
        return jnp.einsum('jgcp,gh->jgphc', cc.reshape(4, 8, S5_GROUP, S5_STATE), eye8).reshape(4, 512, 128)

    return {
        'norm_w': p['norm_w'][l].reshape(1, D_MODEL),
        'w_in': w_perm, 'w_in32': w_perm32,
        'w_small_t': small.astype(BF16),
        'bias_col': bias_col, 'bias_row': bias_col[0, :N_SMALL].reshape(N_SMALL, 1),
        'alog_col': alog_col, 'alog_row': alog_col[0, :N_SMALL].reshape(N_SMALL, 1),
        'ssd_conv_w': p['ssd_conv_w'][l], 'ssd_conv_b': p['ssd_conv_b'][l].reshape(1, SSD_CONV_CH),
        'ssd_D_exp': jnp.repeat(p['ssd_D'][l], SSD_HEADDIM).reshape(1, BRANCH_W),
        'ssd_norm_w': p['ssd_norm_w'][l].reshape(1, BRANCH_W),
        's5_wbr': pack_b(bbr).astype(BF16), 's5_wbi': pack_b(bbi).astype(BF16),
        's5_wcr': pack_c(p['s5_C_re'][l]).astype(BF16), 's5_wci': pack_c(p['s5_C_im'][l]).astype(BF16),
        's5_ar': jnp.broadcast_to(abr.reshape(1, S5_LANES), (8, S5_LANES)),
        's5_ai': jnp.broadcast_to(abi.reshape(1, S5_LANES), (8, S5_LANES)),
        's5_D': p['s5_D'][l].reshape(1, BRANCH_W), 's5_glu_w': p['s5_glu_w'][l].astype(BF16),
        'ml_conv_w': p['ml_conv_w'][l], 'ml_conv_b': p['ml_conv_b'][l].reshape(1, BRANCH_W),
        'ml_wq': p['ml_wq'][l], 'ml_wk': p['ml_wk'][l], 'ml_wv': p['ml_wv'][l],
        'ml_norm_w': p['ml_norm_w'][l].reshape(1, BRANCH_W), 'ml_skip': p['ml_skip'][l].reshape(1, BRANCH_W),
        'hg_lb': hg_lb[l].reshape(1, BRANCH_W), 'hg_norm_w': p['hg_norm_w'][l].reshape(1, BRANCH_W),
        'w_branch': p['w_branch'][l].astype(BF16), 'w_out': p['w_out'][l].astype(BF16),
    }


_MATMUL_WEIGHTS = ('ml_wq', 'ml_wk', 'ml_wv')
S5_CHUNK = 64
MIXER_PLAN = [(('ssd', 'mlstm', 'hgrn'), 2)]
MIXER_PLAN_PADDED = [(('ssd', 'mlstm', 'hgrn'), 8)]


def _run_group(x3, states, layers, fnw, *, q, valid):
    b, lp, _ = x3.shape
    nc = lp // q
    t = b * lp
    lowp = q >= 16
    consts = _chunk_consts(q)
    ssd_conv, ssd_h, s5_re, s5_im, ml_conv, ml_c, ml_n, ml_m, hg_s = states
    s5_re = s5_re.reshape(DEPTH, b, S5_LANES)
    s5_im = s5_im.reshape(DEPTH, b, S5_LANES)
    if valid < q:
        s5_kw = dict(b=b, lp=lp, q=q, valid=valid, bb=min(b, 128))
    else:
        qs = math.gcd(lp, S5_CHUNK)
        s5_kw = dict(b=b, lp=lp, q=qs, valid=qs, bb=8)
    ml_m = jnp.broadcast_to(jnp.pad(ml_m, ((0, 0), (0, 0), (0, 8 - ML_HEADS)))[..., None], (DEPTH, b, 8, 128))
    plan = [(names, math.gcd(b, n)) for names, n in (MIXER_PLAN_PADDED if valid < q else MIXER_PLAN)]
    kw = dict(b=b, q=q, nc=nc, valid=valid, lowp=lowp)
    x = x3.reshape(t, D_MODEL)
    ys, st, st_s5 = {}, {}, None
    for l, w in enumerate(layers):
        if lowp:
            w = dict(w, **{k: w[k].astype(BF16) for k in _MATMUL_WEIGHTS})
        proj, proj32, small_t = _inproj(x, w['norm_w'], w['w_in'], w['w_in32'], w['w_small_t'])
        proj3 = proj.reshape(b, lp, N_PROJ)
        proj32 = proj32.reshape(b, lp, N_PROJ32)
        srow = small_t.reshape(N_SMALL, b, lp).transpose(1, 0, 2)
        mixers = {
            'ssd': lambda n: _ssd(proj3, proj32, srow, ssd_conv, ssd_h, l, w, consts, bs=n, **kw),
            'mlstm': lambda n: _mlstm(proj3, proj32, srow, ml_conv, ml_c, ml_n, ml_m, l, w, consts, bs=n, **kw),
            'hgrn': lambda n: _hgrn(proj3, proj32, hg_s, l, w, consts, bs=n, **kw),
        }
        for names, n in plan:
            outs = _launch([mixers[k](n) for k in names], None if l == 0 else [st[k] for k in names],
                           grid=(b // n, nc), bs=n, nc=nc, name='_'.join(names))
            for k, o in zip(names, outs):
                ys[k], st[k] = o[0], list(o[1:])
        yb, *st_s5 = _s5(proj, s5_re, s5_im, l, w, st_s5, **s5_kw)
        ya, yc, yd = (ys[k].reshape(t, BRANCH_W) for k in ('ssd', 'mlstm', 'hgrn'))
        x = _merge(proj, (ya, yb, yc, yd), x, w['w_branch'], w['w_out'], fnw, final=(l == DEPTH - 1))
    n_ssd_conv, n_ssd_h = st['ssd']
    n_re, n_im = (s.reshape(DEPTH, b, S5_GROUPS, S5_STATE) for s in st_s5)
    n_ml_conv, n_c, n_n, n_m = st['mlstm']
    new_states = (n_ssd_conv, n_ssd_h, n_re, n_im, n_ml_conv, n_c, n_n, n_m[:, :, :ML_HEADS, 0], st['hgrn'][0])
    return x.reshape(b, lp, D_MODEL), new_states


SAMPLE_PAD = 8


def kernel(x_prompt, x_sample, state_ssd_conv, state_ssd, state_s5_re, state_s5_im, state_mlstm_conv, state_mlstm_C, state_mlstm_n, state_mlstm_m, state_hgrn, norm_w, w_in, ssd_conv_w, ssd_conv_b, ssd_dt_bias, ssd_A_log, ssd_D, ssd_norm_w, s5_A_re, s5_A_im, s5_B_re, s5_B_im, s5_C_re, s5_C_im, s5_D, s5_log_dt, s5_glu_w, ml_conv_w, ml_conv_b, ml_wq, ml_wk, ml_wv, ml_i_bias, ml_f_bias, ml_norm_w, ml_skip, hg_lb_logits, hg_norm_w, w_branch, w_out, final_norm_w):
    p = {'norm_w': norm_w, 'w_in': w_in,
         'ssd_conv_w': ssd_conv_w, 'ssd_conv_b': ssd_conv_b, 'ssd_dt_bias': ssd_dt_bias,
         'ssd_A_log': ssd_A_log, 'ssd_D': ssd_D, 'ssd_norm_w': ssd_norm_w,
         's5_A_re': s5_A_re, 's5_A_im': s5_A_im, 's5_B_re': s5_B_re, 's5_B_im': s5_B_im,
         's5_C_re': s5_C_re, 's5_C_im': s5_C_im, 's5_D': s5_D, 's5_log_dt': s5_log_dt, 's5_glu_w': s5_glu_w,
         'ml_conv_w': ml_conv_w, 'ml_conv_b': ml_conv_b, 'ml_wq': ml_wq, 'ml_wk': ml_wk, 'ml_wv': ml_wv,
         'ml_i_bias': ml_i_bias, 'ml_f_bias': ml_f_bias, 'ml_norm_w': ml_norm_w, 'ml_skip': ml_skip,
         'hg_norm_w': hg_norm_w, 'w_branch': w_branch, 'w_out': w_out}
    lb_cum = jnp.cumsum(jax.nn.softmax(hg_lb_logits, axis=0), axis=0)
    hg_lb = lb_cum - lb_cum[0]
    layers = [_layer_params(l, p, hg_lb) for l in range(DEPTH)]
    fnw = final_norm_w.reshape(1, D_MODEL)

    sample_states = (state_ssd_conv, state_ssd, state_s5_re, state_s5_im, state_mlstm_conv,
                     state_mlstm_C, state_mlstm_n, state_mlstm_m, state_hgrn)
    bp, lp_, _ = x_prompt.shape
    prompt_states = tuple(jnp.zeros((DEPTH, bp) + s.shape[2:], F32) for s in sample_states)
    q_prompt = math.gcd(lp_, 128)
    y_prompt, new_p = _run_group(x_prompt, prompt_states, layers, fnw, q=q_prompt, valid=q_prompt)

    ls = x_sample.shape[1]
    xs_pad = jnp.pad(x_sample, ((0, 0), (0, SAMPLE_PAD - ls), (0, 0)))
    y_sample, new_s = _run_group(xs_pad, sample_states, layers, fnw, q=SAMPLE_PAD, valid=ls)
    y_sample = y_sample[:, :ls]

    out = [y_prompt, y_sample]
    for ps, ss in zip(new_p, new_s):
        out += [ps, ss]
    return tuple(out)
```

```python
import functools
import math
from typing import Callable, NamedTuple

import numpy as np
import jax
import jax.numpy as jnp
from jax import lax
from jax.experimental import pallas as pl
from jax.experimental.pallas import tpu as pltpu

F32 = jnp.float32
BF16 = jnp.bfloat16

D_MODEL = 1024
DEPTH = 2
BRANCH_W = 512
CONV_W = 4
EPS = 1e-6
SSD_HEADS = 8
SSD_HEADDIM = 64
SSD_STATE = 64
SSD_GROUPS = 2
SSD_CONV_CH = 768
S5_GROUPS = 32
S5_GROUP = 16
S5_STATE = 64
S5_LANES = S5_GROUPS * S5_STATE
ML_HEADS = 4
ML_HEADDIM = 128
HG_HEADS = 4
HG_HEADDIM = 128
N_BRANCH = 4

NEG = -1e30

N_PROJ = 9216
COL_MERGE = 0
COL_A_Z = 4096
COL_A_X = 4608
COL_B_U = 5120
COL_B_GATE = 5632
COL_C_X = 6144
COL_C_Z = 6656
COL_C_O = 7168
COL_D_I = 7680
COL_D_Q = 8192
COL_D_G = 8704
N_PROJ32 = 896
COL32_D_F = 0
COL32_A_BC = 512
COL32_SMALL = 768
N_SMALL = 16
SMALL_DT = 0
SMALL_I = 8
SMALL_F = 12

NN = (((1,), (0,)), ((), ()))
NT = (((1,), (1,)), ((), ()))
TN = (((0,), (0,)), ((), ()))


def _mm(a, b, lowp, dims=NN):
    if lowp:
        a = a.astype(BF16)
        b = b.astype(BF16)
    return lax.dot_general(a, b, dims, preferred_element_type=F32)


def _split3(a):
    hi = a.astype(BF16)
    r = a - hi.astype(F32)
    mid = r.astype(BF16)
    lo = (r - mid.astype(F32)).astype(BF16)
    return hi, mid, lo


def _xdot(a, b, dims, split, lowp):
    if not lowp and min((b if split == 0 else a).shape) < 16:
        return lax.dot_general(a, b, dims, precision=lax.Precision.HIGHEST,
                               preferred_element_type=F32)
    if split == 0:
        other = b.astype(BF16)
        return sum(lax.dot_general(p, other, dims, preferred_element_type=F32) for p in _split3(a))
    other = a.astype(BF16)
    return sum(lax.dot_general(other, p, dims, preferred_element_type=F32) for p in _split3(b))


def _sigmoid(x):
    return 0.5 * jnp.tanh(0.5 * x) + 0.5


def _silu(x):
    h = 0.5 * x
    return h * jnp.tanh(h) + h


def _softplus(x):
    return jnp.maximum(x, 0.0) + jnp.log1p(jnp.exp(-jnp.abs(x)))


CONV_PAD = 8


def _conv_chunk(xp_scr, cw_ref, cb_ref, q, valid):
    cw = cw_ref[...]
    p = CONV_PAD
    xp = xp_scr[...]
    conv = (cb_ref[...]
            + cw[0:1] * pltpu.roll(xp, 3, 0)[p:p + q, :]
            + cw[1:2] * pltpu.roll(xp, 2, 0)[p:p + q, :]
            + cw[2:3] * pltpu.roll(xp, 1, 0)[p:p + q, :]
            + cw[3:4] * xp[p:p + q, :])
    carry = xp_scr[p - 3 + valid:p + valid, :]
    xp_scr[p - 3:p, :] = carry
    return conv, carry


def _inproj_kernel(x_ref, nw_ref, w_ref, w32_ref, wst_ref, o_ref, o32_ref, st_ref, xn_scr):
    @pl.when(pl.program_id(1) == 0)
    def _():
        x = x_ref[...]
        xn = x * lax.rsqrt(jnp.mean(x * x, -1, keepdims=True) + EPS) * nw_ref[...]
        xb = xn.astype(BF16)
        xn_scr[...] = xb
        o32_ref[...] = lax.dot_general(xb, w32_ref[...], NT, preferred_element_type=F32)
        st_ref[...] = lax.dot_general(wst_ref[...], xb, NT, preferred_element_type=F32)

    o_ref[...] = lax.dot_general(xn_scr[...], w_ref[...], NT, preferred_element_type=F32).astype(BF16)


def _inproj(x, nw, w, w32, wst):
    t = x.shape[0]
    tm = min(1024, t)
    tn = N_PROJ // 4
    return pl.pallas_call(
        _inproj_kernel,
        grid=(t // tm, N_PROJ // tn),
        in_specs=[
            pl.BlockSpec((tm, D_MODEL), lambda i, j: (i, 0)),
            pl.BlockSpec((1, D_MODEL), lambda i, j: (0, 0)),
            pl.BlockSpec((tn, D_MODEL), lambda i, j: (j, 0)),
            pl.BlockSpec((N_PROJ32, D_MODEL), lambda i, j: (0, 0)),
            pl.BlockSpec((N_SMALL, D_MODEL), lambda i, j: (0, 0)),
        ],
        out_specs=[
            pl.BlockSpec((tm, tn), lambda i, j: (i, j)),
            pl.BlockSpec((tm, N_PROJ32), lambda i, j: (i, 0)),
            pl.BlockSpec((N_SMALL, tm), lambda i, j: (0, i)),
        ],
        out_shape=[
            jax.ShapeDtypeStruct((t, N_PROJ), BF16),
            jax.ShapeDtypeStruct((t, N_PROJ32), F32),
            jax.ShapeDtypeStruct((N_SMALL, t), F32),
        ],
        scratch_shapes=[pltpu.VMEM((tm, D_MODEL), BF16)],
        compiler_params=pltpu.CompilerParams(
            dimension_semantics=("arbitrary", "arbitrary"),
            vmem_limit_bytes=48 * 1024 * 1024),
        name="inproj",
    )(x, nw, w, w32, wst)


def _merge_kernel(g_ref, ya_ref, yb_ref, yc_ref, yd_ref, x_ref, wb_ref, wo_ref, fnw_ref, o_ref, *, final):
    acc = None
    for i, y_ref in enumerate((ya_ref, yb_ref, yc_ref, yd_ref)):
        gate = _sigmoid(g_ref[:, i * D_MODEL:(i + 1) * D_MODEL].astype(F32))
        term = gate * jnp.dot(y_ref[...], wb_ref[i], preferred_element_type=F32)
        acc = term if acc is None else acc + term
    out = x_ref[...] + jnp.dot(acc.astype(BF16), wo_ref[...], preferred_element_type=F32)
    if final:
        out = out * lax.rsqrt(jnp.mean(out * out, -1, keepdims=True) + EPS) * fnw_ref[...]
    o_ref[...] = out


def _merge(proj, ys, x, wb, wo, fnw, final):
    t = x.shape[0]
    tm = min(512, t)
    yspec = pl.BlockSpec((tm, BRANCH_W), lambda i: (i, 0))
    return pl.pallas_call(
        functools.partial(_merge_kernel, final=final),
        grid=(t // tm,),
        in_specs=[
            pl.BlockSpec((tm, N_BRANCH * D_MODEL), lambda i: (i, COL_MERGE // (N_BRANCH * D_MODEL))),
            yspec, yspec, yspec, yspec,
            pl.BlockSpec((tm, D_MODEL), lambda i: (i, 0)),
            pl.BlockSpec((N_BRANCH, BRANCH_W, D_MODEL), lambda i: (0, 0, 0)),
            pl.BlockSpec((D_MODEL, D_MODEL), lambda i: (0, 0)),
            pl.BlockSpec((1, D_MODEL), lambda i: (0, 0)),
        ],
        out_specs=pl.BlockSpec((tm, D_MODEL), lambda i: (i, 0)),
        out_shape=jax.ShapeDtypeStruct((t, D_MODEL), F32),
        compiler_params=pltpu.CompilerParams(
            dimension_semantics=("arbitrary",),
            vmem_limit_bytes=48 * 1024 * 1024),
        name="merge",
    )(proj, *ys, x, wb, wo, fnw)


def _col_spec(bs, q, col, width):
    return pl.BlockSpec((bs, q, width), lambda b, c: (b, c, col // width))


def _row_spec(bs, q):
    return pl.BlockSpec((bs, N_SMALL, q), lambda b, c: (b, 0, c))


def _const_spec(shape):
    nd = len(shape)
    return pl.BlockSpec(shape, lambda b, c: (0,) * nd)


def _state_in_spec(layer, bs, tail):
    nd = len(tail)
    return pl.BlockSpec((1, bs) + tail, lambda b, c: (layer, b) + (0,) * nd)


def _state_out_spec(layer, bs, tail):
    nd = len(tail)
    return pl.BlockSpec((1, bs) + tail, lambda b, c: (layer, b) + (0,) * nd)


def _y_spec(bs, q):
    return pl.BlockSpec((bs, q, BRANCH_W), lambda b, c: (b, c, 0))


class _Mixer(NamedTuple):
    in_specs: list
    args: list
    out_specs: list
    out_shapes: list
    scratch: list
    group: Callable


_DONE = object()
_MIXER_PARAMS = pltpu.CompilerParams(
    dimension_semantics=("arbitrary", "arbitrary"),
    vmem_limit_bytes=48 * 1024 * 1024)


def _launch(mixers, prevs, *, grid, bs, nc, name):
    n_in = [len(m.in_specs) for m in mixers]
    n_out = [len(m.out_specs) for m in mixers]
    n_scr = [len(m.scratch) for m in mixers]
    tot_in, tot_out = sum(n_in), sum(n_out)
    state_idx = [sum(n_out[:i]) + k for i in range(len(mixers)) for k in range(1, n_out[i])]
    n_alias = 0 if prevs is None else len(state_idx)

    def kern(*refs):
        ins, outs, scr = refs[:tot_in], refs[tot_in + n_alias:tot_in + n_alias + tot_out], refs[tot_in + n_alias + tot_out:]
        groups = []
        for i, m in enumerate(mixers):
            groups.append(m.group(ins[sum(n_in[:i]):sum(n_in[:i + 1])],
                                  outs[sum(n_out[:i]):sum(n_out[:i + 1])],
                                  scr[sum(n_scr[:i]):sum(n_scr[:i + 1])]))
        c = pl.program_id(1)
        views = [[[r.at[s] for r in g[3]] for s in range(bs)] for g in groups]

        @pl.when(c == 0)
        def _():
            for g, v in zip(groups, views):
                for s in range(bs):
                    g[0](v[s])

        active = [g[1](v[s], g[4]) for g, v in zip(groups, views) for s in range(bs)]
        while active:
            active = [gen for gen in active if next(gen, _DONE) is not _DONE]

        @pl.when(c == nc - 1)
        def _():
            for g, v in zip(groups, views):
                for s in range(bs):
                    g[2](v[s])

    in_specs = [sp for m in mixers for sp in m.in_specs] + [pl.BlockSpec(memory_space=pl.ANY)] * n_alias
    args = [a for m in mixers for a in m.args]
    if prevs is not None:
        args += [a for p in prevs for a in p]
    outs = pl.pallas_call(
        kern,
        grid=grid,
        in_specs=in_specs,
        out_specs=[sp for m in mixers for sp in m.out_specs],
        out_shape=[sh for m in mixers for sh in m.out_shapes],
        scratch_shapes=[sc for m in mixers for sc in m.scratch],
        input_output_aliases={tot_in + k: idx for k, idx in enumerate(state_idx)} if n_alias else {},
        compiler_params=_MIXER_PARAMS,
        name=name,
    )(*args)
    return [outs[sum(n_out[:i]):sum(n_out[:i + 1])] for i in range(len(mixers))]


def _ssd_group(ins, outs, scr, *, q, valid, lowp):
    z_ref, x_ref, bc_ref, sc_ref, sr_ref, conv0_ref, h0_ref, *consts = ins
    y_ref, convo_ref, ho_ref = outs
    xp_scr, h_scr, y_scr = scr

    def init(v):
        v[9][CONV_PAD - 3:CONV_PAD, :] = v[5][...]
        v[10][...] = v[6][...]

    def final(v):
        v[8][...] = v[10][...]

    return (init, functools.partial(_ssd_main, q=q, valid=valid, lowp=lowp), final,
            (z_ref, x_ref, bc_ref, sc_ref, sr_ref, conv0_ref.at[0], h0_ref.at[0],
             y_ref, ho_ref.at[0], xp_scr, h_scr, y_scr, convo_ref.at[0]),
            tuple(consts))


def _ssd_main(v, k, *, q, valid, lowp):
    z_ref, x_ref, bc_ref, sc_ref, sr_ref, _, _, y_ref, _, xp_scr, h_scr, y_scr, convo_ref = v
    cw_ref, cb_ref, bcol_ref, brow_ref, alc_ref, alr_ref, dexp_ref, nw_ref, e_ref, tl_ref, tu_ref = k
    xp_scr[CONV_PAD:CONV_PAD + q, 0:BRANCH_W] = x_ref[...].astype(F32)
    xp_scr[CONV_PAD:CONV_PAD + q, BRANCH_W:SSD_CONV_CH] = bc_ref[...]
    conv, carry = _conv_chunk(xp_scr, cw_ref, cb_ref, q, valid)
    convo_ref[...] = carry
    yield
    xbc = _silu(conv)
    xs = xbc[:, 0:BRANCH_W]
    bm = xbc[:, BRANCH_W:BRANCH_W + 128]
    cm = xbc[:, BRANCH_W + 128:BRANCH_W + 256]

    dt_c = _softplus(sc_ref[...] + bcol_ref[...])
    a_c = dt_c * (-jnp.exp(alc_ref[...]))
    dt_r = _softplus(sr_ref[...] + brow_ref[...])
    a_r = dt_r * (-jnp.exp(alr_ref[...]))
    if valid < q:
        tcol = lax.broadcasted_iota(jnp.int32, (q, 1), 0)
        trow = lax.broadcasted_iota(jnp.int32, (1, q), 1)
        a_c = jnp.where(tcol < valid, a_c, 0.0)
        dt_c = jnp.where(tcol < valid, dt_c, 0.0)
        a_r = jnp.where(trow < valid, a_r, 0.0)

    tl = tl_ref[...]
    acum_c = _xdot(tl, a_c, NN, 1, lowp)
    acum_r = _xdot(a_r, tu_ref[...], NN, 0, lowp)
    e = e_ref[...]
    dt_e = _xdot(dt_c, e, NN, 0, lowp)
    acum_e = _xdot(acum_c, e, NN, 0, lowp)
    xdt = xs * dt_e
    xend = xdt * jnp.exp(acum_e[q - 1:q, :] - acum_e)
    eac = jnp.exp(acum_e)
    dec_last = jnp.exp(acum_c[q - 1:q, :])
    yield

    fine = q < 16
    tri = (lax.broadcasted_iota(jnp.int32, (q, q), 0) >= lax.broadcasted_iota(jnp.int32, (q, q), 1))
    rpg = SSD_HEADS // SSD_GROUPS
    for g in range(SSD_GROUPS):
        bg = bm[:, g * SSD_STATE:(g + 1) * SSD_STATE]
        cg = cm[:, g * SSD_STATE:(g + 1) * SSD_STATE]
        cb = _mm(cg, bg, lowp, NT)
        for r in range(rpg):
            h = g * rpg + r
            hs = slice(h * SSD_HEADDIM, (h + 1) * SSD_HEADDIM)
            seg = acum_c[:, h:h + 1] - acum_r[h:h + 1, :]
            m = cb * jnp.exp(jnp.where(tri, seg, NEG))
            hp = h_scr[h]
            yh = _mm(m, xdt[:, hs], lowp) + _mm(cg, hp, lowp, NT) * eac[:, hs]
            y_scr[:, hs] = yh
            if fine:
                yield
            h_scr[h] = hp * dec_last[:, h:h + 1] + _mm(xend[:, hs], bg, lowp, TN)
            yield

    y = y_scr[...] + xs * dexp_ref[...]
    y = y * _silu(z_ref[...].astype(F32))
    y = y * lax.rsqrt(jnp.mean(y * y, -1, keepdims=True) + EPS) * nw_ref[...]
    y_ref[...] = y.astype(BF16)


def _ssd(proj, proj32, srow, conv_state, h_state, layer, w, consts, *, b, q, nc, valid, lowp, bs):
    return _Mixer(
        in_specs=[
            _col_spec(bs, q, COL_A_Z, BRANCH_W),
            _col_spec(bs, q, COL_A_X, BRANCH_W),
            _col_spec(bs, q, COL32_A_BC, 256),
            _col_spec(bs, q, COL32_SMALL, 128),
            _row_spec(bs, q),
            _state_in_spec(layer, bs, (CONV_W - 1, SSD_CONV_CH)),
            _state_in_spec(layer, bs, (SSD_HEADS, SSD_HEADDIM, SSD_STATE)),
            _const_spec((CONV_W, SSD_CONV_CH)),
            _const_spec((1, SSD_CONV_CH)),
            _const_spec((1, 128)),
            _const_spec((N_SMALL, 1)),
            _const_spec((1, 128)),
            _const_spec((N_SMALL, 1)),
            _const_spec((1, BRANCH_W)),
            _const_spec((1, BRANCH_W)),
            _const_spec((128, BRANCH_W)),
            _const_spec((q, q)),
            _const_spec((q, q)),
        ],
        out_specs=[
            _y_spec(bs, q),
            _state_out_spec(layer, bs, (CONV_W - 1, SSD_CONV_CH)),
            _state_out_spec(layer, bs, (SSD_HEADS, SSD_HEADDIM, SSD_STATE)),
        ],
        out_shapes=[
            jax.ShapeDtypeStruct((b, nc * q, BRANCH_W), BF16),
            jax.ShapeDtypeStruct((DEPTH, b, CONV_W - 1, SSD_CONV_CH), F32),
            jax.ShapeDtypeStruct((DEPTH, b, SSD_HEADS, SSD_HEADDIM, SSD_STATE), F32),
        ],
        scratch=[
            pltpu.VMEM((bs, q + CONV_PAD, SSD_CONV_CH), F32),
            pltpu.VMEM((bs, SSD_HEADS, SSD_HEADDIM, SSD_STATE), F32),
            pltpu.VMEM((bs, q, BRANCH_W), F32),
        ],
        args=[proj, proj, proj32, proj32, srow, conv_state, h_state,
              w['ssd_conv_w'], w['ssd_conv_b'], w['bias_col'], w['bias_row'], w['alog_col'], w['alog_row'],
              w['ssd_D_exp'], w['ssd_norm_w'], consts['expand'], consts['tril'], consts['triu']],
        group=functools.partial(_ssd_group, q=q, valid=valid, lowp=lowp))


def _s5_kernel(u_ref, gate_ref, hr0_ref, hi0_ref, perm_ref, permt_ref, ar_ref, ai_ref,
               wbr_ref, wbi_ref, wcr_ref, wci_ref, d_ref, glu_ref,
               y_ref, hro_ref, hio_ref,
               hr_scr, hi_scr, cr_scr, ci_scr, *, q, nc, ngrp):
    c = pl.program_id(1)

    @pl.when(c == 0)
    def _():
        cr_scr[...] = hr0_ref[0]
        ci_scr[...] = hi0_ref[0]

    rows_in = perm_ref.shape[1]
    u = u_ref[...].reshape(rows_in, BRANCH_W)
    u_tm = jnp.dot(perm_ref[...], u, preferred_element_type=F32).astype(BF16)
    nblk = S5_LANES // 512
    for j in range(nblk):
        uj = u_tm[:, j * 128:(j + 1) * 128]
        hr_scr[:, j * 512:(j + 1) * 512] = jnp.dot(uj, wbr_ref[j], preferred_element_type=F32)
        hi_scr[:, j * 512:(j + 1) * 512] = jnp.dot(uj, wbi_ref[j], preferred_element_type=F32)

    half = S5_LANES // 2
    for hf in range(2):
        sl = slice(hf * half, (hf + 1) * half)
        ar = ar_ref[:, sl]
        ai = ai_ref[:, sl]

        def grp_body(g, carry, sl=sl, ar=ar, ai=ai):
            s0 = pl.multiple_of(g * 8, 8)

            def t_body(t, h):
                r0 = pl.multiple_of(g * (8 * q) + t * 8, 8)
                hr, hi = h
                nr = ar * hr - ai * hi + hr_scr[pl.ds(r0, 8), sl]
                ni = ar * hi + ai * hr + hi_scr[pl.ds(r0, 8), sl]
                hr_scr[pl.ds(r0, 8), sl] = nr
                hi_scr[pl.ds(r0, 8), sl] = ni
                return nr, ni

            h = lax.fori_loop(0, q, t_body, (cr_scr[pl.ds(s0, 8), sl], ci_scr[pl.ds(s0, 8), sl]),
                              unroll=min(q, 4))
            cr_scr[pl.ds(s0, 8), sl] = h[0]
            ci_scr[pl.ds(s0, 8), sl] = h[1]
            return carry

        lax.fori_loop(0, ngrp, grp_body, 0)

    parts = []
    for j in range(nblk):
        sl = slice(j * 512, (j + 1) * 512)
        parts.append(jnp.dot(hr_scr[:, sl].astype(BF16), wcr_ref[j], preferred_element_type=F32)
                     - jnp.dot(hi_scr[:, sl].astype(BF16), wci_ref[j], preferred_element_type=F32))
    y_tm = jnp.concatenate(parts, axis=1)
    y = _xdot(permt_ref[...], y_tm, NN, 1, True) + u.astype(F32) * d_ref[...]
    g = jax.nn.gelu(y)
    y = g * _sigmoid(jnp.dot(g.astype(BF16), glu_ref[...], preferred_element_type=F32))
    y = y * _silu(gate_ref[...].reshape(rows_in, BRANCH_W).astype(F32))
    y_ref[...] = y.reshape(y_ref.shape).astype(BF16)

    @pl.when(c == nc - 1)
    def _():
        hro_ref[0] = cr_scr[...]
        hio_ref[0] = ci_scr[...]


def _s5_perm(bb, lp, valid):
    perm = np.zeros((bb * valid, bb * lp), np.float32)
    for g in range(bb // 8):
        for t in range(valid):
            for s in range(8):
                perm[g * 8 * valid + t * 8 + s, (g * 8 + s) * lp + t] = 1.0
    return jnp.asarray(perm, BF16), jnp.asarray(perm.T, BF16)


def _s5(proj, hr_state, hi_state, layer, w, prev, *, b, lp, q, valid, bb):
    nc = lp // q
    perm, permt = _s5_perm(bb, q, valid)
    rows = bb * valid
    if nc == 1:
        def col_spec(col):
            return pl.BlockSpec((bb * q, BRANCH_W), lambda i, c: (i, col // BRANCH_W))
        src = proj
        y_spec = pl.BlockSpec((bb * q, BRANCH_W), lambda i, c: (i, 0))
        y_shape = jax.ShapeDtypeStruct((b * lp, BRANCH_W), BF16)
    else:
        def col_spec(col):
            return pl.BlockSpec((bb, q, BRANCH_W), lambda i, c: (i, c, col // BRANCH_W))
        src = proj.reshape(b, lp, N_PROJ)
        y_spec = pl.BlockSpec((bb, q, BRANCH_W), lambda i, c: (i, c, 0))
        y_shape = jax.ShapeDtypeStruct((b, lp, BRANCH_W), BF16)
    st_in = pl.BlockSpec((1, bb, S5_LANES), lambda i, c: (layer, i, 0))
    st_out = st_in
    kern = functools.partial(_s5_kernel, q=valid, nc=nc, ngrp=bb // 8)
    n_in = 14
    alias_kw = {}
    if prev is not None:
        base = kern

        def kern(*refs):
            return base(*refs[:n_in], *refs[n_in + 2:])

        alias_kw = dict(input_output_aliases={n_in: 1, n_in + 1: 2})
    y, hr, hi = pl.pallas_call(
        kern,
        grid=(b // bb, nc),
        **alias_kw,
        in_specs=[
            col_spec(COL_B_U),
            col_spec(COL_B_GATE),
            st_in, st_in,
            _const_spec(perm.shape),
            _const_spec(permt.shape),
            _const_spec((8, S5_LANES)),
            _const_spec((8, S5_LANES)),
            _const_spec((4, 128, 512)),
            _const_spec((4, 128, 512)),
            _const_spec((4, 512, 128)),
            _const_spec((4, 512, 128)),
            _const_spec((1, BRANCH_W)),
            _const_spec((BRANCH_W, BRANCH_W)),
        ] + [pl.BlockSpec(memory_space=pl.ANY)] * (0 if prev is None else 2),
        out_specs=[y_spec, st_out, st_out],
        out_shape=[
            y_shape,
            jax.ShapeDtypeStruct((DEPTH, b, S5_LANES), F32),
            jax.ShapeDtypeStruct((DEPTH, b, S5_LANES), F32),
        ],
        scratch_shapes=[
            pltpu.VMEM((rows, S5_LANES), F32),
            pltpu.VMEM((rows, S5_LANES), F32),
            pltpu.VMEM((bb, S5_LANES), F32),
            pltpu.VMEM((bb, S5_LANES), F32),
        ],
        compiler_params=_MIXER_PARAMS,
        name="s5",
    )(src, src, hr_state, hi_state, perm, permt, w['s5_ar'], w['s5_ai'],
      w['s5_wbr'], w['s5_wbi'], w['s5_wcr'], w['s5_wci'], w['s5_D'], w['s5_glu_w'], *(prev or ()))
    return y.reshape(b * lp, BRANCH_W), hr, hi


def _mlstm_group(ins, outs, scr, *, q, valid, lowp):
    x_ref, z_ref, o_ref, sc_ref, sr_ref, conv0_ref, c0_ref, n0_ref, m0_ref, *consts = ins
    y_ref, convo_ref, co_ref, no_ref, mo_ref = outs
    xp_scr, c_scr, n_scr, m_scr, h_scr = scr

    def init(v):
        v[13][CONV_PAD - 3:CONV_PAD, :] = v[5][...]
        v[14][...] = v[6][...]
        v[15][...] = v[7][...]
        v[16][...] = v[8][...]

    def final(v):
        v[10][...] = v[14][...]
        v[11][...] = v[15][...]
        v[12][...] = v[16][...]

    return (init, functools.partial(_mlstm_main, q=q, valid=valid, lowp=lowp), final,
            (x_ref, z_ref, o_ref, sc_ref, sr_ref, conv0_ref.at[0], c0_ref.at[0], n0_ref.at[0], m0_ref.at[0],
             y_ref, co_ref.at[0], no_ref.at[0], mo_ref.at[0], xp_scr, c_scr, n_scr, m_scr, h_scr,
             convo_ref.at[0]),
            tuple(consts))


def _mlstm_main(v, k, *, q, valid, lowp):
    (x_ref, z_ref, o_ref, sc_ref, sr_ref, _, _, _, _, y_ref, _, _, _,
     xp_scr, c_scr, n_scr, m_scr, h_scr, convo_ref) = v
    cw_ref, cb_ref, wq_ref, wk_ref, wv_ref, bcol_ref, brow_ref, nw_ref, skip_ref, tl_ref, tu_ref = k
    x = x_ref[...].astype(F32)
    xp_scr[CONV_PAD:CONV_PAD + q, :] = x
    conv, carry = _conv_chunk(xp_scr, cw_ref, cb_ref, q, valid)
    convo_ref[...] = carry
    yield
    xc = _silu(conv)

    pre_c = sc_ref[...] + bcol_ref[...]
    pre_r = sr_ref[...] + brow_ref[...]
    ig_c = pre_c
    lf_c = -_softplus(-pre_c)
    ig_r = pre_r
    lf_r = -_softplus(-pre_r)
    if valid < q:
        tcol = lax.broadcasted_iota(jnp.int32, (q, 1), 0)
        trow = lax.broadcasted_iota(jnp.int32, (1, q), 1)
        lf_c = jnp.where(tcol < valid, lf_c, 0.0)
        ig_c = jnp.where(tcol < valid, ig_c, NEG)
        lf_r = jnp.where(trow < valid, lf_r, 0.0)
        ig_r = jnp.where(trow < valid, ig_r, NEG)
    b_c = _xdot(tl_ref[...], lf_c, NN, 1, lowp)
    b_r = _xdot(lf_r, tu_ref[...], NN, 0, lowp)
    yield

    fine = q < 16
    tri = (lax.broadcasted_iota(jnp.int32, (q, q), 0) >= lax.broadcasted_iota(jnp.int32, (q, q), 1))
    o_all = o_ref[...].astype(F32)
    for h in range(ML_HEADS):
        hs = slice(h * ML_HEADDIM, (h + 1) * ML_HEADDIM)
        xh = xc[:, hs]
        qh = _mm(xh, wq_ref[h], lowp)
        kh = _mm(xh, wk_ref[h], lowp) * (ML_HEADDIM ** -0.5)
        vh = _mm(x[:, hs], wv_ref[h], lowp)
        yield
        bc = b_c[:, SMALL_F + h:SMALL_F + h + 1]
        br = b_r[SMALL_F + h:SMALL_F + h + 1, :]
        ic = ig_c[:, SMALL_I + h:SMALL_I + h + 1]
        ir = ig_r[SMALL_I + h:SMALL_I + h + 1, :]
        mp = m_scr[h:h + 1, 0:1]
        dlog = jnp.where(tri, bc - br + ir, NEG)
        inter = bc + mp
        m_t = jnp.maximum(inter, jnp.max(dlog, axis=1, keepdims=True))
        wgt = jnp.exp(dlog - m_t)
        yield
        s = _mm(qh, kh, lowp, NT) * wgt
        if fine:
            yield
        scale = jnp.exp(inter - m_t)
        cp = c_scr[h]
        npv = n_scr[h:h + 1, :]
        num = _mm(s, vh, lowp) + scale * _mm(qh, cp, lowp)
        dot =jnp.sum(s, axis=1, keepdims=True) + scale * jnp.sum(qh * npv, axis=1, keepdims=True)
        hh = num / jnp.maximum(jnp.abs(dot), jnp.exp(-m_t))
        yield
        m_new = m_t[q - 1:q, :]
        b_last = bc[q - 1:q, :]
        w_end = jnp.exp(b_last - bc + ic - m_new)
        cs = jnp.exp(b_last + mp - m_new)
        if fine:
            yield
        kw = kh * w_end
        c_scr[h] = cs * cp + _mm(kw, vh, lowp, TN)
        n_scr[h:h + 1, :] = cs * npv + jnp.sum(kw, axis=0, keepdims=True)
        m_scr[h:h + 1, :] = jnp.broadcast_to(m_new, (1, 128))
        yield
        hh = hh * _sigmoid(o_all[:, hs])
        hc = hh - jnp.mean(hh, -1, keepdims=True)
        h_scr[:, hs] = hc * lax.rsqrt(jnp.mean(hc * hc, -1, keepdims=True) + EPS)
        yield

    y = h_scr[...] * nw_ref[...] + skip_ref[...] * xc
    y_ref[...] = (y * _silu(z_ref[...].astype(F32))).astype(BF16)


def _mlstm(proj, proj32, srow, conv_state, c_state, n_state, m_state, layer, w, consts,
           *, b, q, nc, valid, lowp, bs):
    return _Mixer(
        in_specs=[
            _col_spec(bs, q, COL_C_X, BRANCH_W),
            _col_spec(bs, q, COL_C_Z, BRANCH_W),
            _col_spec(bs, q, COL_C_O, BRANCH_W),
            _col_spec(bs, q, COL32_SMALL, 128),
            _row_spec(bs, q),
            _state_in_spec(layer, bs, (CONV_W - 1, BRANCH_W)),
            _state_in_spec(layer, bs, (ML_HEADS, ML_HEADDIM, ML_HEADDIM)),
            _state_in_spec(layer, bs, (ML_HEADS, ML_HEADDIM)),
            _state_in_spec(layer, bs, (8, 128)),
            _const_spec((CONV_W, BRANCH_W)),
            _const_spec((1, BRANCH_W)),
            _const_spec((ML_HEADS, ML_HEADDIM, ML_HEADDIM)),
            _const_spec((ML_HEADS, ML_HEADDIM, ML_HEADDIM)),
            _const_spec((ML_HEADS, ML_HEADDIM, ML_HEADDIM)),
            _const_spec((1, 128)),
            _const_spec((N_SMALL, 1)),
            _const_spec((1, BRANCH_W)),
            _const_spec((1, BRANCH_W)),
            _const_spec((q, q)),
            _const_spec((q, q)),
        ],
        out_specs=[
            _y_spec(bs, q),
            _state_out_spec(layer, bs, (CONV_W - 1, BRANCH_W)),
            _state_out_spec(layer, bs, (ML_HEADS, ML_HEADDIM, ML_HEADDIM)),
            _state_out_spec(layer, bs, (ML_HEADS, ML_HEADDIM)),
            _state_out_spec(layer, bs, (8, 128)),
        ],
        out_shapes=[
            jax.ShapeDtypeStruct((b, nc * q, BRANCH_W), BF16),
            jax.ShapeDtypeStruct((DEPTH, b, CONV_W - 1, BRANCH_W), F32),
            jax.ShapeDtypeStruct((DEPTH, b, ML_HEADS, ML_HEADDIM, ML_HEADDIM), F32),
            jax.ShapeDtypeStruct((DEPTH, b, ML_HEADS, ML_HEADDIM), F32),
            jax.ShapeDtypeStruct((DEPTH, b, 8, 128), F32),
        ],
        scratch=[
            pltpu.VMEM((bs, q + CONV_PAD, BRANCH_W), F32),
            pltpu.VMEM((bs, ML_HEADS, ML_HEADDIM, ML_HEADDIM), F32),
            pltpu.VMEM((bs, ML_HEADS, ML_HEADDIM), F32),
            pltpu.VMEM((bs, 8, 128), F32),
            pltpu.VMEM((bs, q, BRANCH_W), F32),
        ],
        args=[proj, proj, proj, proj32, srow, conv_state, c_state, n_state, m_state,
              w['ml_conv_w'], w['ml_conv_b'], w['ml_wq'], w['ml_wk'], w['ml_wv'], w['bias_col'], w['bias_row'],
              w['ml_norm_w'], w['ml_skip'], consts['tril'], consts['triu']],
        group=functools.partial(_mlstm_group, q=q, valid=valid, lowp=lowp))


def _hgrn_group(ins, outs, scr, *, q, valid, lowp, nlev):
    f_ref, i_ref, q_ref, g_ref, s0_ref, *consts = ins
    y_ref, so_ref = outs
    s_scr, o_scr = scr

    def init(v):
        v[7][...] = v[4][...]

    def final(v):
        v[6][...] = v[7][...]

    return (init, functools.partial(_hgrn_main, q=q, valid=valid, lowp=lowp, nlev=nlev), final,
            (f_ref, i_ref, q_ref, g_ref, s0_ref.at[0], y_ref, so_ref.at[0], s_scr, o_scr),
            tuple(consts))


def _hgrn_main(v, k, *, q, valid, lowp, nlev):
    f_ref, i_ref, q_ref, g_ref, _, y_ref, _, s_scr, o_scr = v
    lb_ref, nw_ref, tl_ref = k
    lb = lb_ref[...]
    fg = lb + (1.0 - lb) * _sigmoid(f_ref[...])
    kk = 1.0 - fg
    qq = _silu(q_ref[...].astype(F32)) * (HG_HEADDIM ** -0.5)
    vv = i_ref[...] if lowp else i_ref[...].astype(F32)
    rcol =lax.broadcasted_iota(jnp.int32, (q, 1), 0)
    if valid < q:
        fg = jnp.where(rcol < valid, fg, 1.0)
        kk = jnp.where(rcol < valid, kk, 0.0)
    lf = jnp.log(fg)

    gcum = _xdot(tl_ref[...], lf, NN, 1, lowp)
    yield

    fine = q < 16
    rt = lax.broadcasted_iota(jnp.int32, (q, q), 0)
    cs = lax.broadcasted_iota(jnp.int32, (q, q), 1)
    rx = rt ^ cs
    att = [None] * HG_HEADS
    for lev in range(nlev):
        m = q >> (lev + 1)
        second = (rcol & m) != 0
        if m >= 4:
            nb = q // (2 * m)
            gb = jnp.broadcast_to(gcum.reshape(nb, 2 * m, BRANCH_W)[:, m - 1:m, :], (nb, 2 * m, BRANCH_W))
            e = jnp.exp(-jnp.abs(gcum - gb.reshape(q, BRANCH_W)))
        elif m == 2:
            r4 = rcol & 3
            e = jnp.where(r4 == 0, pltpu.roll(fg, q - 1, 0),
                          jnp.where(r4 == 1, 1.0, jnp.where(r4 == 2, fg, fg * pltpu.roll(fg, 1, 0))))
        else:
            e = jnp.where(second, fg, 1.0)
        u = jnp.where(second, qq, kk) * e
        if lowp:
            u = u.astype(BF16)
        pair = (rx >> int(math.log2(m))) == 1
        for h in range(HG_HEADS):
            hs = slice(h * HG_HEADDIM, (h + 1) * HG_HEADDIM)
            s = lax.dot_general(u[:, hs], u[:, hs], NT, preferred_element_type=F32)
            att[h] = jnp.where(pair, s, 0.0 if att[h] is None else att[h])
        yield

    qg = qq * jnp.exp(gcum)
    kend = kk * jnp.exp(gcum[q - 1:q, :] - gcum)
    ones = jnp.ones((q, 128), F32)
    for h in range(HG_HEADS):
        hs = slice(h * HG_HEADDIM, (h + 1) * HG_HEADDIM)
        diag = jnp.sum(qq[:, hs] * kk[:, hs], axis=1, keepdims=True)
        a = jnp.where(rt > cs, att[h], jnp.where(rt == cs, diag, 0.0))
        sp = s_scr[h]
        oh = _mm(a, vv[:, hs], lowp) + _mm(qg[:, hs], sp, lowp)
        if fine:
            yield
        dec = jnp.exp(_xdot(lf[:, hs], ones, TN, 0, lowp))
        s_scr[h] =sp * dec + _mm(kend[:, hs], vv[:, hs], lowp, TN)
        o_scr[:, hs] = oh * lax.rsqrt(jnp.mean(oh * oh, -1, keepdims=True) + EPS)
        yield

    y_ref[...] = (o_scr[...] * nw_ref[...] * _silu(g_ref[...].astype(F32))).astype(BF16)


def _hgrn(proj, proj32, s_state, layer, w, consts, *, b, q, nc, valid, lowp, bs):
    nlev = int(math.log2(q))
    return _Mixer(
        in_specs=[
            _col_spec(bs, q, COL32_D_F, BRANCH_W),
            _col_spec(bs, q, COL_D_I, BRANCH_W),
            _col_spec(bs, q, COL_D_Q, BRANCH_W),
            _col_spec(bs, q, COL_D_G, BRANCH_W),
            _state_in_spec(layer, bs, (HG_HEADS, HG_HEADDIM, HG_HEADDIM)),
            _const_spec((1, BRANCH_W)),
            _const_spec((1, BRANCH_W)),
            _const_spec((q, q)),
        ],
        out_specs=[
            _y_spec(bs, q),
            _state_out_spec(layer, bs, (HG_HEADS, HG_HEADDIM, HG_HEADDIM)),
        ],
        out_shapes=[
            jax.ShapeDtypeStruct((b, nc * q, BRANCH_W), BF16),
            jax.ShapeDtypeStruct((DEPTH, b, HG_HEADS, HG_HEADDIM, HG_HEADDIM), F32),
        ],
        scratch=[
            pltpu.VMEM((bs, HG_HEADS, HG_HEADDIM, HG_HEADDIM), F32),
            pltpu.VMEM((bs, q, BRANCH_W), F32),
        ],
        args=[proj32, proj, proj, proj, s_state, w['hg_lb'], w['hg_norm_w'], consts['tril']],
        group=functools.partial(_hgrn_group, q=q, valid=valid, lowp=lowp, nlev=nlev))


def _chunk_consts(q):
    r = np.arange(q)
    tril = (r[:, None] >= r[None, :]).astype(np.float32)
    expand = np.zeros((128, BRANCH_W), np.float32)
    for h in range(SSD_HEADS):
        expand[h, h * SSD_HEADDIM:(h + 1) * SSD_HEADDIM] = 1.0
    return {'tril': jnp.asarray(tril), 'triu': jnp.asarray(tril.T), 'expand': jnp.asarray(expand)}


def _cmul(ar, ai, br, bi):
    return ar * br - ai * bi, ar * bi + ai * br


def _layer_params(l, p, hg_lb):
    w_in_t = p['w_in'][l].T
    small = jnp.concatenate([w_in_t[1280:1288], w_in_t[3848:3856]], axis=0)
    w_perm = jnp.concatenate([
        w_in_t[5904:10000],
        w_in_t[0:512],
        w_in_t[512:1024],
        w_in_t[1288:2312],
        w_in_t[2312:3848],
        w_in_t[4368:5904],
    ], axis=0).astype(BF16)
    w_perm32 = jnp.concatenate([
        w_in_t[3856:4368],
        w_in_t[1024:1280],
        small,
        jnp.zeros((N_PROJ32 - COL32_SMALL - N_SMALL, D_MODEL), F32),
    ], axis=0).astype(BF16)

    def lane_pad(v, off):
        return jnp.zeros((1, 128), F32).at[0, off:off + v.shape[0]].set(v)

    bias_col = (lane_pad(p['ssd_dt_bias'][l], SMALL_DT) + lane_pad(p['ml_i_bias'][l], SMALL_I)
                + lane_pad(p['ml_f_bias'][l], SMALL_F))
    alog_col = lane_pad(p['ssd_A_log'][l], SMALL_DT)

    dt = jnp.exp(p['s5_log_dt'][l])[:, None]
    lr = p['s5_A_re'][l]
    li = p['s5_A_im'][l]
    mag = jnp.exp(lr * dt)
    abr, abi = mag * jnp.cos(li * dt), mag * jnp.sin(li * dt)
    den = lr * lr + li * li
    cr = ((abr - 1.0) * lr + abi * li) / den
    ci = (abi * lr - (abr - 1.0) * li) / den
    bbr, bbi = _cmul(cr[..., None], ci[..., None], p['s5_B_re'][l], p['s5_B_im'][l])
    eye8 = jnp.eye(8, dtype=F32)

    def pack_b(bb):
        return jnp.einsum('jgpc,gh->jgchp', bb.reshape(4, 8, S5_STATE, S5_GROUP), eye8).reshape(4, 128, 512)

    def pack_c(cc):
        return jnp.einsum('jgcp,gh->jgphc', cc.reshape(4, 8, S5_GROUP, S5_STATE), eye8).reshape(4, 512, 128)

    return {
        'norm_w': p['norm_w'][l].reshape(1, D_MODEL),
        'w_in': w_perm, 'w_in32': w_perm32,
        'w_small_t': small.astype(BF16),
        'bias_col': bias_col, 'bias_row': bias_col[0, :N_SMALL].reshape(N_SMALL, 1),
        'alog_col': alog_col, 'alog_row': alog_col[0, :N_SMALL].reshape(N_SMALL, 1),
        'ssd_conv_w': p['ssd_conv_w'][l], 'ssd_conv_b': p['ssd_conv_b'][l].reshape(1, SSD_CONV_CH),
        'ssd_D_exp': jnp.repeat(p['ssd_D'][l], SSD_HEADDIM).reshape(1, BRANCH_W),
        'ssd_norm_w': p['ssd_norm_w'][l].reshape(1, BRANCH_W),
        's5_wbr': pack_b(bbr).astype(BF16), 's5_wbi': pack_b(bbi).astype(BF16),
        's5_wcr': pack_c(p['s5_C_re'][l]).astype(BF16), 's5_wci': pack_c(p['s5_C_im'][l]).astype(BF16),
        's5_ar': jnp.broadcast_to(abr.reshape(1, S5_LANES), (8, S5_LANES)),
        's5_ai': jnp.broadcast_to(abi.reshape(1, S5_LANES), (8, S5_LANES)),
        's5_D': p['s5_D'][l].reshape(1, BRANCH_W), 's5_glu_w': p['s5_glu_w'][l].astype(BF16),
        'ml_conv_w': p['ml_conv_w'][l], 'ml_conv_b': p['ml_conv_b'][l].reshape(1, BRANCH_W),
        'ml_wq': p['ml_wq'][l], 'ml_wk': p['ml_wk'][l], 'ml_wv': p['ml_wv'][l],
        'ml_norm_w': p['ml_norm_w'][l].reshape(1, BRANCH_W), 'ml_skip': p['ml_skip'][l].reshape(1, BRANCH_W),
        'hg_lb': hg_lb[l].reshape(1, BRANCH_W), 'hg_norm_w': p['hg_norm_w'][l].reshape(1, BRANCH_W),
        'w_branch': p['w_branch'][l].astype(BF16), 'w_out': p['w_out'][l].astype(BF16),
    }


_MATMUL_WEIGHTS = ('ml_wq', 'ml_wk', 'ml_wv')
S5_CHUNK = 64
MIXER_PLAN = [(('ssd', 'mlstm', 'hgrn'), 2)]
MIXER_PLAN_PADDED = [(('ssd', 'mlstm', 'hgrn'), 8)]


def _run_group(x3, states, layers, fnw, *, q, valid):
    b, lp, _ = x3.shape
    nc = lp // q
    t = b * lp
    lowp = q >= 16
    consts = _chunk_consts(q)
    ssd_conv, ssd_h, s5_re, s5_im, ml_conv, ml_c, ml_n, ml_m, hg_s = states
    s5_re = s5_re.reshape(DEPTH, b, S5_LANES)
    s5_im = s5_im.reshape(DEPTH, b, S5_LANES)
    if valid < q:
        s5_kw = dict(b=b, lp=lp, q=q, valid=valid, bb=min(b, 128))
    else:
        qs = math.gcd(lp, S5_CHUNK)
        s5_kw = dict(b=b, lp=lp, q=qs, valid=qs, bb=8)
    ml_m = jnp.broadcast_to(jnp.pad(ml_m, ((0, 0), (0, 0), (0, 8 - ML_HEADS)))[..., None], (DEPTH, b, 8, 128))
    plan = [(names, math.gcd(b, n)) for names, n in (MIXER_PLAN_PADDED if valid < q else MIXER_PLAN)]
    kw = dict(b=b, q=q, nc=nc, valid=valid, lowp=lowp)
    x = x3.reshape(t, D_MODEL)
    ys, st, st_s5 = {}, {}, None
    for l, w in enumerate(layers):
        if lowp:
            w = dict(w, **{k: w[k].astype(BF16) for k in _MATMUL_WEIGHTS})
        proj, proj32, small_t = _inproj(x, w['norm_w'], w['w_in'], w['w_in32'], w['w_small_t'])
        proj3 = proj.reshape(b, lp, N_PROJ)
        proj32 = proj32.reshape(b, lp, N_PROJ32)
        srow = small_t.reshape(N_SMALL, b, lp).transpose(1, 0, 2)
        mixers = {
            'ssd': lambda n: _ssd(proj3, proj32, srow, ssd_conv, ssd_h, l, w, consts, bs=n, **kw),
            'mlstm': lambda n: _mlstm(proj3, proj32, srow, ml_conv, ml_c, ml_n, ml_m, l, w, consts, bs=n, **kw),
            'hgrn': lambda n: _hgrn(proj3, proj32, hg_s, l, w, consts, bs=n, **kw),
        }
        for names, n in plan:
            outs = _launch([mixers[k](n) for k in names], None if l == 0 else [st[k] for k in names],
                           grid=(b // n, nc), bs=n, nc=nc, name='_'.join(names))
            for k, o in zip(names, outs):
                ys[k], st[k] = o[0], list(o[1:])
        yb, *st_s5 = _s5(proj, s5_re, s5_im, l, w, st_s5, **s5_kw)
        ya, yc, yd = (ys[k].reshape(t, BRANCH_W) for k in ('ssd', 'mlstm', 'hgrn'))
        x = _merge(proj, (ya, yb, yc, yd), x, w['w_branch'], w['w_out'], fnw, final=(l == DEPTH - 1))
    n_ssd_conv, n_ssd_h = st['ssd']
    n_re, n_im = (s.reshape(DEPTH, b, S5_GROUPS, S5_STATE) for s in st_s5)
    n_ml_conv, n_c, n_n, n_m = st['mlstm']
    new_states = (n_ssd_conv, n_ssd_h, n_re, n_im, n_ml_conv, n_c, n_n, n_m[:, :, :ML_HEADS, 0], st['hgrn'][0])
    return x.reshape(b, lp, D_MODEL), new_states


SAMPLE_PAD = 8


def kernel(x_prompt, x_sample, state_ssd_conv, state_ssd, state_s5_re, state_s5_im, state_mlstm_conv, state_mlstm_C, state_mlstm_n, state_mlstm_m, state_hgrn, norm_w, w_in, ssd_conv_w, ssd_conv_b, ssd_dt_bias, ssd_A_log, ssd_D, ssd_norm_w, s5_A_re, s5_A_im, s5_B_re, s5_B_im, s5_C_re, s5_C_im, s5_D, s5_log_dt, s5_glu_w, ml_conv_w, ml_conv_b, ml_wq, ml_wk, ml_wv, ml_i_bias, ml_f_bias, ml_norm_w, ml_skip, hg_lb_logits, hg_norm_w, w_branch, w_out, final_norm_w):
    p = {'norm_w': norm_w, 'w_in': w_in,
         'ssd_conv_w': ssd_conv_w, 'ssd_conv_b': ssd_conv_b, 'ssd_dt_bias': ssd_dt_bias,
         'ssd_A_log': ssd_A_log, 'ssd_D': ssd_D, 'ssd_norm_w': ssd_norm_w,
         's5_A_re': s5_A_re, 's5_A_im': s5_A_im, 's5_B_re': s5_B_re, 's5_B_im': s5_B_im,
         's5_C_re': s5_C_re, 's5_C_im': s5_C_im, 's5_D': s5_D, 's5_log_dt': s5_log_dt, 's5_glu_w': s5_glu_w,
         'ml_conv_w': ml_conv_w, 'ml_conv_b': ml_conv_b, 'ml_wq': ml_wq, 'ml_wk': ml_wk, 'ml_wv': ml_wv,
         'ml_i_bias': ml_i_bias, 'ml_f_bias': ml_f_bias, 'ml_norm_w': ml_norm_w, 'ml_skip': ml_skip,
         'hg_norm_w': hg_norm_w, 'w_branch': w_branch, 'w_out': w_out}
    lb_cum = jnp.cumsum(jax.nn.softmax(hg_lb_logits, axis=0), axis=0)
    hg_lb = lb_cum - lb_cum[0]
    layers = [_layer_params(l, p, hg_lb) for l in range(DEPTH)]
    fnw = final_norm_w.reshape(1, D_MODEL)

    sample_states = (state_ssd_conv, state_ssd, state_s5_re, state_s5_im, state_mlstm_conv,
                     state_mlstm_C, state_mlstm_n, state_mlstm_m, state_hgrn)
    bp, lp_, _ = x_prompt.shape
    prompt_states = tuple(jnp.zeros((DEPTH, bp) + s.shape[2:], F32) for s in sample_states)
    q_prompt = math.gcd(lp_, 128)
    y_prompt, new_p = _run_group(x_prompt, prompt_states, layers, fnw, q=q_prompt, valid=q_prompt)

    ls = x_sample.shape[1]
    xs_pad = jnp.pad(x_sample, ((0, 0), (0, SAMPLE_PAD - ls), (0, 0)))
    y_sample, new_s = _run_group(xs_pad, sample_states, layers, fnw, q=SAMPLE_PAD, valid=ls)
    y_sample = y_sample[:, :ls]

    out = [y_prompt, y_sample]
    for ps, ss in zip(new_p, new_s):
        out += [ps, ss]
    return tuple(out)
```

```python
import functools
import math
from typing import Callable, NamedTuple

import numpy as np
import jax
import jax.numpy as jnp
from jax import lax
from jax.experimental import pallas as pl
from jax.experimental.pallas import tpu as pltpu

F32 = jnp.float32
BF16 = jnp.bfloat16

D_MODEL = 1024
DEPTH = 2
BRANCH_W = 512
CONV_W = 4
EPS = 1e-6
SSD_HEADS = 8
SSD_HEADDIM = 64
SSD_STATE = 64
SSD_GROUPS = 2
SSD_CONV_CH = 768
S5_GROUPS = 32
S5_GROUP = 16
S5_STATE = 64
S5_LANES = S5_GROUPS * S5_STATE
ML_HEADS = 4
ML_HEADDIM = 128
HG_HEADS = 4
HG_HEADDIM = 128
N_BRANCH = 4

NEG = -1e30

N_PROJ = 9216
COL_MERGE = 0
COL_A_Z = 4096
COL_A_X = 4608
COL_B_U = 5120
COL_B_GATE = 5632
COL_C_X = 6144
COL_C_Z = 6656
COL_C_O = 7168
COL_D_I = 7680
COL_D_Q = 8192
COL_D_G = 8704
N_PROJ32 = 896
COL32_D_F = 0
COL32_A_BC = 512
COL32_SMALL = 768
N_SMALL = 16
SMALL_DT = 0
SMALL_I = 8
SMALL_F = 12

NN = (((1,), (0,)), ((), ()))
NT = (((1,), (1,)), ((), ()))
TN = (((0,), (0,)), ((), ()))


def _mm(a, b, lowp, dims=NN):
    if lowp:
        a = a.astype(BF16)
        b = b.astype(BF16)
    return lax.dot_general(a, b, dims, preferred_element_type=F32)


def _split3(a):
    hi = a.astype(BF16)
    r = a - hi.astype(F32)
    mid = r.astype(BF16)
    lo = (r - mid.astype(F32)).astype(BF16)
    return hi, mid, lo


def _xdot(a, b, dims, split, lowp):
    if not lowp and min((b if split == 0 else a).shape) < 16:
        return lax.dot_general(a, b, dims, precision=lax.Precision.HIGHEST,
                               preferred_element_type=F32)
    if split == 0:
        other = b.astype(BF16)
        return sum(lax.dot_general(p, other, dims, preferred_element_type=F32) for p in _split3(a))
    other = a.astype(BF16)
    return sum(lax.dot_general(other, p, dims, preferred_element_type=F32) for p in _split3(b))


def _sigmoid(x):
    return 0.5 * jnp.tanh(0.5 * x) + 0.5


def _silu(x):
    h = 0.5 * x
    return h * jnp.tanh(h) + h


def _softplus(x):
    return jnp.maximum(x, 0.0) + jnp.log1p(jnp.exp(-jnp.abs(x)))


CONV_PAD = 8


def _conv_chunk(xp_scr, cw_ref, cb_ref, q, valid):
    cw = cw_ref[...]
    p = CONV_PAD
    xp = xp_scr[...]
    conv = (cb_ref[...]
            + cw[0:1] * pltpu.roll(xp, 3, 0)[p:p + q, :]
            + cw[1:2] * pltpu.roll(xp, 2, 0)[p:p + q, :]
            + cw[2:3] * pltpu.roll(xp, 1, 0)[p:p + q, :]
            + cw[3:4] * xp[p:p + q, :])
    xp_scr[p - 3:p, :] = xp_scr[p - 3 + valid:p + valid, :]
    return conv


def _inproj_kernel(x_ref, nw_ref, w_ref, w32_ref, wst_ref, o_ref, o32_ref, st_ref, xn_scr):
    @pl.when(pl.program_id(1) == 0)
    def _():
        x = x_ref[...]
        xn = x * lax.rsqrt(jnp.mean(x * x, -1, keepdims=True) + EPS) * nw_ref[...]
        xb = xn.astype(BF16)
        xn_scr[...] = xb
        o32_ref[...] = lax.dot_general(xb, w32_ref[...], NT, preferred_element_type=F32)
        st_ref[...] = lax.dot_general(wst_ref[...], xb, NT, preferred_element_type=F32)

    o_ref[...] = lax.dot_general(xn_scr[...], w_ref[...], NT, preferred_element_type=F32).astype(BF16)


def _inproj(x, nw, w, w32, wst):
    t = x.shape[0]
    tm = min(1024, t)
    tn = N_PROJ // 4
    return pl.pallas_call(
        _inproj_kernel,
        grid=(t // tm, N_PROJ // tn),
        in_specs=[
            pl.BlockSpec((tm, D_MODEL), lambda i, j: (i, 0)),
            pl.BlockSpec((1, D_MODEL), lambda i, j: (0, 0)),
            pl.BlockSpec((tn, D_MODEL), lambda i, j: (j, 0)),
            pl.BlockSpec((N_PROJ32, D_MODEL), lambda i, j: (0, 0)),
            pl.BlockSpec((N_SMALL, D_MODEL), lambda i, j: (0, 0)),
        ],
        out_specs=[
            pl.BlockSpec((tm, tn), lambda i, j: (i, j)),
            pl.BlockSpec((tm, N_PROJ32), lambda i, j: (i, 0)),
            pl.BlockSpec((N_SMALL, tm), lambda i, j: (0, i)),
        ],
        out_shape=[
            jax.ShapeDtypeStruct((t, N_PROJ), BF16),
            jax.ShapeDtypeStruct((t, N_PROJ32), F32),
            jax.ShapeDtypeStruct((N_SMALL, t), F32),
        ],
        scratch_shapes=[pltpu.VMEM((tm, D_MODEL), BF16)],
        compiler_params=pltpu.CompilerParams(
            dimension_semantics=("arbitrary", "arbitrary"),
            vmem_limit_bytes=48 * 1024 * 1024),
        name="inproj",
    )(x, nw, w, w32, wst)


def _merge_kernel(g_ref, ya_ref, yb_ref, yc_ref, yd_ref, x_ref, wb_ref, wo_ref, fnw_ref, o_ref, *, final):
    acc = None
    for i, y_ref in enumerate((ya_ref, yb_ref, yc_ref, yd_ref)):
        gate = _sigmoid(g_ref[:, i * D_MODEL:(i + 1) * D_MODEL].astype(F32))
        term = gate * jnp.dot(y_ref[...], wb_ref[i], preferred_element_type=F32)
        acc = term if acc is None else acc + term
    out = x_ref[...] + jnp.dot(acc.astype(BF16), wo_ref[...], preferred_element_type=F32)
    if final:
        out = out * lax.rsqrt(jnp.mean(out * out, -1, keepdims=True) + EPS) * fnw_ref[...]
    o_ref[...] = out


def _merge(proj, ys, x, wb, wo, fnw, final):
    t = x.shape[0]
    tm = min(512, t)
    yspec = pl.BlockSpec((tm, BRANCH_W), lambda i: (i, 0))
    return pl.pallas_call(
        functools.partial(_merge_kernel, final=final),
        grid=(t // tm,),
        in_specs=[
            pl.BlockSpec((tm, N_BRANCH * D_MODEL), lambda i: (i, COL_MERGE // (N_BRANCH * D_MODEL))),
            yspec, yspec, yspec, yspec,
            pl.BlockSpec((tm, D_MODEL), lambda i: (i, 0)),
            pl.BlockSpec((N_BRANCH, BRANCH_W, D_MODEL), lambda i: (0, 0, 0)),
            pl.BlockSpec((D_MODEL, D_MODEL), lambda i: (0, 0)),
            pl.BlockSpec((1, D_MODEL), lambda i: (0, 0)),
        ],
        out_specs=pl.BlockSpec((tm, D_MODEL), lambda i: (i, 0)),
        out_shape=jax.ShapeDtypeStruct((t, D_MODEL), F32),
        compiler_params=pltpu.CompilerParams(
            dimension_semantics=("arbitrary",),
            vmem_limit_bytes=48 * 1024 * 1024),
        name="merge",
    )(proj, *ys, x, wb, wo, fnw)


def _col_spec(bs, q, col, width):
    return pl.BlockSpec((bs, q, width), lambda b, c: (b, c, col // width))


def _row_spec(bs, q):
    return pl.BlockSpec((bs, N_SMALL, q), lambda b, c: (b, 0, c))


def _const_spec(shape):
    nd = len(shape)
    return pl.BlockSpec(shape, lambda b, c: (0,) * nd)


def _state_in_spec(layer, bs, tail):
    nd = len(tail)
    return pl.BlockSpec((1, bs) + tail, lambda b, c: (layer, b) + (0,) * nd)


def _state_out_spec(layer, bs, tail):
    nd = len(tail)
    return pl.BlockSpec((1, bs) + tail, lambda b, c: (layer, b) + (0,) * nd)


def _y_spec(bs, q):
    return pl.BlockSpec((bs, q, BRANCH_W), lambda b, c: (b, c, 0))


class _Mixer(NamedTuple):
    in_specs: list
    args: list
    out_specs: list
    out_shapes: list
    scratch: list
    group: Callable


_DONE = object()
_MIXER_PARAMS = pltpu.CompilerParams(
    dimension_semantics=("arbitrary", "arbitrary"),
    vmem_limit_bytes=48 * 1024 * 1024)


def _merge_main(y_bufs, g_ref, yb_ref, x_ref, wb_ref, wo_ref, fnw_ref, o_ref, *, rows, final):
    ya, yc, yd = (y[...].reshape(rows, BRANCH_W) for y in y_bufs)
    ys = (ya, yb_ref[...].reshape(rows, BRANCH_W), yc, yd)
    yield
    acc = None
    for i, y in enumerate(ys):
        gate = _sigmoid(g_ref[:, :, i * D_MODEL:(i + 1) * D_MODEL].reshape(rows, D_MODEL).astype(F32))
        term = gate * jnp.dot(y, wb_ref[i], preferred_element_type=F32)
        acc = term if acc is None else acc + term
        yield
    out = x_ref[...].reshape(rows, D_MODEL) + jnp.dot(acc.astype(BF16), wo_ref[...], preferred_element_type=F32)
    if final:
        out = out * lax.rsqrt(jnp.mean(out * out, -1, keepdims=True) + EPS) * fnw_ref[...]
    o_ref[...] = out.reshape(o_ref.shape)


def _launch(mixers, prevs, merge_args, *, nb, bs, q, nc, final, name):
    total = nb * nc
    n_in = [len(m.in_specs) for m in mixers]
    n_out = [len(m.out_specs) - 1 for m in mixers]
    n_scr = [len(m.scratch) for m in mixers]
    n_merge_in = len(merge_args)
    tot_in, tot_out = sum(n_in) + n_merge_in, sum(n_out) + 1
    n_alias = 0 if prevs is None else sum(n_out)

    def cur(g):
        gm = jnp.minimum(g, total - 1)
        return gm // nc, gm % nc

    def prv(g):
        gp = jnp.maximum(g - 1, 0)
        return gp // nc, gp % nc

    def remap(spec, pos):
        return pl.BlockSpec(spec.block_shape, lambda g: spec.index_map(*pos(g)))

    def kern(*refs):
        ins, outs = refs[:tot_in], refs[tot_in + n_alias:tot_in + n_alias + tot_out]
        scr = refs[tot_in + n_alias + tot_out:]
        y_bufs = scr[sum(n_scr):]
        groups = []
        for i, m in enumerate(mixers):
            groups.append(m.group(ins[sum(n_in[:i]):sum(n_in[:i + 1])],
                                  (y_bufs[i],) + tuple(outs[sum(n_out[:i]):sum(n_out[:i + 1])]),
                                  scr[sum(n_scr[:i]):sum(n_scr[:i + 1])]))
        g_id = pl.program_id(0)
        c = g_id % nc
        live = g_id < total
        views = [[[r.at[s] for r in g[3]] for s in range(bs)] for g in groups]

        @pl.when(g_id == 0)
        def _():
            for y in y_bufs:
                y[...] = jnp.zeros(y.shape, y.dtype)

        @pl.when(jnp.logical_and(c == 0, live))
        def _():
            for g, v in zip(groups, views):
                for s in range(bs):
                    g[0](v[s])

        active = [_merge_main(y_bufs, *ins[sum(n_in):], outs[-1], rows=bs * q, final=final)]
        active += [g[1](v[s], g[4]) for g, v in zip(groups, views) for s in range(bs)]
        while active:
            active = [gen for gen in active if next(gen, _DONE) is not _DONE]

        @pl.when(jnp.logical_and(c == nc - 1, live))
        def _():
            for g, v in zip(groups, views):
                for s in range(bs):
                    g[2](v[s])

    proj3, yb3, x3, wb, wo, fnw = merge_args
    merge_specs = [
        pl.BlockSpec((bs, q, N_BRANCH * D_MODEL), lambda b, c: (b, c, COL_MERGE // (N_BRANCH * D_MODEL))),
        pl.BlockSpec((bs, q, BRANCH_W), lambda b, c: (b, c, 0)),
        pl.BlockSpec((bs, q, D_MODEL), lambda b, c: (b, c, 0)),
        _const_spec(wb.shape), _const_spec(wo.shape), _const_spec(fnw.shape),
    ]
    in_specs = ([remap(sp, cur) for m in mixers for sp in m.in_specs] + [remap(sp, prv) for sp in merge_specs]
                + [pl.BlockSpec(memory_space=pl.ANY)] * n_alias)
    args = [a for m in mixers for a in m.args] + list(merge_args)
    if prevs is not None:
        args += [a for p in prevs for a in p]
    outs = pl.pallas_call(
        kern,
        grid=(total + 1,),
        in_specs=in_specs,
        out_specs=([remap(sp, cur) for m in mixers for sp in m.out_specs[1:]]
                   + [remap(pl.BlockSpec((bs, q, D_MODEL), lambda b, c: (b, c, 0)), prv)]),
        out_shape=[sh for m in mixers for sh in m.out_shapes[1:]] + [jax.ShapeDtypeStruct(x3.shape, F32)],
        scratch_shapes=([sc for m in mixers for sc in m.scratch]
                        + [pltpu.VMEM((bs, q, BRANCH_W), BF16) for _ in mixers]),
        input_output_aliases={tot_in + k: k for k in range(n_alias)},
        compiler_params=pltpu.CompilerParams(dimension_semantics=("arbitrary",),
                                             vmem_limit_bytes=56 * 1024 * 1024),
        name=name,
    )(*args)
    return outs[-1], [outs[sum(n_out[:i]):sum(n_out[:i + 1])] for i in range(len(mixers))]


def _ssd_group(ins, outs, scr, *, q, valid, lowp):
    z_ref, x_ref, bc_ref, sc_ref, sr_ref, conv0_ref, h0_ref, *consts = ins
    y_ref, convo_ref, ho_ref = outs
    xp_scr, h_scr, y_scr = scr

    def init(v):
        v[9][CONV_PAD - 3:CONV_PAD, :] = v[5][...]
        v[10][...] = v[6][...]

    def final(v):
        v[8][...] = v[10][...]
        v[12][...] = v[9][CONV_PAD - 3:CONV_PAD, :]

    return (init, functools.partial(_ssd_main, q=q, valid=valid, lowp=lowp), final,
            (z_ref, x_ref, bc_ref, sc_ref, sr_ref, conv0_ref.at[0], h0_ref.at[0],
             y_ref, ho_ref.at[0], xp_scr, h_scr, y_scr, convo_ref.at[0]),
            tuple(consts))


def _ssd_main(v, k, *, q, valid, lowp):
    z_ref, x_ref, bc_ref, sc_ref, sr_ref, _, _, y_ref, _, xp_scr, h_scr, y_scr, _ = v
    cw_ref, cb_ref, bcol_ref, brow_ref, alc_ref, alr_ref, dexp_ref, nw_ref, e_ref, tl_ref, tu_ref = k
    xp_scr[CONV_PAD:CONV_PAD + q, 0:BRANCH_W] = x_ref[...].astype(F32)
    xp_scr[CONV_PAD:CONV_PAD + q, BRANCH_W:SSD_CONV_CH] = bc_ref[...]
    conv = _conv_chunk(xp_scr, cw_ref, cb_ref, q, valid)
    yield
    xbc = _silu(conv)
    xs = xbc[:, 0:BRANCH_W]
    bm = xbc[:, BRANCH_W:BRANCH_W + 128]
    cm = xbc[:, BRANCH_W + 128:BRANCH_W + 256]

    dt_c = _softplus(sc_ref[...] + bcol_ref[...])
    a_c = dt_c * (-jnp.exp(alc_ref[...]))
    dt_r = _softplus(sr_ref[...] + brow_ref[...])
    a_r = dt_r * (-jnp.exp(alr_ref[...]))
    if valid < q:
        tcol = lax.broadcasted_iota(jnp.int32, (q, 1), 0)
        trow = lax.broadcasted_iota(jnp.int32, (1, q), 1)
        a_c = jnp.where(tcol < valid, a_c, 0.0)
        dt_c = jnp.where(tcol < valid, dt_c, 0.0)
        a_r = jnp.where(trow < valid, a_r, 0.0)

    tl = tl_ref[...]
    acum_c = _xdot(tl, a_c, NN, 1, lowp)
    acum_r = _xdot(a_r, tu_ref[...], NN, 0, lowp)
    e = e_ref[...]
    dt_e = _xdot(dt_c, e, NN, 0, lowp)
    acum_e = _xdot(acum_c, e, NN, 0, lowp)
    xdt = xs * dt_e
    xend = xdt * jnp.exp(acum_e[q - 1:q, :] - acum_e)
    eac = jnp.exp(acum_e)
    dec_last = jnp.exp(acum_c[q - 1:q, :])
    yield

    fine = q < 16
    tri = (lax.broadcasted_iota(jnp.int32, (q, q), 0) >= lax.broadcasted_iota(jnp.int32, (q, q), 1))
    rpg = SSD_HEADS // SSD_GROUPS
    for g in range(SSD_GROUPS):
        bg = bm[:, g * SSD_STATE:(g + 1) * SSD_STATE]
        cg = cm[:, g * SSD_STATE:(g + 1) * SSD_STATE]
        cb = _mm(cg, bg, lowp, NT)
        for r in range(rpg):
            h = g * rpg + r
            hs = slice(h * SSD_HEADDIM, (h + 1) * SSD_HEADDIM)
            seg = acum_c[:, h:h + 1] - acum_r[h:h + 1, :]
            m = cb * jnp.exp(jnp.where(tri, seg, NEG))
            hp = h_scr[h]
            yh = _mm(m, xdt[:, hs], lowp) + _mm(cg, hp, lowp, NT) * eac[:, hs]
            y_scr[:, hs] = yh
            if fine:
                yield
            h_scr[h] = hp * dec_last[:, h:h + 1] + _mm(xend[:, hs], bg, lowp, TN)
            yield

    y = y_scr[...] + xs * dexp_ref[...]
    y = y * _silu(z_ref[...].astype(F32))
    y = y * lax.rsqrt(jnp.mean(y * y, -1, keepdims=True) + EPS) * nw_ref[...]
    y_ref[...] = y.astype(BF16)


def _ssd(proj, proj32, srow, conv_state, h_state, layer, w, consts, *, b, q, nc, valid, lowp, bs):
    return _Mixer(
        in_specs=[
            _col_spec(bs, q, COL_A_Z, BRANCH_W),
            _col_spec(bs, q, COL_A_X, BRANCH_W),
            _col_spec(bs, q, COL32_A_BC, 256),
            _col_spec(bs, q, COL32_SMALL, 128),
            _row_spec(bs, q),
            _state_in_spec(layer, bs, (CONV_W - 1, SSD_CONV_CH)),
            _state_in_spec(layer, bs, (SSD_HEADS, SSD_HEADDIM, SSD_STATE)),
            _const_spec((CONV_W, SSD_CONV_CH)),
            _const_spec((1, SSD_CONV_CH)),
            _const_spec((1, 128)),
            _const_spec((N_SMALL, 1)),
            _const_spec((1, 128)),
            _const_spec((N_SMALL, 1)),
            _const_spec((1, BRANCH_W)),
            _const_spec((1, BRANCH_W)),
            _const_spec((128, BRANCH_W)),
            _const_spec((q, q)),
            _const_spec((q, q)),
        ],
        out_specs=[
            _y_spec(bs, q),
            _state_out_spec(layer, bs, (CONV_W - 1, SSD_CONV_CH)),
            _state_out_spec(layer, bs, (SSD_HEADS, SSD_HEADDIM, SSD_STATE)),
        ],
        out_shapes=[
            jax.ShapeDtypeStruct((b, nc * q, BRANCH_W), BF16),
            jax.ShapeDtypeStruct((DEPTH, b, CONV_W - 1, SSD_CONV_CH), F32),
            jax.ShapeDtypeStruct((DEPTH, b, SSD_HEADS, SSD_HEADDIM, SSD_STATE), F32),
        ],
        scratch=[
            pltpu.VMEM((bs, q + CONV_PAD, SSD_CONV_CH), F32),
            pltpu.VMEM((bs, SSD_HEADS, SSD_HEADDIM, SSD_STATE), F32),
            pltpu.VMEM((bs, q, BRANCH_W), F32),
        ],
        args=[proj, proj, proj32, proj32, srow, conv_state, h_state,
              w['ssd_conv_w'], w['ssd_conv_b'], w['bias_col'], w['bias_row'], w['alog_col'], w['alog_row'],
              w['ssd_D_exp'], w['ssd_norm_w'], consts['expand'], consts['tril'], consts['triu']],
        group=functools.partial(_ssd_group, q=q, valid=valid, lowp=lowp))


def _s5_kernel(u_ref, gate_ref, hr0_ref, hi0_ref, perm_ref, permt_ref, ar_ref, ai_ref,
               wbr_ref, wbi_ref, wcr_ref, wci_ref, d_ref, glu_ref,
               y_ref, hro_ref, hio_ref,
               hr_scr, hi_scr, cr_scr, ci_scr, *, q, nc, ngrp):
    c = pl.program_id(1)

    @pl.when(c == 0)
    def _():
        cr_scr[...] = hr0_ref[0]
        ci_scr[...] = hi0_ref[0]

    rows_in = perm_ref.shape[1]
    u = u_ref[...].reshape(rows_in, BRANCH_W)
    u_tm = jnp.dot(perm_ref[...], u, preferred_element_type=F32).astype(BF16)
    nblk = S5_LANES // 512
    for j in range(nblk):
        uj = u_tm[:, j * 128:(j + 1) * 128]
        hr_scr[:, j * 512:(j + 1) * 512] = jnp.dot(uj, wbr_ref[j], preferred_element_type=F32)
        hi_scr[:, j * 512:(j + 1) * 512] = jnp.dot(uj, wbi_ref[j], preferred_element_type=F32)

    half = S5_LANES // 2
    for hf in range(2):
        sl = slice(hf * half, (hf + 1) * half)
        ar = ar_ref[:, sl]
        ai = ai_ref[:, sl]

        def grp_body(g, carry, sl=sl, ar=ar, ai=ai):
            s0 = pl.multiple_of(g * 8, 8)

            def t_body(t, h):
                r0 = pl.multiple_of(g * (8 * q) + t * 8, 8)
                hr, hi = h
                nr = ar * hr - ai * hi + hr_scr[pl.ds(r0, 8), sl]
                ni = ar * hi + ai * hr + hi_scr[pl.ds(r0, 8), sl]
                hr_scr[pl.ds(r0, 8), sl] = nr
                hi_scr[pl.ds(r0, 8), sl] = ni
                return nr, ni

            h = lax.fori_loop(0, q, t_body, (cr_scr[pl.ds(s0, 8), sl], ci_scr[pl.ds(s0, 8), sl]),
                              unroll=min(q, 4))
            cr_scr[pl.ds(s0, 8), sl] = h[0]
            ci_scr[pl.ds(s0, 8), sl] = h[1]
            return carry

        lax.fori_loop(0, ngrp, grp_body, 0)

    parts = []
    for j in range(nblk):
        sl = slice(j * 512, (j + 1) * 512)
        parts.append(jnp.dot(hr_scr[:, sl].astype(BF16), wcr_ref[j], preferred_element_type=F32)
                     - jnp.dot(hi_scr[:, sl].astype(BF16), wci_ref[j], preferred_element_type=F32))
    y_tm = jnp.concatenate(parts, axis=1)
    y = _xdot(permt_ref[...], y_tm, NN, 1, True) + u.astype(F32) * d_ref[...]
    g = jax.nn.gelu(y)
    y = g * _sigmoid(jnp.dot(g.astype(BF16), glu_ref[...], preferred_element_type=F32))
    y = y * _silu(gate_ref[...].reshape(rows_in, BRANCH_W).astype(F32))
    y_ref[...] = y.reshape(y_ref.shape).astype(BF16)

    @pl.when(c == nc - 1)
    def _():
        hro_ref[0] = cr_scr[...]
        hio_ref[0] = ci_scr[...]


def _s5_perm(bb, lp, valid):
    perm = np.zeros((bb * valid, bb * lp), np.float32)
    for g in range(bb // 8):
        for t in range(valid):
            for s in range(8):
                perm[g * 8 * valid + t * 8 + s, (g * 8 + s) * lp + t] = 1.0
    return jnp.asarray(perm, BF16), jnp.asarray(perm.T, BF16)


def _s5(proj, hr_state, hi_state, layer, w, prev, *, b, lp, q, valid, bb):
    nc = lp // q
    perm, permt = _s5_perm(bb, q, valid)
    rows = bb * valid
    if nc == 1:
        def col_spec(col):
            return pl.BlockSpec((bb * q, BRANCH_W), lambda i, c: (i, col // BRANCH_W))
        src = proj
        y_spec = pl.BlockSpec((bb * q, BRANCH_W), lambda i, c: (i, 0))
        y_shape = jax.ShapeDtypeStruct((b * lp, BRANCH_W), BF16)
    else:
        def col_spec(col):
            return pl.BlockSpec((bb, q, BRANCH_W), lambda i, c: (i, c, col // BRANCH_W))
        src = proj.reshape(b, lp, N_PROJ)
        y_spec = pl.BlockSpec((bb, q, BRANCH_W), lambda i, c: (i, c, 0))
        y_shape = jax.ShapeDtypeStruct((b, lp, BRANCH_W), BF16)
    st_in = pl.BlockSpec((1, bb, S5_LANES), lambda i, c: (layer, i, 0))
    st_out = st_in
    kern = functools.partial(_s5_kernel, q=valid, nc=nc, ngrp=bb // 8)
    n_in = 14
    alias_kw = {}
    if prev is not None:
        base = kern

        def kern(*refs):
            return base(*refs[:n_in], *refs[n_in + 2:])

        alias_kw = dict(input_output_aliases={n_in: 1, n_in + 1: 2})
    y, hr, hi = pl.pallas_call(
        kern,
        grid=(b // bb, nc),
        **alias_kw,
        in_specs=[
            col_spec(COL_B_U),
            col_spec(COL_B_GATE),
            st_in, st_in,
            _const_spec(perm.shape),
            _const_spec(permt.shape),
            _const_spec((8, S5_LANES)),
            _const_spec((8, S5_LANES)),
            _const_spec((4, 128, 512)),
            _const_spec((4, 128, 512)),
            _const_spec((4, 512, 128)),
            _const_spec((4, 512, 128)),
            _const_spec((1, BRANCH_W)),
            _const_spec((BRANCH_W, BRANCH_W)),
        ] + [pl.BlockSpec(memory_space=pl.ANY)] * (0 if prev is None else 2),
        out_specs=[y_spec, st_out, st_out],
        out_shape=[
            y_shape,
            jax.ShapeDtypeStruct((DEPTH, b, S5_LANES), F32),
            jax.ShapeDtypeStruct((DEPTH, b, S5_LANES), F32),
        ],
        scratch_shapes=[
            pltpu.VMEM((rows, S5_LANES), F32),
            pltpu.VMEM((rows, S5_LANES), F32),
            pltpu.VMEM((bb, S5_LANES), F32),
            pltpu.VMEM((bb, S5_LANES), F32),
        ],
        compiler_params=_MIXER_PARAMS,
        name="s5",
    )(src, src, hr_state, hi_state, perm, permt, w['s5_ar'], w['s5_ai'],
      w['s5_wbr'], w['s5_wbi'], w['s5_wcr'], w['s5_wci'], w['s5_D'], w['s5_glu_w'], *(prev or ()))
    return y.reshape(b * lp, BRANCH_W), hr, hi


def _mlstm_group(ins, outs, scr, *, q, valid, lowp):
    x_ref, z_ref, o_ref, sc_ref, sr_ref, conv0_ref, c0_ref, n0_ref, m0_ref, *consts = ins
    y_ref, convo_ref, co_ref, no_ref, mo_ref = outs
    xp_scr, c_scr, n_scr, m_scr, h_scr = scr

    def init(v):
        v[13][CONV_PAD - 3:CONV_PAD, :] = v[5][...]
        v[14][...] = v[6][...]
        v[15][...] = v[7][...]
        v[16][...] = v[8][...]

    def final(v):
        v[10][...] = v[14][...]
        v[11][...] = v[15][...]
        v[12][...] = v[16][...]
        v[18][...] = v[13][CONV_PAD - 3:CONV_PAD, :]

    return (init, functools.partial(_mlstm_main, q=q, valid=valid, lowp=lowp), final,
            (x_ref, z_ref, o_ref, sc_ref, sr_ref, conv0_ref.at[0], c0_ref.at[0], n0_ref.at[0], m0_ref.at[0],
             y_ref, co_ref.at[0], no_ref.at[0], mo_ref.at[0], xp_scr, c_scr, n_scr, m_scr, h_scr,
             convo_ref.at[0]),
            tuple(consts))


def _mlstm_main(v, k, *, q, valid, lowp):
    (x_ref, z_ref, o_ref, sc_ref, sr_ref, _, _, _, _, y_ref, _, _, _,
     xp_scr, c_scr, n_scr, m_scr, h_scr, _) = v
    cw_ref, cb_ref, wq_ref, wk_ref, wv_ref, bcol_ref, brow_ref, nw_ref, skip_ref, tl_ref, tu_ref = k
    x = x_ref[...].astype(F32)
    xp_scr[CONV_PAD:CONV_PAD + q, :] = x
    conv = _conv_chunk(xp_scr, cw_ref, cb_ref, q, valid)
    yield
    xc = _silu(conv)

    pre_c = sc_ref[...] + bcol_ref[...]
    pre_r = sr_ref[...] + brow_ref[...]
    ig_c = pre_c
    lf_c = -_softplus(-pre_c)
    ig_r = pre_r
    lf_r = -_softplus(-pre_r)
    if valid < q:
        tcol = lax.broadcasted_iota(jnp.int32, (q, 1), 0)
        trow = lax.broadcasted_iota(jnp.int32, (1, q), 1)
        lf_c = jnp.where(tcol < valid, lf_c, 0.0)
        ig_c = jnp.where(tcol < valid, ig_c, NEG)
        lf_r = jnp.where(trow < valid, lf_r, 0.0)
        ig_r = jnp.where(trow < valid, ig_r, NEG)
    b_c = _xdot(tl_ref[...], lf_c, NN, 1, lowp)
    b_r = _xdot(lf_r, tu_ref[...], NN, 0, lowp)
    yield

    fine = q < 16
    tri = (lax.broadcasted_iota(jnp.int32, (q, q), 0) >= lax.broadcasted_iota(jnp.int32, (q, q), 1))
    o_all = o_ref[...].astype(F32)
    for h in range(ML_HEADS):
        hs = slice(h * ML_HEADDIM, (h + 1) * ML_HEADDIM)
        xh = xc[:, hs]
        qh = _mm(xh, wq_ref[h], lowp)
        kh = _mm(xh, wk_ref[h], lowp) * (ML_HEADDIM ** -0.5)
        vh = _mm(x[:, hs], wv_ref[h], lowp)
        yield
        bc = b_c[:, SMALL_F + h:SMALL_F + h + 1]
        br = b_r[SMALL_F + h:SMALL_F + h + 1, :]
        ic = ig_c[:, SMALL_I + h:SMALL_I + h + 1]
        ir = ig_r[SMALL_I + h:SMALL_I + h + 1, :]
        mp = m_scr[h:h + 1, 0:1]
        dlog = jnp.where(tri, bc - br + ir, NEG)
        inter = bc + mp
        m_t = jnp.maximum(inter, jnp.max(dlog, axis=1, keepdims=True))
        wgt = jnp.exp(dlog - m_t)
        yield
        s = _mm(qh, kh, lowp, NT) * wgt
        if fine:
            yield
        scale = jnp.exp(inter - m_t)
        cp = c_scr[h]
        npv = n_scr[h:h + 1, :]
        num = _mm(s, vh, lowp) + scale * _mm(qh, cp, lowp)
        dot =jnp.sum(s, axis=1, keepdims=True) + scale * jnp.sum(qh * npv, axis=1, keepdims=True)
        hh = num / jnp.maximum(jnp.abs(dot), jnp.exp(-m_t))
        yield
        m_new = m_t[q - 1:q, :]
        b_last = bc[q - 1:q, :]
        w_end = jnp.exp(b_last - bc + ic - m_new)
        cs = jnp.exp(b_last + mp - m_new)
        if fine:
            yield
        kw = kh * w_end
        c_scr[h] = cs * cp + _mm(kw, vh, lowp, TN)
        n_scr[h:h + 1, :] = cs * npv + jnp.sum(kw, axis=0, keepdims=True)
        m_scr[h:h + 1, :] = jnp.broadcast_to(m_new, (1, 128))
        yield
        hh = hh * _sigmoid(o_all[:, hs])
        hc = hh - jnp.mean(hh, -1, keepdims=True)
        h_scr[:, hs] = hc * lax.rsqrt(jnp.mean(hc * hc, -1, keepdims=True) + EPS)
        yield

    y = h_scr[...] * nw_ref[...] + skip_ref[...] * xc
    y_ref[...] = (y * _silu(z_ref[...].astype(F32))).astype(BF16)


def _mlstm(proj, proj32, srow, conv_state, c_state, n_state, m_state, layer, w, consts,
           *, b, q, nc, valid, lowp, bs):
    return _Mixer(
        in_specs=[
            _col_spec(bs, q, COL_C_X, BRANCH_W),
            _col_spec(bs, q, COL_C_Z, BRANCH_W),
            _col_spec(bs, q, COL_C_O, BRANCH_W),
            _col_spec(bs, q, COL32_SMALL, 128),
            _row_spec(bs, q),
            _state_in_spec(layer, bs, (CONV_W - 1, BRANCH_W)),
            _state_in_spec(layer, bs, (ML_HEADS, ML_HEADDIM, ML_HEADDIM)),
            _state_in_spec(layer, bs, (ML_HEADS, ML_HEADDIM)),
            _state_in_spec(layer, bs, (8, 128)),
            _const_spec((CONV_W, BRANCH_W)),
            _const_spec((1, BRANCH_W)),
            _const_spec((ML_HEADS, ML_HEADDIM, ML_HEADDIM)),
            _const_spec((ML_HEADS, ML_HEADDIM, ML_HEADDIM)),
            _const_spec((ML_HEADS, ML_HEADDIM, ML_HEADDIM)),
            _const_spec((1, 128)),
            _const_spec((N_SMALL, 1)),
            _const_spec((1, BRANCH_W)),
            _const_spec((1, BRANCH_W)),
            _const_spec((q, q)),
            _const_spec((q, q)),
        ],
        out_specs=[
            _y_spec(bs, q),
            _state_out_spec(layer, bs, (CONV_W - 1, BRANCH_W)),
            _state_out_spec(layer, bs, (ML_HEADS, ML_HEADDIM, ML_HEADDIM)),
            _state_out_spec(layer, bs, (ML_HEADS, ML_HEADDIM)),
            _state_out_spec(layer, bs, (8, 128)),
        ],
        out_shapes=[
            jax.ShapeDtypeStruct((b, nc * q, BRANCH_W), BF16),
            jax.ShapeDtypeStruct((DEPTH, b, CONV_W - 1, BRANCH_W), F32),
            jax.ShapeDtypeStruct((DEPTH, b, ML_HEADS, ML_HEADDIM, ML_HEADDIM), F32),
            jax.ShapeDtypeStruct((DEPTH, b, ML_HEADS, ML_HEADDIM), F32),
            jax.ShapeDtypeStruct((DEPTH, b, 8, 128), F32),
        ],
        scratch=[
            pltpu.VMEM((bs, q + CONV_PAD, BRANCH_W), F32),
            pltpu.VMEM((bs, ML_HEADS, ML_HEADDIM, ML_HEADDIM), F32),
            pltpu.VMEM((bs, ML_HEADS, ML_HEADDIM), F32),
            pltpu.VMEM((bs, 8, 128), F32),
            pltpu.VMEM((bs, q, BRANCH_W), F32),
        ],
        args=[proj, proj, proj, proj32, srow, conv_state, c_state, n_state, m_state,
              w['ml_conv_w'], w['ml_conv_b'], w['ml_wq'], w['ml_wk'], w['ml_wv'], w['bias_col'], w['bias_row'],
              w['ml_norm_w'], w['ml_skip'], consts['tril'], consts['triu']],
        group=functools.partial(_mlstm_group, q=q, valid=valid, lowp=lowp))


def _hgrn_group(ins, outs, scr, *, q, valid, lowp, nlev):
    f_ref, i_ref, q_ref, g_ref, s0_ref, *consts = ins
    y_ref, so_ref = outs
    s_scr, o_scr = scr

    def init(v):
        v[7][...] = v[4][...]

    def final(v):
        v[6][...] = v[7][...]

    return (init, functools.partial(_hgrn_main, q=q, valid=valid, lowp=lowp, nlev=nlev), final,
            (f_ref, i_ref, q_ref, g_ref, s0_ref.at[0], y_ref, so_ref.at[0], s_scr, o_scr),
            tuple(consts))


def _hgrn_main(v, k, *, q, valid, lowp, nlev):
    f_ref, i_ref, q_ref, g_ref, _, y_ref, _, s_scr, o_scr = v
    lb_ref, nw_ref, tl_ref = k
    lb = lb_ref[...]
    fg = lb + (1.0 - lb) * _sigmoid(f_ref[...])
    kk = 1.0 - fg
    qq = _silu(q_ref[...].astype(F32)) * (HG_HEADDIM ** -0.5)
    vv = i_ref[...] if lowp else i_ref[...].astype(F32)
    rcol =lax.broadcasted_iota(jnp.int32, (q, 1), 0)
    if valid < q:
        fg = jnp.where(rcol < valid, fg, 1.0)
        kk = jnp.where(rcol < valid, kk, 0.0)
    lf = jnp.log(fg)

    gcum = _xdot(tl_ref[...], lf, NN, 1, lowp)
    yield

    fine = q < 16
    rt = lax.broadcasted_iota(jnp.int32, (q, q), 0)
    cs = lax.broadcasted_iota(jnp.int32, (q, q), 1)
    rx = rt ^ cs
    att = [None] * HG_HEADS
    for lev in range(nlev):
        m = q >> (lev + 1)
        second = (rcol & m) != 0
        if m >= 4:
            nb = q // (2 * m)
            gb = jnp.broadcast_to(gcum.reshape(nb, 2 * m, BRANCH_W)[:, m - 1:m, :], (nb, 2 * m, BRANCH_W))
            e = jnp.exp(-jnp.abs(gcum - gb.reshape(q, BRANCH_W)))
        elif m == 2:
            r4 = rcol & 3
            e = jnp.where(r4 == 0, pltpu.roll(fg, q - 1, 0),
                          jnp.where(r4 == 1, 1.0, jnp.where(r4 == 2, fg, fg * pltpu.roll(fg, 1, 0))))
        else:
            e = jnp.where(second, fg, 1.0)
        u = jnp.where(second, qq, kk) * e
        if lowp:
            u = u.astype(BF16)
        pair = (rx >> int(math.log2(m))) == 1
        for h in range(HG_HEADS):
            hs = slice(h * HG_HEADDIM, (h + 1) * HG_HEADDIM)
            s = lax.dot_general(u[:, hs], u[:, hs], NT, preferred_element_type=F32)
            att[h] = jnp.where(pair, s, 0.0 if att[h] is None else att[h])
        yield

    qg = qq * jnp.exp(gcum)
    kend = kk * jnp.exp(gcum[q - 1:q, :] - gcum)
    ones = jnp.ones((q, 128), F32)
    for h in range(HG_HEADS):
        hs = slice(h * HG_HEADDIM, (h + 1) * HG_HEADDIM)
        diag = jnp.sum(qq[:, hs] * kk[:, hs], axis=1, keepdims=True)
        a = jnp.where(rt > cs, att[h], jnp.where(rt == cs, diag, 0.0))
        sp = s_scr[h]
        oh = _mm(a, vv[:, hs], lowp) + _mm(qg[:, hs], sp, lowp)
        if fine:
            yield
        dec = jnp.exp(_xdot(lf[:, hs], ones, TN, 0, lowp))
        s_scr[h] =sp * dec + _mm(kend[:, hs], vv[:, hs], lowp, TN)
        o_scr[:, hs] = oh * lax.rsqrt(jnp.mean(oh * oh, -1, keepdims=True) + EPS)
        yield

    y_ref[...] = (o_scr[...] * nw_ref[...] * _silu(g_ref[...].astype(F32))).astype(BF16)


def _hgrn(proj, proj32, s_state, layer, w, consts, *, b, q, nc, valid, lowp, bs):
    nlev = int(math.log2(q))
    return _Mixer(
        in_specs=[
            _col_spec(bs, q, COL32_D_F, BRANCH_W),
            _col_spec(bs, q, COL_D_I, BRANCH_W),
            _col_spec(bs, q, COL_D_Q, BRANCH_W),
            _col_spec(bs, q, COL_D_G, BRANCH_W),
            _state_in_spec(layer, bs, (HG_HEADS, HG_HEADDIM, HG_HEADDIM)),
            _const_spec((1, BRANCH_W)),
            _const_spec((1, BRANCH_W)),
            _const_spec((q, q)),
        ],
        out_specs=[
            _y_spec(bs, q),
            _state_out_spec(layer, bs, (HG_HEADS, HG_HEADDIM, HG_HEADDIM)),
        ],
        out_shapes=[
            jax.ShapeDtypeStruct((b, nc * q, BRANCH_W), BF16),
            jax.ShapeDtypeStruct((DEPTH, b, HG_HEADS, HG_HEADDIM, HG_HEADDIM), F32),
        ],
        scratch=[
            pltpu.VMEM((bs, HG_HEADS, HG_HEADDIM, HG_HEADDIM), F32),
            pltpu.VMEM((bs, q, BRANCH_W), F32),
        ],
        args=[proj32, proj, proj, proj, s_state, w['hg_lb'], w['hg_norm_w'], consts['tril']],
        group=functools.partial(_hgrn_group, q=q, valid=valid, lowp=lowp, nlev=nlev))


def _chunk_consts(q):
    r = np.arange(q)
    tril = (r[:, None] >= r[None, :]).astype(np.float32)
    expand = np.zeros((128, BRANCH_W), np.float32)
    for h in range(SSD_HEADS):
        expand[h, h * SSD_HEADDIM:(h + 1) * SSD_HEADDIM] = 1.0
    return {'tril': jnp.asarray(tril), 'triu': jnp.asarray(tril.T), 'expand': jnp.asarray(expand)}


def _cmul(ar, ai, br, bi):
    return ar * br - ai * bi, ar * bi + ai * br


def _layer_params(l, p, hg_lb):
    w_in_t = p['w_in'][l].T
    small = jnp.concatenate([w_in_t[1280:1288], w_in_t[3848:3856]], axis=0)
    w_perm = jnp.concatenate([
        w_in_t[5904:10000],
        w_in_t[0:512],
        w_in_t[512:1024],
        w_in_t[1288:2312],
        w_in_t[2312:3848],
        w_in_t[4368:5904],
    ], axis=0).astype(BF16)
    w_perm32 = jnp.concatenate([
        w_in_t[3856:4368],
        w_in_t[1024:1280],
        small,
        jnp.zeros((N_PROJ32 - COL32_SMALL - N_SMALL, D_MODEL), F32),
    ], axis=0).astype(BF16)

    def lane_pad(v, off):
        return jnp.zeros((1, 128), F32).at[0, off:off + v.shape[0]].set(v)

    bias_col = (lane_pad(p['ssd_dt_bias'][l], SMALL_DT) + lane_pad(p['ml_i_bias'][l], SMALL_I)
                + lane_pad(p['ml_f_bias'][l], SMALL_F))
    alog_col = lane_pad(p['ssd_A_log'][l], SMALL_DT)

    dt = jnp.exp(p['s5_log_dt'][l])[:, None]
    lr = p['s5_A_re'][l]
    li = p['s5_A_im'][l]
    mag = jnp.exp(lr * dt)
    abr, abi = mag * jnp.cos(li * dt), mag * jnp.sin(li * dt)
    den = lr * lr + li * li
    cr = ((abr - 1.0) * lr + abi * li) / den
    ci = (abi * lr - (abr - 1.0) * li) / den
    bbr, bbi = _cmul(cr[..., None], ci[..., None], p['s5_B_re'][l], p['s5_B_im'][l])
    eye8 = jnp.eye(8, dtype=F32)

    def pack_b(bb):
        return jnp.einsum('jgpc,gh->jgchp', bb.reshape(4, 8, S5_STATE, S5_GROUP), eye8).reshape(4, 128, 512)

    def pack_c(cc):
        return jnp.einsum('jgcp,gh->jgphc', cc.reshape(4, 8, S5_GROUP, S5_STATE), eye8).reshape(4, 512, 128)

    return {
        'norm_w': p['norm_w'][l].reshape(1, D_MODEL),
        'w_in': w_perm, 'w_in32': w_perm32,
        'w_small_t': small.astype(BF16),
        'bias_col': bias_col, 'bias_row': bias_col[0, :N_SMALL].reshape(N_SMALL, 1),
        'alog_col': alog_col, 'alog_row': alog_col[0, :N_SMALL].reshape(N_SMALL, 1),
        'ssd_conv_w': p['ssd_conv_w'][l], 'ssd_conv_b': p['ssd_conv_b'][l].reshape(1, SSD_CONV_CH),
        'ssd_D_exp': jnp.repeat(p['ssd_D'][l], SSD_HEADDIM).reshape(1, BRANCH_W),
        'ssd_norm_w': p['ssd_norm_w'][l].reshape(1, BRANCH_W),
        's5_wbr': pack_b(bbr).astype(BF16), 's5_wbi': pack_b(bbi).astype(BF16),
        's5_wcr': pack_c(p['s5_C_re'][l]).astype(BF16), 's5_wci': pack_c(p['s5_C_im'][l]).astype(BF16),
        's5_ar': jnp.broadcast_to(abr.reshape(1, S5_LANES), (8, S5_LANES)),
        's5_ai': jnp.broadcast_to(abi.reshape(1, S5_LANES), (8, S5_LANES)),
        's5_D': p['s5_D'][l].reshape(1, BRANCH_W), 's5_glu_w': p['s5_glu_w'][l].astype(BF16),
        'ml_conv_w': p['ml_conv_w'][l], 'ml_conv_b': p['ml_conv_b'][l].reshape(1, BRANCH_W),
        'ml_wq': p['ml_wq'][l], 'ml_wk': p['ml_wk'][l], 'ml_wv': p['ml_wv'][l],
        'ml_norm_w': p['ml_norm_w'][l].reshape(1, BRANCH_W), 'ml_skip': p['ml_skip'][l].reshape(1, BRANCH_W),
        'hg_lb': hg_lb[l].reshape(1, BRANCH_W), 'hg_norm_w': p['hg_norm_w'][l].reshape(1, BRANCH_W),
        'w_branch': p['w_branch'][l].astype(BF16), 'w_out': p['w_out'][l].astype(BF16),
    }


_MATMUL_WEIGHTS = ('ml_wq', 'ml_wk', 'ml_wv')
S5_CHUNK = 64
SEQS_PER_STEP = 2
SEQS_PER_STEP_PADDED = 8


def _run_group(x3, states, layers, fnw, *, q, valid):
    b, lp, _ = x3.shape
    nc = lp // q
    t = b * lp
    lowp = q >= 16
    consts = _chunk_consts(q)
    ssd_conv, ssd_h, s5_re, s5_im, ml_conv, ml_c, ml_n, ml_m, hg_s = states
    s5_re = s5_re.reshape(DEPTH, b, S5_LANES)
    s5_im = s5_im.reshape(DEPTH, b, S5_LANES)
    if valid < q:
        s5_kw = dict(b=b, lp=lp, q=q, valid=valid, bb=min(b, 128))
    else:
        qs = math.gcd(lp, S5_CHUNK)
        s5_kw = dict(b=b, lp=lp, q=qs, valid=qs, bb=8)
    ml_m = jnp.broadcast_to(jnp.pad(ml_m, ((0, 0), (0, 0), (0, 8 - ML_HEADS)))[..., None], (DEPTH, b, 8, 128))
    bs = math.gcd(b, SEQS_PER_STEP_PADDED if valid < q else SEQS_PER_STEP)
    kw = dict(b=b, q=q, nc=nc, valid=valid, lowp=lowp, bs=bs)
    x = x3.reshape(t, D_MODEL)
    st, st_s5 = None, None
    for l, w in enumerate(layers):
        if lowp:
            w = dict(w, **{k: w[k].astype(BF16) for k in _MATMUL_WEIGHTS})
        proj, proj32, small_t = _inproj(x, w['norm_w'], w['w_in'], w['w_in32'], w['w_small_t'])
        proj3 = proj.reshape(b, lp, N_PROJ)
        proj32 = proj32.reshape(b, lp, N_PROJ32)
        srow = small_t.reshape(N_SMALL, b, lp).transpose(1, 0, 2)
        yb, *st_s5 = _s5(proj, s5_re, s5_im, l, w, st_s5, **s5_kw)
        mixers = [_ssd(proj3, proj32, srow, ssd_conv, ssd_h, l, w, consts, **kw),
                  _mlstm(proj3, proj32, srow, ml_conv, ml_c, ml_n, ml_m, l, w, consts, **kw),
                  _hgrn(proj3, proj32, hg_s, l, w, consts, **kw)]
        merge_args = (proj3, yb.reshape(b, lp, BRANCH_W), x.reshape(b, lp, D_MODEL),
                      w['w_branch'], w['w_out'], fnw)
        x, st = _launch(mixers, st, merge_args, nb=b // bs, bs=bs, q=q, nc=nc,
                        final=(l == DEPTH - 1), name='mixers_merge')
        x = x.reshape(t, D_MODEL)
    (n_ssd_conv, n_ssd_h), (n_ml_conv, n_c, n_n, n_m), (n_s,) = st
    n_re, n_im = (s.reshape(DEPTH, b, S5_GROUPS, S5_STATE) for s in st_s5)
    new_states = (n_ssd_conv, n_ssd_h, n_re, n_im, n_ml_conv, n_c, n_n, n_m[:, :, :ML_HEADS, 0], n_s)
    return x.reshape(b, lp, D_MODEL), new_states


SAMPLE_PAD = 8


def kernel(x_prompt, x_sample, state_ssd_conv, state_ssd, state_s5_re, state_s5_im, state_mlstm_conv, state_mlstm_C, state_mlstm_n, state_mlstm_m, state_hgrn, norm_w, w_in, ssd_conv_w, ssd_conv_b, ssd_dt_bias, ssd_A_log, ssd_D, ssd_norm_w, s5_A_re, s5_A_im, s5_B_re, s5_B_im, s5_C_re, s5_C_im, s5_D, s5_log_dt, s5_glu_w, ml_conv_w, ml_conv_b, ml_wq, ml_wk, ml_wv, ml_i_bias, ml_f_bias, ml_norm_w, ml_skip, hg_lb_logits, hg_norm_w, w_branch, w_out, final_norm_w):
    p = {'norm_w': norm_w, 'w_in': w_in,
         'ssd_conv_w': ssd_conv_w, 'ssd_conv_b': ssd_conv_b, 'ssd_dt_bias': ssd_dt_bias,
         'ssd_A_log': ssd_A_log, 'ssd_D': ssd_D, 'ssd_norm_w': ssd_norm_w,
         's5_A_re': s5_A_re, 's5_A_im': s5_A_im, 's5_B_re': s5_B_re, 's5_B_im': s5_B_im,
         's5_C_re': s5_C_re, 's5_C_im': s5_C_im, 's5_D': s5_D, 's5_log_dt': s5_log_dt, 's5_glu_w': s5_glu_w,
         'ml_conv_w': ml_conv_w, 'ml_conv_b': ml_conv_b, 'ml_wq': ml_wq, 'ml_wk': ml_wk, 'ml_wv': ml_wv,
         'ml_i_bias': ml_i_bias, 'ml_f_bias': ml_f_bias, 'ml_norm_w': ml_norm_w, 'ml_skip': ml_skip,
         'hg_norm_w': hg_norm_w, 'w_branch': w_branch, 'w_out': w_out}
    lb_cum = jnp.cumsum(jax.nn.softmax(hg_lb_logits, axis=0), axis=0)
    hg_lb = lb_cum - lb_cum[0]
    layers = [_layer_params(l, p, hg_lb) for l in range(DEPTH)]
    fnw = final_norm_w.reshape(1, D_MODEL)

    sample_states = (state_ssd_conv, state_ssd, state_s5_re, state_s5_im, state_mlstm_conv,
                     state_mlstm_C, state_mlstm_n, state_mlstm_m, state_hgrn)
    bp, lp_, _ = x_prompt.shape
    prompt_states = tuple(jnp.zeros((DEPTH, bp) + s.shape[2:], F32) for s in sample_states)
    q_prompt = math.gcd(lp_, 128)
    y_prompt, new_p = _run_group(x_prompt, prompt_states, layers, fnw, q=q_prompt, valid=q_prompt)

    ls = x_sample.shape[1]
    xs_pad = jnp.pad(x_sample, ((0, 0), (0, SAMPLE_PAD - ls), (0, 0)))
    y_sample, new_s = _run_group(xs_pad, sample_states, layers, fnw, q=SAMPLE_PAD, valid=ls)
    y_sample = y_sample[:, :ls]

    out = [y_prompt, y_sample]
    for ps, ss in zip(new_p, new_s):
        out += [ps, ss]
    return tuple(out)
```

```python
import functools
import math
from typing import Callable, NamedTuple

import numpy as np
import jax
import jax.numpy as jnp
from jax import lax
from jax.experimental import pallas as pl
from jax.experimental.pallas import tpu as pltpu

F32 = jnp.float32
BF16 = jnp.bfloat16

D_MODEL = 1024
DEPTH = 2
BRANCH_W = 512
CONV_W = 4
EPS = 1e-6
SSD_HEADS = 8
SSD_HEADDIM = 64
SSD_STATE = 64
SSD_GROUPS = 2
SSD_CONV_CH = 768
S5_GROUPS = 32
S5_GROUP = 16
S5_STATE = 64
S5_LANES = S5_GROUPS * S5_STATE
ML_HEADS = 4
ML_HEADDIM = 128
HG_HEADS = 4
HG_HEADDIM = 128
N_BRANCH = 4

NEG = -1e30

N_PROJ = 9216
COL_MERGE = 0
COL_A_Z = 4096
COL_A_X = 4608
COL_B_U = 5120
COL_B_GATE = 5632
COL_C_X = 6144
COL_C_Z = 6656
COL_C_O = 7168
COL_D_I = 7680
COL_D_Q = 8192
COL_D_G = 8704
N_PROJ32 = 896
COL32_D_F = 0
COL32_A_BC = 512
COL32_SMALL = 768
N_SMALL = 16
SMALL_DT = 0
SMALL_I = 8
SMALL_F = 12

NN = (((1,), (0,)), ((), ()))
NT = (((1,), (1,)), ((), ()))
TN = (((0,), (0,)), ((), ()))


def _mm(a, b, lowp, dims=NN):
    if lowp:
        a = a.astype(BF16)
        b = b.astype(BF16)
    return lax.dot_general(a, b, dims, preferred_element_type=F32)


def _split3(a):
    hi = a.astype(BF16)
    r = a - hi.astype(F32)
    mid = r.astype(BF16)
    lo = (r - mid.astype(F32)).astype(BF16)
    return hi, mid, lo


def _xdot(a, b, dims, split, lowp):
    if not lowp and min((b if split == 0 else a).shape) < 8:
        return lax.dot_general(a, b, dims, precision=lax.Precision.HIGHEST,
                               preferred_element_type=F32)
    if split == 0:
        other = b.astype(BF16)
        return sum(lax.dot_general(p, other, dims, preferred_element_type=F32) for p in _split3(a))
    other = a.astype(BF16)
    return sum(lax.dot_general(other, p, dims, preferred_element_type=F32) for p in _split3(b))


def _cumsum_rows(tl, x, lowp):
    q = x.shape[0]
    if q != 8:
        return _xdot(tl, x, NN, 1, lowp)
    r = lax.broadcasted_iota(jnp.int32, (q, 1), 0)
    for d in (1, 2, 4):
        x = x + jnp.where(r >= d, pltpu.roll(x, d, 0), 0.0)
    return x


def _sigmoid(x):
    return 0.5 * jnp.tanh(0.5 * x) + 0.5


def _silu(x):
    h = 0.5 * x
    return h * jnp.tanh(h) + h


def _softplus(x):
    return jnp.maximum(x, 0.0) + jnp.log1p(jnp.exp(-jnp.abs(x)))


CONV_PAD = 8


def _conv_chunk(xp_scr, cw_ref, cb_ref, q, valid):
    cw = cw_ref[...]
    p = CONV_PAD
    xp = xp_scr[...]
    conv = (cb_ref[...]
            + cw[0:1] * pltpu.roll(xp, 3, 0)[p:p + q, :]
            + cw[1:2] * pltpu.roll(xp, 2, 0)[p:p + q, :]
            + cw[2:3] * pltpu.roll(xp, 1, 0)[p:p + q, :]
            + cw[3:4] * xp[p:p + q, :])
    carry = xp_scr[p - 3 + valid:p + valid, :]
    xp_scr[p - 3:p, :] = carry
    return conv, carry


def _inproj_kernel(x_ref, nw_ref, w_ref, w32_ref, wst_ref, o_ref, o32_ref, st_ref, xn_scr):
    @pl.when(pl.program_id(1) == 0)
    def _():
        x = x_ref[...]
        xn = x * lax.rsqrt(jnp.mean(x * x, -1, keepdims=True) + EPS) * nw_ref[...]
        xb = xn.astype(BF16)
        xn_scr[...] = xb
        o32_ref[...] = lax.dot_general(xb, w32_ref[...], NT, preferred_element_type=F32)
        st_ref[...] = lax.dot_general(wst_ref[...], xb, NT, preferred_element_type=F32)

    o_ref[...] = lax.dot_general(xn_scr[...], w_ref[...], NT, preferred_element_type=F32).astype(BF16)


def _inproj(x, nw, w, w32, wst):
    t = x.shape[0]
    tm = min(1024, t)
    tn = N_PROJ // 4
    return pl.pallas_call(
        _inproj_kernel,
        grid=(t // tm, N_PROJ // tn),
        in_specs=[
            pl.BlockSpec((tm, D_MODEL), lambda i, j: (i, 0)),
            pl.BlockSpec((1, D_MODEL), lambda i, j: (0, 0)),
            pl.BlockSpec((tn, D_MODEL), lambda i, j: (j, 0)),
            pl.BlockSpec((N_PROJ32, D_MODEL), lambda i, j: (0, 0)),
            pl.BlockSpec((N_SMALL, D_MODEL), lambda i, j: (0, 0)),
        ],
        out_specs=[
            pl.BlockSpec((tm, tn), lambda i, j: (i, j)),
            pl.BlockSpec((tm, N_PROJ32), lambda i, j: (i, 0)),
            pl.BlockSpec((N_SMALL, tm), lambda i, j: (0, i)),
        ],
        out_shape=[
            jax.ShapeDtypeStruct((t, N_PROJ), BF16),
            jax.ShapeDtypeStruct((t, N_PROJ32), F32),
            jax.ShapeDtypeStruct((N_SMALL, t), F32),
        ],
        scratch_shapes=[pltpu.VMEM((tm, D_MODEL), BF16)],
        compiler_params=pltpu.CompilerParams(
            dimension_semantics=("arbitrary", "arbitrary"),
            vmem_limit_bytes=48 * 1024 * 1024),
        name="inproj",
    )(x, nw, w, w32, wst)


def _merge_kernel(g_ref, ya_ref, yb_ref, yc_ref, yd_ref, x_ref, wb_ref, wo_ref, fnw_ref, o_ref, *, final):
    acc = None
    for i, y_ref in enumerate((ya_ref, yb_ref, yc_ref, yd_ref)):
        gate = _sigmoid(g_ref[:, i * D_MODEL:(i + 1) * D_MODEL].astype(F32))
        term = gate * jnp.dot(y_ref[...], wb_ref[i], preferred_element_type=F32)
        acc = term if acc is None else acc + term
    out = x_ref[...] + jnp.dot(acc.astype(BF16), wo_ref[...], preferred_element_type=F32)
    if final:
        out = out * lax.rsqrt(jnp.mean(out * out, -1, keepdims=True) + EPS) * fnw_ref[...]
    o_ref[...] = out


def _merge(proj, ys, x, wb, wo, fnw, final):
    t = x.shape[0]
    tm = min(512, t)
    yspec = pl.BlockSpec((tm, BRANCH_W), lambda i: (i, 0))
    return pl.pallas_call(
        functools.partial(_merge_kernel, final=final),
        grid=(t // tm,),
        in_specs=[
            pl.BlockSpec((tm, N_BRANCH * D_MODEL), lambda i: (i, COL_MERGE // (N_BRANCH * D_MODEL))),
            yspec, yspec, yspec, yspec,
            pl.BlockSpec((tm, D_MODEL), lambda i: (i, 0)),
            pl.BlockSpec((N_BRANCH, BRANCH_W, D_MODEL), lambda i: (0, 0, 0)),
            pl.BlockSpec((D_MODEL, D_MODEL), lambda i: (0, 0)),
            pl.BlockSpec((1, D_MODEL), lambda i: (0, 0)),
        ],
        out_specs=pl.BlockSpec((tm, D_MODEL), lambda i: (i, 0)),
        out_shape=jax.ShapeDtypeStruct((t, D_MODEL), F32),
        compiler_params=pltpu.CompilerParams(
            dimension_semantics=("arbitrary",),
            vmem_limit_bytes=48 * 1024 * 1024),
        name="merge",
    )(proj, *ys, x, wb, wo, fnw)


def _col_spec(bs, q, col, width):
    return pl.BlockSpec((bs, q, width), lambda b, c: (b, c, col // width))


def _row_spec(bs, q):
    return pl.BlockSpec((bs, N_SMALL, q), lambda b, c: (b, 0, c))


def _const_spec(shape):
    nd = len(shape)
    return pl.BlockSpec(shape, lambda b, c: (0,) * nd)


def _state_in_spec(layer, bs, tail):
    nd = len(tail)
    return pl.BlockSpec((1, bs) + tail, lambda b, c: (layer, b) + (0,) * nd)


def _state_out_spec(layer, bs, tail):
    nd = len(tail)
    return pl.BlockSpec((1, bs) + tail, lambda b, c: (layer, b) + (0,) * nd)


def _y_spec(bs, q):
    return pl.BlockSpec((bs, q, BRANCH_W), lambda b, c: (b, c, 0))


class _Mixer(NamedTuple):
    in_specs: list
    args: list
    out_specs: list
    out_shapes: list
    scratch: list
    group: Callable


_DONE = object()
_MIXER_PARAMS = pltpu.CompilerParams(
    dimension_semantics=("arbitrary", "arbitrary"),
    vmem_limit_bytes=48 * 1024 * 1024)


def _launch(mixers, prevs, *, grid, bs, nc, name):
    n_in = [len(m.in_specs) for m in mixers]
    n_out = [len(m.out_specs) for m in mixers]
    n_scr = [len(m.scratch) for m in mixers]
    tot_in, tot_out = sum(n_in), sum(n_out)
    state_idx = [sum(n_out[:i]) + k for i in range(len(mixers)) for k in range(1, n_out[i])]
    n_alias = 0 if prevs is None else len(state_idx)

    def kern(*refs):
        ins, outs, scr = refs[:tot_in], refs[tot_in + n_alias:tot_in + n_alias + tot_out], refs[tot_in + n_alias + tot_out:]
        groups = []
        for i, m in enumerate(mixers):
            groups.append(m.group(ins[sum(n_in[:i]):sum(n_in[:i + 1])],
                                  outs[sum(n_out[:i]):sum(n_out[:i + 1])],
                                  scr[sum(n_scr[:i]):sum(n_scr[:i + 1])]))
        c = pl.program_id(1)
        views = [[[r.at[s] for r in g[3]] for s in range(bs)] for g in groups]

        @pl.when(c == 0)
        def _():
            for g, v in zip(groups, views):
                for s in range(bs):
                    g[0](v[s])

        active = [g[1](v[s], g[4]) for g, v in zip(groups, views) for s in range(bs)]
        while active:
            active = [gen for gen in active if next(gen, _DONE) is not _DONE]

        @pl.when(c == nc - 1)
        def _():
            for g, v in zip(groups, views):
                for s in range(bs):
                    g[2](v[s])

    in_specs = [sp for m in mixers for sp in m.in_specs] + [pl.BlockSpec(memory_space=pl.ANY)] * n_alias
    args = [a for m in mixers for a in m.args]
    if prevs is not None:
        args += [a for p in prevs for a in p]
    outs = pl.pallas_call(
        kern,
        grid=grid,
        in_specs=in_specs,
        out_specs=[sp for m in mixers for sp in m.out_specs],
        out_shape=[sh for m in mixers for sh in m.out_shapes],
        scratch_shapes=[sc for m in mixers for sc in m.scratch],
        input_output_aliases={tot_in + k: idx for k, idx in enumerate(state_idx)} if n_alias else {},
        compiler_params=_MIXER_PARAMS,
        name=name,
    )(*args)
    return [outs[sum(n_out[:i]):sum(n_out[:i + 1])] for i in range(len(mixers))]


def _ssd_group(ins, outs, scr, *, q, valid, lowp):
    z_ref, x_ref, bc_ref, sc_ref, sr_ref, conv0_ref, h0_ref, *consts = ins
    y_ref, convo_ref, ho_ref = outs
    xp_scr, h_scr, y_scr = scr

    def init(v):
        v[9][CONV_PAD - 3:CONV_PAD, :] = v[5][...]
        v[10][...] = v[6][...]

    def final(v):
        v[8][...] = v[10][...]

    return (init, functools.partial(_ssd_main, q=q, valid=valid, lowp=lowp), final,
            (z_ref, x_ref, bc_ref, sc_ref, sr_ref, conv0_ref.at[0], h0_ref.at[0],
             y_ref, ho_ref.at[0], xp_scr, h_scr, y_scr, convo_ref.at[0]),
            tuple(consts))


def _ssd_main(v, k, *, q, valid, lowp):
    z_ref, x_ref, bc_ref, sc_ref, sr_ref, _, _, y_ref, _, xp_scr, h_scr, y_scr, convo_ref = v
    cw_ref, cb_ref, bcol_ref, brow_ref, alc_ref, alr_ref, dexp_ref, nw_ref, e_ref, tl_ref, tu_ref = k
    xp_scr[CONV_PAD:CONV_PAD + q, 0:BRANCH_W] = x_ref[...].astype(F32)
    xp_scr[CONV_PAD:CONV_PAD + q, BRANCH_W:SSD_CONV_CH] = bc_ref[...]
    conv, carry = _conv_chunk(xp_scr, cw_ref, cb_ref, q, valid)
    convo_ref[...] = carry
    yield
    xbc = _silu(conv)
    xs = xbc[:, 0:BRANCH_W]
    bm = xbc[:, BRANCH_W:BRANCH_W + 128]
    cm = xbc[:, BRANCH_W + 128:BRANCH_W + 256]

    dt_c = _softplus(sc_ref[...] + bcol_ref[...])
    a_c = dt_c * (-jnp.exp(alc_ref[...]))
    dt_r = _softplus(sr_ref[...] + brow_ref[...])
    a_r = dt_r * (-jnp.exp(alr_ref[...]))
    if valid < q:
        tcol = lax.broadcasted_iota(jnp.int32, (q, 1), 0)
        trow = lax.broadcasted_iota(jnp.int32, (1, q), 1)
        a_c = jnp.where(tcol < valid, a_c, 0.0)
        dt_c = jnp.where(tcol < valid, dt_c, 0.0)
        a_r = jnp.where(trow < valid, a_r, 0.0)

    tl = tl_ref[...]
    acum_c = _cumsum_rows(tl, a_c, lowp)
    acum_r = _xdot(a_r, tu_ref[...], NN, 0, lowp)
    e = e_ref[...]
    dt_e = _xdot(dt_c, e, NN, 0, lowp)
    acum_e = _xdot(acum_c, e, NN, 0, lowp)
    xdt = xs * dt_e
    xend = xdt * jnp.exp(acum_e[q - 1:q, :] - acum_e)
    eac = jnp.exp(acum_e)
    dec_last = jnp.exp(acum_c[q - 1:q, :])
    yield

    fine = q < 16
    tri = (lax.broadcasted_iota(jnp.int32, (q, q), 0) >= lax.broadcasted_iota(jnp.int32, (q, q), 1))
    rpg = SSD_HEADS // SSD_GROUPS
    for g in range(SSD_GROUPS):
        bg = bm[:, g * SSD_STATE:(g + 1) * SSD_STATE]
        cg = cm[:, g * SSD_STATE:(g + 1) * SSD_STATE]
        cb = _mm(cg, bg, lowp, NT)
        for r in range(rpg):
            h = g * rpg + r
            hs = slice(h * SSD_HEADDIM, (h + 1) * SSD_HEADDIM)
            seg = acum_c[:, h:h + 1] - acum_r[h:h + 1, :]
            m = cb * jnp.exp(jnp.where(tri, seg, NEG))
            hp = h_scr[h]
            yh = _mm(m, xdt[:, hs], lowp) + _mm(cg, hp, lowp, NT) * eac[:, hs]
            y_scr[:, hs] = yh
            if fine:
                yield
            h_scr[h] = hp * dec_last[:, h:h + 1] + _mm(xend[:, hs], bg, lowp, TN)
            yield

    y = y_scr[...] + xs * dexp_ref[...]
    y = y * _silu(z_ref[...].astype(F32))
    y = y * lax.rsqrt(jnp.mean(y * y, -1, keepdims=True) + EPS) * nw_ref[...]
    y_ref[...] = y.astype(BF16)


def _ssd(proj, proj32, srow, conv_state, h_state, layer, w, consts, *, b, q, nc, valid, lowp, bs):
    return _Mixer(
        in_specs=[
            _col_spec(bs, q, COL_A_Z, BRANCH_W),
            _col_spec(bs, q, COL_A_X, BRANCH_W),
            _col_spec(bs, q, COL32_A_BC, 256),
            _col_spec(bs, q, COL32_SMALL, 128),
            _row_spec(bs, q),
            _state_in_spec(layer, bs, (CONV_W - 1, SSD_CONV_CH)),
            _state_in_spec(layer, bs, (SSD_HEADS, SSD_HEADDIM, SSD_STATE)),
            _const_spec((CONV_W, SSD_CONV_CH)),
            _const_spec((1, SSD_CONV_CH)),
            _const_spec((1, 128)),
            _const_spec((N_SMALL, 1)),
            _const_spec((1, 128)),
            _const_spec((N_SMALL, 1)),
            _const_spec((1, BRANCH_W)),
            _const_spec((1, BRANCH_W)),
            _const_spec((128, BRANCH_W)),
            _const_spec((q, q)),
            _const_spec((q, q)),
        ],
        out_specs=[
            _y_spec(bs, q),
            _state_out_spec(layer, bs, (CONV_W - 1, SSD_CONV_CH)),
            _state_out_spec(layer, bs, (SSD_HEADS, SSD_HEADDIM, SSD_STATE)),
        ],
        out_shapes=[
            jax.ShapeDtypeStruct((b, nc * q, BRANCH_W), BF16),
            jax.ShapeDtypeStruct((DEPTH, b, CONV_W - 1, SSD_CONV_CH), F32),
            jax.ShapeDtypeStruct((DEPTH, b, SSD_HEADS, SSD_HEADDIM, SSD_STATE), F32),
        ],
        scratch=[
            pltpu.VMEM((bs, q + CONV_PAD, SSD_CONV_CH), F32),
            pltpu.VMEM((bs, SSD_HEADS, SSD_HEADDIM, SSD_STATE), F32),
            pltpu.VMEM((bs, q, BRANCH_W), F32),
        ],
        args=[proj, proj, proj32, proj32, srow, conv_state, h_state,
              w['ssd_conv_w'], w['ssd_conv_b'], w['bias_col'], w['bias_row'], w['alog_col'], w['alog_row'],
              w['ssd_D_exp'], w['ssd_norm_w'], consts['expand'], consts['tril'], consts['triu']],
        group=functools.partial(_ssd_group, q=q, valid=valid, lowp=lowp))


def _s5_kernel(u_ref, gate_ref, hr0_ref, hi0_ref, perm_ref, permt_ref, ar_ref, ai_ref,
               wbr_ref, wbi_ref, wcr_ref, wci_ref, d_ref, glu_ref,
               y_ref, hro_ref, hio_ref,
               hr_scr, hi_scr, cr_scr, ci_scr, *, q, nc, ngrp):
    c = pl.program_id(1)

    @pl.when(c == 0)
    def _():
        cr_scr[...] = hr0_ref[0]
        ci_scr[...] = hi0_ref[0]

    rows_in = perm_ref.shape[1]
    perm = perm_ref[...]
    u_tm = jnp.dot(perm, u_ref[...].reshape(rows_in, BRANCH_W), preferred_element_type=F32).astype(BF16)
    gate_tm = jnp.dot(perm, gate_ref[...].reshape(rows_in, BRANCH_W), preferred_element_type=F32)
    nblk = S5_LANES // 512
    for j in range(nblk):
        uj = u_tm[:, j * 128:(j + 1) * 128]
        hr_scr[:, j * 512:(j + 1) * 512] = jnp.dot(uj, wbr_ref[j], preferred_element_type=F32)
        hi_scr[:, j * 512:(j + 1) * 512] = jnp.dot(uj, wbi_ref[j], preferred_element_type=F32)

    half = S5_LANES // 2
    for hf in range(2):
        sl = slice(hf * half, (hf + 1) * half)
        ar = ar_ref[:, sl]
        ai = ai_ref[:, sl]

        def grp_body(g, carry, sl=sl, ar=ar, ai=ai):
            s0 = pl.multiple_of(g * 8, 8)

            def t_body(t, h):
                r0 = pl.multiple_of(g * (8 * q) + t * 8, 8)
                hr, hi = h
                nr = ar * hr - ai * hi + hr_scr[pl.ds(r0, 8), sl]
                ni = ar * hi + ai * hr + hi_scr[pl.ds(r0, 8), sl]
                hr_scr[pl.ds(r0, 8), sl] = nr
                hi_scr[pl.ds(r0, 8), sl] = ni
                return nr, ni

            h = lax.fori_loop(0, q, t_body, (cr_scr[pl.ds(s0, 8), sl], ci_scr[pl.ds(s0, 8), sl]),
                              unroll=min(q, 4))
            cr_scr[pl.ds(s0, 8), sl] = h[0]
            ci_scr[pl.ds(s0, 8), sl] = h[1]
            return carry

        lax.fori_loop(0, ngrp, grp_body, 0)

    parts = []
    for j in range(nblk):
        sl = slice(j * 512, (j + 1) * 512)
        parts.append(jnp.dot(hr_scr[:, sl].astype(BF16), wcr_ref[j], preferred_element_type=F32)
                     - jnp.dot(hi_scr[:, sl].astype(BF16), wci_ref[j], preferred_element_type=F32))
    y = jnp.concatenate(parts, axis=1) + u_tm.astype(F32) * d_ref[...]
    g = jax.nn.gelu(y)
    y = g * _sigmoid(jnp.dot(g.astype(BF16), glu_ref[...], preferred_element_type=F32))
    y = (y * _silu(gate_tm)).astype(BF16)
    y = jnp.dot(permt_ref[...], y, preferred_element_type=F32).astype(BF16)
    y_ref[...] = y.reshape(y_ref.shape)

    @pl.when(c == nc - 1)
    def _():
        hro_ref[0] = cr_scr[...]
        hio_ref[0] = ci_scr[...]


def _s5_perm(bb, lp, valid):
    perm = np.zeros((bb * valid, bb * lp), np.float32)
    for g in range(bb // 8):
        for t in range(valid):
            for s in range(8):
                perm[g * 8 * valid + t * 8 + s, (g * 8 + s) * lp + t] = 1.0
    return jnp.asarray(perm, BF16), jnp.asarray(perm.T, BF16)


def _s5(proj, hr_state, hi_state, layer, w, prev, *, b, lp, q, valid, bb):
    nc = lp // q
    perm, permt = _s5_perm(bb, q, valid)
    rows = bb * valid
    if nc == 1:
        def col_spec(col):
            return pl.BlockSpec((bb * q, BRANCH_W), lambda i, c: (i, col // BRANCH_W))
        src = proj
        y_spec = pl.BlockSpec((bb * q, BRANCH_W), lambda i, c: (i, 0))
        y_shape = jax.ShapeDtypeStruct((b * lp, BRANCH_W), BF16)
    else:
        def col_spec(col):
            return pl.BlockSpec((bb, q, BRANCH_W), lambda i, c: (i, c, col // BRANCH_W))
        src = proj.reshape(b, lp, N_PROJ)
        y_spec = pl.BlockSpec((bb, q, BRANCH_W), lambda i, c: (i, c, 0))
        y_shape = jax.ShapeDtypeStruct((b, lp, BRANCH_W), BF16)
    st_in = pl.BlockSpec((1, bb, S5_LANES), lambda i, c: (layer, i, 0))
    st_out = st_in
    kern = functools.partial(_s5_kernel, q=valid, nc=nc, ngrp=bb // 8)
    n_in = 14
    alias_kw = {}
    if prev is not None:
        base = kern

        def kern(*refs):
            return base(*refs[:n_in], *refs[n_in + 2:])

        alias_kw = dict(input_output_aliases={n_in: 1, n_in + 1: 2})
    y, hr, hi = pl.pallas_call(
        kern,
        grid=(b // bb, nc),
        **alias_kw,
        in_specs=[
            col_spec(COL_B_U),
            col_spec(COL_B_GATE),
            st_in, st_in,
            _const_spec(perm.shape),
            _const_spec(permt.shape),
            _const_spec((8, S5_LANES)),
            _const_spec((8, S5_LANES)),
            _const_spec((4, 128, 512)),
            _const_spec((4, 128, 512)),
            _const_spec((4, 512, 128)),
            _const_spec((4, 512, 128)),
            _const_spec((1, BRANCH_W)),
            _const_spec((BRANCH_W, BRANCH_W)),
        ] + [pl.BlockSpec(memory_space=pl.ANY)] * (0 if prev is None else 2),
        out_specs=[y_spec, st_out, st_out],
        out_shape=[
            y_shape,
            jax.ShapeDtypeStruct((DEPTH, b, S5_LANES), F32),
            jax.ShapeDtypeStruct((DEPTH, b, S5_LANES), F32),
        ],
        scratch_shapes=[
            pltpu.VMEM((rows, S5_LANES), F32),
            pltpu.VMEM((rows, S5_LANES), F32),
            pltpu.VMEM((bb, S5_LANES), F32),
            pltpu.VMEM((bb, S5_LANES), F32),
        ],
        compiler_params=_MIXER_PARAMS,
        name="s5",
    )(src, src, hr_state, hi_state, perm, permt, w['s5_ar'], w['s5_ai'],
      w['s5_wbr'], w['s5_wbi'], w['s5_wcr'], w['s5_wci'], w['s5_D'], w['s5_glu_w'], *(prev or ()))
    return y.reshape(b * lp, BRANCH_W), hr, hi


def _mlstm_group(ins, outs, scr, *, q, valid, lowp):
    x_ref, z_ref, o_ref, sc_ref, sr_ref, conv0_ref, c0_ref, n0_ref, m0_ref, *consts = ins
    y_ref, convo_ref, co_ref, no_ref, mo_ref = outs
    xp_scr, c_scr, n_scr, m_scr, h_scr = scr

    def init(v):
        v[13][CONV_PAD - 3:CONV_PAD, :] = v[5][...]
        v[14][...] = v[6][...]
        v[15][...] = v[7][...]
        v[16][...] = v[8][...]

    def final(v):
        v[10][...] = v[14][...]
        v[11][...] = v[15][...]
        v[12][...] = v[16][...]

    return (init, functools.partial(_mlstm_main, q=q, valid=valid, lowp=lowp), final,
            (x_ref, z_ref, o_ref, sc_ref, sr_ref, conv0_ref.at[0], c0_ref.at[0], n0_ref.at[0], m0_ref.at[0],
             y_ref, co_ref.at[0], no_ref.at[0], mo_ref.at[0], xp_scr, c_scr, n_scr, m_scr, h_scr,
             convo_ref.at[0]),
            tuple(consts))


def _mlstm_main(v, k, *, q, valid, lowp):
    (x_ref, z_ref, o_ref, sc_ref, sr_ref, _, _, _, _, y_ref, _, _, _,
     xp_scr, c_scr, n_scr, m_scr, h_scr, convo_ref) = v
    cw_ref, cb_ref, wq_ref, wk_ref, wv_ref, bcol_ref, brow_ref, nw_ref, skip_ref, tl_ref, tu_ref = k
    x = x_ref[...].astype(F32)
    xp_scr[CONV_PAD:CONV_PAD + q, :] = x
    conv, carry = _conv_chunk(xp_scr, cw_ref, cb_ref, q, valid)
    convo_ref[...] = carry
    yield
    xc = _silu(conv)

    pre_c = sc_ref[...] + bcol_ref[...]
    pre_r = sr_ref[...] + brow_ref[...]
    ig_c = pre_c
    lf_c = -_softplus(-pre_c)
    ig_r = pre_r
    lf_r = -_softplus(-pre_r)
    if valid < q:
        tcol = lax.broadcasted_iota(jnp.int32, (q, 1), 0)
        trow = lax.broadcasted_iota(jnp.int32, (1, q), 1)
        lf_c = jnp.where(tcol < valid, lf_c, 0.0)
        ig_c = jnp.where(tcol < valid, ig_c, NEG)
        lf_r = jnp.where(trow < valid, lf_r, 0.0)
        ig_r = jnp.where(trow < valid, ig_r, NEG)
    b_c = _cumsum_rows(tl_ref[...], lf_c, lowp)
    b_r = _xdot(lf_r, tu_ref[...], NN, 0, lowp)
    yield

    fine = q < 16
    tri = (lax.broadcasted_iota(jnp.int32, (q, q), 0) >= lax.broadcasted_iota(jnp.int32, (q, q), 1))
    o_all = o_ref[...].astype(F32)
    for h in range(ML_HEADS):
        hs = slice(h * ML_HEADDIM, (h + 1) * ML_HEADDIM)
        xh = xc[:, hs]
        qh = _mm(xh, wq_ref[h], lowp)
        kh = _mm(xh, wk_ref[h], lowp) * (ML_HEADDIM ** -0.5)
        vh = _mm(x[:, hs], wv_ref[h], lowp)
        yield
        bc = b_c[:, SMALL_F + h:SMALL_F + h + 1]
        br = b_r[SMALL_F + h:SMALL_F + h + 1, :]
        ic = ig_c[:, SMALL_I + h:SMALL_I + h + 1]
        ir = ig_r[SMALL_I + h:SMALL_I + h + 1, :]
        mp = m_scr[h:h + 1, 0:1]
        dlog = jnp.where(tri, bc - br + ir, NEG)
        inter = bc + mp
        m_t = jnp.maximum(inter, jnp.max(dlog, axis=1, keepdims=True))
        wgt = jnp.exp(dlog - m_t)
        yield
        s = _mm(qh, kh, lowp, NT) * wgt
        if fine:
            yield
        scale = jnp.exp(inter - m_t)
        cp = c_scr[h]
        npv = n_scr[h:h + 1, :]
        num = _mm(s, vh, lowp) + scale * _mm(qh, cp, lowp)
        dot =jnp.sum(s, axis=1, keepdims=True) + scale * jnp.sum(qh * npv, axis=1, keepdims=True)
        hh = num / jnp.maximum(jnp.abs(dot), jnp.exp(-m_t))
        yield
        m_new = m_t[q - 1:q, :]
        b_last = bc[q - 1:q, :]
        w_end = jnp.exp(b_last - bc + ic - m_new)
        cs = jnp.exp(b_last + mp - m_new)
        if fine:
            yield
        kw = kh * w_end
        c_scr[h] = cs * cp + _mm(kw, vh, lowp, TN)
        n_scr[h:h + 1, :] = cs * npv + jnp.sum(kw, axis=0, keepdims=True)
        m_scr[h:h + 1, :] = jnp.broadcast_to(m_new, (1, 128))
        yield
        hh = hh * _sigmoid(o_all[:, hs])
        hc = hh - jnp.mean(hh, -1, keepdims=True)
        h_scr[:, hs] = hc * lax.rsqrt(jnp.mean(hc * hc, -1, keepdims=True) + EPS)
        yield

    y = h_scr[...] * nw_ref[...] + skip_ref[...] * xc
    y_ref[...] = (y * _silu(z_ref[...].astype(F32))).astype(BF16)


def _mlstm(proj, proj32, srow, conv_state, c_state, n_state, m_state, layer, w, consts,
           *, b, q, nc, valid, lowp, bs):
    return _Mixer(
        in_specs=[
            _col_spec(bs, q, COL_C_X, BRANCH_W),
            _col_spec(bs, q, COL_C_Z, BRANCH_W),
            _col_spec(bs, q, COL_C_O, BRANCH_W),
            _col_spec(bs, q, COL32_SMALL, 128),
            _row_spec(bs, q),
            _state_in_spec(layer, bs, (CONV_W - 1, BRANCH_W)),
            _state_in_spec(layer, bs, (ML_HEADS, ML_HEADDIM, ML_HEADDIM)),
            _state_in_spec(layer, bs, (ML_HEADS, ML_HEADDIM)),
            _state_in_spec(layer, bs, (8, 128)),
            _const_spec((CONV_W, BRANCH_W)),
            _const_spec((1, BRANCH_W)),
            _const_spec((ML_HEADS, ML_HEADDIM, ML_HEADDIM)),
            _const_spec((ML_HEADS, ML_HEADDIM, ML_HEADDIM)),
            _const_spec((ML_HEADS, ML_HEADDIM, ML_HEADDIM)),
            _const_spec((1, 128)),
            _const_spec((N_SMALL, 1)),
            _const_spec((1, BRANCH_W)),
            _const_spec((1, BRANCH_W)),
            _const_spec((q, q)),
            _const_spec((q, q)),
        ],
        out_specs=[
            _y_spec(bs, q),
            _state_out_spec(layer, bs, (CONV_W - 1, BRANCH_W)),
            _state_out_spec(layer, bs, (ML_HEADS, ML_HEADDIM, ML_HEADDIM)),
            _state_out_spec(layer, bs, (ML_HEADS, ML_HEADDIM)),
            _state_out_spec(layer, bs, (8, 128)),
        ],
        out_shapes=[
            jax.ShapeDtypeStruct((b, nc * q, BRANCH_W), BF16),
            jax.ShapeDtypeStruct((DEPTH, b, CONV_W - 1, BRANCH_W), F32),
            jax.ShapeDtypeStruct((DEPTH, b, ML_HEADS, ML_HEADDIM, ML_HEADDIM), F32),
            jax.ShapeDtypeStruct((DEPTH, b, ML_HEADS, ML_HEADDIM), F32),
            jax.ShapeDtypeStruct((DEPTH, b, 8, 128), F32),
        ],
        scratch=[
            pltpu.VMEM((bs, q + CONV_PAD, BRANCH_W), F32),
            pltpu.VMEM((bs, ML_HEADS, ML_HEADDIM, ML_HEADDIM), F32),
            pltpu.VMEM((bs, ML_HEADS, ML_HEADDIM), F32),
            pltpu.VMEM((bs, 8, 128), F32),
            pltpu.VMEM((bs, q, BRANCH_W), F32),
        ],
        args=[proj, proj, proj, proj32, srow, conv_state, c_state, n_state, m_state,
              w['ml_conv_w'], w['ml_conv_b'], w['ml_wq'], w['ml_wk'], w['ml_wv'], w['bias_col'], w['bias_row'],
              w['ml_norm_w'], w['ml_skip'], consts['tril'], consts['triu']],
        group=functools.partial(_mlstm_group, q=q, valid=valid, lowp=lowp))


def _hgrn_group(ins, outs, scr, *, q, valid, lowp, nlev):
    f_ref, i_ref, q_ref, g_ref, s0_ref, *consts = ins
    y_ref, so_ref = outs
    s_scr, o_scr = scr

    def init(v):
        v[7][...] = v[4][...]

    def final(v):
        v[6][...] = v[7][...]

    return (init, functools.partial(_hgrn_main, q=q, valid=valid, lowp=lowp, nlev=nlev), final,
            (f_ref, i_ref, q_ref, g_ref, s0_ref.at[0], y_ref, so_ref.at[0], s_scr, o_scr),
            tuple(consts))


def _hgrn_main(v, k, *, q, valid, lowp, nlev):
    f_ref, i_ref, q_ref, g_ref, _, y_ref, _, s_scr, o_scr = v
    lb_ref, nw_ref, tl_ref = k
    lb = lb_ref[...]
    fg = lb + (1.0 - lb) * _sigmoid(f_ref[...])
    kk = 1.0 - fg
    qq = _silu(q_ref[...].astype(F32)) * (HG_HEADDIM ** -0.5)
    vv = i_ref[...] if lowp else i_ref[...].astype(F32)
    rcol =lax.broadcasted_iota(jnp.int32, (q, 1), 0)
    if valid < q:
        fg = jnp.where(rcol < valid, fg, 1.0)
        kk = jnp.where(rcol < valid, kk, 0.0)
    lf = jnp.log(fg)

    gcum = _cumsum_rows(tl_ref[...], lf, lowp)
    yield

    fine = q < 16
    rt = lax.broadcasted_iota(jnp.int32, (q, q), 0)
    cs = lax.broadcasted_iota(jnp.int32, (q, q), 1)
    rx = rt ^ cs
    att = [None] * HG_HEADS
    for lev in range(nlev):
        m = q >> (lev + 1)
        second = (rcol & m) != 0
        if m >= 4:
            nb = q // (2 * m)
            gb = jnp.broadcast_to(gcum.reshape(nb, 2 * m, BRANCH_W)[:, m - 1:m, :], (nb, 2 * m, BRANCH_W))
            e = jnp.exp(-jnp.abs(gcum - gb.reshape(q, BRANCH_W)))
        elif m == 2:
            r4 = rcol & 3
            e = jnp.where(r4 == 0, pltpu.roll(fg, q - 1, 0),
                          jnp.where(r4 == 1, 1.0, jnp.where(r4 == 2, fg, fg * pltpu.roll(fg, 1, 0))))
        else:
            e = jnp.where(second, fg, 1.0)
        u = jnp.where(second, qq, kk) * e
        if lowp:
            u = u.astype(BF16)
        pair = (rx >> int(math.log2(m))) == 1
        for h in range(HG_HEADS):
            hs = slice(h * HG_HEADDIM, (h + 1) * HG_HEADDIM)
            s = lax.dot_general(u[:, hs], u[:, hs], NT, preferred_element_type=F32)
            att[h] = jnp.where(pair, s, 0.0 if att[h] is None else att[h])
        yield

    qg = qq * jnp.exp(gcum)
    kend = kk * jnp.exp(gcum[q - 1:q, :] - gcum)
    ones = jnp.ones((q, 128), F32)
    for h in range(HG_HEADS):
        hs = slice(h * HG_HEADDIM, (h + 1) * HG_HEADDIM)
        diag = jnp.sum(qq[:, hs] * kk[:, hs], axis=1, keepdims=True)
        a = jnp.where(rt > cs, att[h], jnp.where(rt == cs, diag, 0.0))
        sp = s_scr[h]
        oh = _mm(a, vv[:, hs], lowp) + _mm(qg[:, hs], sp, lowp)
        if fine:
            yield
        dec = jnp.exp(_xdot(lf[:, hs], ones, TN, 0, lowp))
        s_scr[h] =sp * dec + _mm(kend[:, hs], vv[:, hs], lowp, TN)
        o_scr[:, hs] = oh * lax.rsqrt(jnp.mean(oh * oh, -1, keepdims=True) + EPS)
        yield

    y_ref[...] = (o_scr[...] * nw_ref[...] * _silu(g_ref[...].astype(F32))).astype(BF16)


def _hgrn(proj, proj32, s_state, layer, w, consts, *, b, q, nc, valid, lowp, bs):
    nlev = int(math.log2(q))
    return _Mixer(
        in_specs=[
            _col_spec(bs, q, COL32_D_F, BRANCH_W),
            _col_spec(bs, q, COL_D_I, BRANCH_W),
            _col_spec(bs, q, COL_D_Q, BRANCH_W),
            _col_spec(bs, q, COL_D_G, BRANCH_W),
            _state_in_spec(layer, bs, (HG_HEADS, HG_HEADDIM, HG_HEADDIM)),
            _const_spec((1, BRANCH_W)),
            _const_spec((1, BRANCH_W)),
            _const_spec((q, q)),
        ],
        out_specs=[
            _y_spec(bs, q),
            _state_out_spec(layer, bs, (HG_HEADS, HG_HEADDIM, HG_HEADDIM)),
        ],
        out_shapes=[
            jax.ShapeDtypeStruct((b, nc * q, BRANCH_W), BF16),
            jax.ShapeDtypeStruct((DEPTH, b, HG_HEADS, HG_HEADDIM, HG_HEADDIM), F32),
        ],
        scratch=[
            pltpu.VMEM((bs, HG_HEADS, HG_HEADDIM, HG_HEADDIM), F32),
            pltpu.VMEM((bs, q, BRANCH_W), F32),
        ],
        args=[proj32, proj, proj, proj, s_state, w['hg_lb'], w['hg_norm_w'], consts['tril']],
        group=functools.partial(_hgrn_group, q=q, valid=valid, lowp=lowp, nlev=nlev))


def _chunk_consts(q):
    r = np.arange(q)
    tril = (r[:, None] >= r[None, :]).astype(np.float32)
    expand = np.zeros((128, BRANCH_W), np.float32)
    for h in range(SSD_HEADS):
        expand[h, h * SSD_HEADDIM:(h + 1) * SSD_HEADDIM] = 1.0
    return {'tril': jnp.asarray(tril), 'triu': jnp.asarray(tril.T), 'expand': jnp.asarray(expand)}


def _cmul(ar, ai, br, bi):
    return ar * br - ai * bi, ar * bi + ai * br


def _layer_params(l, p, hg_lb):
    w_in_t = p['w_in'][l].T
    small = jnp.concatenate([w_in_t[1280:1288], w_in_t[3848:3856]], axis=0)
    w_perm = jnp.concatenate([
        w_in_t[5904:10000],
        w_in_t[0:512],
        w_in_t[512:1024],
        w_in_t[1288:2312],
        w_in_t[2312:3848],
        w_in_t[4368:5904],
    ], axis=0).astype(BF16)
    w_perm32 = jnp.concatenate([
        w_in_t[3856:4368],
        w_in_t[1024:1280],
        small,
        jnp.zeros((N_PROJ32 - COL32_SMALL - N_SMALL, D_MODEL), F32),
    ], axis=0).astype(BF16)

    def lane_pad(v, off):
        return jnp.zeros((1, 128), F32).at[0, off:off + v.shape[0]].set(v)

    bias_col = (lane_pad(p['ssd_dt_bias'][l], SMALL_DT) + lane_pad(p['ml_i_bias'][l], SMALL_I)
                + lane_pad(p['ml_f_bias'][l], SMALL_F))
    alog_col = lane_pad(p['ssd_A_log'][l], SMALL_DT)

    dt = jnp.exp(p['s5_log_dt'][l])[:, None]
    lr = p['s5_A_re'][l]
    li = p['s5_A_im'][l]
    mag = jnp.exp(lr * dt)
    abr, abi = mag * jnp.cos(li * dt), mag * jnp.sin(li * dt)
    den = lr * lr + li * li
    cr = ((abr - 1.0) * lr + abi * li) / den
    ci = (abi * lr - (abr - 1.0) * li) / den
    bbr, bbi = _cmul(cr[..., None], ci[..., None], p['s5_B_re'][l], p['s5_B_im'][l])
    eye8 = jnp.eye(8, dtype=F32)

    def pack_b(bb):
        return jnp.einsum('jgpc,gh->jgchp', bb.reshape(4, 8, S5_STATE, S5_GROUP), eye8).reshape(4, 128, 512)

    def pack_c(cc):
        return jnp.einsum('jgcp,gh->jgphc', cc.reshape(4, 8, S5_GROUP, S5_STATE), eye8).reshape(4, 512, 128)

    return {
        'norm_w': p['norm_w'][l].reshape(1, D_MODEL),
        'w_in': w_perm, 'w_in32': w_perm32,
        'w_small_t': small.astype(BF16),
        'bias_col': bias_col, 'bias_row': bias_col[0, :N_SMALL].reshape(N_SMALL, 1),
        'alog_col': alog_col, 'alog_row': alog_col[0, :N_SMALL].reshape(N_SMALL, 1),
        'ssd_conv_w': p['ssd_conv_w'][l], 'ssd_conv_b': p['ssd_conv_b'][l].reshape(1, SSD_CONV_CH),
        'ssd_D_exp': jnp.repeat(p['ssd_D'][l], SSD_HEADDIM).reshape(1, BRANCH_W),
        'ssd_norm_w': p['ssd_norm_w'][l].reshape(1, BRANCH_W),
        's5_wbr': pack_b(bbr).astype(BF16), 's5_wbi': pack_b(bbi).astype(BF16),
        's5_wcr': pack_c(p['s5_C_re'][l]).astype(BF16), 's5_wci': pack_c(p['s5_C_im'][l]).astype(BF16),
        's5_ar': jnp.broadcast_to(abr.reshape(1, S5_LANES), (8, S5_LANES)),
        's5_ai': jnp.broadcast_to(abi.reshape(1, S5_LANES), (8, S5_LANES)),
        's5_D': p['s5_D'][l].reshape(1, BRANCH_W), 's5_glu_w': p['s5_glu_w'][l].astype(BF16),
        'ml_conv_w': p['ml_conv_w'][l], 'ml_conv_b': p['ml_conv_b'][l].reshape(1, BRANCH_W),
        'ml_wq': p['ml_wq'][l], 'ml_wk': p['ml_wk'][l], 'ml_wv': p['ml_wv'][l],
        'ml_norm_w': p['ml_norm_w'][l].reshape(1, BRANCH_W), 'ml_skip': p['ml_skip'][l].reshape(1, BRANCH_W),
        'hg_lb': hg_lb[l].reshape(1, BRANCH_W), 'hg_norm_w': p['hg_norm_w'][l].reshape(1, BRANCH_W),
        'w_branch': p['w_branch'][l].astype(BF16), 'w_out': p['w_out'][l].astype(BF16),
    }


_MATMUL_WEIGHTS = ('ml_wq', 'ml_wk', 'ml_wv')
S5_CHUNK = 64
MIXER_PLAN = [(('ssd', 'mlstm', 'hgrn'), 2)]
MIXER_PLAN_PADDED = [(('ssd', 'mlstm', 'hgrn'), 8)]


def _run_group(x3, states, layers, fnw, *, q, valid):
    b, lp, _ = x3.shape
    nc = lp // q
    t = b * lp
    lowp = q >= 16
    consts = _chunk_consts(q)
    ssd_conv, ssd_h, s5_re, s5_im, ml_conv, ml_c, ml_n, ml_m, hg_s = states
    s5_re = s5_re.reshape(DEPTH, b, S5_LANES)
    s5_im = s5_im.reshape(DEPTH, b, S5_LANES)
    if valid < q:
        s5_kw = dict(b=b, lp=lp, q=q, valid=valid, bb=min(b, 128))
    else:
        qs = math.gcd(lp, S5_CHUNK)
        s5_kw = dict(b=b, lp=lp, q=qs, valid=qs, bb=8)
    ml_m = jnp.broadcast_to(jnp.pad(ml_m, ((0, 0), (0, 0), (0, 8 - ML_HEADS)))[..., None], (DEPTH, b, 8, 128))
    plan = [(names, math.gcd(b, n)) for names, n in (MIXER_PLAN_PADDED if valid < q else MIXER_PLAN)]
    kw = dict(b=b, q=q, nc=nc, valid=valid, lowp=lowp)
    x = x3.reshape(t, D_MODEL)
    ys, st, st_s5 = {}, {}, None
    for l, w in enumerate(layers):
        if lowp:
            w = dict(w, **{k: w[k].astype(BF16) for k in _MATMUL_WEIGHTS})
        proj, proj32, small_t = _inproj(x, w['norm_w'], w['w_in'], w['w_in32'], w['w_small_t'])
        proj3 = proj.reshape(b, lp, N_PROJ)
        proj32 = proj32.reshape(b, lp, N_PROJ32)
        srow = small_t.reshape(N_SMALL, b, lp).transpose(1, 0, 2)
        mixers = {
            'ssd': lambda n: _ssd(proj3, proj32, srow, ssd_conv, ssd_h, l, w, consts, bs=n, **kw),
            'mlstm': lambda n: _mlstm(proj3, proj32, srow, ml_conv, ml_c, ml_n, ml_m, l, w, consts, bs=n, **kw),
            'hgrn': lambda n: _hgrn(proj3, proj32, hg_s, l, w, consts, bs=n, **kw),
        }
        for names, n in plan:
            outs = _launch([mixers[k](n) for k in names], None if l == 0 else [st[k] for k in names],
                           grid=(b // n, nc), bs=n, nc=nc, name='_'.join(names))
            for k, o in zip(names, outs):
                ys[k], st[k] = o[0], list(o[1:])
        yb, *st_s5 = _s5(proj, s5_re, s5_im, l, w, st_s5, **s5_kw)
        ya, yc, yd = (ys[k].reshape(t, BRANCH_W) for k in ('ssd', 'mlstm', 'hgrn'))
        x = _merge(proj, (ya, yb, yc, yd), x, w['w_branch'], w['w_out'], fnw, final=(l == DEPTH - 1))
    n_ssd_conv, n_ssd_h = st['ssd']
    n_re, n_im = (s.reshape(DEPTH, b, S5_GROUPS, S5_STATE) for s in st_s5)
    n_ml_conv, n_c, n_n, n_m = st['mlstm']
    new_states = (n_ssd_conv, n_ssd_h, n_re, n_im, n_ml_conv, n_c, n_n, n_m[:, :, :ML_HEADS, 0], st['hgrn'][0])
    return x.reshape(b, lp, D_MODEL), new_states


SAMPLE_PAD = 8


def kernel(x_prompt, x_sample, state_ssd_conv, state_ssd, state_s5_re, state_s5_im, state_mlstm_conv, state_mlstm_C, state_mlstm_n, state_mlstm_m, state_hgrn, norm_w, w_in, ssd_conv_w, ssd_conv_b, ssd_dt_bias, ssd_A_log, ssd_D, ssd_norm_w, s5_A_re, s5_A_im, s5_B_re, s5_B_im, s5_C_re, s5_C_im, s5_D, s5_log_dt, s5_glu_w, ml_conv_w, ml_conv_b, ml_wq, ml_wk, ml_wv, ml_i_bias, ml_f_bias, ml_norm_w, ml_skip, hg_lb_logits, hg_norm_w, w_branch, w_out, final_norm_w):
    p = {'norm_w': norm_w, 'w_in': w_in,
         'ssd_conv_w': ssd_conv_w, 'ssd_conv_b': ssd_conv_b, 'ssd_dt_bias': ssd_dt_bias,
         'ssd_A_log': ssd_A_log, 'ssd_D': ssd_D, 'ssd_norm_w': ssd_norm_w,
         's5_A_re': s5_A_re, 's5_A_im': s5_A_im, 's5_B_re': s5_B_re, 's5_B_im': s5_B_im,
         's5_C_re': s5_C_re, 's5_C_im': s5_C_im, 's5_D': s5_D, 's5_log_dt': s5_log_dt, 's5_glu_w': s5_glu_w,
         'ml_conv_w': ml_conv_w, 'ml_conv_b': ml_conv_b, 'ml_wq': ml_wq, 'ml_wk': ml_wk, 'ml_wv': ml_wv,
         'ml_i_bias': ml_i_bias, 'ml_f_bias': ml_f_bias, 'ml_norm_w': ml_norm_w, 'ml_skip': ml_skip,
         'hg_norm_w': hg_norm_w, 'w_branch': w_branch, 'w_out': w_out}
    lb_cum = jnp.cumsum(jax.nn.softmax(hg_lb_logits, axis=0), axis=0)
    hg_lb = lb_cum - lb_cum[0]
    layers = [_layer_params(l, p, hg_lb) for l in range(DEPTH)]
    fnw = final_norm_w.reshape(1, D_MODEL)

    sample_states = (state_ssd_conv, state_ssd, state_s5_re, state_s5_im, state_mlstm_conv,
                     state_mlstm_C, state_mlstm_n, state_mlstm_m, state_hgrn)
    bp, lp_, _ = x_prompt.shape
    prompt_states = tuple(jnp.zeros((DEPTH, bp) + s.shape[2:], F32) for s in sample_states)
    q_prompt = math.gcd(lp_, 128)
    y_prompt, new_p = _run_group(x_prompt, prompt_states, layers, fnw, q=q_prompt, valid=q_prompt)

    ls = x_sample.shape[1]
    xs_pad = jnp.pad(x_sample, ((0, 0), (0, SAMPLE_PAD - ls), (0, 0)))
    y_sample, new_s = _run_group(xs_pad, sample_states, layers, fnw, q=SAMPLE_PAD, valid=ls)
    y_sample = y_sample[:, :ls]

    out = [y_prompt, y_sample]
    for ps, ss in zip(new_p, new_s):
        out += [ps, ss]
    return tuple(out)
```

```python
import functools
import math
from typing import Callable, NamedTuple

import numpy as np
import jax
import jax.numpy as jnp
from jax import lax
from jax.experimental import pallas as pl
from jax.experimental.pallas import tpu as pltpu

F32 = jnp.float32
BF16 = jnp.bfloat16

D_MODEL = 1024
DEPTH = 2
BRANCH_W = 512
CONV_W = 4
EPS = 1e-6
SSD_HEADS = 8
SSD_HEADDIM = 64
SSD_STATE = 64
SSD_GROUPS = 2
SSD_CONV_CH = 768
S5_GROUPS = 32
S5_GROUP = 16
S5_STATE = 64
S5_LANES = S5_GROUPS * S5_STATE
ML_HEADS = 4
ML_HEADDIM = 128
HG_HEADS = 4
HG_HEADDIM = 128
N_BRANCH = 4

NEG = -1e30

VMEM_LIMIT_BYTES = 48 * 1024 * 1024
INPROJ_ROW_TILE = 1024
INPROJ_COL_STEPS = 4
MERGE_ROW_TILE = 512

_IN_SPLITS = (('a_z', 512), ('a_x', 512), ('a_bc', 256), ('a_dt', 8), ('b_u', 512), ('b_gate', 512),
              ('c_x', 512), ('c_z', 512), ('c_o', 512), ('c_i', 4), ('c_f', 4),
              ('d_f', 512), ('d_i', 512), ('d_q', 512), ('d_g', 512), ('merge', 4096))

N_PROJ = 9216
COL_MERGE = 0
COL_A_Z = 4096
COL_A_X = 4608
COL_B_U = 5120
COL_B_GATE = 5632
COL_C_X = 6144
COL_C_Z = 6656
COL_C_O = 7168
COL_D_I = 7680
COL_D_Q = 8192
COL_D_G = 8704
N_PROJ32 = 896
COL32_D_F = 0
COL32_A_BC = 512
COL32_SMALL = 768
N_SMALL = 16
SMALL_DT = 0
SMALL_I = 8
SMALL_F = 12

NN = (((1,), (0,)), ((), ()))
NT = (((1,), (1,)), ((), ()))
TN = (((0,), (0,)), ((), ()))


def _mm(a, b, lowp, dims=NN):
    if lowp:
        a = a.astype(BF16)
        b = b.astype(BF16)
    return lax.dot_general(a, b, dims, preferred_element_type=F32)


def _split3(a):
    hi = a.astype(BF16)
    r = a - hi.astype(F32)
    mid = r.astype(BF16)
    lo = (r - mid.astype(F32)).astype(BF16)
    return hi, mid, lo


def _xdot(a, b, dims, split, lowp):
    if not lowp and min((b if split == 0 else a).shape) < 8:
        return lax.dot_general(a, b, dims, precision=lax.Precision.HIGHEST,
                               preferred_element_type=F32)
    if split == 0:
        other = b.astype(BF16)
        return sum(lax.dot_general(p, other, dims, preferred_element_type=F32) for p in _split3(a))
    other = a.astype(BF16)
    return sum(lax.dot_general(other, p, dims, preferred_element_type=F32) for p in _split3(b))


def _cumsum_rows(tl, x, lowp):
    q = x.shape[0]
    if q != 8:
        return _xdot(tl, x, NN, 1, lowp)
    r = lax.broadcasted_iota(jnp.int32, (q, 1), 0)
    for d in (1, 2, 4):
        x = x + jnp.where(r >= d, pltpu.roll(x, d, 0), 0.0)
    return x


def _sigmoid(x):
    return 0.5 * jnp.tanh(0.5 * x) + 0.5


def _silu(x):
    h = 0.5 * x
    return h * jnp.tanh(h) + h


def _softplus(x):
    return jnp.maximum(x, 0.0) + jnp.log1p(jnp.exp(-jnp.abs(x)))


CONV_PAD = 8


def _conv_chunk(xp_scr, cw_ref, cb_ref, q, valid):
    cw = cw_ref[...]
    p = CONV_PAD
    xp = xp_scr[...]
    conv = (cb_ref[...]
            + cw[0:1] * pltpu.roll(xp, 3, 0)[p:p + q, :]
            + cw[1:2] * pltpu.roll(xp, 2, 0)[p:p + q, :]
            + cw[2:3] * pltpu.roll(xp, 1, 0)[p:p + q, :]
            + cw[3:4] * xp[p:p + q, :])
    carry = xp_scr[p - 3 + valid:p + valid, :]
    xp_scr[p - 3:p, :] = carry
    return conv, carry


def _inproj_kernel(x_ref, nw_ref, w_ref, w32_ref, wst_ref, o_ref, o32_ref, st_ref, xn_scr):
    @pl.when(pl.program_id(1) == 0)
    def _():
        x = x_ref[...]
        xn = x * lax.rsqrt(jnp.mean(x * x, -1, keepdims=True) + EPS) * nw_ref[...]
        xb = xn.astype(BF16)
        xn_scr[...] = xb
        o32_ref[...] = lax.dot_general(xb, w32_ref[...], NT, preferred_element_type=F32)
        st_ref[...] = lax.dot_general(wst_ref[...], xb, NT, preferred_element_type=F32)

    o_ref[...] = lax.dot_general(xn_scr[...], w_ref[...], NT, preferred_element_type=F32).astype(BF16)


def _inproj(x, nw, w, w32, wst):
    t = x.shape[0]
    tm = min(INPROJ_ROW_TILE, t)
    tn = N_PROJ // INPROJ_COL_STEPS
    return pl.pallas_call(
        _inproj_kernel,
        grid=(t // tm, N_PROJ // tn),
        in_specs=[
            pl.BlockSpec((tm, D_MODEL), lambda i, j: (i, 0)),
            pl.BlockSpec((1, D_MODEL), lambda i, j: (0, 0)),
            pl.BlockSpec((tn, D_MODEL), lambda i, j: (j, 0)),
            pl.BlockSpec((N_PROJ32, D_MODEL), lambda i, j: (0, 0)),
            pl.BlockSpec((N_SMALL, D_MODEL), lambda i, j: (0, 0)),
        ],
        out_specs=[
            pl.BlockSpec((tm, tn), lambda i, j: (i, j)),
            pl.BlockSpec((tm, N_PROJ32), lambda i, j: (i, 0)),
            pl.BlockSpec((N_SMALL, tm), lambda i, j: (0, i)),
        ],
        out_shape=[
            jax.ShapeDtypeStruct((t, N_PROJ), BF16),
            jax.ShapeDtypeStruct((t, N_PROJ32), F32),
            jax.ShapeDtypeStruct((N_SMALL, t), F32),
        ],
        scratch_shapes=[pltpu.VMEM((tm, D_MODEL), BF16)],
        compiler_params=pltpu.CompilerParams(
            dimension_semantics=("arbitrary", "arbitrary"),
            vmem_limit_bytes=VMEM_LIMIT_BYTES),
        name="inproj",
    )(x, nw, w, w32, wst)


def _merge_kernel(g_ref, ya_ref, yb_ref, yc_ref, yd_ref, x_ref, wb_ref, wo_ref, fnw_ref, o_ref, *, final):
    acc = None
    for i, y_ref in enumerate((ya_ref, yb_ref, yc_ref, yd_ref)):
        gate = _sigmoid(g_ref[:, i * D_MODEL:(i + 1) * D_MODEL].astype(F32))
        term = gate * jnp.dot(y_ref[...], wb_ref[i], preferred_element_type=F32)
        acc = term if acc is None else acc + term
    out = x_ref[...] + jnp.dot(acc.astype(BF16), wo_ref[...], preferred_element_type=F32)
    if final:
        out = out * lax.rsqrt(jnp.mean(out * out, -1, keepdims=True) + EPS) * fnw_ref[...]
    o_ref[...] = out


def _merge(proj, ys, x, wb, wo, fnw, final):
    t = x.shape[0]
    tm = min(MERGE_ROW_TILE, t)
    yspec = pl.BlockSpec((tm, BRANCH_W), lambda i: (i, 0))
    return pl.pallas_call(
        functools.partial(_merge_kernel, final=final),
        grid=(t // tm,),
        in_specs=[
            pl.BlockSpec((tm, N_BRANCH * D_MODEL), lambda i: (i, COL_MERGE // (N_BRANCH * D_MODEL))),
            yspec, yspec, yspec, yspec,
            pl.BlockSpec((tm, D_MODEL), lambda i: (i, 0)),
            pl.BlockSpec((N_BRANCH, BRANCH_W, D_MODEL), lambda i: (0, 0, 0)),
            pl.BlockSpec((D_MODEL, D_MODEL), lambda i: (0, 0)),
            pl.BlockSpec((1, D_MODEL), lambda i: (0, 0)),
        ],
        out_specs=pl.BlockSpec((tm, D_MODEL), lambda i: (i, 0)),
        out_shape=jax.ShapeDtypeStruct((t, D_MODEL), F32),
        compiler_params=pltpu.CompilerParams(
            dimension_semantics=("arbitrary",),
            vmem_limit_bytes=VMEM_LIMIT_BYTES),
        name="merge",
    )(proj, *ys, x, wb, wo, fnw)


def _col_spec(bs, q, col, width):
    return pl.BlockSpec((bs, q, width), lambda b, c: (b, c, col // width))


def _row_spec(bs, q):
    return pl.BlockSpec((bs, N_SMALL, q), lambda b, c: (b, 0, c))


def _const_spec(shape):
    nd = len(shape)
    return pl.BlockSpec(shape, lambda b, c: (0,) * nd)


def _state_in_spec(layer, bs, tail):
    nd = len(tail)
    return pl.BlockSpec((1, bs) + tail, lambda b, c: (layer, b) + (0,) * nd)


def _state_out_spec(layer, bs, tail):
    nd = len(tail)
    return pl.BlockSpec((1, bs) + tail, lambda b, c: (layer, b) + (0,) * nd)


def _y_spec(bs, q):
    return pl.BlockSpec((bs, q, BRANCH_W), lambda b, c: (b, c, 0))


class _Mixer(NamedTuple):
    in_specs: list
    args: list
    out_specs: list
    out_shapes: list
    scratch: list
    group: Callable


_DONE = object()
_MIXER_PARAMS = pltpu.CompilerParams(
    dimension_semantics=("arbitrary", "arbitrary"),
    vmem_limit_bytes=VMEM_LIMIT_BYTES)


def _launch(mixers, prevs, *, grid, bs, nc, name):
    n_in = [len(m.in_specs) for m in mixers]
    n_out = [len(m.out_specs) for m in mixers]
    n_scr = [len(m.scratch) for m in mixers]
    tot_in, tot_out = sum(n_in), sum(n_out)
    state_idx = [sum(n_out[:i]) + k for i in range(len(mixers)) for k in range(1, n_out[i])]
    n_alias = 0 if prevs is None else len(state_idx)

    def kern(*refs):
        ins, outs, scr = refs[:tot_in], refs[tot_in + n_alias:tot_in + n_alias + tot_out], refs[tot_in + n_alias + tot_out:]
        groups = []
        for i, m in enumerate(mixers):
            groups.append(m.group(ins[sum(n_in[:i]):sum(n_in[:i + 1])],
                                  outs[sum(n_out[:i]):sum(n_out[:i + 1])],
                                  scr[sum(n_scr[:i]):sum(n_scr[:i + 1])]))
        c = pl.program_id(1)
        views = [[[r.at[s] for r in g[3]] for s in range(bs)] for g in groups]

        @pl.when(c == 0)
        def _():
            for g, v in zip(groups, views):
                for s in range(bs):
                    g[0](v[s])

        active = [g[1](v[s], g[4]) for g, v in zip(groups, views) for s in range(bs)]
        while active:
            active = [gen for gen in active if next(gen, _DONE) is not _DONE]

        @pl.when(c == nc - 1)
        def _():
            for g, v in zip(groups, views):
                for s in range(bs):
                    g[2](v[s])

    in_specs = [sp for m in mixers for sp in m.in_specs] + [pl.BlockSpec(memory_space=pl.ANY)] * n_alias
    args = [a for m in mixers for a in m.args]
    if prevs is not None:
        args += [a for p in prevs for a in p]
    outs = pl.pallas_call(
        kern,
        grid=grid,
        in_specs=in_specs,
        out_specs=[sp for m in mixers for sp in m.out_specs],
        out_shape=[sh for m in mixers for sh in m.out_shapes],
        scratch_shapes=[sc for m in mixers for sc in m.scratch],
        input_output_aliases={tot_in + k: idx for k, idx in enumerate(state_idx)} if n_alias else {},
        compiler_params=_MIXER_PARAMS,
        name=name,
    )(*args)
    return [outs[sum(n_out[:i]):sum(n_out[:i + 1])] for i in range(len(mixers))]


def _ssd_group(ins, outs, scr, *, q, valid, lowp):
    z_ref, x_ref, bc_ref, sc_ref, sr_ref, conv0_ref, h0_ref, *consts = ins
    y_ref, convo_ref, ho_ref = outs
    xp_scr, h_scr, y_scr = scr

    def init(v):
        v[9][CONV_PAD - 3:CONV_PAD, :] = v[5][...]
        v[10][...] = v[6][...]

    def final(v):
        v[8][...] = v[10][...]

    return (init, functools.partial(_ssd_main, q=q, valid=valid, lowp=lowp), final,
            (z_ref, x_ref, bc_ref, sc_ref, sr_ref, conv0_ref.at[0], h0_ref.at[0],
             y_ref, ho_ref.at[0], xp_scr, h_scr, y_scr, convo_ref.at[0]),
            tuple(consts))


def _ssd_main(v, k, *, q, valid, lowp):
    z_ref, x_ref, bc_ref, sc_ref, sr_ref, _, _, y_ref, _, xp_scr, h_scr, y_scr, convo_ref = v
    cw_ref, cb_ref, bcol_ref, brow_ref, alc_ref, alr_ref, dexp_ref, nw_ref, e_ref, tl_ref, tu_ref = k
    xp_scr[CONV_PAD:CONV_PAD + q, 0:BRANCH_W] = x_ref[...].astype(F32)
    xp_scr[CONV_PAD:CONV_PAD + q, BRANCH_W:SSD_CONV_CH] = bc_ref[...]
    conv, carry = _conv_chunk(xp_scr, cw_ref, cb_ref, q, valid)
    convo_ref[...] = carry
    yield
    xbc = _silu(conv)
    xs = xbc[:, 0:BRANCH_W]
    bm = xbc[:, BRANCH_W:BRANCH_W + 128]
    cm = xbc[:, BRANCH_W + 128:BRANCH_W + 256]

    dt_c = _softplus(sc_ref[...] + bcol_ref[...])
    a_c = dt_c * (-jnp.exp(alc_ref[...]))
    dt_r = _softplus(sr_ref[...] + brow_ref[...])
    a_r = dt_r * (-jnp.exp(alr_ref[...]))
    if valid < q:
        tcol = lax.broadcasted_iota(jnp.int32, (q, 1), 0)
        trow = lax.broadcasted_iota(jnp.int32, (1, q), 1)
        a_c = jnp.where(tcol < valid, a_c, 0.0)
        dt_c = jnp.where(tcol < valid, dt_c, 0.0)
        a_r = jnp.where(trow < valid, a_r, 0.0)

    tl = tl_ref[...]
    acum_c = _cumsum_rows(tl, a_c, lowp)
    acum_r = _xdot(a_r, tu_ref[...], NN, 0, lowp)
    e = e_ref[...]
    dt_e = _xdot(dt_c, e, NN, 0, lowp)
    acum_e = _xdot(acum_c, e, NN, 0, lowp)
    xdt = xs * dt_e
    xend = xdt * jnp.exp(acum_e[q - 1:q, :] - acum_e)
    eac = jnp.exp(acum_e)
    dec_last = jnp.exp(acum_c[q - 1:q, :])
    yield

    fine = q < 16
    tri = (lax.broadcasted_iota(jnp.int32, (q, q), 0) >= lax.broadcasted_iota(jnp.int32, (q, q), 1))
    rpg = SSD_HEADS // SSD_GROUPS
    for g in range(SSD_GROUPS):
        bg = bm[:, g * SSD_STATE:(g + 1) * SSD_STATE]
        cg = cm[:, g * SSD_STATE:(g + 1) * SSD_STATE]
        cb = _mm(cg, bg, lowp, NT)
        for r in range(rpg):
            h = g * rpg + r
            hs = slice(h * SSD_HEADDIM, (h + 1) * SSD_HEADDIM)
            seg = acum_c[:, h:h + 1] - acum_r[h:h + 1, :]
            m = cb * jnp.exp(jnp.where(tri, seg, NEG))
            hp = h_scr[h]
            yh = _mm(m, xdt[:, hs], lowp) + _mm(cg, hp, lowp, NT) * eac[:, hs]
            y_scr[:, hs] = yh
            if fine:
                yield
            h_scr[h] = hp * dec_last[:, h:h + 1] + _mm(xend[:, hs], bg, lowp, TN)
            yield

    y = y_scr[...] + xs * dexp_ref[...]
    y = y * _silu(z_ref[...].astype(F32))
    y = y * lax.rsqrt(jnp.mean(y * y, -1, keepdims=True) + EPS) * nw_ref[...]
    y_ref[...] = y.astype(BF16)


def _ssd(proj, proj32, srow, conv_state, h_state, layer, w, consts, *, b, q, nc, valid, lowp, bs):
    return _Mixer(
        in_specs=[
            _col_spec(bs, q, COL_A_Z, BRANCH_W),
            _col_spec(bs, q, COL_A_X, BRANCH_W),
            _col_spec(bs, q, COL32_A_BC, 256),
            _col_spec(bs, q, COL32_SMALL, 128),
            _row_spec(bs, q),
            _state_in_spec(layer, bs, (CONV_W - 1, SSD_CONV_CH)),
            _state_in_spec(layer, bs, (SSD_HEADS, SSD_HEADDIM, SSD_STATE)),
            _const_spec((CONV_W, SSD_CONV_CH)),
            _const_spec((1, SSD_CONV_CH)),
            _const_spec((1, 128)),
            _const_spec((N_SMALL, 1)),
            _const_spec((1, 128)),
            _const_spec((N_SMALL, 1)),
            _const_spec((1, BRANCH_W)),
            _const_spec((1, BRANCH_W)),
            _const_spec((128, BRANCH_W)),
            _const_spec((q, q)),
            _const_spec((q, q)),
        ],
        out_specs=[
            _y_spec(bs, q),
            _state_out_spec(layer, bs, (CONV_W - 1, SSD_CONV_CH)),
            _state_out_spec(layer, bs, (SSD_HEADS, SSD_HEADDIM, SSD_STATE)),
        ],
        out_shapes=[
            jax.ShapeDtypeStruct((b, nc * q, BRANCH_W), BF16),
            jax.ShapeDtypeStruct((DEPTH, b, CONV_W - 1, SSD_CONV_CH), F32),
            jax.ShapeDtypeStruct((DEPTH, b, SSD_HEADS, SSD_HEADDIM, SSD_STATE), F32),
        ],
        scratch=[
            pltpu.VMEM((bs, q + CONV_PAD, SSD_CONV_CH), F32),
            pltpu.VMEM((bs, SSD_HEADS, SSD_HEADDIM, SSD_STATE), F32),
            pltpu.VMEM((bs, q, BRANCH_W), F32),
        ],
        args=[proj, proj, proj32, proj32, srow, conv_state, h_state,
              w['ssd_conv_w'], w['ssd_conv_b'], w['bias_col'], w['bias_row'], w['alog_col'], w['alog_row'],
              w['ssd_D_exp'], w['ssd_norm_w'], consts['expand'], consts['tril'], consts['triu']],
        group=functools.partial(_ssd_group, q=q, valid=valid, lowp=lowp))


def _s5_kernel(u_ref, gate_ref, hr0_ref, hi0_ref, perm_ref, permt_ref, ar_ref, ai_ref,
               wbr_ref, wbi_ref, wcr_ref, wci_ref, d_ref, glu_ref,
               y_ref, hro_ref, hio_ref,
               hr_scr, hi_scr, cr_scr, ci_scr, *, q, nc, ngrp):
    c = pl.program_id(1)

    @pl.when(c == 0)
    def _():
        cr_scr[...] = hr0_ref[0]
        ci_scr[...] = hi0_ref[0]

    rows_in = perm_ref.shape[1]
    perm = perm_ref[...]
    u_tm = jnp.dot(perm, u_ref[...].reshape(rows_in, BRANCH_W), preferred_element_type=F32).astype(BF16)
    gate_tm = jnp.dot(perm, gate_ref[...].reshape(rows_in, BRANCH_W), preferred_element_type=F32)
    nblk = S5_LANES // 512
    for j in range(nblk):
        uj = u_tm[:, j * 128:(j + 1) * 128]
        hr_scr[:, j * 512:(j + 1) * 512] = jnp.dot(uj, wbr_ref[j], preferred_element_type=F32)
        hi_scr[:, j * 512:(j + 1) * 512] = jnp.dot(uj, wbi_ref[j], preferred_element_type=F32)

    half = S5_LANES // 2
    for hf in range(2):
        sl = slice(hf * half, (hf + 1) * half)
        ar = ar_ref[:, sl]
        ai = ai_ref[:, sl]

        def grp_body(g, carry, sl=sl, ar=ar, ai=ai):
            s0 = pl.multiple_of(g * 8, 8)

            def t_body(t, h):
                r0 = pl.multiple_of(g * (8 * q) + t * 8, 8)
                hr, hi = h
                nr = ar * hr - ai * hi + hr_scr[pl.ds(r0, 8), sl]
                ni = ar * hi + ai * hr + hi_scr[pl.ds(r0, 8), sl]
                hr_scr[pl.ds(r0, 8), sl] = nr
                hi_scr[pl.ds(r0, 8), sl] = ni
                return nr, ni

            h = lax.fori_loop(0, q, t_body, (cr_scr[pl.ds(s0, 8), sl], ci_scr[pl.ds(s0, 8), sl]),
                              unroll=min(q, 4))
            cr_scr[pl.ds(s0, 8), sl] = h[0]
            ci_scr[pl.ds(s0, 8), sl] = h[1]
            return carry

        lax.fori_loop(0, ngrp, grp_body, 0)

    parts = []
    for j in range(nblk):
        sl = slice(j * 512, (j + 1) * 512)
        parts.append(jnp.dot(hr_scr[:, sl].astype(BF16), wcr_ref[j], preferred_element_type=F32)
                     - jnp.dot(hi_scr[:, sl].astype(BF16), wci_ref[j], preferred_element_type=F32))
    y = jnp.concatenate(parts, axis=1) + u_tm.astype(F32) * d_ref[...]
    g = jax.nn.gelu(y)
    y = g * _sigmoid(jnp.dot(g.astype(BF16), glu_ref[...], preferred_element_type=F32))
    y = (y * _silu(gate_tm)).astype(BF16)
    y = jnp.dot(permt_ref[...], y, preferred_element_type=F32).astype(BF16)
    y_ref[...] = y.reshape(y_ref.shape)

    @pl.when(c == nc - 1)
    def _():
        hro_ref[0] = cr_scr[...]
        hio_ref[0] = ci_scr[...]


def _s5_perm(bb, lp, valid):
    perm = np.zeros((bb * valid, bb * lp), np.float32)
    for g in range(bb // 8):
        for t in range(valid):
            for s in range(8):
                perm[g * 8 * valid + t * 8 + s, (g * 8 + s) * lp + t] = 1.0
    return jnp.asarray(perm, BF16), jnp.asarray(perm.T, BF16)


def _s5(proj, hr_state, hi_state, layer, w, prev, *, b, lp, q, valid, bb):
    nc = lp // q
    perm, permt = _s5_perm(bb, q, valid)
    rows = bb * valid
    if nc == 1:
        def col_spec(col):
            return pl.BlockSpec((bb * q, BRANCH_W), lambda i, c: (i, col // BRANCH_W))
        src = proj
        y_spec = pl.BlockSpec((bb * q, BRANCH_W), lambda i, c: (i, 0))
        y_shape = jax.ShapeDtypeStruct((b * lp, BRANCH_W), BF16)
    else:
        def col_spec(col):
            return pl.BlockSpec((bb, q, BRANCH_W), lambda i, c: (i, c, col // BRANCH_W))
        src = proj.reshape(b, lp, N_PROJ)
        y_spec = pl.BlockSpec((bb, q, BRANCH_W), lambda i, c: (i, c, 0))
        y_shape = jax.ShapeDtypeStruct((b, lp, BRANCH_W), BF16)
    st_in = pl.BlockSpec((1, bb, S5_LANES), lambda i, c: (layer, i, 0))
    st_out = st_in
    kern = functools.partial(_s5_kernel, q=valid, nc=nc, ngrp=bb // 8)
    n_in = 14
    alias_kw = {}
    if prev is not None:
        base = kern

        def kern(*refs):
            return base(*refs[:n_in], *refs[n_in + 2:])

        alias_kw = dict(input_output_aliases={n_in: 1, n_in + 1: 2})
    y, hr, hi = pl.pallas_call(
        kern,
        grid=(b // bb, nc),
        **alias_kw,
        in_specs=[
            col_spec(COL_B_U),
            col_spec(COL_B_GATE),
            st_in, st_in,
            _const_spec(perm.shape),
            _const_spec(permt.shape),
            _const_spec((8, S5_LANES)),
            _const_spec((8, S5_LANES)),
            _const_spec((4, 128, 512)),
            _const_spec((4, 128, 512)),
            _const_spec((4, 512, 128)),
            _const_spec((4, 512, 128)),
            _const_spec((1, BRANCH_W)),
            _const_spec((BRANCH_W, BRANCH_W)),
        ] + [pl.BlockSpec(memory_space=pl.ANY)] * (0 if prev is None else 2),
        out_specs=[y_spec, st_out, st_out],
        out_shape=[
            y_shape,
            jax.ShapeDtypeStruct((DEPTH, b, S5_LANES), F32),
            jax.ShapeDtypeStruct((DEPTH, b, S5_LANES), F32),
        ],
        scratch_shapes=[
            pltpu.VMEM((rows, S5_LANES), F32),
            pltpu.VMEM((rows, S5_LANES), F32),
            pltpu.VMEM((bb, S5_LANES), F32),
            pltpu.VMEM((bb, S5_LANES), F32),
        ],
        compiler_params=_MIXER_PARAMS,
        name="s5",
    )(src, src, hr_state, hi_state, perm, permt, w['s5_ar'], w['s5_ai'],
      w['s5_wbr'], w['s5_wbi'], w['s5_wcr'], w['s5_wci'], w['s5_D'], w['s5_glu_w'], *(prev or ()))
    return y.reshape(b * lp, BRANCH_W), hr, hi


def _mlstm_group(ins, outs, scr, *, q, valid, lowp):
    x_ref, z_ref, o_ref, sc_ref, sr_ref, conv0_ref, c0_ref, n0_ref, m0_ref, *consts = ins
    y_ref, convo_ref, co_ref, no_ref, mo_ref = outs
    xp_scr, c_scr, n_scr, m_scr, h_scr = scr

    def init(v):
        v[13][CONV_PAD - 3:CONV_PAD, :] = v[5][...]
        v[14][...] = v[6][...]
        v[15][...] = v[7][...]
        v[16][...] = v[8][...]

    def final(v):
        v[10][...] = v[14][...]
        v[11][...] = v[15][...]
        v[12][...] = v[16][...]

    return (init, functools.partial(_mlstm_main, q=q, valid=valid, lowp=lowp), final,
            (x_ref, z_ref, o_ref, sc_ref, sr_ref, conv0_ref.at[0], c0_ref.at[0], n0_ref.at[0], m0_ref.at[0],
             y_ref, co_ref.at[0], no_ref.at[0], mo_ref.at[0], xp_scr, c_scr, n_scr, m_scr, h_scr,
             convo_ref.at[0]),
            tuple(consts))


def _mlstm_main(v, k, *, q, valid, lowp):
    (x_ref, z_ref, o_ref, sc_ref, sr_ref, _, _, _, _, y_ref, _, _, _,
     xp_scr, c_scr, n_scr, m_scr, h_scr, convo_ref) = v
    cw_ref, cb_ref, wq_ref, wk_ref, wv_ref, bcol_ref, brow_ref, nw_ref, skip_ref, tl_ref, tu_ref = k
    x = x_ref[...].astype(F32)
    xp_scr[CONV_PAD:CONV_PAD + q, :] = x
    conv, carry = _conv_chunk(xp_scr, cw_ref, cb_ref, q, valid)
    convo_ref[...] = carry
    yield
    xc = _silu(conv)

    pre_c = sc_ref[...] + bcol_ref[...]
    pre_r = sr_ref[...] + brow_ref[...]
    ig_c = pre_c
    lf_c = -_softplus(-pre_c)
    ig_r = pre_r
    lf_r = -_softplus(-pre_r)
    if valid < q:
        tcol = lax.broadcasted_iota(jnp.int32, (q, 1), 0)
        trow = lax.broadcasted_iota(jnp.int32, (1, q), 1)
        lf_c = jnp.where(tcol < valid, lf_c, 0.0)
        ig_c = jnp.where(tcol < valid, ig_c, NEG)
        lf_r = jnp.where(trow < valid, lf_r, 0.0)
        ig_r = jnp.where(trow < valid, ig_r, NEG)
    b_c = _cumsum_rows(tl_ref[...], lf_c, lowp)
    b_r = _xdot(lf_r, tu_ref[...], NN, 0, lowp)
    yield

    fine = q < 16
    tri = (lax.broadcasted_iota(jnp.int32, (q, q), 0) >= lax.broadcasted_iota(jnp.int32, (q, q), 1))
    o_all = o_ref[...].astype(F32)
    for h in range(ML_HEADS):
        hs = slice(h * ML_HEADDIM, (h + 1) * ML_HEADDIM)
        xh = xc[:, hs]
        qh = _mm(xh, wq_ref[h], lowp)
        kh = _mm(xh, wk_ref[h], lowp) * (ML_HEADDIM ** -0.5)
        vh = _mm(x[:, hs], wv_ref[h], lowp)
        yield
        bc = b_c[:, SMALL_F + h:SMALL_F + h + 1]
        br = b_r[SMALL_F + h:SMALL_F + h + 1, :]
        ic = ig_c[:, SMALL_I + h:SMALL_I + h + 1]
        ir = ig_r[SMALL_I + h:SMALL_I + h + 1, :]
        mp = m_scr[h:h + 1, 0:1]
        dlog = jnp.where(tri, bc - br + ir, NEG)
        inter = bc + mp
        m_t = jnp.maximum(inter, jnp.max(dlog, axis=1, keepdims=True))
        wgt = jnp.exp(dlog - m_t)
        yield
        s = _mm(qh, kh, lowp, NT) * wgt
        if fine:
            yield
        scale = jnp.exp(inter - m_t)
        cp = c_scr[h]
        npv = n_scr[h:h + 1, :]
        num = _mm(s, vh, lowp) + scale * _mm(qh, cp, lowp)
        dot =jnp.sum(s, axis=1, keepdims=True) + scale * jnp.sum(qh * npv, axis=1, keepdims=True)
        hh = num / jnp.maximum(jnp.abs(dot), jnp.exp(-m_t))
        yield
        m_new = m_t[q - 1:q, :]
        b_last = bc[q - 1:q, :]
        w_end = jnp.exp(b_last - bc + ic - m_new)
        cs = jnp.exp(b_last + mp - m_new)
        if fine:
            yield
        kw = kh * w_end
        c_scr[h] = cs * cp + _mm(kw, vh, lowp, TN)
        n_scr[h:h + 1, :] = cs * npv + jnp.sum(kw, axis=0, keepdims=True)
        m_scr[h:h + 1, :] = jnp.broadcast_to(m_new, (1, 128))
        yield
        hh = hh * _sigmoid(o_all[:, hs])
        hc = hh - jnp.mean(hh, -1, keepdims=True)
        h_scr[:, hs] = hc * lax.rsqrt(jnp.mean(hc * hc, -1, keepdims=True) + EPS)
        yield

    y = h_scr[...] * nw_ref[...] + skip_ref[...] * xc
    y_ref[...] = (y * _silu(z_ref[...].astype(F32))).astype(BF16)


def _mlstm(proj, proj32, srow, conv_state, c_state, n_state, m_state, layer, w, consts,
           *, b, q, nc, valid, lowp, bs):
    return _Mixer(
        in_specs=[
            _col_spec(bs, q, COL_C_X, BRANCH_W),
            _col_spec(bs, q, COL_C_Z, BRANCH_W),
            _col_spec(bs, q, COL_C_O, BRANCH_W),
            _col_spec(bs, q, COL32_SMALL, 128),
            _row_spec(bs, q),
            _state_in_spec(layer, bs, (CONV_W - 1, BRANCH_W)),
            _state_in_spec(layer, bs, (ML_HEADS, ML_HEADDIM, ML_HEADDIM)),
            _state_in_spec(layer, bs, (ML_HEADS, ML_HEADDIM)),
            _state_in_spec(layer, bs, (8, 128)),
            _const_spec((CONV_W, BRANCH_W)),
            _const_spec((1, BRANCH_W)),
            _const_spec((ML_HEADS, ML_HEADDIM, ML_HEADDIM)),
            _const_spec((ML_HEADS, ML_HEADDIM, ML_HEADDIM)),
            _const_spec((ML_HEADS, ML_HEADDIM, ML_HEADDIM)),
            _const_spec((1, 128)),
            _const_spec((N_SMALL, 1)),
            _const_spec((1, BRANCH_W)),
            _const_spec((1, BRANCH_W)),
            _const_spec((q, q)),
            _const_spec((q, q)),
        ],
        out_specs=[
            _y_spec(bs, q),
            _state_out_spec(layer, bs, (CONV_W - 1, BRANCH_W)),
            _state_out_spec(layer, bs, (ML_HEADS, ML_HEADDIM, ML_HEADDIM)),
            _state_out_spec(layer, bs, (ML_HEADS, ML_HEADDIM)),
            _state_out_spec(layer, bs, (8, 128)),
        ],
        out_shapes=[
            jax.ShapeDtypeStruct((b, nc * q, BRANCH_W), BF16),
            jax.ShapeDtypeStruct((DEPTH, b, CONV_W - 1, BRANCH_W), F32),
            jax.ShapeDtypeStruct((DEPTH, b, ML_HEADS, ML_HEADDIM, ML_HEADDIM), F32),
            jax.ShapeDtypeStruct((DEPTH, b, ML_HEADS, ML_HEADDIM), F32),
            jax.ShapeDtypeStruct((DEPTH, b, 8, 128), F32),
        ],
        scratch=[
            pltpu.VMEM((bs, q + CONV_PAD, BRANCH_W), F32),
            pltpu.VMEM((bs, ML_HEADS, ML_HEADDIM, ML_HEADDIM), F32),
            pltpu.VMEM((bs, ML_HEADS, ML_HEADDIM), F32),
            pltpu.VMEM((bs, 8, 128), F32),
            pltpu.VMEM((bs, q, BRANCH_W), F32),
        ],
        args=[proj, proj, proj, proj32, srow, conv_state, c_state, n_state, m_state,
              w['ml_conv_w'], w['ml_conv_b'], w['ml_wq'], w['ml_wk'], w['ml_wv'], w['bias_col'], w['bias_row'],
              w['ml_norm_w'], w['ml_skip'], consts['tril'], consts['triu']],
        group=functools.partial(_mlstm_group, q=q, valid=valid, lowp=lowp))


def _hgrn_group(ins, outs, scr, *, q, valid, lowp, nlev):
    f_ref, i_ref, q_ref, g_ref, s0_ref, *consts = ins
    y_ref, so_ref = outs
    s_scr, o_scr = scr

    def init(v):
        v[7][...] = v[4][...]

    def final(v):
        v[6][...] = v[7][...]

    return (init, functools.partial(_hgrn_main, q=q, valid=valid, lowp=lowp, nlev=nlev), final,
            (f_ref, i_ref, q_ref, g_ref, s0_ref.at[0], y_ref, so_ref.at[0], s_scr, o_scr),
            tuple(consts))


def _hgrn_main(v, k, *, q, valid, lowp, nlev):
    f_ref, i_ref, q_ref, g_ref, _, y_ref, _, s_scr, o_scr = v
    lb_ref, nw_ref, tl_ref = k
    lb = lb_ref[...]
    fg = lb + (1.0 - lb) * _sigmoid(f_ref[...])
    kk = 1.0 - fg
    qq = _silu(q_ref[...].astype(F32)) * (HG_HEADDIM ** -0.5)
    vv = i_ref[...] if lowp else i_ref[...].astype(F32)
    rcol =lax.broadcasted_iota(jnp.int32, (q, 1), 0)
    if valid < q:
        fg = jnp.where(rcol < valid, fg, 1.0)
        kk = jnp.where(rcol < valid, kk, 0.0)
    lf = jnp.log(fg)

    gcum = _cumsum_rows(tl_ref[...], lf, lowp)
    yield

    fine = q < 16
    rt = lax.broadcasted_iota(jnp.int32, (q, q), 0)
    cs = lax.broadcasted_iota(jnp.int32, (q, q), 1)
    rx = rt ^ cs
    att = [None] * HG_HEADS
    for lev in range(nlev):
        m = q >> (lev + 1)
        second = (rcol & m) != 0
        if m >= 4:
            nb = q // (2 * m)
            gb = jnp.broadcast_to(gcum.reshape(nb, 2 * m, BRANCH_W)[:, m - 1:m, :], (nb, 2 * m, BRANCH_W))
            e = jnp.exp(-jnp.abs(gcum - gb.reshape(q, BRANCH_W)))
        elif m == 2:
            r4 = rcol & 3
            e = jnp.where(r4 == 0, pltpu.roll(fg, q - 1, 0),
                          jnp.where(r4 == 1, 1.0, jnp.where(r4 == 2, fg, fg * pltpu.roll(fg, 1, 0))))
        else:
            e = jnp.where(second, fg, 1.0)
        u = jnp.where(second, qq, kk) * e
        if lowp:
            u = u.astype(BF16)
        pair = (rx >> int(math.log2(m))) == 1
        for h in range(HG_HEADS):
            hs = slice(h * HG_HEADDIM, (h + 1) * HG_HEADDIM)
            s = lax.dot_general(u[:, hs], u[:, hs], NT, preferred_element_type=F32)
            att[h] = jnp.where(pair, s, 0.0 if att[h] is None else att[h])
        yield

    qg = qq * jnp.exp(gcum)
    kend = kk * jnp.exp(gcum[q - 1:q, :] - gcum)
    ones = jnp.ones((q, 128), F32)
    for h in range(HG_HEADS):
        hs = slice(h * HG_HEADDIM, (h + 1) * HG_HEADDIM)
        diag = jnp.sum(qq[:, hs] * kk[:, hs], axis=1, keepdims=True)
        a = jnp.where(rt > cs, att[h], jnp.where(rt == cs, diag, 0.0))
        sp = s_scr[h]
        oh = _mm(a, vv[:, hs], lowp) + _mm(qg[:, hs], sp, lowp)
        if fine:
            yield
        dec = jnp.exp(_xdot(lf[:, hs], ones, TN, 0, lowp))
        s_scr[h] =sp * dec + _mm(kend[:, hs], vv[:, hs], lowp, TN)
        o_scr[:, hs] = oh * lax.rsqrt(jnp.mean(oh * oh, -1, keepdims=True) + EPS)
        yield

    y_ref[...] = (o_scr[...] * nw_ref[...] * _silu(g_ref[...].astype(F32))).astype(BF16)


def _hgrn(proj, proj32, s_state, layer, w, consts, *, b, q, nc, valid, lowp, bs):
    nlev = int(math.log2(q))
    return _Mixer(
        in_specs=[
            _col_spec(bs, q, COL32_D_F, BRANCH_W),
            _col_spec(bs, q, COL_D_I, BRANCH_W),
            _col_spec(bs, q, COL_D_Q, BRANCH_W),
            _col_spec(bs, q, COL_D_G, BRANCH_W),
            _state_in_spec(layer, bs, (HG_HEADS, HG_HEADDIM, HG_HEADDIM)),
            _const_spec((1, BRANCH_W)),
            _const_spec((1, BRANCH_W)),
            _const_spec((q, q)),
        ],
        out_specs=[
            _y_spec(bs, q),
            _state_out_spec(layer, bs, (HG_HEADS, HG_HEADDIM, HG_HEADDIM)),
        ],
        out_shapes=[
            jax.ShapeDtypeStruct((b, nc * q, BRANCH_W), BF16),
            jax.ShapeDtypeStruct((DEPTH, b, HG_HEADS, HG_HEADDIM, HG_HEADDIM), F32),
        ],
        scratch=[
            pltpu.VMEM((bs, HG_HEADS, HG_HEADDIM, HG_HEADDIM), F32),
            pltpu.VMEM((bs, q, BRANCH_W), F32),
        ],
        args=[proj32, proj, proj, proj, s_state, w['hg_lb'], w['hg_norm_w'], consts['tril']],
        group=functools.partial(_hgrn_group, q=q, valid=valid, lowp=lowp, nlev=nlev))


def _chunk_consts(q):
    r = np.arange(q)
    tril = (r[:, None] >= r[None, :]).astype(np.float32)
    expand = np.zeros((128, BRANCH_W), np.float32)
    for h in range(SSD_HEADS):
        expand[h, h * SSD_HEADDIM:(h + 1) * SSD_HEADDIM] = 1.0
    return {'tril': jnp.asarray(tril), 'triu': jnp.asarray(tril.T), 'expand': jnp.asarray(expand)}


def _cmul(ar, ai, br, bi):
    return ar * br - ai * bi, ar * bi + ai * br


def _layer_params(l, p, hg_lb):
    w_in_t = p['w_in'][l].T
    starts = np.cumsum([0] + [n for _, n in _IN_SPLITS])

    def rows(first, last):
        names = [name for name, _ in _IN_SPLITS]
        return w_in_t[starts[names.index(first)]:starts[names.index(last) + 1]]

    small = jnp.concatenate([rows('a_dt', 'a_dt'), rows('c_i', 'c_f')], axis=0)
    w_perm = jnp.concatenate([
        rows('merge', 'merge'), rows('a_z', 'a_x'), rows('b_u', 'b_gate'), rows('c_x', 'c_o'), rows('d_i', 'd_g'),
    ], axis=0).astype(BF16)
    w_perm32 = jnp.concatenate([
        rows('d_f', 'd_f'), rows('a_bc', 'a_bc'), small,
        jnp.zeros((N_PROJ32 - COL32_SMALL - N_SMALL, D_MODEL), F32),
    ], axis=0).astype(BF16)

    def lane_pad(v, off):
        return jnp.zeros((1, 128), F32).at[0, off:off + v.shape[0]].set(v)

    bias_col = (lane_pad(p['ssd_dt_bias'][l], SMALL_DT) + lane_pad(p['ml_i_bias'][l], SMALL_I)
                + lane_pad(p['ml_f_bias'][l], SMALL_F))
    alog_col = lane_pad(p['ssd_A_log'][l], SMALL_DT)

    dt = jnp.exp(p['s5_log_dt'][l])[:, None]
    lr = p['s5_A_re'][l]
    li = p['s5_A_im'][l]
    mag = jnp.exp(lr * dt)
    abr, abi = mag * jnp.cos(li * dt), mag * jnp.sin(li * dt)
    den = lr * lr + li * li
    cr = ((abr - 1.0) * lr + abi * li) / den
    ci = (abi * lr - (abr - 1.0) * li) / den
    bbr, bbi = _cmul(cr[..., None], ci[..., None], p['s5_B_re'][l], p['s5_B_im'][l])
    eye8 = jnp.eye(8, dtype=F32)

    def pack_b(bb):
        return jnp.einsum('jgpc,gh->jgchp', bb.reshape(4, 8, S5_STATE, S5_GROUP), eye8).reshape(4, 128, 512)

    def pack_c(cc):
        return jnp.einsum('jgcp,gh->jgphc', cc.reshape(4, 8, S5_GROUP, S5_STATE), eye8).reshape(4, 512, 128)

    return {
        'norm_w': p['norm_w'][l].reshape(1, D_MODEL),
        'w_in': w_perm, 'w_in32': w_perm32,
        'w_small_t': small.astype(BF16),
        'bias_col': bias_col, 'bias_row': bias_col[0, :N_SMALL].reshape(N_SMALL, 1),
        'alog_col': alog_col, 'alog_row': alog_col[0, :N_SMALL].reshape(N_SMALL, 1),
        'ssd_conv_w': p['ssd_conv_w'][l], 'ssd_conv_b': p['ssd_conv_b'][l].reshape(1, SSD_CONV_CH),
        'ssd_D_exp': jnp.repeat(p['ssd_D'][l], SSD_HEADDIM).reshape(1, BRANCH_W),
        'ssd_norm_w': p['ssd_norm_w'][l].reshape(1, BRANCH_W),
        's5_wbr': pack_b(bbr).astype(BF16), 's5_wbi': pack_b(bbi).astype(BF16),
        's5_wcr': pack_c(p['s5_C_re'][l]).astype(BF16), 's5_wci': pack_c(p['s5_C_im'][l]).astype(BF16),
        's5_ar': jnp.broadcast_to(abr.reshape(1, S5_LANES), (8, S5_LANES)),
        's5_ai': jnp.broadcast_to(abi.reshape(1, S5_LANES), (8, S5_LANES)),
        's5_D': p['s5_D'][l].reshape(1, BRANCH_W), 's5_glu_w': p['s5_glu_w'][l].astype(BF16),
        'ml_conv_w': p['ml_conv_w'][l], 'ml_conv_b': p['ml_conv_b'][l].reshape(1, BRANCH_W),
        'ml_wq': p['ml_wq'][l], 'ml_wk': p['ml_wk'][l], 'ml_wv': p['ml_wv'][l],
        'ml_norm_w': p['ml_norm_w'][l].reshape(1, BRANCH_W), 'ml_skip': p['ml_skip'][l].reshape(1, BRANCH_W),
        'hg_lb': hg_lb[l].reshape(1, BRANCH_W), 'hg_norm_w': p['hg_norm_w'][l].reshape(1, BRANCH_W),
        'w_branch': p['w_branch'][l].astype(BF16), 'w_out': p['w_out'][l].astype(BF16),
    }


_MATMUL_WEIGHTS = ('ml_wq', 'ml_wk', 'ml_wv')
S5_CHUNK = 64
MIXER_PLAN = [(('ssd', 'mlstm', 'hgrn'), 2)]
MIXER_PLAN_PADDED = [(('ssd', 'mlstm', 'hgrn'), 8)]


def _run_group(x3, states, layers, fnw, *, q, valid):
    b, lp, _ = x3.shape
    nc = lp // q
    t = b * lp
    lowp = q >= 16
    consts = _chunk_consts(q)
    ssd_conv, ssd_h, s5_re, s5_im, ml_conv, ml_c, ml_n, ml_m, hg_s = states
    s5_re = s5_re.reshape(DEPTH, b, S5_LANES)
    s5_im = s5_im.reshape(DEPTH, b, S5_LANES)
    if valid < q:
        s5_kw = dict(b=b, lp=lp, q=q, valid=valid, bb=min(b, 128))
    else:
        qs = math.gcd(lp, S5_CHUNK)
        s5_kw = dict(b=b, lp=lp, q=qs, valid=qs, bb=8)
    ml_m = jnp.broadcast_to(jnp.pad(ml_m, ((0, 0), (0, 0), (0, 8 - ML_HEADS)))[..., None], (DEPTH, b, 8, 128))
    plan = [(names, math.gcd(b, n)) for names, n in (MIXER_PLAN_PADDED if valid < q else MIXER_PLAN)]
    kw = dict(b=b, q=q, nc=nc, valid=valid, lowp=lowp)
    x = x3.reshape(t, D_MODEL)
    ys, st, st_s5 = {}, {}, None
    for l, w in enumerate(layers):
        if lowp:
            w = dict(w, **{k: w[k].astype(BF16) for k in _MATMUL_WEIGHTS})
        proj, proj32, small_t = _inproj(x, w['norm_w'], w['w_in'], w['w_in32'], w['w_small_t'])
        proj3 = proj.reshape(b, lp, N_PROJ)
        proj32 = proj32.reshape(b, lp, N_PROJ32)
        srow = small_t.reshape(N_SMALL, b, lp).transpose(1, 0, 2)
        mixers = {
            'ssd': lambda n: _ssd(proj3, proj32, srow, ssd_conv, ssd_h, l, w, consts, bs=n, **kw),
            'mlstm': lambda n: _mlstm(proj3, proj32, srow, ml_conv, ml_c, ml_n, ml_m, l, w, consts, bs=n, **kw),
            'hgrn': lambda n: _hgrn(proj3, proj32, hg_s, l, w, consts, bs=n, **kw),
        }
        for names, n in plan:
            outs = _launch([mixers[k](n) for k in names], None if l == 0 else [st[k] for k in names],
                           grid=(b // n, nc), bs=n, nc=nc, name='_'.join(names))
            for k, o in zip(names, outs):
                ys[k], st[k] = o[0], list(o[1:])
        yb, *st_s5 = _s5(proj, s5_re, s5_im, l, w, st_s5, **s5_kw)
        ya, yc, yd = (ys[k].reshape(t, BRANCH_W) for k in ('ssd', 'mlstm', 'hgrn'))
        x = _merge(proj, (ya, yb, yc, yd), x, w['w_branch'], w['w_out'], fnw, final=(l == DEPTH - 1))
    n_ssd_conv, n_ssd_h = st['ssd']
    n_re, n_im = (s.reshape(DEPTH, b, S5_GROUPS, S5_STATE) for s in st_s5)
    n_ml_conv, n_c, n_n, n_m = st['mlstm']
    new_states = (n_ssd_conv, n_ssd_h, n_re, n_im, n_ml_conv, n_c, n_n, n_m[:, :, :ML_HEADS, 0], st['hgrn'][0])
    return x.reshape(b, lp, D_MODEL), new_states


SAMPLE_PAD = 8


def kernel(x_prompt, x_sample, state_ssd_conv, state_ssd, state_s5_re, state_s5_im, state_mlstm_conv, state_mlstm_C, state_mlstm_n, state_mlstm_m, state_hgrn, norm_w, w_in, ssd_conv_w, ssd_conv_b, ssd_dt_bias, ssd_A_log, ssd_D, ssd_norm_w, s5_A_re, s5_A_im, s5_B_re, s5_B_im, s5_C_re, s5_C_im, s5_D, s5_log_dt, s5_glu_w, ml_conv_w, ml_conv_b, ml_wq, ml_wk, ml_wv, ml_i_bias, ml_f_bias, ml_norm_w, ml_skip, hg_lb_logits, hg_norm_w, w_branch, w_out, final_norm_w):
    p = {'norm_w': norm_w, 'w_in': w_in,
         'ssd_conv_w': ssd_conv_w, 'ssd_conv_b': ssd_conv_b, 'ssd_dt_bias': ssd_dt_bias,
         'ssd_A_log': ssd_A_log, 'ssd_D': ssd_D, 'ssd_norm_w': ssd_norm_w,
         's5_A_re': s5_A_re, 's5_A_im': s5_A_im, 's5_B_re': s5_B_re, 's5_B_im': s5_B_im,
         's5_C_re': s5_C_re, 's5_C_im': s5_C_im, 's5_D': s5_D, 's5_log_dt': s5_log_dt, 's5_glu_w': s5_glu_w,
         'ml_conv_w': ml_conv_w, 'ml_conv_b': ml_conv_b, 'ml_wq': ml_wq, 'ml_wk': ml_wk, 'ml_wv': ml_wv,
         'ml_i_bias': ml_i_bias, 'ml_f_bias': ml_f_bias, 'ml_norm_w': ml_norm_w, 'ml_skip': ml_skip,
         'hg_norm_w': hg_norm_w, 'w_branch': w_branch, 'w_out': w_out}
    lb_cum = jnp.cumsum(jax.nn.softmax(hg_lb_logits, axis=0), axis=0)
    hg_lb = lb_cum - lb_cum[0]
    layers = [_layer_params(l, p, hg_lb) for l in range(DEPTH)]
    fnw = final_norm_w.reshape(1, D_MODEL)

    sample_states = (state_ssd_conv, state_ssd, state_s5_re, state_s5_im, state_mlstm_conv,
                     state_mlstm_C, state_mlstm_n, state_mlstm_m, state_hgrn)
    bp, lp_, _ = x_prompt.shape
    prompt_states = tuple(jnp.zeros((DEPTH, bp) + s.shape[2:], F32) for s in sample_states)
    q_prompt = math.gcd(lp_, 128)
    y_prompt, new_p = _run_group(x_prompt, prompt_states, layers, fnw, q=q_prompt, valid=q_prompt)

    ls = x_sample.shape[1]
    xs_pad = jnp.pad(x_sample, ((0, 0), (0, SAMPLE_PAD - ls), (0, 0)))
    y_sample, new_s = _run_group(xs_pad, sample_states, layers, fnw, q=SAMPLE_PAD, valid=ls)
    y_sample = y_sample[:, :ls]

    out = [y_prompt, y_sample]
    for ps, ss in zip(new_p, new_s):
        out += [ps, ss]
    return tuple(out)
```

```python
import functools
import math
from typing import Callable, NamedTuple

import numpy as np
import jax
import jax.numpy as jnp
from jax import lax
from jax.experimental import pallas as pl
from jax.experimental.pallas import tpu as pltpu

F32 = jnp.float32
BF16 = jnp.bfloat16

D_MODEL = 1024
DEPTH = 2
BRANCH_W = 512
CONV_W = 4
EPS = 1e-6
SSD_HEADS = 8
SSD_HEADDIM = 64
SSD_STATE = 64
SSD_GROUPS = 2
SSD_CONV_CH = 768
S5_GROUPS = 32
S5_GROUP = 16
S5_STATE = 64
S5_LANES = S5_GROUPS * S5_STATE
ML_HEADS = 4
ML_HEADDIM = 128
HG_HEADS = 4
HG_HEADDIM = 128
N_BRANCH = 4

NEG = -1e30

VMEM_LIMIT_BYTES = 48 * 1024 * 1024
INPROJ_ROW_TILE = 1024
INPROJ_COL_STEPS = 4
MERGE_ROW_TILE = 512

_IN_SPLITS = (('a_z', 512), ('a_x', 512), ('a_bc', 256), ('a_dt', 8), ('b_u', 512), ('b_gate', 512),
              ('c_x', 512), ('c_z', 512), ('c_o', 512), ('c_i', 4), ('c_f', 4),
              ('d_f', 512), ('d_i', 512), ('d_q', 512), ('d_g', 512), ('merge', 4096))

N_PROJ = 9216
COL_MERGE = 0
COL_A_Z = 4096
COL_A_X = 4608
COL_B_U = 5120
COL_B_GATE = 5632
COL_C_X = 6144
COL_C_Z = 6656
COL_C_O = 7168
COL_D_I = 7680
COL_D_Q = 8192
COL_D_G = 8704
N_PROJ32 = 896
COL32_D_F = 0
COL32_A_BC = 512
COL32_SMALL = 768
N_SMALL = 16
SMALL_DT = 0
SMALL_I = 8
SMALL_F = 12

NN = (((1,), (0,)), ((), ()))
NT = (((1,), (1,)), ((), ()))
TN = (((0,), (0,)), ((), ()))


def _mm(a, b, lowp, dims=NN):
    if lowp:
        a = a.astype(BF16)
        b = b.astype(BF16)
    return lax.dot_general(a, b, dims, preferred_element_type=F32)


def _split3(a):
    hi = a.astype(BF16)
    r = a - hi.astype(F32)
    mid = r.astype(BF16)
    lo = (r - mid.astype(F32)).astype(BF16)
    return hi, mid, lo


def _xdot(a, b, dims, split, lowp):
    if not lowp and min((b if split == 0 else a).shape) < 8:
        return lax.dot_general(a, b, dims, precision=lax.Precision.HIGHEST,
                               preferred_element_type=F32)
    if split == 0:
        other = b.astype(BF16)
        return sum(lax.dot_general(p, other, dims, preferred_element_type=F32) for p in _split3(a))
    other = a.astype(BF16)
    return sum(lax.dot_general(other, p, dims, preferred_element_type=F32) for p in _split3(b))


def _cumsum_rows(tl, x, lowp):
    q = x.shape[0]
    if q != 8:
        return _xdot(tl, x, NN, 1, lowp)
    r = lax.broadcasted_iota(jnp.int32, (q, 1), 0)
    for d in (1, 2, 4):
        x = x + jnp.where(r >= d, pltpu.roll(x, d, 0), 0.0)
    return x


def _sigmoid(x):
    return 0.5 * jnp.tanh(0.5 * x) + 0.5


def _silu(x):
    h = 0.5 * x
    return h * jnp.tanh(h) + h


def _softplus(x):
    return jnp.maximum(x, 0.0) + jnp.log1p(jnp.exp(-jnp.abs(x)))


CONV_PAD = 8


def _conv_chunk(xp_scr, cw_ref, cb_ref, q, valid):
    cw = cw_ref[...]
    p = CONV_PAD
    xp = xp_scr[...]
    conv = (cb_ref[...]
            + cw[0:1] * pltpu.roll(xp, 3, 0)[p:p + q, :]
            + cw[1:2] * pltpu.roll(xp, 2, 0)[p:p + q, :]
            + cw[2:3] * pltpu.roll(xp, 1, 0)[p:p + q, :]
            + cw[3:4] * xp[p:p + q, :])
    carry = xp_scr[p - 3 + valid:p + valid, :]
    xp_scr[p - 3:p, :] = carry
    return conv, carry


def _inproj_kernel(x_ref, nw_ref, w_ref, w32_ref, wst_ref, o_ref, o32_ref, st_ref, xn_scr):
    @pl.when(pl.program_id(1) == 0)
    def _():
        x = x_ref[...]
        xn = x * lax.rsqrt(jnp.mean(x * x, -1, keepdims=True) + EPS) * nw_ref[...]
        xb = xn.astype(BF16)
        xn_scr[...] = xb
        o32_ref[...] = lax.dot_general(xb, w32_ref[...], NT, preferred_element_type=F32)
        st_ref[...] = lax.dot_general(wst_ref[...], xb, NT, preferred_element_type=F32)

    o_ref[...] = lax.dot_general(xn_scr[...], w_ref[...], NT, preferred_element_type=F32).astype(BF16)


def _inproj(x, nw, w, w32, wst):
    t = x.shape[0]
    tm = min(INPROJ_ROW_TILE, t)
    tn = N_PROJ // INPROJ_COL_STEPS
    return pl.pallas_call(
        _inproj_kernel,
        grid=(t // tm, N_PROJ // tn),
        in_specs=[
            pl.BlockSpec((tm, D_MODEL), lambda i, j: (i, 0)),
            pl.BlockSpec((1, D_MODEL), lambda i, j: (0, 0)),
            pl.BlockSpec((tn, D_MODEL), lambda i, j: (j, 0)),
            pl.BlockSpec((N_PROJ32, D_MODEL), lambda i, j: (0, 0)),
            pl.BlockSpec((N_SMALL, D_MODEL), lambda i, j: (0, 0)),
        ],
        out_specs=[
            pl.BlockSpec((tm, tn), lambda i, j: (i, j)),
            pl.BlockSpec((tm, N_PROJ32), lambda i, j: (i, 0)),
            pl.BlockSpec((N_SMALL, tm), lambda i, j: (0, i)),
        ],
        out_shape=[
            jax.ShapeDtypeStruct((t, N_PROJ), BF16),
            jax.ShapeDtypeStruct((t, N_PROJ32), F32),
            jax.ShapeDtypeStruct((N_SMALL, t), F32),
        ],
        scratch_shapes=[pltpu.VMEM((tm, D_MODEL), BF16)],
        compiler_params=pltpu.CompilerParams(
            dimension_semantics=("arbitrary", "arbitrary"),
            vmem_limit_bytes=VMEM_LIMIT_BYTES),
        name="inproj",
    )(x, nw, w, w32, wst)


def _merge_kernel(g_ref, ya_ref, yb_ref, yc_ref, yd_ref, x_ref, wb_ref, wo_ref, fnw_ref, o_ref, *, final):
    acc = None
    for i, y_ref in enumerate((ya_ref, yb_ref, yc_ref, yd_ref)):
        gate = _sigmoid(g_ref[:, i * D_MODEL:(i + 1) * D_MODEL].astype(F32))
        term = gate * jnp.dot(y_ref[...], wb_ref[i], preferred_element_type=F32)
        acc = term if acc is None else acc + term
    out = x_ref[...] + jnp.dot(acc.astype(BF16), wo_ref[...], preferred_element_type=F32)
    if final:
        out = out * lax.rsqrt(jnp.mean(out * out, -1, keepdims=True) + EPS) * fnw_ref[...]
    o_ref[...] = out


def _merge(proj, ys, x, wb, wo, fnw, final):
    t = x.shape[0]
    tm = min(MERGE_ROW_TILE, t)
    yspec = pl.BlockSpec((tm, BRANCH_W), lambda i: (i, 0))
    return pl.pallas_call(
        functools.partial(_merge_kernel, final=final),
        grid=(t // tm,),
        in_specs=[
            pl.BlockSpec((tm, N_BRANCH * D_MODEL), lambda i: (i, COL_MERGE // (N_BRANCH * D_MODEL))),
            yspec, yspec, yspec, yspec,
            pl.BlockSpec((tm, D_MODEL), lambda i: (i, 0)),
            pl.BlockSpec((N_BRANCH, BRANCH_W, D_MODEL), lambda i: (0, 0, 0)),
            pl.BlockSpec((D_MODEL, D_MODEL), lambda i: (0, 0)),
            pl.BlockSpec((1, D_MODEL), lambda i: (0, 0)),
        ],
        out_specs=pl.BlockSpec((tm, D_MODEL), lambda i: (i, 0)),
        out_shape=jax.ShapeDtypeStruct((t, D_MODEL), F32),
        compiler_params=pltpu.CompilerParams(
            dimension_semantics=("arbitrary",),
            vmem_limit_bytes=VMEM_LIMIT_BYTES),
        name="merge",
    )(proj, *ys, x, wb, wo, fnw)


def _col_spec(bs, q, col, width):
    return pl.BlockSpec((bs, q, width), lambda b, c: (b, c, col // width))


def _row_spec(bs, q):
    return pl.BlockSpec((bs, N_SMALL, q), lambda b, c: (b, 0, c))


def _const_spec(shape):
    nd = len(shape)
    return pl.BlockSpec(shape, lambda b, c: (0,) * nd)


def _state_in_spec(layer, bs, tail):
    nd = len(tail)
    return pl.BlockSpec((1, bs) + tail, lambda b, c: (layer, b) + (0,) * nd)


def _state_out_spec(layer, bs, tail):
    nd = len(tail)
    return pl.BlockSpec((1, bs) + tail, lambda b, c: (layer, b) + (0,) * nd)


def _y_spec(bs, q):
    return pl.BlockSpec((bs, q, BRANCH_W), lambda b, c: (b, c, 0))


class _Mixer(NamedTuple):
    in_specs: list
    args: list
    out_specs: list
    out_shapes: list
    scratch: list
    group: Callable


_DONE = object()
_MIXER_PARAMS = pltpu.CompilerParams(
    dimension_semantics=("arbitrary", "arbitrary"),
    vmem_limit_bytes=VMEM_LIMIT_BYTES)


def _launch(mixers, prevs, *, grid, bs, nc, name):
    n_in = [len(m.in_specs) for m in mixers]
    n_out = [len(m.out_specs) for m in mixers]
    n_scr = [len(m.scratch) for m in mixers]
    tot_in, tot_out = sum(n_in), sum(n_out)
    state_idx = [sum(n_out[:i]) + k for i in range(len(mixers)) for k in range(1, n_out[i])]
    n_alias = 0 if prevs is None else len(state_idx)

    def kern(*refs):
        ins, outs, scr = refs[:tot_in], refs[tot_in + n_alias:tot_in + n_alias + tot_out], refs[tot_in + n_alias + tot_out:]
        groups = []
        for i, m in enumerate(mixers):
            groups.append(m.group(ins[sum(n_in[:i]):sum(n_in[:i + 1])],
                                  outs[sum(n_out[:i]):sum(n_out[:i + 1])],
                                  scr[sum(n_scr[:i]):sum(n_scr[:i + 1])]))
        c = pl.program_id(1)
        views = [[[r.at[s] for r in g[3]] for s in range(bs)] for g in groups]

        @pl.when(c == 0)
        def _():
            for g, v in zip(groups, views):
                for s in range(bs):
                    g[0](v[s])

        active = [g[1](v[s], g[4]) for g, v in zip(groups, views) for s in range(bs)]
        while active:
            active = [gen for gen in active if next(gen, _DONE) is not _DONE]

        @pl.when(c == nc - 1)
        def _():
            for g, v in zip(groups, views):
                for s in range(bs):
                    g[2](v[s])

    in_specs = [sp for m in mixers for sp in m.in_specs] + [pl.BlockSpec(memory_space=pl.ANY)] * n_alias
    args = [a for m in mixers for a in m.args]
    if prevs is not None:
        args += [a for p in prevs for a in p]
    outs = pl.pallas_call(
        kern,
        grid=grid,
        in_specs=in_specs,
        out_specs=[sp for m in mixers for sp in m.out_specs],
        out_shape=[sh for m in mixers for sh in m.out_shapes],
        scratch_shapes=[sc for m in mixers for sc in m.scratch],
        input_output_aliases={tot_in + k: idx for k, idx in enumerate(state_idx)} if n_alias else {},
        compiler_params=_MIXER_PARAMS,
        name=name,
    )(*args)
    return [outs[sum(n_out[:i]):sum(n_out[:i + 1])] for i in range(len(mixers))]


def _ssd_group(ins, outs, scr, *, q, valid, lowp):
    z_ref, x_ref, bc_ref, sc_ref, sr_ref, conv0_ref, h0_ref, *consts = ins
    y_ref, convo_ref, ho_ref = outs
    xp_scr, h_scr, y_scr = scr

    def init(v):
        v[9][CONV_PAD - 3:CONV_PAD, :] = v[5][...]
        v[10][...] = v[6][...]

    def final(v):
        v[8][...] = v[10][...]

    return (init, functools.partial(_ssd_main, q=q, valid=valid, lowp=lowp), final,
            (z_ref, x_ref, bc_ref, sc_ref, sr_ref, conv0_ref.at[0], h0_ref.at[0],
             y_ref, ho_ref.at[0], xp_scr, h_scr, y_scr, convo_ref.at[0]),
            tuple(consts))


def _ssd_main(v, k, *, q, valid, lowp):
    z_ref, x_ref, bc_ref, sc_ref, sr_ref, _, _, y_ref, _, xp_scr, h_scr, y_scr, convo_ref = v
    cw_ref, cb_ref, bcol_ref, brow_ref, alc_ref, alr_ref, dexp_ref, nw_ref, e_ref, tl_ref, tu_ref = k
    xp_scr[CONV_PAD:CONV_PAD + q, 0:BRANCH_W] = x_ref[...].astype(F32)
    xp_scr[CONV_PAD:CONV_PAD + q, BRANCH_W:SSD_CONV_CH] = bc_ref[...]
    conv, carry = _conv_chunk(xp_scr, cw_ref, cb_ref, q, valid)
    convo_ref[...] = carry
    yield
    xbc = _silu(conv)
    xs = xbc[:, 0:BRANCH_W]
    bm = xbc[:, BRANCH_W:BRANCH_W + 128]
    cm = xbc[:, BRANCH_W + 128:BRANCH_W + 256]

    dt_c = _softplus(sc_ref[...] + bcol_ref[...])
    a_c = dt_c * (-jnp.exp(alc_ref[...]))
    dt_r = _softplus(sr_ref[...] + brow_ref[...])
    a_r = dt_r * (-jnp.exp(alr_ref[...]))
    if valid < q:
        tcol = lax.broadcasted_iota(jnp.int32, (q, 1), 0)
        trow = lax.broadcasted_iota(jnp.int32, (1, q), 1)
        a_c = jnp.where(tcol < valid, a_c, 0.0)
        dt_c = jnp.where(tcol < valid, dt_c, 0.0)
        a_r = jnp.where(trow < valid, a_r, 0.0)

    tl = tl_ref[...]
    acum_c = _cumsum_rows(tl, a_c, lowp)
    acum_r = _xdot(a_r, tu_ref[...], NN, 0, lowp)
    e = e_ref[...]
    dt_e = _xdot(dt_c, e, NN, 0, lowp)
    acum_e = _xdot(acum_c, e, NN, 0, lowp)
    xdt = xs * dt_e
    xend = xdt * jnp.exp(acum_e[q - 1:q, :] - acum_e)
    eac = jnp.exp(acum_e)
    dec_last = jnp.exp(acum_c[q - 1:q, :])
    yield

    fine = q < 16
    tri = (lax.broadcasted_iota(jnp.int32, (q, q), 0) >= lax.broadcasted_iota(jnp.int32, (q, q), 1))
    rpg = SSD_HEADS // SSD_GROUPS
    for g in range(SSD_GROUPS):
        bg = bm[:, g * SSD_STATE:(g + 1) * SSD_STATE]
        cg = cm[:, g * SSD_STATE:(g + 1) * SSD_STATE]
        cb = _mm(cg, bg, lowp, NT)
        for r in range(rpg):
            h = g * rpg + r
            hs = slice(h * SSD_HEADDIM, (h + 1) * SSD_HEADDIM)
            seg = acum_c[:, h:h + 1] - acum_r[h:h + 1, :]
            m = cb * jnp.exp(jnp.where(tri, seg, NEG))
            hp = h_scr[h]
            yh = _mm(m, xdt[:, hs], lowp) + _mm(cg, hp, lowp, NT) * eac[:, hs]
            y_scr[:, hs] = yh
            if fine:
                yield
            h_scr[h] = hp * dec_last[:, h:h + 1] + _mm(xend[:, hs], bg, lowp, TN)
            yield

    y = y_scr[...] + xs * dexp_ref[...]
    y = y * _silu(z_ref[...].astype(F32))
    y = y * lax.rsqrt(jnp.mean(y * y, -1, keepdims=True) + EPS) * nw_ref[...]
    y_ref[...] = y.astype(BF16)


def _ssd(proj, proj32, srow, conv_state, h_state, layer, w, consts, *, b, q, nc, valid, lowp, bs):
    return _Mixer(
        in_specs=[
            _col_spec(bs, q, COL_A_Z, BRANCH_W),
            _col_spec(bs, q, COL_A_X, BRANCH_W),
            _col_spec(bs, q, COL32_A_BC, 256),
            _col_spec(bs, q, COL32_SMALL, 128),
            _row_spec(bs, q),
            _state_in_spec(layer, bs, (CONV_W - 1, SSD_CONV_CH)),
            _state_in_spec(layer, bs, (SSD_HEADS, SSD_HEADDIM, SSD_STATE)),
            _const_spec((CONV_W, SSD_CONV_CH)),
            _const_spec((1, SSD_CONV_CH)),
            _const_spec((1, 128)),
            _const_spec((N_SMALL, 1)),
            _const_spec((1, 128)),
            _const_spec((N_SMALL, 1)),
            _const_spec((1, BRANCH_W)),
            _const_spec((1, BRANCH_W)),
            _const_spec((128, BRANCH_W)),
            _const_spec((q, q)),
            _const_spec((q, q)),
        ],
        out_specs=[
            _y_spec(bs, q),
            _state_out_spec(layer, bs, (CONV_W - 1, SSD_CONV_CH)),
            _state_out_spec(layer, bs, (SSD_HEADS, SSD_HEADDIM, SSD_STATE)),
        ],
        out_shapes=[
            jax.ShapeDtypeStruct((b, nc * q, BRANCH_W), BF16),
            jax.ShapeDtypeStruct((DEPTH, b, CONV_W - 1, SSD_CONV_CH), F32),
            jax.ShapeDtypeStruct((DEPTH, b, SSD_HEADS, SSD_HEADDIM, SSD_STATE), F32),
        ],
        scratch=[
            pltpu.VMEM((bs, q + CONV_PAD, SSD_CONV_CH), F32),
            pltpu.VMEM((bs, SSD_HEADS, SSD_HEADDIM, SSD_STATE), F32),
            pltpu.VMEM((bs, q, BRANCH_W), F32),
        ],
        args=[proj, proj, proj32, proj32, srow, conv_state, h_state,
              w['ssd_conv_w'], w['ssd_conv_b'], w['bias_col'], w['bias_row'], w['alog_col'], w['alog_row'],
              w['ssd_D_exp'], w['ssd_norm_w'], consts['expand'], consts['tril'], consts['triu']],
        group=functools.partial(_ssd_group, q=q, valid=valid, lowp=lowp))


def _s5_kernel(u_ref, gate_ref, hr0_ref, hi0_ref, perm_ref, permt_ref, ar_ref, ai_ref,
               wbr_ref, wbi_ref, wcr_ref, wci_ref, d_ref, glu_ref,
               y_ref, hro_ref, hio_ref,
               hr_scr, hi_scr, cr_scr, ci_scr, *, q, nc, ngrp):
    c = pl.program_id(1)

    @pl.when(c == 0)
    def _():
        cr_scr[...] = hr0_ref[0]
        ci_scr[...] = hi0_ref[0]

    rows_in = perm_ref.shape[1]
    perm = perm_ref[...]
    u_tm = jnp.dot(perm, u_ref[...].reshape(rows_in, BRANCH_W), preferred_element_type=F32).astype(BF16)
    gate_tm = jnp.dot(perm, gate_ref[...].reshape(rows_in, BRANCH_W), preferred_element_type=F32)
    nblk = S5_LANES // 512
    for j in range(nblk):
        uj = u_tm[:, j * 128:(j + 1) * 128]
        hr_scr[:, j * 512:(j + 1) * 512] = jnp.dot(uj, wbr_ref[j], preferred_element_type=F32)
        hi_scr[:, j * 512:(j + 1) * 512] = jnp.dot(uj, wbi_ref[j], preferred_element_type=F32)

    half = S5_LANES // 2
    for hf in range(2):
        sl = slice(hf * half, (hf + 1) * half)
        ar = ar_ref[:, sl]
        ai = ai_ref[:, sl]

        def grp_body(g, carry, sl=sl, ar=ar, ai=ai):
            s0 = pl.multiple_of(g * 8, 8)

            def t_body(t, h):
                r0 = pl.multiple_of(g * (8 * q) + t * 8, 8)
                hr, hi = h
                nr = ar * hr - ai * hi + hr_scr[pl.ds(r0, 8), sl]
                ni = ar * hi + ai * hr + hi_scr[pl.ds(r0, 8), sl]
                hr_scr[pl.ds(r0, 8), sl] = nr
                hi_scr[pl.ds(r0, 8), sl] = ni
                return nr, ni

            h = lax.fori_loop(0, q, t_body, (cr_scr[pl.ds(s0, 8), sl], ci_scr[pl.ds(s0, 8), sl]),
                              unroll=min(q, 4))
            cr_scr[pl.ds(s0, 8), sl] = h[0]
            ci_scr[pl.ds(s0, 8), sl] = h[1]
            return carry

        lax.fori_loop(0, ngrp, grp_body, 0)

    parts = []
    for j in range(nblk):
        sl = slice(j * 512, (j + 1) * 512)
        parts.append(jnp.dot(hr_scr[:, sl].astype(BF16), wcr_ref[j], preferred_element_type=F32)
                     - jnp.dot(hi_scr[:, sl].astype(BF16), wci_ref[j], preferred_element_type=F32))
    y = jnp.concatenate(parts, axis=1) + u_tm.astype(F32) * d_ref[...]
    g = jax.nn.gelu(y)
    y = g * _sigmoid(jnp.dot(g.astype(BF16), glu_ref[...], preferred_element_type=F32))
    y = (y * _silu(gate_tm)).astype(BF16)
    y = jnp.dot(permt_ref[...], y, preferred_element_type=F32).astype(BF16)
    y_ref[...] = y.reshape(y_ref.shape)

    @pl.when(c == nc - 1)
    def _():
        hro_ref[0] = cr_scr[...]
        hio_ref[0] = ci_scr[...]


def _s5_perm(bb, lp, valid):
    perm = np.zeros((bb * valid, bb * lp), np.float32)
    for g in range(bb // 8):
        for t in range(valid):
            for s in range(8):
                perm[g * 8 * valid + t * 8 + s, (g * 8 + s) * lp + t] = 1.0
    return jnp.asarray(perm, BF16), jnp.asarray(perm.T, BF16)


def _s5(proj, hr_state, hi_state, layer, w, prev, *, b, lp, q, valid, bb):
    nc = lp // q
    perm, permt = _s5_perm(bb, q, valid)
    rows = bb * valid
    if nc == 1:
        def col_spec(col):
            return pl.BlockSpec((bb * q, BRANCH_W), lambda i, c: (i, col // BRANCH_W))
        src = proj
        y_spec = pl.BlockSpec((bb * q, BRANCH_W), lambda i, c: (i, 0))
        y_shape = jax.ShapeDtypeStruct((b * lp, BRANCH_W), BF16)
    else:
        def col_spec(col):
            return pl.BlockSpec((bb, q, BRANCH_W), lambda i, c: (i, c, col // BRANCH_W))
        src = proj.reshape(b, lp, N_PROJ)
        y_spec = pl.BlockSpec((bb, q, BRANCH_W), lambda i, c: (i, c, 0))
        y_shape = jax.ShapeDtypeStruct((b, lp, BRANCH_W), BF16)
    st_in = pl.BlockSpec((1, bb, S5_LANES), lambda i, c: (layer, i, 0))
    st_out = st_in
    kern = functools.partial(_s5_kernel, q=valid, nc=nc, ngrp=bb // 8)
    n_in = 14
    alias_kw = {}
    if prev is not None:
        base = kern

        def kern(*refs):
            return base(*refs[:n_in], *refs[n_in + 2:])

        alias_kw = dict(input_output_aliases={n_in: 1, n_in + 1: 2})
    y, hr, hi = pl.pallas_call(
        kern,
        grid=(b // bb, nc),
        **alias_kw,
        in_specs=[
            col_spec(COL_B_U),
            col_spec(COL_B_GATE),
            st_in, st_in,
            _const_spec(perm.shape),
            _const_spec(permt.shape),
            _const_spec((8, S5_LANES)),
            _const_spec((8, S5_LANES)),
            _const_spec((4, 128, 512)),
            _const_spec((4, 128, 512)),
            _const_spec((4, 512, 128)),
            _const_spec((4, 512, 128)),
            _const_spec((1, BRANCH_W)),
            _const_spec((BRANCH_W, BRANCH_W)),
        ] + [pl.BlockSpec(memory_space=pl.ANY)] * (0 if prev is None else 2),
        out_specs=[y_spec, st_out, st_out],
        out_shape=[
            y_shape,
            jax.ShapeDtypeStruct((DEPTH, b, S5_LANES), F32),
            jax.ShapeDtypeStruct((DEPTH, b, S5_LANES), F32),
        ],
        scratch_shapes=[
            pltpu.VMEM((rows, S5_LANES), F32),
            pltpu.VMEM((rows, S5_LANES), F32),
            pltpu.VMEM((bb, S5_LANES), F32),
            pltpu.VMEM((bb, S5_LANES), F32),
        ],
        compiler_params=_MIXER_PARAMS,
        name="s5",
    )(src, src, hr_state, hi_state, perm, permt, w['s5_ar'], w['s5_ai'],
      w['s5_wbr'], w['s5_wbi'], w['s5_wcr'], w['s5_wci'], w['s5_D'], w['s5_glu_w'], *(prev or ()))
    return y.reshape(b * lp, BRANCH_W), hr, hi


def _mlstm_group(ins, outs, scr, *, q, valid, lowp):
    x_ref, z_ref, o_ref, sc_ref, sr_ref, conv0_ref, c0_ref, n0_ref, m0_ref, *consts = ins
    y_ref, convo_ref, co_ref, no_ref, mo_ref = outs
    xp_scr, c_scr, n_scr, m_scr, h_scr = scr

    def init(v):
        v[13][CONV_PAD - 3:CONV_PAD, :] = v[5][...]
        v[14][...] = v[6][...]
        v[15][...] = v[7][...]
        v[16][...] = v[8][...]

    def final(v):
        v[10][...] = v[14][...]
        v[11][...] = v[15][...]
        v[12][...] = v[16][...]

    return (init, functools.partial(_mlstm_main, q=q, valid=valid, lowp=lowp), final,
            (x_ref, z_ref, o_ref, sc_ref, sr_ref, conv0_ref.at[0], c0_ref.at[0], n0_ref.at[0], m0_ref.at[0],
             y_ref, co_ref.at[0], no_ref.at[0], mo_ref.at[0], xp_scr, c_scr, n_scr, m_scr, h_scr,
             convo_ref.at[0]),
            tuple(consts))


def _mlstm_main(v, k, *, q, valid, lowp):
    (x_ref, z_ref, o_ref, sc_ref, sr_ref, _, _, _, _, y_ref, _, _, _,
     xp_scr, c_scr, n_scr, m_scr, h_scr, convo_ref) = v
    cw_ref, cb_ref, wq_ref, wk_ref, wv_ref, bcol_ref, brow_ref, nw_ref, skip_ref, tl_ref, tu_ref = k
    x = x_ref[...].astype(F32)
    xp_scr[CONV_PAD:CONV_PAD + q, :] = x
    conv, carry = _conv_chunk(xp_scr, cw_ref, cb_ref, q, valid)
    convo_ref[...] = carry
    yield
    xc = _silu(conv)

    pre_c = sc_ref[...] + bcol_ref[...]
    pre_r = sr_ref[...] + brow_ref[...]
    ig_c = pre_c
    lf_c = -_softplus(-pre_c)
    ig_r = pre_r
    lf_r = -_softplus(-pre_r)
    if valid < q:
        tcol = lax.broadcasted_iota(jnp.int32, (q, 1), 0)
        trow = lax.broadcasted_iota(jnp.int32, (1, q), 1)
        lf_c = jnp.where(tcol < valid, lf_c, 0.0)
        ig_c = jnp.where(tcol < valid, ig_c, NEG)
        lf_r = jnp.where(trow < valid, lf_r, 0.0)
        ig_r = jnp.where(trow < valid, ig_r, NEG)
    b_c = _cumsum_rows(tl_ref[...], lf_c, lowp)
    b_r = _xdot(lf_r, tu_ref[...], NN, 0, lowp)
    yield

    fine = q < 16
    tri = (lax.broadcasted_iota(jnp.int32, (q, q), 0) >= lax.broadcasted_iota(jnp.int32, (q, q), 1))
    o_all = o_ref[...].astype(F32)
    for h in range(ML_HEADS):
        hs = slice(h * ML_HEADDIM, (h + 1) * ML_HEADDIM)
        xh = xc[:, hs]
        qh = _mm(xh, wq_ref[h], lowp)
        kh = _mm(xh, wk_ref[h], lowp) * (ML_HEADDIM ** -0.5)
        vh = _mm(x[:, hs], wv_ref[h], lowp)
        yield
        bc = b_c[:, SMALL_F + h:SMALL_F + h + 1]
        br = b_r[SMALL_F + h:SMALL_F + h + 1, :]
        ic = ig_c[:, SMALL_I + h:SMALL_I + h + 1]
        ir = ig_r[SMALL_I + h:SMALL_I + h + 1, :]
        mp = m_scr[h:h + 1, 0:1]
        if q == ML_HEADDIM:
            bc = jnp.broadcast_to(bc, (q, q))
            ic = jnp.broadcast_to(ic, (q, q))
        dlog = jnp.where(tri, bc - br + ir, NEG)
        inter = bc + mp
        m_t = jnp.maximum(inter, jnp.max(dlog, axis=1, keepdims=True))
        wgt = jnp.exp(dlog - m_t)
        yield
        s = _mm(qh, kh, lowp, NT) * wgt
        if fine:
            yield
        scale = jnp.exp(inter - m_t)
        cp = c_scr[h]
        npv = n_scr[h:h + 1, :]
        num = _mm(s, vh, lowp) + scale * _mm(qh, cp, lowp)
        dot =jnp.sum(s, axis=1, keepdims=True) + scale * jnp.sum(qh * npv, axis=1, keepdims=True)
        hh = num / jnp.maximum(jnp.abs(dot), jnp.exp(-m_t))
        yield
        m_new = m_t[q - 1:q, :]
        b_last = bc[q - 1:q, :]
        w_end = jnp.exp(b_last - bc + ic - m_new)
        cs = jnp.exp(b_last + mp - m_new)
        if fine:
            yield
        kw = kh * w_end
        c_scr[h] = cs * cp + _mm(kw, vh, lowp, TN)
        n_scr[h:h + 1, :] = cs * npv + jnp.sum(kw, axis=0, keepdims=True)
        m_scr[h:h + 1, :] = jnp.broadcast_to(m_new, (1, 128))
        yield
        hh = hh * _sigmoid(o_all[:, hs])
        hc = hh - jnp.mean(hh, -1, keepdims=True)
        h_scr[:, hs] = hc * lax.rsqrt(jnp.mean(hc * hc, -1, keepdims=True) + EPS)
        yield

    y = h_scr[...] * nw_ref[...] + skip_ref[...] * xc
    y_ref[...] = (y * _silu(z_ref[...].astype(F32))).astype(BF16)


def _mlstm(proj, proj32, srow, conv_state, c_state, n_state, m_state, layer, w, consts,
           *, b, q, nc, valid, lowp, bs):
    return _Mixer(
        in_specs=[
            _col_spec(bs, q, COL_C_X, BRANCH_W),
            _col_spec(bs, q, COL_C_Z, BRANCH_W),
            _col_spec(bs, q, COL_C_O, BRANCH_W),
            _col_spec(bs, q, COL32_SMALL, 128),
            _row_spec(bs, q),
            _state_in_spec(layer, bs, (CONV_W - 1, BRANCH_W)),
            _state_in_spec(layer, bs, (ML_HEADS, ML_HEADDIM, ML_HEADDIM)),
            _state_in_spec(layer, bs, (ML_HEADS, ML_HEADDIM)),
            _state_in_spec(layer, bs, (8, 128)),
            _const_spec((CONV_W, BRANCH_W)),
            _const_spec((1, BRANCH_W)),
            _const_spec((ML_HEADS, ML_HEADDIM, ML_HEADDIM)),
            _const_spec((ML_HEADS, ML_HEADDIM, ML_HEADDIM)),
            _const_spec((ML_HEADS, ML_HEADDIM, ML_HEADDIM)),
            _const_spec((1, 128)),
            _const_spec((N_SMALL, 1)),
            _const_spec((1, BRANCH_W)),
            _const_spec((1, BRANCH_W)),
            _const_spec((q, q)),
            _const_spec((q, q)),
        ],
        out_specs=[
            _y_spec(bs, q),
            _state_out_spec(layer, bs, (CONV_W - 1, BRANCH_W)),
            _state_out_spec(layer, bs, (ML_HEADS, ML_HEADDIM, ML_HEADDIM)),
            _state_out_spec(layer, bs, (ML_HEADS, ML_HEADDIM)),
            _state_out_spec(layer, bs, (8, 128)),
        ],
        out_shapes=[
            jax.ShapeDtypeStruct((b, nc * q, BRANCH_W), BF16),
            jax.ShapeDtypeStruct((DEPTH, b, CONV_W - 1, BRANCH_W), F32),
            jax.ShapeDtypeStruct((DEPTH, b, ML_HEADS, ML_HEADDIM, ML_HEADDIM), F32),
            jax.ShapeDtypeStruct((DEPTH, b, ML_HEADS, ML_HEADDIM), F32),
            jax.ShapeDtypeStruct((DEPTH, b, 8, 128), F32),
        ],
        scratch=[
            pltpu.VMEM((bs, q + CONV_PAD, BRANCH_W), F32),
            pltpu.VMEM((bs, ML_HEADS, ML_HEADDIM, ML_HEADDIM), F32),
            pltpu.VMEM((bs, ML_HEADS, ML_HEADDIM), F32),
            pltpu.VMEM((bs, 8, 128), F32),
            pltpu.VMEM((bs, q, BRANCH_W), F32),
        ],
        args=[proj, proj, proj, proj32, srow, conv_state, c_state, n_state, m_state,
              w['ml_conv_w'], w['ml_conv_b'], w['ml_wq'], w['ml_wk'], w['ml_wv'], w['bias_col'], w['bias_row'],
              w['ml_norm_w'], w['ml_skip'], consts['tril'], consts['triu']],
        group=functools.partial(_mlstm_group, q=q, valid=valid, lowp=lowp))


def _hgrn_group(ins, outs, scr, *, q, valid, lowp, nlev):
    f_ref, i_ref, q_ref, g_ref, s0_ref, *consts = ins
    y_ref, so_ref = outs
    s_scr, o_scr = scr

    def init(v):
        v[7][...] = v[4][...]

    def final(v):
        v[6][...] = v[7][...]

    return (init, functools.partial(_hgrn_main, q=q, valid=valid, lowp=lowp, nlev=nlev), final,
            (f_ref, i_ref, q_ref, g_ref, s0_ref.at[0], y_ref, so_ref.at[0], s_scr, o_scr),
            tuple(consts))


def _hgrn_main(v, k, *, q, valid, lowp, nlev):
    f_ref, i_ref, q_ref, g_ref, _, y_ref, _, s_scr, o_scr = v
    lb_ref, nw_ref, tl_ref = k
    lb = lb_ref[...]
    fg = lb + (1.0 - lb) * _sigmoid(f_ref[...])
    kk = 1.0 - fg
    qq = _silu(q_ref[...].astype(F32)) * (HG_HEADDIM ** -0.5)
    vv = i_ref[...] if lowp else i_ref[...].astype(F32)
    rcol =lax.broadcasted_iota(jnp.int32, (q, 1), 0)
    if valid < q:
        fg = jnp.where(rcol < valid, fg, 1.0)
        kk = jnp.where(rcol < valid, kk, 0.0)
    lf = jnp.log(fg)

    gcum = _cumsum_rows(tl_ref[...], lf, lowp)
    yield

    fine = q < 16
    rt = lax.broadcasted_iota(jnp.int32, (q, q), 0)
    cs = lax.broadcasted_iota(jnp.int32, (q, q), 1)
    rx = rt ^ cs
    att = [None] * HG_HEADS
    for lev in range(nlev):
        m = q >> (lev + 1)
        second = (rcol & m) != 0
        if m >= 4:
            nb = q // (2 * m)
            gb = jnp.broadcast_to(gcum.reshape(nb, 2 * m, BRANCH_W)[:, m - 1:m, :], (nb, 2 * m, BRANCH_W))
            e = jnp.exp(-jnp.abs(gcum - gb.reshape(q, BRANCH_W)))
        elif m == 2:
            r4 = rcol & 3
            e = jnp.where(r4 == 0, pltpu.roll(fg, q - 1, 0),
                          jnp.where(r4 == 1, 1.0, jnp.where(r4 == 2, fg, fg * pltpu.roll(fg, 1, 0))))
        else:
            e = jnp.where(second, fg, 1.0)
        u = jnp.where(second, qq, kk) * e
        if lowp:
            u = u.astype(BF16)
        pair = (rx >> int(math.log2(m))) == 1
        for h in range(HG_HEADS):
            hs = slice(h * HG_HEADDIM, (h + 1) * HG_HEADDIM)
            s = lax.dot_general(u[:, hs], u[:, hs], NT, preferred_element_type=F32)
            att[h] = jnp.where(pair, s, 0.0 if att[h] is None else att[h])
        yield

    qg = qq * jnp.exp(gcum)
    kend = kk * jnp.exp(gcum[q - 1:q, :] - gcum)
    ones = jnp.ones((q, 128), F32)
    for h in range(HG_HEADS):
        hs = slice(h * HG_HEADDIM, (h + 1) * HG_HEADDIM)
        diag = jnp.sum(qq[:, hs] * kk[:, hs], axis=1, keepdims=True)
        a = jnp.where(rt > cs, att[h], jnp.where(rt == cs, diag, 0.0))
        sp = s_scr[h]
        oh = _mm(a, vv[:, hs], lowp) + _mm(qg[:, hs], sp, lowp)
        if fine:
            yield
        dec = jnp.exp(_xdot(lf[:, hs], ones, TN, 0, lowp))
        s_scr[h] =sp * dec + _mm(kend[:, hs], vv[:, hs], lowp, TN)
        o_scr[:, hs] = oh * lax.rsqrt(jnp.mean(oh * oh, -1, keepdims=True) + EPS)
        yield

    y_ref[...] = (o_scr[...] * nw_ref[...] * _silu(g_ref[...].astype(F32))).astype(BF16)


def _hgrn(proj, proj32, s_state, layer, w, consts, *, b, q, nc, valid, lowp, bs):
    nlev = int(math.log2(q))
    return _Mixer(
        in_specs=[
            _col_spec(bs, q, COL32_D_F, BRANCH_W),
            _col_spec(bs, q, COL_D_I, BRANCH_W),
            _col_spec(bs, q, COL_D_Q, BRANCH_W),
            _col_spec(bs, q, COL_D_G, BRANCH_W),
            _state_in_spec(layer, bs, (HG_HEADS, HG_HEADDIM, HG_HEADDIM)),
            _const_spec((1, BRANCH_W)),
            _const_spec((1, BRANCH_W)),
            _const_spec((q, q)),
        ],
        out_specs=[
            _y_spec(bs, q),
            _state_out_spec(layer, bs, (HG_HEADS, HG_HEADDIM, HG_HEADDIM)),
        ],
        out_shapes=[
            jax.ShapeDtypeStruct((b, nc * q, BRANCH_W), BF16),
            jax.ShapeDtypeStruct((DEPTH, b, HG_HEADS, HG_HEADDIM, HG_HEADDIM), F32),
        ],
        scratch=[
            pltpu.VMEM((bs, HG_HEADS, HG_HEADDIM, HG_HEADDIM), F32),
            pltpu.VMEM((bs, q, BRANCH_W), F32),
        ],
        args=[proj32, proj, proj, proj, s_state, w['hg_lb'], w['hg_norm_w'], consts['tril']],
        group=functools.partial(_hgrn_group, q=q, valid=valid, lowp=lowp, nlev=nlev))


def _chunk_consts(q):
    r = np.arange(q)
    tril = (r[:, None] >= r[None, :]).astype(np.float32)
    expand = np.zeros((128, BRANCH_W), np.float32)
    for h in range(SSD_HEADS):
        expand[h, h * SSD_HEADDIM:(h + 1) * SSD_HEADDIM] = 1.0
    return {'tril': jnp.asarray(tril), 'triu': jnp.asarray(tril.T), 'expand': jnp.asarray(expand)}


def _cmul(ar, ai, br, bi):
    return ar * br - ai * bi, ar * bi + ai * br


def _layer_params(l, p, hg_lb):
    w_in_t = p['w_in'][l].T
    starts = np.cumsum([0] + [n for _, n in _IN_SPLITS])

    def rows(first, last):
        names = [name for name, _ in _IN_SPLITS]
        return w_in_t[starts[names.index(first)]:starts[names.index(last) + 1]]

    small = jnp.concatenate([rows('a_dt', 'a_dt'), rows('c_i', 'c_f')], axis=0)
    w_perm = jnp.concatenate([
        rows('merge', 'merge'), rows('a_z', 'a_x'), rows('b_u', 'b_gate'), rows('c_x', 'c_o'), rows('d_i', 'd_g'),
    ], axis=0).astype(BF16)
    w_perm32 = jnp.concatenate([
        rows('d_f', 'd_f'), rows('a_bc', 'a_bc'), small,
        jnp.zeros((N_PROJ32 - COL32_SMALL - N_SMALL, D_MODEL), F32),
    ], axis=0).astype(BF16)

    def lane_pad(v, off):
        return jnp.zeros((1, 128), F32).at[0, off:off + v.shape[0]].set(v)

    bias_col = (lane_pad(p['ssd_dt_bias'][l], SMALL_DT) + lane_pad(p['ml_i_bias'][l], SMALL_I)
                + lane_pad(p['ml_f_bias'][l], SMALL_F))
    alog_col = lane_pad(p['ssd_A_log'][l], SMALL_DT)

    dt = jnp.exp(p['s5_log_dt'][l])[:, None]
    lr = p['s5_A_re'][l]
    li = p['s5_A_im'][l]
    mag = jnp.exp(lr * dt)
    abr, abi = mag * jnp.cos(li * dt), mag * jnp.sin(li * dt)
    den = lr * lr + li * li
    cr = ((abr - 1.0) * lr + abi * li) / den
    ci = (abi * lr - (abr - 1.0) * li) / den
    bbr, bbi = _cmul(cr[..., None], ci[..., None], p['s5_B_re'][l], p['s5_B_im'][l])
    eye8 = jnp.eye(8, dtype=F32)

    def pack_b(bb):
        return jnp.einsum('jgpc,gh->jgchp', bb.reshape(4, 8, S5_STATE, S5_GROUP), eye8).reshape(4, 128, 512)

    def pack_c(cc):
        return jnp.einsum('jgcp,gh->jgphc', cc.reshape(4, 8, S5_GROUP, S5_STATE), eye8).reshape(4, 512, 128)

    return {
        'norm_w': p['norm_w'][l].reshape(1, D_MODEL),
        'w_in': w_perm, 'w_in32': w_perm32,
        'w_small_t': small.astype(BF16),
        'bias_col': bias_col, 'bias_row': bias_col[0, :N_SMALL].reshape(N_SMALL, 1),
        'alog_col': alog_col, 'alog_row': alog_col[0, :N_SMALL].reshape(N_SMALL, 1),
        'ssd_conv_w': p['ssd_conv_w'][l], 'ssd_conv_b': p['ssd_conv_b'][l].reshape(1, SSD_CONV_CH),
        'ssd_D_exp': jnp.repeat(p['ssd_D'][l], SSD_HEADDIM).reshape(1, BRANCH_W),
        'ssd_norm_w': p['ssd_norm_w'][l].reshape(1, BRANCH_W),
        's5_wbr': pack_b(bbr).astype(BF16), 's5_wbi': pack_b(bbi).astype(BF16),
        's5_wcr': pack_c(p['s5_C_re'][l]).astype(BF16), 's5_wci': pack_c(p['s5_C_im'][l]).astype(BF16),
        's5_ar': jnp.broadcast_to(abr.reshape(1, S5_LANES), (8, S5_LANES)),
        's5_ai': jnp.broadcast_to(abi.reshape(1, S5_LANES), (8, S5_LANES)),
        's5_D': p['s5_D'][l].reshape(1, BRANCH_W), 's5_glu_w': p['s5_glu_w'][l].astype(BF16),
        'ml_conv_w': p['ml_conv_w'][l], 'ml_conv_b': p['ml_conv_b'][l].reshape(1, BRANCH_W),
        'ml_wq': p['ml_wq'][l], 'ml_wk': p['ml_wk'][l], 'ml_wv': p['ml_wv'][l],
        'ml_norm_w': p['ml_norm_w'][l].reshape(1, BRANCH_W), 'ml_skip': p['ml_skip'][l].reshape(1, BRANCH_W),
        'hg_lb': hg_lb[l].reshape(1, BRANCH_W), 'hg_norm_w': p['hg_norm_w'][l].reshape(1, BRANCH_W),
        'w_branch': p['w_branch'][l].astype(BF16), 'w_out': p['w_out'][l].astype(BF16),
    }


_MATMUL_WEIGHTS = ('ml_wq', 'ml_wk', 'ml_wv')
S5_CHUNK = 64
MIXER_PLAN = [(('ssd', 'mlstm', 'hgrn'), 2)]
MIXER_PLAN_PADDED = [(('ssd', 'mlstm', 'hgrn'), 8)]


def _run_group(x3, states, layers, fnw, *, q, valid):
    b, lp, _ = x3.shape
    nc = lp // q
    t = b * lp
    lowp = q >= 16
    consts = _chunk_consts(q)
    ssd_conv, ssd_h, s5_re, s5_im, ml_conv, ml_c, ml_n, ml_m, hg_s = states
    s5_re = s5_re.reshape(DEPTH, b, S5_LANES)
    s5_im = s5_im.reshape(DEPTH, b, S5_LANES)
    if valid < q:
        s5_kw = dict(b=b, lp=lp, q=q, valid=valid, bb=min(b, 128))
    else:
        qs = math.gcd(lp, S5_CHUNK)
        s5_kw = dict(b=b, lp=lp, q=qs, valid=qs, bb=8)
    ml_m = jnp.broadcast_to(jnp.pad(ml_m, ((0, 0), (0, 0), (0, 8 - ML_HEADS)))[..., None], (DEPTH, b, 8, 128))
    plan = [(names, math.gcd(b, n)) for names, n in (MIXER_PLAN_PADDED if valid < q else MIXER_PLAN)]
    kw = dict(b=b, q=q, nc=nc, valid=valid, lowp=lowp)
    x = x3.reshape(t, D_MODEL)
    ys, st, st_s5 = {}, {}, None
    for l, w in enumerate(layers):
        if lowp:
            w = dict(w, **{k: w[k].astype(BF16) for k in _MATMUL_WEIGHTS})
        proj, proj32, small_t = _inproj(x, w['norm_w'], w['w_in'], w['w_in32'], w['w_small_t'])
        proj3 = proj.reshape(b, lp, N_PROJ)
        proj32 = proj32.reshape(b, lp, N_PROJ32)
        srow = small_t.reshape(N_SMALL, b, lp).transpose(1, 0, 2)
        mixers = {
            'ssd': lambda n: _ssd(proj3, proj32, srow, ssd_conv, ssd_h, l, w, consts, bs=n, **kw),
            'mlstm': lambda n: _mlstm(proj3, proj32, srow, ml_conv, ml_c, ml_n, ml_m, l, w, consts, bs=n, **kw),
            'hgrn': lambda n: _hgrn(proj3, proj32, hg_s, l, w, consts, bs=n, **kw),
        }
        for names, n in plan:
            outs = _launch([mixers[k](n) for k in names], None if l == 0 else [st[k] for k in names],
                           grid=(b // n, nc), bs=n, nc=nc, name='_'.join(names))
            for k, o in zip(names, outs):
                ys[k], st[k] = o[0], list(o[1:])
        yb, *st_s5 = _s5(proj, s5_re, s5_im, l, w, st_s5, **s5_kw)
        ya, yc, yd = (ys[k].reshape(t, BRANCH_W) for k in ('ssd', 'mlstm', 'hgrn'))
        x = _merge(proj, (ya, yb, yc, yd), x, w['w_branch'], w['w_out'], fnw, final=(l == DEPTH - 1))
    n_ssd_conv, n_ssd_h = st['ssd']
    n_re, n_im = (s.reshape(DEPTH, b, S5_GROUPS, S5_STATE) for s in st_s5)
    n_ml_conv, n_c, n_n, n_m = st['mlstm']
    new_states = (n_ssd_conv, n_ssd_h, n_re, n_im, n_ml_conv, n_c, n_n, n_m[:, :, :ML_HEADS, 0], st['hgrn'][0])
    return x.reshape(b, lp, D_MODEL), new_states


SAMPLE_PAD = 8


def kernel(x_prompt, x_sample, state_ssd_conv, state_ssd, state_s5_re, state_s5_im, state_mlstm_conv, state_mlstm_C, state_mlstm_n, state_mlstm_m, state_hgrn, norm_w, w_in, ssd_conv_w, ssd_conv_b, ssd_dt_bias, ssd_A_log, ssd_D, ssd_norm_w, s5_A_re, s5_A_im, s5_B_re, s5_B_im, s5_C_re, s5_C_im, s5_D, s5_log_dt, s5_glu_w, ml_conv_w, ml_conv_b, ml_wq, ml_wk, ml_wv, ml_i_bias, ml_f_bias, ml_norm_w, ml_skip, hg_lb_logits, hg_norm_w, w_branch, w_out, final_norm_w):
    p = {'norm_w': norm_w, 'w_in': w_in,
         'ssd_conv_w': ssd_conv_w, 'ssd_conv_b': ssd_conv_b, 'ssd_dt_bias': ssd_dt_bias,
         'ssd_A_log': ssd_A_log, 'ssd_D': ssd_D, 'ssd_norm_w': ssd_norm_w,
         's5_A_re': s5_A_re, 's5_A_im': s5_A_im, 's5_B_re': s5_B_re, 's5_B_im': s5_B_im,
         's5_C_re': s5_C_re, 's5_C_im': s5_C_im, 's5_D': s5_D, 's5_log_dt': s5_log_dt, 's5_glu_w': s5_glu_w,
         'ml_conv_w': ml_conv_w, 'ml_conv_b': ml_conv_b, 'ml_wq': ml_wq, 'ml_wk': ml_wk, 'ml_wv': ml_wv,
         'ml_i_bias': ml_i_bias, 'ml_f_bias': ml_f_bias, 'ml_norm_w': ml_norm_w, 'ml_skip': ml_skip,
         'hg_norm_w': hg_norm_w, 'w_branch': w_branch, 'w_out': w_out}
    lb_cum = jnp.cumsum(jax.nn.softmax(hg_lb_logits, axis=0), axis=0)
    hg_lb = lb_cum - lb_cum[0]
    layers = [_layer_params(l, p, hg_lb) for l in range(DEPTH)]
    fnw = final_norm_w.reshape(1, D_MODEL)

    sample_states = (state_ssd_conv, state_ssd, state_s5_re, state_s5_im, state_mlstm_conv,
                     state_mlstm_C, state_mlstm_n, state_mlstm_m, state_hgrn)
    bp, lp_, _ = x_prompt.shape
    prompt_states = tuple(jnp.zeros((DEPTH, bp) + s.shape[2:], F32) for s in sample_states)
    q_prompt = math.gcd(lp_, 128)
    y_prompt, new_p = _run_group(x_prompt, prompt_states, layers, fnw, q=q_prompt, valid=q_prompt)

    ls = x_sample.shape[1]
    xs_pad = jnp.pad(x_sample, ((0, 0), (0, SAMPLE_PAD - ls), (0, 0)))
    y_sample, new_s = _run_group(xs_pad, sample_states, layers, fnw, q=SAMPLE_PAD, valid=ls)
    y_sample = y_sample[:, :ls]

    out = [y_prompt, y_sample]
    for ps, ss in zip(new_p, new_s):
        out += [ps, ss]
    return tuple(out)
```

```python
import functools
import math
from typing import Callable, NamedTuple

import numpy as np
import jax
import jax.numpy as jnp
from jax import lax
from jax.experimental import pallas as pl
from jax.experimental.pallas import tpu as pltpu

F32 = jnp.float32
BF16 = jnp.bfloat16

D_MODEL = 1024
DEPTH = 2
BRANCH_W = 512
CONV_W = 4
EPS = 1e-6
SSD_HEADS = 8
SSD_HEADDIM = 64
SSD_STATE = 64
SSD_GROUPS = 2
SSD_CONV_CH = 768
S5_GROUPS = 32
S5_GROUP = 16
S5_STATE = 64
S5_LANES = S5_GROUPS * S5_STATE
ML_HEADS = 4
ML_HEADDIM = 128
HG_HEADS = 4
HG_HEADDIM = 128
N_BRANCH = 4

NEG = -1e30

VMEM_LIMIT_BYTES = 48 * 1024 * 1024
INPROJ_ROW_TILE = 1024
INPROJ_COL_STEPS = 4
MERGE_ROW_TILE = 512

_IN_SPLITS = (('a_z', 512), ('a_x', 512), ('a_bc', 256), ('a_dt', 8), ('b_u', 512), ('b_gate', 512),
              ('c_x', 512), ('c_z', 512), ('c_o', 512), ('c_i', 4), ('c_f', 4),
              ('d_f', 512), ('d_i', 512), ('d_q', 512), ('d_g', 512), ('merge', 4096))

N_PROJ = 9216
COL_MERGE = 0
COL_A_Z = 4096
COL_A_X = 4608
COL_B_U = 5120
COL_B_GATE = 5632
COL_C_X = 6144
COL_C_Z = 6656
COL_C_O = 7168
COL_D_I = 7680
COL_D_Q = 8192
COL_D_G = 8704
N_PROJ32 = 896
COL32_D_F = 0
COL32_A_BC = 512
COL32_SMALL = 768
N_SMALL = 16
SMALL_DT = 0
SMALL_I = 8
SMALL_F = 12

NN = (((1,), (0,)), ((), ()))
NT = (((1,), (1,)), ((), ()))
TN = (((0,), (0,)), ((), ()))


def _mm(a, b, lowp, dims=NN):
    if lowp:
        a = a.astype(BF16)
        b = b.astype(BF16)
    return lax.dot_general(a, b, dims, preferred_element_type=F32)


def _split3(a):
    hi = a.astype(BF16)
    r = a - hi.astype(F32)
    mid = r.astype(BF16)
    lo = (r - mid.astype(F32)).astype(BF16)
    return hi, mid, lo


def _xdot(a, b, dims, split, lowp):
    if not lowp and min((b if split == 0 else a).shape) < 8:
        return lax.dot_general(a, b, dims, precision=lax.Precision.HIGHEST,
                               preferred_element_type=F32)
    if split == 0:
        other = b.astype(BF16)
        return sum(lax.dot_general(p, other, dims, preferred_element_type=F32) for p in _split3(a))
    other = a.astype(BF16)
    return sum(lax.dot_general(other, p, dims, preferred_element_type=F32) for p in _split3(b))


def _cumsum_rows(tl, x, lowp):
    q = x.shape[0]
    if q != 8:
        return _xdot(tl, x, NN, 1, lowp)
    r = lax.broadcasted_iota(jnp.int32, (q, 1), 0)
    for d in (1, 2, 4):
        x = x + jnp.where(r >= d, pltpu.roll(x, d, 0), 0.0)
    return x


def _sigmoid(x):
    return 0.5 * jnp.tanh(0.5 * x) + 0.5


def _silu(x):
    h = 0.5 * x
    return h * jnp.tanh(h) + h


def _softplus(x):
    return jnp.maximum(x, 0.0) + jnp.log1p(jnp.exp(-jnp.abs(x)))


CONV_PAD = 8


def _conv_chunk(xp_scr, cw_ref, cb_ref, q, valid):
    cw = cw_ref[...]
    p = CONV_PAD
    xp = xp_scr[...]
    conv = (cb_ref[...]
            + cw[0:1] * pltpu.roll(xp, 3, 0)[p:p + q, :]
            + cw[1:2] * pltpu.roll(xp, 2, 0)[p:p + q, :]
            + cw[2:3] * pltpu.roll(xp, 1, 0)[p:p + q, :]
            + cw[3:4] * xp[p:p + q, :])
    carry = xp_scr[p - 3 + valid:p + valid, :]
    xp_scr[p - 3:p, :] = carry
    return conv, carry


def _inproj_kernel(x_ref, nw_ref, w_ref, w32_ref, wst_ref, o_ref, o32_ref, st_ref, xn_scr):
    @pl.when(pl.program_id(1) == 0)
    def _():
        x = x_ref[...]
        xn = x * lax.rsqrt(jnp.mean(x * x, -1, keepdims=True) + EPS) * nw_ref[...]
        xb = xn.astype(BF16)
        xn_scr[...] = xb
        o32_ref[...] = lax.dot_general(xb, w32_ref[...], NT, preferred_element_type=F32)
        st_ref[...] = lax.dot_general(wst_ref[...], xb, NT, preferred_element_type=F32)

    o_ref[...] = lax.dot_general(xn_scr[...], w_ref[...], NT, preferred_element_type=F32).astype(BF16)


def _inproj(x, nw, w, w32, wst):
    t = x.shape[0]
    tm = min(INPROJ_ROW_TILE, t)
    tn = N_PROJ // INPROJ_COL_STEPS
    return pl.pallas_call(
        _inproj_kernel,
        grid=(t // tm, N_PROJ // tn),
        in_specs=[
            pl.BlockSpec((tm, D_MODEL), lambda i, j: (i, 0)),
            pl.BlockSpec((1, D_MODEL), lambda i, j: (0, 0)),
            pl.BlockSpec((tn, D_MODEL), lambda i, j: (j, 0)),
            pl.BlockSpec((N_PROJ32, D_MODEL), lambda i, j: (0, 0)),
            pl.BlockSpec((N_SMALL, D_MODEL), lambda i, j: (0, 0)),
        ],
        out_specs=[
            pl.BlockSpec((tm, tn), lambda i, j: (i, j)),
            pl.BlockSpec((tm, N_PROJ32), lambda i, j: (i, 0)),
            pl.BlockSpec((N_SMALL, tm), lambda i, j: (0, i)),
        ],
        out_shape=[
            jax.ShapeDtypeStruct((t, N_PROJ), BF16),
            jax.ShapeDtypeStruct((t, N_PROJ32), F32),
            jax.ShapeDtypeStruct((N_SMALL, t), F32),
        ],
        scratch_shapes=[pltpu.VMEM((tm, D_MODEL), BF16)],
        compiler_params=pltpu.CompilerParams(
            dimension_semantics=("arbitrary", "arbitrary"),
            vmem_limit_bytes=VMEM_LIMIT_BYTES),
        name="inproj",
    )(x, nw, w, w32, wst)


def _merge_kernel(g_ref, ya_ref, yb_ref, yc_ref, yd_ref, x_ref, wb_ref, wo_ref, fnw_ref, o_ref, *, final):
    acc = None
    for i, y_ref in enumerate((ya_ref, yb_ref, yc_ref, yd_ref)):
        gate = _sigmoid(g_ref[:, i * D_MODEL:(i + 1) * D_MODEL].astype(F32))
        term = gate * jnp.dot(y_ref[...], wb_ref[i], preferred_element_type=F32)
        acc = term if acc is None else acc + term
    out = x_ref[...] + jnp.dot(acc.astype(BF16), wo_ref[...], preferred_element_type=F32)
    if final:
        out = out * lax.rsqrt(jnp.mean(out * out, -1, keepdims=True) + EPS) * fnw_ref[...]
    o_ref[...] = out


def _merge(proj, ys, x, wb, wo, fnw, final):
    t = x.shape[0]
    tm = min(MERGE_ROW_TILE, t)
    yspec = pl.BlockSpec((tm, BRANCH_W), lambda i: (i, 0))
    return pl.pallas_call(
        functools.partial(_merge_kernel, final=final),
        grid=(t // tm,),
        in_specs=[
            pl.BlockSpec((tm, N_BRANCH * D_MODEL), lambda i: (i, COL_MERGE // (N_BRANCH * D_MODEL))),
            yspec, yspec, yspec, yspec,
            pl.BlockSpec((tm, D_MODEL), lambda i: (i, 0)),
            pl.BlockSpec((N_BRANCH, BRANCH_W, D_MODEL), lambda i: (0, 0, 0)),
            pl.BlockSpec((D_MODEL, D_MODEL), lambda i: (0, 0)),
            pl.BlockSpec((1, D_MODEL), lambda i: (0, 0)),
        ],
        out_specs=pl.BlockSpec((tm, D_MODEL), lambda i: (i, 0)),
        out_shape=jax.ShapeDtypeStruct((t, D_MODEL), F32),
        compiler_params=pltpu.CompilerParams(
            dimension_semantics=("arbitrary",),
            vmem_limit_bytes=VMEM_LIMIT_BYTES),
        name="merge",
    )(proj, *ys, x, wb, wo, fnw)


def _col_spec(bs, q, col, width):
    return pl.BlockSpec((bs, q, width), lambda b, c: (b, c, col // width))


def _row_spec(bs, q):
    return pl.BlockSpec((bs, N_SMALL, q), lambda b, c: (b, 0, c))


def _const_spec(shape):
    nd = len(shape)
    return pl.BlockSpec(shape, lambda b, c: (0,) * nd)


def _state_in_spec(layer, bs, tail):
    nd = len(tail)
    return pl.BlockSpec((1, bs) + tail, lambda b, c: (layer, b) + (0,) * nd)


def _state_out_spec(layer, bs, tail):
    nd = len(tail)
    return pl.BlockSpec((1, bs) + tail, lambda b, c: (layer, b) + (0,) * nd)


def _y_spec(bs, q):
    return pl.BlockSpec((bs, q, BRANCH_W), lambda b, c: (b, c, 0))


class _Mixer(NamedTuple):
    in_specs: list
    args: list
    out_specs: list
    out_shapes: list
    scratch: list
    group: Callable


_DONE = object()
_MIXER_PARAMS = pltpu.CompilerParams(
    dimension_semantics=("arbitrary", "arbitrary"),
    vmem_limit_bytes=VMEM_LIMIT_BYTES)


def _launch(mixers, prevs, *, grid, bs, nc, name):
    n_in = [len(m.in_specs) for m in mixers]
    n_out = [len(m.out_specs) for m in mixers]
    n_scr = [len(m.scratch) for m in mixers]
    tot_in, tot_out = sum(n_in), sum(n_out)
    state_idx = [sum(n_out[:i]) + k for i in range(len(mixers)) for k in range(1, n_out[i])]
    n_alias = 0 if prevs is None else len(state_idx)

    def kern(*refs):
        ins, outs, scr = refs[:tot_in], refs[tot_in + n_alias:tot_in + n_alias + tot_out], refs[tot_in + n_alias + tot_out:]
        groups = []
        for i, m in enumerate(mixers):
            groups.append(m.group(ins[sum(n_in[:i]):sum(n_in[:i + 1])],
                                  outs[sum(n_out[:i]):sum(n_out[:i + 1])],
                                  scr[sum(n_scr[:i]):sum(n_scr[:i + 1])]))
        c = pl.program_id(1)
        views = [[[r.at[s] for r in g[3]] for s in range(bs)] for g in groups]

        @pl.when(c == 0)
        def _():
            for g, v in zip(groups, views):
                for s in range(bs):
                    g[0](v[s])

        active = [g[1](v[s], g[4]) for g, v in zip(groups, views) for s in range(bs)]
        while active:
            active = [gen for gen in active if next(gen, _DONE) is not _DONE]

        @pl.when(c == nc - 1)
        def _():
            for g, v in zip(groups, views):
                for s in range(bs):
                    g[2](v[s])

    in_specs = [sp for m in mixers for sp in m.in_specs] + [pl.BlockSpec(memory_space=pl.ANY)] * n_alias
    args = [a for m in mixers for a in m.args]
    if prevs is not None:
        args += [a for p in prevs for a in p]
    outs = pl.pallas_call(
        kern,
        grid=grid,
        in_specs=in_specs,
        out_specs=[sp for m in mixers for sp in m.out_specs],
        out_shape=[sh for m in mixers for sh in m.out_shapes],
        scratch_shapes=[sc for m in mixers for sc in m.scratch],
        input_output_aliases={tot_in + k: idx for k, idx in enumerate(state_idx)} if n_alias else {},
        compiler_params=_MIXER_PARAMS,
        name=name,
    )(*args)
    return [outs[sum(n_out[:i]):sum(n_out[:i + 1])] for i in range(len(mixers))]


def _ssd_group(ins, outs, scr, *, q, valid, lowp):
    z_ref, x_ref, bc_ref, sc_ref, sr_ref, conv0_ref, h0_ref, *consts = ins
    y_ref, convo_ref, ho_ref = outs
    xp_scr, h_scr, y_scr = scr

    def init(v):
        v[9][CONV_PAD - 3:CONV_PAD, :] = v[5][...]
        v[10][...] = v[6][...]

    def final(v):
        v[8][...] = v[10][...]

    return (init, functools.partial(_ssd_main, q=q, valid=valid, lowp=lowp), final,
            (z_ref, x_ref, bc_ref, sc_ref, sr_ref, conv0_ref.at[0], h0_ref.at[0],
             y_ref, ho_ref.at[0], xp_scr, h_scr, y_scr, convo_ref.at[0]),
            tuple(consts))


def _ssd_main(v, k, *, q, valid, lowp):
    z_ref, x_ref, bc_ref, sc_ref, sr_ref, _, _, y_ref, _, xp_scr, h_scr, y_scr, convo_ref = v
    cw_ref, cb_ref, bcol_ref, brow_ref, alc_ref, alr_ref, dexp_ref, nw_ref, e_ref, tl_ref, tu_ref = k
    xp_scr[CONV_PAD:CONV_PAD + q, 0:BRANCH_W] = x_ref[...].astype(F32)
    xp_scr[CONV_PAD:CONV_PAD + q, BRANCH_W:SSD_CONV_CH] = bc_ref[...]
    conv, carry = _conv_chunk(xp_scr, cw_ref, cb_ref, q, valid)
    convo_ref[...] = carry
    yield
    xbc = _silu(conv)
    xs = xbc[:, 0:BRANCH_W]
    bm = xbc[:, BRANCH_W:BRANCH_W + 128]
    cm = xbc[:, BRANCH_W + 128:BRANCH_W + 256]

    dt_c = _softplus(sc_ref[...] + bcol_ref[...])
    a_c = dt_c * (-jnp.exp(alc_ref[...]))
    dt_r = _softplus(sr_ref[...] + brow_ref[...])
    a_r = dt_r * (-jnp.exp(alr_ref[...]))
    if valid < q:
        tcol = lax.broadcasted_iota(jnp.int32, (q, 1), 0)
        trow = lax.broadcasted_iota(jnp.int32, (1, q), 1)
        a_c = jnp.where(tcol < valid, a_c, 0.0)
        dt_c = jnp.where(tcol < valid, dt_c, 0.0)
        a_r = jnp.where(trow < valid, a_r, 0.0)

    tl = tl_ref[...]
    acum_c = _cumsum_rows(tl, a_c, lowp)
    acum_r = _xdot(a_r, tu_ref[...], NN, 0, lowp)
    e = e_ref[...]
    dt_e = _xdot(dt_c, e, NN, 0, lowp)
    acum_e = _xdot(acum_c, e, NN, 0, lowp)
    xdt = xs * dt_e
    xend = xdt * jnp.exp(acum_e[q - 1:q, :] - acum_e)
    eac = jnp.exp(acum_e)
    dec_last = jnp.exp(acum_c[q - 1:q, :])
    yield

    fine = q < 16
    tri = (lax.broadcasted_iota(jnp.int32, (q, q), 0) >= lax.broadcasted_iota(jnp.int32, (q, q), 1))
    rpg = SSD_HEADS // SSD_GROUPS
    for g in range(SSD_GROUPS):
        bg = bm[:, g * SSD_STATE:(g + 1) * SSD_STATE]
        cg = cm[:, g * SSD_STATE:(g + 1) * SSD_STATE]
        cb = _mm(cg, bg, lowp, NT)
        for r in range(rpg):
            h = g * rpg + r
            hs = slice(h * SSD_HEADDIM, (h + 1) * SSD_HEADDIM)
            seg = acum_c[:, h:h + 1] - acum_r[h:h + 1, :]
            m = cb * jnp.exp(jnp.where(tri, seg, NEG))
            hp = h_scr[h]
            yh = _mm(m, xdt[:, hs], lowp) + _mm(cg, hp, lowp, NT) * eac[:, hs]
            y_scr[:, hs] = yh
            if fine:
                yield
            h_scr[h] = hp * dec_last[:, h:h + 1] + _mm(xend[:, hs], bg, lowp, TN)
            yield

    y = y_scr[...] + xs * dexp_ref[...]
    y = y * _silu(z_ref[...].astype(F32))
    y = y * lax.rsqrt(jnp.mean(y * y, -1, keepdims=True) + EPS) * nw_ref[...]
    y_ref[...] = y.astype(BF16)


def _ssd(proj, proj32, srow, conv_state, h_state, layer, w, consts, *, b, q, nc, valid, lowp, bs):
    return _Mixer(
        in_specs=[
            _col_spec(bs, q, COL_A_Z, BRANCH_W),
            _col_spec(bs, q, COL_A_X, BRANCH_W),
            _col_spec(bs, q, COL32_A_BC, 256),
            _col_spec(bs, q, COL32_SMALL, 128),
            _row_spec(bs, q),
            _state_in_spec(layer, bs, (CONV_W - 1, SSD_CONV_CH)),
            _state_in_spec(layer, bs, (SSD_HEADS, SSD_HEADDIM, SSD_STATE)),
            _const_spec((CONV_W, SSD_CONV_CH)),
            _const_spec((1, SSD_CONV_CH)),
            _const_spec((1, 128)),
            _const_spec((N_SMALL, 1)),
            _const_spec((1, 128)),
            _const_spec((N_SMALL, 1)),
            _const_spec((1, BRANCH_W)),
            _const_spec((1, BRANCH_W)),
            _const_spec((128, BRANCH_W)),
            _const_spec((q, q)),
            _const_spec((q, q)),
        ],
        out_specs=[
            _y_spec(bs, q),
            _state_out_spec(layer, bs, (CONV_W - 1, SSD_CONV_CH)),
            _state_out_spec(layer, bs, (SSD_HEADS, SSD_HEADDIM, SSD_STATE)),
        ],
        out_shapes=[
            jax.ShapeDtypeStruct((b, nc * q, BRANCH_W), BF16),
            jax.ShapeDtypeStruct((DEPTH, b, CONV_W - 1, SSD_CONV_CH), F32),
            jax.ShapeDtypeStruct((DEPTH, b, SSD_HEADS, SSD_HEADDIM, SSD_STATE), F32),
        ],
        scratch=[
            pltpu.VMEM((bs, q + CONV_PAD, SSD_CONV_CH), F32),
            pltpu.VMEM((bs, SSD_HEADS, SSD_HEADDIM, SSD_STATE), F32),
            pltpu.VMEM((bs, q, BRANCH_W), F32),
        ],
        args=[proj, proj, proj32, proj32, srow, conv_state, h_state,
              w['ssd_conv_w'], w['ssd_conv_b'], w['bias_col'], w['bias_row'], w['alog_col'], w['alog_row'],
              w['ssd_D_exp'], w['ssd_norm_w'], consts['expand'], consts['tril'], consts['triu']],
        group=functools.partial(_ssd_group, q=q, valid=valid, lowp=lowp))


def _s5_kernel(u_ref, gate_ref, hr0_ref, hi0_ref, perm_ref, permt_ref, ar_ref, ai_ref,
               wbr_ref, wbi_ref, wcr_ref, wci_ref, d_ref, glu_ref,
               y_ref, hro_ref, hio_ref,
               hr_scr, hi_scr, cr_scr, ci_scr, *, q, nc, ngrp):
    c = pl.program_id(1)

    @pl.when(c == 0)
    def _():
        cr_scr[...] = hr0_ref[0]
        ci_scr[...] = hi0_ref[0]

    rows_in = perm_ref.shape[1]
    perm = perm_ref[...]
    u_tm = jnp.dot(perm, u_ref[...].reshape(rows_in, BRANCH_W), preferred_element_type=F32).astype(BF16)
    gate_tm = jnp.dot(perm, gate_ref[...].reshape(rows_in, BRANCH_W), preferred_element_type=F32)
    nblk = S5_LANES // 512
    for j in range(nblk):
        uj = u_tm[:, j * 128:(j + 1) * 128]
        hr_scr[:, j * 512:(j + 1) * 512] = jnp.dot(uj, wbr_ref[j], preferred_element_type=F32)
        hi_scr[:, j * 512:(j + 1) * 512] = jnp.dot(uj, wbi_ref[j], preferred_element_type=F32)

    half = S5_LANES // 2
    for hf in range(2):
        sl = slice(hf * half, (hf + 1) * half)
        ar = ar_ref[:, sl]
        ai = ai_ref[:, sl]

        def grp_body(g, carry, sl=sl, ar=ar, ai=ai):
            s0 = pl.multiple_of(g * 8, 8)

            def t_body(t, h):
                r0 = pl.multiple_of(g * (8 * q) + t * 8, 8)
                hr, hi = h
                nr = ar * hr - ai * hi + hr_scr[pl.ds(r0, 8), sl]
                ni = ar * hi + ai * hr + hi_scr[pl.ds(r0, 8), sl]
                hr_scr[pl.ds(r0, 8), sl] = nr
                hi_scr[pl.ds(r0, 8), sl] = ni
                return nr, ni

            h = lax.fori_loop(0, q, t_body, (cr_scr[pl.ds(s0, 8), sl], ci_scr[pl.ds(s0, 8), sl]),
                              unroll=min(q, 4))
            cr_scr[pl.ds(s0, 8), sl] = h[0]
            ci_scr[pl.ds(s0, 8), sl] = h[1]
            return carry

        lax.fori_loop(0, ngrp, grp_body, 0)

    parts = []
    for j in range(nblk):
        sl = slice(j * 512, (j + 1) * 512)
        parts.append(jnp.dot(hr_scr[:, sl].astype(BF16), wcr_ref[j], preferred_element_type=F32)
                     - jnp.dot(hi_scr[:, sl].astype(BF16), wci_ref[j], preferred_element_type=F32))
    y = jnp.concatenate(parts, axis=1) + u_tm.astype(F32) * d_ref[...]
    g = jax.nn.gelu(y)
    y = g * _sigmoid(jnp.dot(g.astype(BF16), glu_ref[...], preferred_element_type=F32))
    y = (y * _silu(gate_tm)).astype(BF16)
    y = jnp.dot(permt_ref[...], y, preferred_element_type=F32).astype(BF16)
    y_ref[...] = y.reshape(y_ref.shape)

    @pl.when(c == nc - 1)
    def _():
        hro_ref[0] = cr_scr[...]
        hio_ref[0] = ci_scr[...]


def _s5_perm(bb, lp, valid):
    perm = np.zeros((bb * valid, bb * lp), np.float32)
    for g in range(bb // 8):
        for t in range(valid):
            for s in range(8):
                perm[g * 8 * valid + t * 8 + s, (g * 8 + s) * lp + t] = 1.0
    return jnp.asarray(perm, BF16), jnp.asarray(perm.T, BF16)


def _s5(proj, hr_state, hi_state, layer, w, prev, *, b, lp, q, valid, bb):
    nc = lp // q
    perm, permt = _s5_perm(bb, q, valid)
    rows = bb * valid
    if nc == 1:
        def col_spec(col):
            return pl.BlockSpec((bb * q, BRANCH_W), lambda i, c: (i, col // BRANCH_W))
        src = proj
        y_spec = pl.BlockSpec((bb * q, BRANCH_W), lambda i, c: (i, 0))
        y_shape = jax.ShapeDtypeStruct((b * lp, BRANCH_W), BF16)
    else:
        def col_spec(col):
            return pl.BlockSpec((bb, q, BRANCH_W), lambda i, c: (i, c, col // BRANCH_W))
        src = proj.reshape(b, lp, N_PROJ)
        y_spec = pl.BlockSpec((bb, q, BRANCH_W), lambda i, c: (i, c, 0))
        y_shape = jax.ShapeDtypeStruct((b, lp, BRANCH_W), BF16)
    st_in = pl.BlockSpec((1, bb, S5_LANES), lambda i, c: (layer, i, 0))
    st_out = st_in
    kern = functools.partial(_s5_kernel, q=valid, nc=nc, ngrp=bb // 8)
    n_in = 14
    alias_kw = {}
    if prev is not None:
        base = kern

        def kern(*refs):
            return base(*refs[:n_in], *refs[n_in + 2:])

        alias_kw = dict(input_output_aliases={n_in: 1, n_in + 1: 2})
    y, hr, hi = pl.pallas_call(
        kern,
        grid=(b // bb, nc),
        **alias_kw,
        in_specs=[
            col_spec(COL_B_U),
            col_spec(COL_B_GATE),
            st_in, st_in,
            _const_spec(perm.shape),
            _const_spec(permt.shape),
            _const_spec((8, S5_LANES)),
            _const_spec((8, S5_LANES)),
            _const_spec((4, 128, 512)),
            _const_spec((4, 128, 512)),
            _const_spec((4, 512, 128)),
            _const_spec((4, 512, 128)),
            _const_spec((1, BRANCH_W)),
            _const_spec((BRANCH_W, BRANCH_W)),
        ] + [pl.BlockSpec(memory_space=pl.ANY)] * (0 if prev is None else 2),
        out_specs=[y_spec, st_out, st_out],
        out_shape=[
            y_shape,
            jax.ShapeDtypeStruct((DEPTH, b, S5_LANES), F32),
            jax.ShapeDtypeStruct((DEPTH, b, S5_LANES), F32),
        ],
        scratch_shapes=[
            pltpu.VMEM((rows, S5_LANES), F32),
            pltpu.VMEM((rows, S5_LANES), F32),
            pltpu.VMEM((bb, S5_LANES), F32),
            pltpu.VMEM((bb, S5_LANES), F32),
        ],
        compiler_params=_MIXER_PARAMS,
        name="s5",
    )(src, src, hr_state, hi_state, perm, permt, w['s5_ar'], w['s5_ai'],
      w['s5_wbr'], w['s5_wbi'], w['s5_wcr'], w['s5_wci'], w['s5_D'], w['s5_glu_w'], *(prev or ()))
    return y.reshape(b * lp, BRANCH_W), hr, hi


def _mlstm_group(ins, outs, scr, *, q, valid, lowp):
    x_ref, z_ref, o_ref, sc_ref, sr_ref, conv0_ref, c0_ref, n0_ref, m0_ref, *consts = ins
    y_ref, convo_ref, co_ref, no_ref, mo_ref = outs
    xp_scr, c_scr, n_scr, m_scr, h_scr = scr

    def init(v):
        v[13][CONV_PAD - 3:CONV_PAD, :] = v[5][...]
        v[14][...] = v[6][...]
        v[15][...] = v[7][...]
        v[16][...] = v[8][...]

    def final(v):
        v[10][...] = v[14][...]
        v[11][...] = v[15][...]
        v[12][...] = v[16][...]

    return (init, functools.partial(_mlstm_main, q=q, valid=valid, lowp=lowp), final,
            (x_ref, z_ref, o_ref, sc_ref, sr_ref, conv0_ref.at[0], c0_ref.at[0], n0_ref.at[0], m0_ref.at[0],
             y_ref, co_ref.at[0], no_ref.at[0], mo_ref.at[0], xp_scr, c_scr, n_scr, m_scr, h_scr,
             convo_ref.at[0]),
            tuple(consts))


def _mlstm_main(v, k, *, q, valid, lowp):
    (x_ref, z_ref, o_ref, sc_ref, sr_ref, _, _, _, _, y_ref, _, _, _,
     xp_scr, c_scr, n_scr, m_scr, h_scr, convo_ref) = v
    cw_ref, cb_ref, wq_ref, wk_ref, wv_ref, bcol_ref, brow_ref, nw_ref, skip_ref, tl_ref, tu_ref = k
    x = x_ref[...].astype(F32)
    xp_scr[CONV_PAD:CONV_PAD + q, :] = x
    conv, carry = _conv_chunk(xp_scr, cw_ref, cb_ref, q, valid)
    convo_ref[...] = carry
    yield
    xc = _silu(conv)

    pre_c = sc_ref[...] + bcol_ref[...]
    pre_r = sr_ref[...] + brow_ref[...]
    ig_c = pre_c
    lf_c = -_softplus(-pre_c)
    ig_r = pre_r
    lf_r = -_softplus(-pre_r)
    if valid < q:
        tcol = lax.broadcasted_iota(jnp.int32, (q, 1), 0)
        trow = lax.broadcasted_iota(jnp.int32, (1, q), 1)
        lf_c = jnp.where(tcol < valid, lf_c, 0.0)
        ig_c = jnp.where(tcol < valid, ig_c, NEG)
        lf_r = jnp.where(trow < valid, lf_r, 0.0)
        ig_r = jnp.where(trow < valid, ig_r, NEG)
    b_c = _cumsum_rows(tl_ref[...], lf_c, lowp)
    b_r = _xdot(lf_r, tu_ref[...], NN, 0, lowp)
    yield

    fine = q < 16
    tri = (lax.broadcasted_iota(jnp.int32, (q, q), 0) >= lax.broadcasted_iota(jnp.int32, (q, q), 1))
    o_all = o_ref[...].astype(F32)
    for h in range(ML_HEADS):
        hs = slice(h * ML_HEADDIM, (h + 1) * ML_HEADDIM)
        xh = xc[:, hs]
        qh = _mm(xh, wq_ref[h], lowp)
        kh = _mm(xh, wk_ref[h], lowp) * (ML_HEADDIM ** -0.5)
        vh = _mm(x[:, hs], wv_ref[h], lowp)
        yield
        bc = b_c[:, SMALL_F + h:SMALL_F + h + 1]
        br = b_r[SMALL_F + h:SMALL_F + h + 1, :]
        ic = ig_c[:, SMALL_I + h:SMALL_I + h + 1]
        ir = ig_r[SMALL_I + h:SMALL_I + h + 1, :]
        mp = m_scr[h:h + 1, 0:1]
        bq = jnp.broadcast_to(bc, (q, q))
        bc = bq if q == ML_HEADDIM else jnp.broadcast_to(bc, (q, ML_HEADDIM))
        ic = jnp.broadcast_to(ic, (q, ML_HEADDIM))
        dlog = jnp.where(tri, bq - br + ir, NEG)
        mrow = jnp.max(dlog, axis=1, keepdims=True)
        inter = bc + mp
        m_t = jnp.maximum(inter, mrow)
        wgt = jnp.exp(dlog - (m_t if q == ML_HEADDIM else jnp.maximum(bq + mp, mrow)))
        yield
        s = _mm(qh, kh, lowp, NT) * wgt
        if fine:
            yield
        scale = jnp.exp(inter - m_t)
        cp = c_scr[h]
        npv = n_scr[h:h + 1, :]
        num = _mm(s, vh, lowp) + scale * _mm(qh, cp, lowp)
        dot =jnp.sum(s, axis=1, keepdims=True) + scale * jnp.sum(qh * npv, axis=1, keepdims=True)
        hh = num / jnp.maximum(jnp.abs(dot), jnp.exp(-m_t))
        yield
        m_new = m_t[q - 1:q, :]
        b_last = bc[q - 1:q, :]
        w_end = jnp.exp(b_last - bc + ic - m_new)
        cs = jnp.exp(b_last + mp - m_new)
        if fine:
            yield
        kw = kh * w_end
        c_scr[h] = cs * cp + _mm(kw, vh, lowp, TN)
        n_scr[h:h + 1, :] = cs * npv + jnp.sum(kw, axis=0, keepdims=True)
        m_scr[h:h + 1, :] = jnp.broadcast_to(m_new, (1, 128))
        yield
        hh = hh * _sigmoid(o_all[:, hs])
        hc = hh - jnp.mean(hh, -1, keepdims=True)
        h_scr[:, hs] = hc * lax.rsqrt(jnp.mean(hc * hc, -1, keepdims=True) + EPS)
        yield

    y = h_scr[...] * nw_ref[...] + skip_ref[...] * xc
    y_ref[...] = (y * _silu(z_ref[...].astype(F32))).astype(BF16)


def _mlstm(proj, proj32, srow, conv_state, c_state, n_state, m_state, layer, w, consts,
           *, b, q, nc, valid, lowp, bs):
    return _Mixer(
        in_specs=[
            _col_spec(bs, q, COL_C_X, BRANCH_W),
            _col_spec(bs, q, COL_C_Z, BRANCH_W),
            _col_spec(bs, q, COL_C_O, BRANCH_W),
            _col_spec(bs, q, COL32_SMALL, 128),
            _row_spec(bs, q),
            _state_in_spec(layer, bs, (CONV_W - 1, BRANCH_W)),
            _state_in_spec(layer, bs, (ML_HEADS, ML_HEADDIM, ML_HEADDIM)),
            _state_in_spec(layer, bs, (ML_HEADS, ML_HEADDIM)),
            _state_in_spec(layer, bs, (8, 128)),
            _const_spec((CONV_W, BRANCH_W)),
            _const_spec((1, BRANCH_W)),
            _const_spec((ML_HEADS, ML_HEADDIM, ML_HEADDIM)),
            _const_spec((ML_HEADS, ML_HEADDIM, ML_HEADDIM)),
            _const_spec((ML_HEADS, ML_HEADDIM, ML_HEADDIM)),
            _const_spec((1, 128)),
            _const_spec((N_SMALL, 1)),
            _const_spec((1, BRANCH_W)),
            _const_spec((1, BRANCH_W)),
            _const_spec((q, q)),
            _const_spec((q, q)),
        ],
        out_specs=[
            _y_spec(bs, q),
            _state_out_spec(layer, bs, (CONV_W - 1, BRANCH_W)),
            _state_out_spec(layer, bs, (ML_HEADS, ML_HEADDIM, ML_HEADDIM)),
            _state_out_spec(layer, bs, (ML_HEADS, ML_HEADDIM)),
            _state_out_spec(layer, bs, (8, 128)),
        ],
        out_shapes=[
            jax.ShapeDtypeStruct((b, nc * q, BRANCH_W), BF16),
            jax.ShapeDtypeStruct((DEPTH, b, CONV_W - 1, BRANCH_W), F32),
            jax.ShapeDtypeStruct((DEPTH, b, ML_HEADS, ML_HEADDIM, ML_HEADDIM), F32),
            jax.ShapeDtypeStruct((DEPTH, b, ML_HEADS, ML_HEADDIM), F32),
            jax.ShapeDtypeStruct((DEPTH, b, 8, 128), F32),
        ],
        scratch=[
            pltpu.VMEM((bs, q + CONV_PAD, BRANCH_W), F32),
            pltpu.VMEM((bs, ML_HEADS, ML_HEADDIM, ML_HEADDIM), F32),
            pltpu.VMEM((bs, ML_HEADS, ML_HEADDIM), F32),
            pltpu.VMEM((bs, 8, 128), F32),
            pltpu.VMEM((bs, q, BRANCH_W), F32),
        ],
        args=[proj, proj, proj, proj32, srow, conv_state, c_state, n_state, m_state,
              w['ml_conv_w'], w['ml_conv_b'], w['ml_wq'], w['ml_wk'], w['ml_wv'], w['bias_col'], w['bias_row'],
              w['ml_norm_w'], w['ml_skip'], consts['tril'], consts['triu']],
        group=functools.partial(_mlstm_group, q=q, valid=valid, lowp=lowp))


def _hgrn_group(ins, outs, scr, *, q, valid, lowp, nlev):
    f_ref, i_ref, q_ref, g_ref, s0_ref, *consts = ins
    y_ref, so_ref = outs
    s_scr, o_scr = scr

    def init(v):
        v[7][...] = v[4][...]

    def final(v):
        v[6][...] = v[7][...]

    return (init, functools.partial(_hgrn_main, q=q, valid=valid, lowp=lowp, nlev=nlev), final,
            (f_ref, i_ref, q_ref, g_ref, s0_ref.at[0], y_ref, so_ref.at[0], s_scr, o_scr),
            tuple(consts))


def _hgrn_main(v, k, *, q, valid, lowp, nlev):
    f_ref, i_ref, q_ref, g_ref, _, y_ref, _, s_scr, o_scr = v
    lb_ref, nw_ref, tl_ref = k
    lb = lb_ref[...]
    fg = lb + (1.0 - lb) * _sigmoid(f_ref[...])
    kk = 1.0 - fg
    qq = _silu(q_ref[...].astype(F32)) * (HG_HEADDIM ** -0.5)
    vv = i_ref[...] if lowp else i_ref[...].astype(F32)
    rcol =lax.broadcasted_iota(jnp.int32, (q, 1), 0)
    if valid < q:
        fg = jnp.where(rcol < valid, fg, 1.0)
        kk = jnp.where(rcol < valid, kk, 0.0)
    lf = jnp.log(fg)

    gcum = _cumsum_rows(tl_ref[...], lf, lowp)
    yield

    fine = q < 16
    rt = lax.broadcasted_iota(jnp.int32, (q, q), 0)
    cs = lax.broadcasted_iota(jnp.int32, (q, q), 1)
    rx = rt ^ cs
    att = [None] * HG_HEADS
    for lev in range(nlev):
        m = q >> (lev + 1)
        second = (rcol & m) != 0
        if m >= 4:
            nb = q // (2 * m)
            gb = jnp.broadcast_to(gcum.reshape(nb, 2 * m, BRANCH_W)[:, m - 1:m, :], (nb, 2 * m, BRANCH_W))
            e = jnp.exp(-jnp.abs(gcum - gb.reshape(q, BRANCH_W)))
        elif m == 2:
            r4 = rcol & 3
            e = jnp.where(r4 == 0, pltpu.roll(fg, q - 1, 0),
                          jnp.where(r4 == 1, 1.0, jnp.where(r4 == 2, fg, fg * pltpu.roll(fg, 1, 0))))
        else:
            e = jnp.where(second, fg, 1.0)
        u = jnp.where(second, qq, kk) * e
        if lowp:
            u = u.astype(BF16)
        pair = (rx >> int(math.log2(m))) == 1
        for h in range(HG_HEADS):
            hs = slice(h * HG_HEADDIM, (h + 1) * HG_HEADDIM)
            s = lax.dot_general(u[:, hs], u[:, hs], NT, preferred_element_type=F32)
            att[h] = jnp.where(pair, s, 0.0 if att[h] is None else att[h])
        yield

    qg = qq * jnp.exp(gcum)
    kend = kk * jnp.exp(gcum[q - 1:q, :] - gcum)
    ones = jnp.ones((q, 128), F32)
    for h in range(HG_HEADS):
        hs = slice(h * HG_HEADDIM, (h + 1) * HG_HEADDIM)
        diag = jnp.sum(qq[:, hs] * kk[:, hs], axis=1, keepdims=True)
        a = jnp.where(rt > cs, att[h], jnp.where(rt == cs, diag, 0.0))
        sp = s_scr[h]
        oh = _mm(a, vv[:, hs], lowp) + _mm(qg[:, hs], sp, lowp)
        if fine:
            yield
        dec = jnp.exp(_xdot(lf[:, hs], ones, TN, 0, lowp))
        s_scr[h] =sp * dec + _mm(kend[:, hs], vv[:, hs], lowp, TN)
        o_scr[:, hs] = oh * lax.rsqrt(jnp.mean(oh * oh, -1, keepdims=True) + EPS)
        yield

    y_ref[...] = (o_scr[...] * nw_ref[...] * _silu(g_ref[...].astype(F32))).astype(BF16)


def _hgrn(proj, proj32, s_state, layer, w, consts, *, b, q, nc, valid, lowp, bs):
    nlev = int(math.log2(q))
    return _Mixer(
        in_specs=[
            _col_spec(bs, q, COL32_D_F, BRANCH_W),
            _col_spec(bs, q, COL_D_I, BRANCH_W),
            _col_spec(bs, q, COL_D_Q, BRANCH_W),
            _col_spec(bs, q, COL_D_G, BRANCH_W),
            _state_in_spec(layer, bs, (HG_HEADS, HG_HEADDIM, HG_HEADDIM)),
            _const_spec((1, BRANCH_W)),
            _const_spec((1, BRANCH_W)),
            _const_spec((q, q)),
        ],
        out_specs=[
            _y_spec(bs, q),
            _state_out_spec(layer, bs, (HG_HEADS, HG_HEADDIM, HG_HEADDIM)),
        ],
        out_shapes=[
            jax.ShapeDtypeStruct((b, nc * q, BRANCH_W), BF16),
            jax.ShapeDtypeStruct((DEPTH, b, HG_HEADS, HG_HEADDIM, HG_HEADDIM), F32),
        ],
        scratch=[
            pltpu.VMEM((bs, HG_HEADS, HG_HEADDIM, HG_HEADDIM), F32),
            pltpu.VMEM((bs, q, BRANCH_W), F32),
        ],
        args=[proj32, proj, proj, proj, s_state, w['hg_lb'], w['hg_norm_w'], consts['tril']],
        group=functools.partial(_hgrn_group, q=q, valid=valid, lowp=lowp, nlev=nlev))


def _chunk_consts(q):
    r = np.arange(q)
    tril = (r[:, None] >= r[None, :]).astype(np.float32)
    expand = np.zeros((128, BRANCH_W), np.float32)
    for h in range(SSD_HEADS):
        expand[h, h * SSD_HEADDIM:(h + 1) * SSD_HEADDIM] = 1.0
    return {'tril': jnp.asarray(tril), 'triu': jnp.asarray(tril.T), 'expand': jnp.asarray(expand)}


def _cmul(ar, ai, br, bi):
    return ar * br - ai * bi, ar * bi + ai * br


def _layer_params(l, p, hg_lb):
    w_in_t = p['w_in'][l].T
    starts = np.cumsum([0] + [n for _, n in _IN_SPLITS])

    def rows(first, last):
        names = [name for name, _ in _IN_SPLITS]
        return w_in_t[starts[names.index(first)]:starts[names.index(last) + 1]]

    small = jnp.concatenate([rows('a_dt', 'a_dt'), rows('c_i', 'c_f')], axis=0)
    w_perm = jnp.concatenate([
        rows('merge', 'merge'), rows('a_z', 'a_x'), rows('b_u', 'b_gate'), rows('c_x', 'c_o'), rows('d_i', 'd_g'),
    ], axis=0).astype(BF16)
    w_perm32 = jnp.concatenate([
        rows('d_f', 'd_f'), rows('a_bc', 'a_bc'), small,
        jnp.zeros((N_PROJ32 - COL32_SMALL - N_SMALL, D_MODEL), F32),
    ], axis=0).astype(BF16)

    def lane_pad(v, off):
        return jnp.zeros((1, 128), F32).at[0, off:off + v.shape[0]].set(v)

    bias_col = (lane_pad(p['ssd_dt_bias'][l], SMALL_DT) + lane_pad(p['ml_i_bias'][l], SMALL_I)
                + lane_pad(p['ml_f_bias'][l], SMALL_F))
    alog_col = lane_pad(p['ssd_A_log'][l], SMALL_DT)

    dt = jnp.exp(p['s5_log_dt'][l])[:, None]
    lr = p['s5_A_re'][l]
    li = p['s5_A_im'][l]
    mag = jnp.exp(lr * dt)
    abr, abi = mag * jnp.cos(li * dt), mag * jnp.sin(li * dt)
    den = lr * lr + li * li
    cr = ((abr - 1.0) * lr + abi * li) / den
    ci = (abi * lr - (abr - 1.0) * li) / den
    bbr, bbi = _cmul(cr[..., None], ci[..., None], p['s5_B_re'][l], p['s5_B_im'][l])
    eye8 = jnp.eye(8, dtype=F32)

    def pack_b(bb):
        return jnp.einsum('jgpc,gh->jgchp', bb.reshape(4, 8, S5_STATE, S5_GROUP), eye8).reshape(4, 128, 512)

    def pack_c(cc):
        return jnp.einsum('jgcp,gh->jgphc', cc.reshape(4, 8, S5_GROUP, S5_STATE), eye8).reshape(4, 512, 128)

    return {
        'norm_w': p['norm_w'][l].reshape(1, D_MODEL),
        'w_in': w_perm, 'w_in32': w_perm32,
        'w_small_t': small.astype(BF16),
        'bias_col': bias_col, 'bias_row': bias_col[0, :N_SMALL].reshape(N_SMALL, 1),
        'alog_col': alog_col, 'alog_row': alog_col[0, :N_SMALL].reshape(N_SMALL, 1),
        'ssd_conv_w': p['ssd_conv_w'][l], 'ssd_conv_b': p['ssd_conv_b'][l].reshape(1, SSD_CONV_CH),
        'ssd_D_exp': jnp.repeat(p['ssd_D'][l], SSD_HEADDIM).reshape(1, BRANCH_W),
        'ssd_norm_w': p['ssd_norm_w'][l].reshape(1, BRANCH_W),
        's5_wbr': pack_b(bbr).astype(BF16), 's5_wbi': pack_b(bbi).astype(BF16),
        's5_wcr': pack_c(p['s5_C_re'][l]).astype(BF16), 's5_wci': pack_c(p['s5_C_im'][l]).astype(BF16),
        's5_ar': jnp.broadcast_to(abr.reshape(1, S5_LANES), (8, S5_LANES)),
        's5_ai': jnp.broadcast_to(abi.reshape(1, S5_LANES), (8, S5_LANES)),
        's5_D': p['s5_D'][l].reshape(1, BRANCH_W), 's5_glu_w': p['s5_glu_w'][l].astype(BF16),
        'ml_conv_w': p['ml_conv_w'][l], 'ml_conv_b': p['ml_conv_b'][l].reshape(1, BRANCH_W),
        'ml_wq': p['ml_wq'][l], 'ml_wk': p['ml_wk'][l], 'ml_wv': p['ml_wv'][l],
        'ml_norm_w': p['ml_norm_w'][l].reshape(1, BRANCH_W), 'ml_skip': p['ml_skip'][l].reshape(1, BRANCH_W),
        'hg_lb': hg_lb[l].reshape(1, BRANCH_W), 'hg_norm_w': p['hg_norm_w'][l].reshape(1, BRANCH_W),
        'w_branch': p['w_branch'][l].astype(BF16), 'w_out': p['w_out'][l].astype(BF16),
    }


_MATMUL_WEIGHTS = ('ml_wq', 'ml_wk', 'ml_wv')
S5_CHUNK = 64
MIXER_PLAN = [(('ssd', 'mlstm', 'hgrn'), 2)]
MIXER_PLAN_PADDED = [(('ssd', 'mlstm', 'hgrn'), 8)]


def _run_group(x3, states, layers, fnw, *, q, valid):
    b, lp, _ = x3.shape
    nc = lp // q
    t = b * lp
    lowp = q >= 16
    consts = _chunk_consts(q)
    ssd_conv, ssd_h, s5_re, s5_im, ml_conv, ml_c, ml_n, ml_m, hg_s = states
    s5_re = s5_re.reshape(DEPTH, b, S5_LANES)
    s5_im = s5_im.reshape(DEPTH, b, S5_LANES)
    if valid < q:
        s5_kw = dict(b=b, lp=lp, q=q, valid=valid, bb=min(b, 128))
    else:
        qs = math.gcd(lp, S5_CHUNK)
        s5_kw = dict(b=b, lp=lp, q=qs, valid=qs, bb=8)
    ml_m = jnp.broadcast_to(jnp.pad(ml_m, ((0, 0), (0, 0), (0, 8 - ML_HEADS)))[..., None], (DEPTH, b, 8, 128))
    plan = [(names, math.gcd(b, n)) for names, n in (MIXER_PLAN_PADDED if valid < q else MIXER_PLAN)]
    kw = dict(b=b, q=q, nc=nc, valid=valid, lowp=lowp)
    x = x3.reshape(t, D_MODEL)
    ys, st, st_s5 = {}, {}, None
    for l, w in enumerate(layers):
        if lowp:
            w = dict(w, **{k: w[k].astype(BF16) for k in _MATMUL_WEIGHTS})
        proj, proj32, small_t = _inproj(x, w['norm_w'], w['w_in'], w['w_in32'], w['w_small_t'])
        proj3 = proj.reshape(b, lp, N_PROJ)
        proj32 = proj32.reshape(b, lp, N_PROJ32)
        srow = small_t.reshape(N_SMALL, b, lp).transpose(1, 0, 2)
        mixers = {
            'ssd': lambda n: _ssd(proj3, proj32, srow, ssd_conv, ssd_h, l, w, consts, bs=n, **kw),
            'mlstm': lambda n: _mlstm(proj3, proj32, srow, ml_conv, ml_c, ml_n, ml_m, l, w, consts, bs=n, **kw),
            'hgrn': lambda n: _hgrn(proj3, proj32, hg_s, l, w, consts, bs=n, **kw),
        }
        for names, n in plan:
            outs = _launch([mixers[k](n) for k in names], None if l == 0 else [st[k] for k in names],
                           grid=(b // n, nc), bs=n, nc=nc, name='_'.join(names))
            for k, o in zip(names, outs):
                ys[k], st[k] = o[0], list(o[1:])
        yb, *st_s5 = _s5(proj, s5_re, s5_im, l, w, st_s5, **s5_kw)
        ya, yc, yd = (ys[k].reshape(t, BRANCH_W) for k in ('ssd', 'mlstm', 'hgrn'))
        x = _merge(proj, (ya, yb, yc, yd), x, w['w_branch'], w['w_out'], fnw, final=(l == DEPTH - 1))
    n_ssd_conv, n_ssd_h = st['ssd']
    n_re, n_im = (s.reshape(DEPTH, b, S5_GROUPS, S5_STATE) for s in st_s5)
    n_ml_conv, n_c, n_n, n_m = st['mlstm']
    new_states = (n_ssd_conv, n_ssd_h, n_re, n_im, n_ml_conv, n_c, n_n, n_m[:, :, :ML_HEADS, 0], st['hgrn'][0])
    return x.reshape(b, lp, D_MODEL), new_states


SAMPLE_PAD = 8


def kernel(x_prompt, x_sample, state_ssd_conv, state_ssd, state_s5_re, state_s5_im, state_mlstm_conv, state_mlstm_C, state_mlstm_n, state_mlstm_m, state_hgrn, norm_w, w_in, ssd_conv_w, ssd_conv_b, ssd_dt_bias, ssd_A_log, ssd_D, ssd_norm_w, s5_A_re, s5_A_im, s5_B_re, s5_B_im, s5_C_re, s5_C_im, s5_D, s5_log_dt, s5_glu_w, ml_conv_w, ml_conv_b, ml_wq, ml_wk, ml_wv, ml_i_bias, ml_f_bias, ml_norm_w, ml_skip, hg_lb_logits, hg_norm_w, w_branch, w_out, final_norm_w):
    p = {'norm_w': norm_w, 'w_in': w_in,
         'ssd_conv_w': ssd_conv_w, 'ssd_conv_b': ssd_conv_b, 'ssd_dt_bias': ssd_dt_bias,
         'ssd_A_log': ssd_A_log, 'ssd_D': ssd_D, 'ssd_norm_w': ssd_norm_w,
         's5_A_re': s5_A_re, 's5_A_im': s5_A_im, 's5_B_re': s5_B_re, 's5_B_im': s5_B_im,
         's5_C_re': s5_C_re, 's5_C_im': s5_C_im, 's5_D': s5_D, 's5_log_dt': s5_log_dt, 's5_glu_w': s5_glu_w,
         'ml_conv_w': ml_conv_w, 'ml_conv_b': ml_conv_b, 'ml_wq': ml_wq, 'ml_wk': ml_wk, 'ml_wv': ml_wv,
         'ml_i_bias': ml_i_bias, 'ml_f_bias': ml_f_bias, 'ml_norm_w': ml_norm_w, 'ml_skip': ml_skip,
         'hg_norm_w': hg_norm_w, 'w_branch': w_branch, 'w_out': w_out}
    lb_cum = jnp.cumsum(jax.nn.softmax(hg_lb_logits, axis=0), axis=0)
    hg_lb = lb_cum - lb_cum[0]
    layers = [_layer_params(l, p, hg_lb) for l in range(DEPTH)]
    fnw = final_norm_w.reshape(1, D_MODEL)

    sample_states = (state_ssd_conv, state_ssd, state_s5_re, state_s5_im, state_mlstm_conv,
                     state_mlstm_C, state_mlstm_n, state_mlstm_m, state_hgrn)
    bp, lp_, _ = x_prompt.shape
    prompt_states = tuple(jnp.zeros((DEPTH, bp) + s.shape[2:], F32) for s in sample_states)
    q_prompt = math.gcd(lp_, 128)
    y_prompt, new_p = _run_group(x_prompt, prompt_states, layers, fnw, q=q_prompt, valid=q_prompt)

    ls = x_sample.shape[1]
    xs_pad = jnp.pad(x_sample, ((0, 0), (0, SAMPLE_PAD - ls), (0, 0)))
    y_sample, new_s = _run_group(xs_pad, sample_states, layers, fnw, q=SAMPLE_PAD, valid=ls)
    y_sample = y_sample[:, :ls]

    out = [y_prompt, y_sample]
    for ps, ss in zip(new_p, new_s):
        out += [ps, ss]
    return tuple(out)
```

```python
import functools
import math
from typing import Callable, NamedTuple

import numpy as np
import jax
import jax.numpy as jnp
from jax import lax
from jax.experimental import pallas as pl
from jax.experimental.pallas import tpu as pltpu

F32 = jnp.float32
BF16 = jnp.bfloat16

D_MODEL = 1024
DEPTH = 2
BRANCH_W = 512
CONV_W = 4
EPS = 1e-6
SSD_HEADS = 8
SSD_HEADDIM = 64
SSD_STATE = 64
SSD_GROUPS = 2
SSD_CONV_CH = 768
S5_GROUPS = 32
S5_GROUP = 16
S5_STATE = 64
S5_LANES = S5_GROUPS * S5_STATE
ML_HEADS = 4
ML_HEADDIM = 128
HG_HEADS = 4
HG_HEADDIM = 128
N_BRANCH = 4

NEG = -1e30

VMEM_LIMIT_BYTES = 48 * 1024 * 1024
INPROJ_ROW_TILE = 1024
INPROJ_COL_STEPS = 4
MERGE_ROW_TILE = 512

_IN_SPLITS = (('a_z', 512), ('a_x', 512), ('a_bc', 256), ('a_dt', 8), ('b_u', 512), ('b_gate', 512),
              ('c_x', 512), ('c_z', 512), ('c_o', 512), ('c_i', 4), ('c_f', 4),
              ('d_f', 512), ('d_i', 512), ('d_q', 512), ('d_g', 512), ('merge', 4096))

N_PROJ = 9216
COL_MERGE = 0
COL_A_Z = 4096
COL_A_X = 4608
COL_B_U = 5120
COL_B_GATE = 5632
COL_C_X = 6144
COL_C_Z = 6656
COL_C_O = 7168
COL_D_I = 7680
COL_D_Q = 8192
COL_D_G = 8704
N_PROJ32 = 896
COL32_D_F = 0
COL32_A_BC = 512
COL32_SMALL = 768
N_SMALL = 16
SMALL_DT = 0
SMALL_I = 8
SMALL_F = 12

NN = (((1,), (0,)), ((), ()))
NT = (((1,), (1,)), ((), ()))
TN = (((0,), (0,)), ((), ()))


def _mm(a, b, lowp, dims=NN):
    if lowp:
        a = a.astype(BF16)
        b = b.astype(BF16)
    return lax.dot_general(a, b, dims, preferred_element_type=F32)


def _split3(a):
    hi = a.astype(BF16)
    r = a - hi.astype(F32)
    mid = r.astype(BF16)
    lo = (r - mid.astype(F32)).astype(BF16)
    return hi, mid, lo


def _xdot(a, b, dims, split, lowp):
    if not lowp and min((b if split == 0 else a).shape) < 8:
        return lax.dot_general(a, b, dims, precision=lax.Precision.HIGHEST,
                               preferred_element_type=F32)
    if split == 0:
        other = b.astype(BF16)
        return sum(lax.dot_general(p, other, dims, preferred_element_type=F32) for p in _split3(a))
    other = a.astype(BF16)
    return sum(lax.dot_general(other, p, dims, preferred_element_type=F32) for p in _split3(b))


def _cumsum_rows(tl, x, lowp):
    q = x.shape[0]
    if q != 8:
        return _xdot(tl, x, NN, 1, lowp)
    r = lax.broadcasted_iota(jnp.int32, (q, 1), 0)
    for d in (1, 2, 4):
        x = x + jnp.where(r >= d, pltpu.roll(x, d, 0), 0.0)
    return x


def _sigmoid(x):
    return 0.5 * jnp.tanh(0.5 * x) + 0.5


def _silu(x):
    h = 0.5 * x
    return h * jnp.tanh(h) + h


def _softplus(x):
    return jnp.maximum(x, 0.0) + jnp.log1p(jnp.exp(-jnp.abs(x)))


CONV_PAD = 8


def _conv_chunk(xp_scr, cw_ref, cb_ref, q, valid):
    cw = cw_ref[...]
    p = CONV_PAD
    xp = xp_scr[...]
    conv = (cb_ref[...]
            + cw[0:1] * pltpu.roll(xp, 3, 0)[p:p + q, :]
            + cw[1:2] * pltpu.roll(xp, 2, 0)[p:p + q, :]
            + cw[2:3] * pltpu.roll(xp, 1, 0)[p:p + q, :]
            + cw[3:4] * xp[p:p + q, :])
    carry = xp_scr[p - 3 + valid:p + valid, :]
    xp_scr[p - 3:p, :] = carry
    return conv, carry


def _inproj_kernel(x_ref, nw_ref, w_ref, w32_ref, wst_ref, o_ref, o32_ref, st_ref, xn_scr):
    @pl.when(pl.program_id(1) == 0)
    def _():
        x = x_ref[...]
        xn = x * lax.rsqrt(jnp.mean(x * x, -1, keepdims=True) + EPS) * nw_ref[...]
        xb = xn.astype(BF16)
        xn_scr[...] = xb
        o32_ref[...] = lax.dot_general(xb, w32_ref[...], NT, preferred_element_type=F32)
        st_ref[...] = lax.dot_general(wst_ref[...], xb, NT, preferred_element_type=F32)

    o_ref[...] = lax.dot_general(xn_scr[...], w_ref[...], NT, preferred_element_type=F32).astype(BF16)


def _inproj(x, nw, w, w32, wst):
    t = x.shape[0]
    tm = min(INPROJ_ROW_TILE, t)
    tn = N_PROJ // INPROJ_COL_STEPS
    return pl.pallas_call(
        _inproj_kernel,
        grid=(t // tm, N_PROJ // tn),
        in_specs=[
            pl.BlockSpec((tm, D_MODEL), lambda i, j: (i, 0)),
            pl.BlockSpec((1, D_MODEL), lambda i, j: (0, 0)),
            pl.BlockSpec((tn, D_MODEL), lambda i, j: (j, 0)),
            pl.BlockSpec((N_PROJ32, D_MODEL), lambda i, j: (0, 0)),
            pl.BlockSpec((N_SMALL, D_MODEL), lambda i, j: (0, 0)),
        ],
        out_specs=[
            pl.BlockSpec((tm, tn), lambda i, j: (i, j)),
            pl.BlockSpec((tm, N_PROJ32), lambda i, j: (i, 0)),
            pl.BlockSpec((N_SMALL, tm), lambda i, j: (0, i)),
        ],
        out_shape=[
            jax.ShapeDtypeStruct((t, N_PROJ), BF16),
            jax.ShapeDtypeStruct((t, N_PROJ32), F32),
            jax.ShapeDtypeStruct((N_SMALL, t), F32),
        ],
        scratch_shapes=[pltpu.VMEM((tm, D_MODEL), BF16)],
        compiler_params=pltpu.CompilerParams(
            dimension_semantics=("arbitrary", "arbitrary"),
            vmem_limit_bytes=VMEM_LIMIT_BYTES),
        name="inproj",
    )(x, nw, w, w32, wst)


def _merge_kernel(g_ref, ya_ref, yb_ref, yc_ref, yd_ref, x_ref, wb_ref, wo_ref, fnw_ref, o_ref, *, final):
    acc = None
    for i, y_ref in enumerate((ya_ref, yb_ref, yc_ref, yd_ref)):
        gate = _sigmoid(g_ref[:, i * D_MODEL:(i + 1) * D_MODEL].astype(F32))
        term = gate * jnp.dot(y_ref[...], wb_ref[i], preferred_element_type=F32)
        acc = term if acc is None else acc + term
    out = x_ref[...] + jnp.dot(acc.astype(BF16), wo_ref[...], preferred_element_type=F32)
    if final:
        out = out * lax.rsqrt(jnp.mean(out * out, -1, keepdims=True) + EPS) * fnw_ref[...]
    o_ref[...] = out


def _merge(proj, ys, x, wb, wo, fnw, final):
    t = x.shape[0]
    tm = min(MERGE_ROW_TILE, t)
    yspec = pl.BlockSpec((tm, BRANCH_W), lambda i: (i, 0))
    return pl.pallas_call(
        functools.partial(_merge_kernel, final=final),
        grid=(t // tm,),
        in_specs=[
            pl.BlockSpec((tm, N_BRANCH * D_MODEL), lambda i: (i, COL_MERGE // (N_BRANCH * D_MODEL))),
            yspec, yspec, yspec, yspec,
            pl.BlockSpec((tm, D_MODEL), lambda i: (i, 0)),
            pl.BlockSpec((N_BRANCH, BRANCH_W, D_MODEL), lambda i: (0, 0, 0)),
            pl.BlockSpec((D_MODEL, D_MODEL), lambda i: (0, 0)),
            pl.BlockSpec((1, D_MODEL), lambda i: (0, 0)),
        ],
        out_specs=pl.BlockSpec((tm, D_MODEL), lambda i: (i, 0)),
        out_shape=jax.ShapeDtypeStruct((t, D_MODEL), F32),
        compiler_params=pltpu.CompilerParams(
            dimension_semantics=("arbitrary",),
            vmem_limit_bytes=VMEM_LIMIT_BYTES),
        name="merge",
    )(proj, *ys, x, wb, wo, fnw)


def _col_spec(bs, q, col, width):
    return pl.BlockSpec((bs, q, width), lambda b, c: (b, c, col // width))


def _row_spec(bs, q):
    return pl.BlockSpec((bs, N_SMALL, q), lambda b, c: (b, 0, c))


def _const_spec(shape):
    nd = len(shape)
    return pl.BlockSpec(shape, lambda b, c: (0,) * nd)


def _state_in_spec(layer, bs, tail):
    nd = len(tail)
    return pl.BlockSpec((1, bs) + tail, lambda b, c: (layer, b) + (0,) * nd)


def _state_out_spec(layer, bs, tail):
    nd = len(tail)
    return pl.BlockSpec((1, bs) + tail, lambda b, c: (layer, b) + (0,) * nd)


def _y_spec(bs, q):
    return pl.BlockSpec((bs, q, BRANCH_W), lambda b, c: (b, c, 0))


class _Mixer(NamedTuple):
    in_specs: list
    args: list
    out_specs: list
    out_shapes: list
    scratch: list
    group: Callable


_DONE = object()
_MIXER_PARAMS = pltpu.CompilerParams(
    dimension_semantics=("arbitrary", "arbitrary"),
    vmem_limit_bytes=VMEM_LIMIT_BYTES)


def _launch(mixers, prevs, *, grid, bs, nc, name):
    n_in = [len(m.in_specs) for m in mixers]
    n_out = [len(m.out_specs) for m in mixers]
    n_scr = [len(m.scratch) for m in mixers]
    tot_in, tot_out = sum(n_in), sum(n_out)
    state_idx = [sum(n_out[:i]) + k for i in range(len(mixers)) for k in range(1, n_out[i])]
    n_alias = 0 if prevs is None else len(state_idx)

    def kern(*refs):
        ins, outs, scr = refs[:tot_in], refs[tot_in + n_alias:tot_in + n_alias + tot_out], refs[tot_in + n_alias + tot_out:]
        groups = []
        for i, m in enumerate(mixers):
            groups.append(m.group(ins[sum(n_in[:i]):sum(n_in[:i + 1])],
                                  outs[sum(n_out[:i]):sum(n_out[:i + 1])],
                                  scr[sum(n_scr[:i]):sum(n_scr[:i + 1])]))
        c = pl.program_id(1)
        views = [[[r.at[s] for r in g[3]] for s in range(bs)] for g in groups]

        @pl.when(c == 0)
        def _():
            for g, v in zip(groups, views):
                for s in range(bs):
                    g[0](v[s])

        active = [g[1](v[s], g[4]) for g, v in zip(groups, views) for s in range(bs)]
        while active:
            active = [gen for gen in active if next(gen, _DONE) is not _DONE]

        @pl.when(c == nc - 1)
        def _():
            for g, v in zip(groups, views):
                for s in range(bs):
                    g[2](v[s])

    in_specs = [sp for m in mixers for sp in m.in_specs] + [pl.BlockSpec(memory_space=pl.ANY)] * n_alias
    args = [a for m in mixers for a in m.args]
    if prevs is not None:
        args += [a for p in prevs for a in p]
    outs = pl.pallas_call(
        kern,
        grid=grid,
        in_specs=in_specs,
        out_specs=[sp for m in mixers for sp in m.out_specs],
        out_shape=[sh for m in mixers for sh in m.out_shapes],
        scratch_shapes=[sc for m in mixers for sc in m.scratch],
        input_output_aliases={tot_in + k: idx for k, idx in enumerate(state_idx)} if n_alias else {},
        compiler_params=_MIXER_PARAMS,
        name=name,
    )(*args)
    return [outs[sum(n_out[:i]):sum(n_out[:i + 1])] for i in range(len(mixers))]


def _ssd_group(ins, outs, scr, *, q, valid, lowp):
    z_ref, x_ref, bc_ref, sc_ref, sr_ref, conv0_ref, h0_ref, *consts = ins
    y_ref, convo_ref, ho_ref = outs
    xp_scr, h_scr, y_scr = scr

    def init(v):
        v[9][CONV_PAD - 3:CONV_PAD, :] = v[5][...]
        v[10][...] = v[6][...]

    def final(v):
        v[8][...] = v[10][...]

    return (init, functools.partial(_ssd_main, q=q, valid=valid, lowp=lowp), final,
            (z_ref, x_ref, bc_ref, sc_ref, sr_ref, conv0_ref.at[0], h0_ref.at[0],
             y_ref, ho_ref.at[0], xp_scr, h_scr, y_scr, convo_ref.at[0]),
            tuple(consts))


def _ssd_main(v, k, *, q, valid, lowp):
    z_ref, x_ref, bc_ref, sc_ref, sr_ref, _, _, y_ref, _, xp_scr, h_scr, y_scr, convo_ref = v
    cw_ref, cb_ref, bcol_ref, brow_ref, alc_ref, alr_ref, dexp_ref, nw_ref, e_ref, tl_ref, tu_ref = k
    xp_scr[CONV_PAD:CONV_PAD + q, 0:BRANCH_W] = x_ref[...].astype(F32)
    xp_scr[CONV_PAD:CONV_PAD + q, BRANCH_W:SSD_CONV_CH] = bc_ref[...]
    conv, carry = _conv_chunk(xp_scr, cw_ref, cb_ref, q, valid)
    convo_ref[...] = carry
    yield
    xbc = _silu(conv)
    xs = xbc[:, 0:BRANCH_W]
    bm = xbc[:, BRANCH_W:BRANCH_W + 128]
    cm = xbc[:, BRANCH_W + 128:BRANCH_W + 256]

    dt_c = _softplus(sc_ref[...] + bcol_ref[...])
    a_c = dt_c * (-jnp.exp(alc_ref[...]))
    dt_r = _softplus(sr_ref[...] + brow_ref[...])
    a_r = dt_r * (-jnp.exp(alr_ref[...]))
    if valid < q:
        tcol = lax.broadcasted_iota(jnp.int32, (q, 1), 0)
        trow = lax.broadcasted_iota(jnp.int32, (1, q), 1)
        a_c = jnp.where(tcol < valid, a_c, 0.0)
        dt_c = jnp.where(tcol < valid, dt_c, 0.0)
        a_r = jnp.where(trow < valid, a_r, 0.0)

    tl = tl_ref[...]
    acum_c = _cumsum_rows(tl, a_c, lowp)
    acum_r = _xdot(a_r, tu_ref[...], NN, 0, lowp)
    e = e_ref[...]
    dt_e = _xdot(dt_c, e, NN, 0, lowp)
    acum_e = _xdot(acum_c, e, NN, 0, lowp)
    xdt = xs * dt_e
    xend = xdt * jnp.exp(acum_e[q - 1:q, :] - acum_e)
    eac = jnp.exp(acum_e)
    dec_last = jnp.exp(acum_c[q - 1:q, :])
    yield

    fine = q < 16
    tri = (lax.broadcasted_iota(jnp.int32, (q, q), 0) >= lax.broadcasted_iota(jnp.int32, (q, q), 1))
    rpg = SSD_HEADS // SSD_GROUPS
    for g in range(SSD_GROUPS):
        bg = bm[:, g * SSD_STATE:(g + 1) * SSD_STATE]
        cg = cm[:, g * SSD_STATE:(g + 1) * SSD_STATE]
        cb = _mm(cg, bg, lowp, NT)
        for r in range(rpg):
            h = g * rpg + r
            hs = slice(h * SSD_HEADDIM, (h + 1) * SSD_HEADDIM)
            seg = acum_c[:, h:h + 1] - acum_r[h:h + 1, :]
            m = cb * jnp.exp(jnp.where(tri, seg, NEG))
            hp = h_scr[h]
            yh = _mm(m, xdt[:, hs], lowp) + _mm(cg, hp, lowp, NT) * eac[:, hs]
            y_scr[:, hs] = yh
            if fine:
                yield
            h_scr[h] = hp * dec_last[:, h:h + 1] + _mm(xend[:, hs], bg, lowp, TN)
            yield

    y = y_scr[...] + xs * dexp_ref[...]
    y = y * _silu(z_ref[...].astype(F32))
    y = y * lax.rsqrt(jnp.mean(y * y, -1, keepdims=True) + EPS) * nw_ref[...]
    y_ref[...] = y.astype(BF16)


def _ssd(proj, proj32, srow, conv_state, h_state, layer, w, consts, *, b, q, nc, valid, lowp, bs):
    return _Mixer(
        in_specs=[
            _col_spec(bs, q, COL_A_Z, BRANCH_W),
            _col_spec(bs, q, COL_A_X, BRANCH_W),
            _col_spec(bs, q, COL32_A_BC, 256),
            _col_spec(bs, q, COL32_SMALL, 128),
            _row_spec(bs, q),
            _state_in_spec(layer, bs, (CONV_W - 1, SSD_CONV_CH)),
            _state_in_spec(layer, bs, (SSD_HEADS, SSD_HEADDIM, SSD_STATE)),
            _const_spec((CONV_W, SSD_CONV_CH)),
            _const_spec((1, SSD_CONV_CH)),
            _const_spec((1, 128)),
            _const_spec((N_SMALL, 1)),
            _const_spec((1, 128)),
            _const_spec((N_SMALL, 1)),
            _const_spec((1, BRANCH_W)),
            _const_spec((1, BRANCH_W)),
            _const_spec((128, BRANCH_W)),
            _const_spec((q, q)),
            _const_spec((q, q)),
        ],
        out_specs=[
            _y_spec(bs, q),
            _state_out_spec(layer, bs, (CONV_W - 1, SSD_CONV_CH)),
            _state_out_spec(layer, bs, (SSD_HEADS, SSD_HEADDIM, SSD_STATE)),
        ],
        out_shapes=[
            jax.ShapeDtypeStruct((b, nc * q, BRANCH_W), BF16),
            jax.ShapeDtypeStruct((DEPTH, b, CONV_W - 1, SSD_CONV_CH), F32),
            jax.ShapeDtypeStruct((DEPTH, b, SSD_HEADS, SSD_HEADDIM, SSD_STATE), F32),
        ],
        scratch=[
            pltpu.VMEM((bs, q + CONV_PAD, SSD_CONV_CH), F32),
            pltpu.VMEM((bs, SSD_HEADS, SSD_HEADDIM, SSD_STATE), F32),
            pltpu.VMEM((bs, q, BRANCH_W), F32),
        ],
        args=[proj, proj, proj32, proj32, srow, conv_state, h_state,
              w['ssd_conv_w'], w['ssd_conv_b'], w['bias_col'], w['bias_row'], w['alog_col'], w['alog_row'],
              w['ssd_D_exp'], w['ssd_norm_w'], consts['expand'], consts['tril'], consts['triu']],
        group=functools.partial(_ssd_group, q=q, valid=valid, lowp=lowp))


def _s5_kernel(u_ref, gate_ref, hr0_ref, hi0_ref, perm_ref, permt_ref, ar_ref, ai_ref,
               wbr_ref, wbi_ref, wcr_ref, wci_ref, d_ref, glu_ref,
               y_ref, hro_ref, hio_ref,
               hr_scr, hi_scr, cr_scr, ci_scr, *, q, nc, ngrp):
    c = pl.program_id(1)

    @pl.when(c == 0)
    def _():
        cr_scr[...] = hr0_ref[0]
        ci_scr[...] = hi0_ref[0]

    rows_in = perm_ref.shape[1]
    perm = perm_ref[...]
    u_tm = jnp.dot(perm, u_ref[...].reshape(rows_in, BRANCH_W), preferred_element_type=F32).astype(BF16)
    gate_tm = jnp.dot(perm, gate_ref[...].reshape(rows_in, BRANCH_W), preferred_element_type=F32)
    nblk = S5_LANES // 512
    for j in range(nblk):
        uj = u_tm[:, j * 128:(j + 1) * 128]
        hr_scr[:, j * 512:(j + 1) * 512] = jnp.dot(uj, wbr_ref[j], preferred_element_type=F32)
        hi_scr[:, j * 512:(j + 1) * 512] = jnp.dot(uj, wbi_ref[j], preferred_element_type=F32)

    half = S5_LANES // 2
    for hf in range(2):
        sl = slice(hf * half, (hf + 1) * half)
        ar = ar_ref[:, sl]
        ai = ai_ref[:, sl]

        def grp_body(g, carry, sl=sl, ar=ar, ai=ai):
            s0 = pl.multiple_of(g * 8, 8)

            def t_body(t, h):
                r0 = pl.multiple_of(g * (8 * q) + t * 8, 8)
                hr, hi = h
                nr = ar * hr - ai * hi + hr_scr[pl.ds(r0, 8), sl]
                ni = ar * hi + ai * hr + hi_scr[pl.ds(r0, 8), sl]
                hr_scr[pl.ds(r0, 8), sl] = nr
                hi_scr[pl.ds(r0, 8), sl] = ni
                return nr, ni

            h = lax.fori_loop(0, q, t_body, (cr_scr[pl.ds(s0, 8), sl], ci_scr[pl.ds(s0, 8), sl]),
                              unroll=min(q, 4))
            cr_scr[pl.ds(s0, 8), sl] = h[0]
            ci_scr[pl.ds(s0, 8), sl] = h[1]
            return carry

        lax.fori_loop(0, ngrp, grp_body, 0)

    parts = []
    for j in range(nblk):
        sl = slice(j * 512, (j + 1) * 512)
        parts.append(jnp.dot(hr_scr[:, sl].astype(BF16), wcr_ref[j], preferred_element_type=F32)
                     - jnp.dot(hi_scr[:, sl].astype(BF16), wci_ref[j], preferred_element_type=F32))
    y = jnp.concatenate(parts, axis=1) + u_tm.astype(F32) * d_ref[...]
    g = jax.nn.gelu(y)
    y = g * _sigmoid(jnp.dot(g.astype(BF16), glu_ref[...], preferred_element_type=F32))
    y = (y * _silu(gate_tm)).astype(BF16)
    y = jnp.dot(permt_ref[...], y, preferred_element_type=F32).astype(BF16)
    y_ref[...] = y.reshape(y_ref.shape)

    @pl.when(c == nc - 1)
    def _():
        hro_ref[0] = cr_scr[...]
        hio_ref[0] = ci_scr[...]


def _s5_perm(bb, lp, valid):
    perm = np.zeros((bb * valid, bb * lp), np.float32)
    for g in range(bb // 8):
        for t in range(valid):
            for s in range(8):
                perm[g * 8 * valid + t * 8 + s, (g * 8 + s) * lp + t] = 1.0
    return jnp.asarray(perm, BF16), jnp.asarray(perm.T, BF16)


def _s5(proj, hr_state, hi_state, layer, w, prev, *, b, lp, q, valid, bb):
    nc = lp // q
    perm, permt = _s5_perm(bb, q, valid)
    rows = bb * valid
    if nc == 1:
        def col_spec(col):
            return pl.BlockSpec((bb * q, BRANCH_W), lambda i, c: (i, col // BRANCH_W))
        src = proj
        y_spec = pl.BlockSpec((bb * q, BRANCH_W), lambda i, c: (i, 0))
        y_shape = jax.ShapeDtypeStruct((b * lp, BRANCH_W), BF16)
    else:
        def col_spec(col):
            return pl.BlockSpec((bb, q, BRANCH_W), lambda i, c: (i, c, col // BRANCH_W))
        src = proj.reshape(b, lp, N_PROJ)
        y_spec = pl.BlockSpec((bb, q, BRANCH_W), lambda i, c: (i, c, 0))
        y_shape = jax.ShapeDtypeStruct((b, lp, BRANCH_W), BF16)
    st_in = pl.BlockSpec((1, bb, S5_LANES), lambda i, c: (layer, i, 0))
    st_out = st_in
    kern = functools.partial(_s5_kernel, q=valid, nc=nc, ngrp=bb // 8)
    n_in = 14
    alias_kw = {}
    if prev is not None:
        base = kern

        def kern(*refs):
            return base(*refs[:n_in], *refs[n_in + 2:])

        alias_kw = dict(input_output_aliases={n_in: 1, n_in + 1: 2})
    y, hr, hi = pl.pallas_call(
        kern,
        grid=(b // bb, nc),
        **alias_kw,
        in_specs=[
            col_spec(COL_B_U),
            col_spec(COL_B_GATE),
            st_in, st_in,
            _const_spec(perm.shape),
            _const_spec(permt.shape),
            _const_spec((8, S5_LANES)),
            _const_spec((8, S5_LANES)),
            _const_spec((4, 128, 512)),
            _const_spec((4, 128, 512)),
            _const_spec((4, 512, 128)),
            _const_spec((4, 512, 128)),
            _const_spec((1, BRANCH_W)),
            _const_spec((BRANCH_W, BRANCH_W)),
        ] + [pl.BlockSpec(memory_space=pl.ANY)] * (0 if prev is None else 2),
        out_specs=[y_spec, st_out, st_out],
        out_shape=[
            y_shape,
            jax.ShapeDtypeStruct((DEPTH, b, S5_LANES), F32),
            jax.ShapeDtypeStruct((DEPTH, b, S5_LANES), F32),
        ],
        scratch_shapes=[
            pltpu.VMEM((rows, S5_LANES), F32),
            pltpu.VMEM((rows, S5_LANES), F32),
            pltpu.VMEM((bb, S5_LANES), F32),
            pltpu.VMEM((bb, S5_LANES), F32),
        ],
        compiler_params=_MIXER_PARAMS,
        name="s5",
    )(src, src, hr_state, hi_state, perm, permt, w['s5_ar'], w['s5_ai'],
      w['s5_wbr'], w['s5_wbi'], w['s5_wcr'], w['s5_wci'], w['s5_D'], w['s5_glu_w'], *(prev or ()))
    return y.reshape(b * lp, BRANCH_W), hr, hi


def _mlstm_group(ins, outs, scr, *, q, valid, lowp):
    x_ref, z_ref, o_ref, sc_ref, sr_ref, conv0_ref, c0_ref, n0_ref, m0_ref, *consts = ins
    y_ref, convo_ref, co_ref, no_ref, mo_ref = outs
    xp_scr, c_scr, n_scr, m_scr, h_scr = scr

    def init(v):
        v[13][CONV_PAD - 3:CONV_PAD, :] = v[5][...]
        v[14][...] = v[6][...]
        v[15][...] = v[7][...]
        v[16][...] = v[8][...]

    def final(v):
        v[10][...] = v[14][...]
        v[11][...] = v[15][...]
        v[12][...] = v[16][...]

    return (init, functools.partial(_mlstm_main, q=q, valid=valid, lowp=lowp), final,
            (x_ref, z_ref, o_ref, sc_ref, sr_ref, conv0_ref.at[0], c0_ref.at[0], n0_ref.at[0], m0_ref.at[0],
             y_ref, co_ref.at[0], no_ref.at[0], mo_ref.at[0], xp_scr, c_scr, n_scr, m_scr, h_scr,
             convo_ref.at[0]),
            tuple(consts))


def _mlstm_main(v, k, *, q, valid, lowp):
    (x_ref, z_ref, o_ref, sc_ref, sr_ref, _, _, _, _, y_ref, _, _, _,
     xp_scr, c_scr, n_scr, m_scr, h_scr, convo_ref) = v
    cw_ref, cb_ref, wq_ref, wk_ref, wv_ref, bcol_ref, brow_ref, nw_ref, skip_ref, tl_ref, tu_ref = k
    x = x_ref[...].astype(F32)
    xp_scr[CONV_PAD:CONV_PAD + q, :] = x
    conv, carry = _conv_chunk(xp_scr, cw_ref, cb_ref, q, valid)
    convo_ref[...] = carry
    yield
    xc = _silu(conv)

    pre_c = sc_ref[...] + bcol_ref[...]
    pre_r = sr_ref[...] + brow_ref[...]
    ig_c = pre_c
    lf_c = -_softplus(-pre_c)
    ig_r = pre_r
    lf_r = -_softplus(-pre_r)
    if valid < q:
        tcol = lax.broadcasted_iota(jnp.int32, (q, 1), 0)
        trow = lax.broadcasted_iota(jnp.int32, (1, q), 1)
        lf_c = jnp.where(tcol < valid, lf_c, 0.0)
        ig_c = jnp.where(tcol < valid, ig_c, NEG)
        lf_r = jnp.where(trow < valid, lf_r, 0.0)
        ig_r = jnp.where(trow < valid, ig_r, NEG)
    b_c = _cumsum_rows(tl_ref[...], lf_c, lowp)
    b_r = _xdot(lf_r, tu_ref[...], NN, 0, lowp)
    yield

    fine = q < 16
    tri = (lax.broadcasted_iota(jnp.int32, (q, q), 0) >= lax.broadcasted_iota(jnp.int32, (q, q), 1))
    o_all = o_ref[...].astype(F32)
    for h in range(ML_HEADS):
        hs = slice(h * ML_HEADDIM, (h + 1) * ML_HEADDIM)
        xh = xc[:, hs]
        qh = _mm(xh, wq_ref[h], lowp)
        kh = _mm(xh, wk_ref[h], lowp) * (ML_HEADDIM ** -0.5)
        vh = _mm(x[:, hs], wv_ref[h], lowp)
        yield
        bc = b_c[:, SMALL_F + h:SMALL_F + h + 1]
        br = b_r[SMALL_F + h:SMALL_F + h + 1, :]
        ic = ig_c[:, SMALL_I + h:SMALL_I + h + 1]
        ir = ig_r[SMALL_I + h:SMALL_I + h + 1, :]
        mp = m_scr[h:h + 1, 0:1]
        bq = jnp.broadcast_to(bc, (q, q))
        bc = bq if q == ML_HEADDIM else jnp.broadcast_to(bc, (q, ML_HEADDIM))
        ic = jnp.broadcast_to(ic, (q, ML_HEADDIM))
        dlog = jnp.where(tri, bq - br + ir, NEG)
        mrow = jnp.max(dlog, axis=1, keepdims=True)
        inter = bc + mp
        m_t = jnp.maximum(inter, mrow)
        wgt = jnp.exp(dlog - (m_t if q == ML_HEADDIM else jnp.maximum(bq + mp, mrow)))
        yield
        s = _mm(qh, kh, lowp, NT) * wgt
        if fine:
            yield
        scale = jnp.exp(inter - m_t)
        cp = c_scr[h]
        npv = n_scr[h:h + 1, :]
        num = _mm(s, vh, lowp) + scale * _mm(qh, cp, lowp)
        dot =jnp.sum(s, axis=1, keepdims=True) + scale * jnp.sum(qh * npv, axis=1, keepdims=True)
        hh = num / jnp.maximum(jnp.abs(dot), jnp.exp(-m_t))
        yield
        m_new = m_t[q - 1:q, :]
        b_last = bc[q - 1:q, :]
        w_end = jnp.exp(b_last - bc + ic - m_new)
        cs = jnp.exp(b_last + mp - m_new)
        if fine:
            yield
        kw = kh * w_end
        c_scr[h] = cs * cp + _mm(kw, vh, lowp, TN)
        n_scr[h:h + 1, :] = cs * npv + jnp.sum(kw, axis=0, keepdims=True)
        m_scr[h:h + 1, :] = jnp.broadcast_to(m_new, (1, 128))
        yield
        hh = hh * _sigmoid(o_all[:, hs])
        hc = hh - jnp.mean(hh, -1, keepdims=True)
        h_scr[:, hs] = hc * lax.rsqrt(jnp.mean(hc * hc, -1, keepdims=True) + EPS)
        yield

    y = h_scr[...] * nw_ref[...] + skip_ref[...] * xc
    y_ref[...] = (y * _silu(z_ref[...].astype(F32))).astype(BF16)


def _mlstm(proj, proj32, srow, conv_state, c_state, n_state, m_state, layer, w, consts,
           *, b, q, nc, valid, lowp, bs):
    return _Mixer(
        in_specs=[
            _col_spec(bs, q, COL_C_X, BRANCH_W),
            _col_spec(bs, q, COL_C_Z, BRANCH_W),
            _col_spec(bs, q, COL_C_O, BRANCH_W),
            _col_spec(bs, q, COL32_SMALL, 128),
            _row_spec(bs, q),
            _state_in_spec(layer, bs, (CONV_W - 1, BRANCH_W)),
            _state_in_spec(layer, bs, (ML_HEADS, ML_HEADDIM, ML_HEADDIM)),
            _state_in_spec(layer, bs, (ML_HEADS, ML_HEADDIM)),
            _state_in_spec(layer, bs, (8, 128)),
            _const_spec((CONV_W, BRANCH_W)),
            _const_spec((1, BRANCH_W)),
            _const_spec((ML_HEADS, ML_HEADDIM, ML_HEADDIM)),
            _const_spec((ML_HEADS, ML_HEADDIM, ML_HEADDIM)),
            _const_spec((ML_HEADS, ML_HEADDIM, ML_HEADDIM)),
            _const_spec((1, 128)),
            _const_spec((N_SMALL, 1)),
            _const_spec((1, BRANCH_W)),
            _const_spec((1, BRANCH_W)),
            _const_spec((q, q)),
            _const_spec((q, q)),
        ],
        out_specs=[
            _y_spec(bs, q),
            _state_out_spec(layer, bs, (CONV_W - 1, BRANCH_W)),
            _state_out_spec(layer, bs, (ML_HEADS, ML_HEADDIM, ML_HEADDIM)),
            _state_out_spec(layer, bs, (ML_HEADS, ML_HEADDIM)),
            _state_out_spec(layer, bs, (8, 128)),
        ],
        out_shapes=[
            jax.ShapeDtypeStruct((b, nc * q, BRANCH_W), BF16),
            jax.ShapeDtypeStruct((DEPTH, b, CONV_W - 1, BRANCH_W), F32),
            jax.ShapeDtypeStruct((DEPTH, b, ML_HEADS, ML_HEADDIM, ML_HEADDIM), F32),
            jax.ShapeDtypeStruct((DEPTH, b, ML_HEADS, ML_HEADDIM), F32),
            jax.ShapeDtypeStruct((DEPTH, b, 8, 128), F32),
        ],
        scratch=[
            pltpu.VMEM((bs, q + CONV_PAD, BRANCH_W), F32),
            pltpu.VMEM((bs, ML_HEADS, ML_HEADDIM, ML_HEADDIM), F32),
            pltpu.VMEM((bs, ML_HEADS, ML_HEADDIM), F32),
            pltpu.VMEM((bs, 8, 128), F32),
            pltpu.VMEM((bs, q, BRANCH_W), F32),
        ],
        args=[proj, proj, proj, proj32, srow, conv_state, c_state, n_state, m_state,
              w['ml_conv_w'], w['ml_conv_b'], w['ml_wq'], w['ml_wk'], w['ml_wv'], w['bias_col'], w['bias_row'],
              w['ml_norm_w'], w['ml_skip'], consts['tril'], consts['triu']],
        group=functools.partial(_mlstm_group, q=q, valid=valid, lowp=lowp))


def _hgrn_group(ins, outs, scr, *, q, valid, lowp, nlev):
    f_ref, i_ref, q_ref, g_ref, s0_ref, *consts = ins
    y_ref, so_ref = outs
    s_scr, o_scr = scr

    transposed = lowp

    def init(v):
        for h in range(HG_HEADS):
            v[7][h] = v[4][h].T if transposed else v[4][h]

    def final(v):
        for h in range(HG_HEADS):
            v[6][h] = v[7][h].T if transposed else v[7][h]

    return (init, functools.partial(_hgrn_main, q=q, valid=valid, lowp=lowp, nlev=nlev), final,
            (f_ref, i_ref, q_ref, g_ref, s0_ref.at[0], y_ref, so_ref.at[0], s_scr, o_scr),
            tuple(consts))


def _hgrn_main(v, k, *, q, valid, lowp, nlev):
    f_ref, i_ref, q_ref, g_ref, _, y_ref, _, s_scr, o_scr = v
    lb_ref, nw_ref, tl_ref = k
    lb = lb_ref[...]
    fg = lb + (1.0 - lb) * _sigmoid(f_ref[...])
    kk = 1.0 - fg
    qq = _silu(q_ref[...].astype(F32)) * (HG_HEADDIM ** -0.5)
    vv = i_ref[...] if lowp else i_ref[...].astype(F32)
    rcol =lax.broadcasted_iota(jnp.int32, (q, 1), 0)
    if valid < q:
        fg = jnp.where(rcol < valid, fg, 1.0)
        kk = jnp.where(rcol < valid, kk, 0.0)
    lf = jnp.log(fg)

    gcum = _cumsum_rows(tl_ref[...], lf, lowp)
    yield

    fine = q < 16
    rt = lax.broadcasted_iota(jnp.int32, (q, q), 0)
    cs = lax.broadcasted_iota(jnp.int32, (q, q), 1)
    rx = rt ^ cs
    att = [None] * HG_HEADS
    for lev in range(nlev):
        m = q >> (lev + 1)
        second = (rcol & m) != 0
        if m >= 4:
            nb = q // (2 * m)
            gb = jnp.broadcast_to(gcum.reshape(nb, 2 * m, BRANCH_W)[:, m - 1:m, :], (nb, 2 * m, BRANCH_W))
            e = jnp.exp(-jnp.abs(gcum - gb.reshape(q, BRANCH_W)))
        elif m == 2:
            r4 = rcol & 3
            e = jnp.where(r4 == 0, pltpu.roll(fg, q - 1, 0),
                          jnp.where(r4 == 1, 1.0, jnp.where(r4 == 2, fg, fg * pltpu.roll(fg, 1, 0))))
        else:
            e = jnp.where(second, fg, 1.0)
        u = jnp.where(second, qq, kk) * e
        if lowp:
            u = u.astype(BF16)
        pair = (rx >> int(math.log2(m))) == 1
        for h in range(HG_HEADS):
            hs = slice(h * HG_HEADDIM, (h + 1) * HG_HEADDIM)
            s = lax.dot_general(u[:, hs], u[:, hs], NT, preferred_element_type=F32)
            att[h] = jnp.where(pair, s, 0.0 if att[h] is None else att[h])
        yield

    qg = qq * jnp.exp(gcum)
    kend = kk * jnp.exp(gcum[q - 1:q, :] - gcum)
    ones = jnp.ones((q, 128), F32)
    for h in range(HG_HEADS):
        hs = slice(h * HG_HEADDIM, (h + 1) * HG_HEADDIM)
        diag = jnp.sum(qq[:, hs] * kk[:, hs], axis=1, keepdims=True)
        a = jnp.where(rt > cs, att[h], jnp.where(rt == cs, diag, 0.0))
        sp = s_scr[h]
        if lowp:
            oh = _mm(a, vv[:, hs], lowp) + _mm(qg[:, hs], sp, lowp, NT)
            s_scr[h] = sp * jnp.exp(gcum[q - 1:q, hs]) + _mm(vv[:, hs], kend[:, hs], lowp, TN)
        else:
            oh = _mm(a, vv[:, hs], lowp) + _mm(qg[:, hs], sp, lowp)
            if fine:
                yield
            dec = jnp.exp(_xdot(lf[:, hs], ones, TN, 0, lowp))
            s_scr[h] = sp * dec + _mm(kend[:, hs], vv[:, hs], lowp, TN)
        o_scr[:, hs] = oh * lax.rsqrt(jnp.mean(oh * oh, -1, keepdims=True) + EPS)
        yield

    y_ref[...] = (o_scr[...] * nw_ref[...] * _silu(g_ref[...].astype(F32))).astype(BF16)


def _hgrn(proj, proj32, s_state, layer, w, consts, *, b, q, nc, valid, lowp, bs):
    nlev = int(math.log2(q))
    return _Mixer(
        in_specs=[
            _col_spec(bs, q, COL32_D_F, BRANCH_W),
            _col_spec(bs, q, COL_D_I, BRANCH_W),
            _col_spec(bs, q, COL_D_Q, BRANCH_W),
            _col_spec(bs, q, COL_D_G, BRANCH_W),
            _state_in_spec(layer, bs, (HG_HEADS, HG_HEADDIM, HG_HEADDIM)),
            _const_spec((1, BRANCH_W)),
            _const_spec((1, BRANCH_W)),
            _const_spec((q, q)),
        ],
        out_specs=[
            _y_spec(bs, q),
            _state_out_spec(layer, bs, (HG_HEADS, HG_HEADDIM, HG_HEADDIM)),
        ],
        out_shapes=[
            jax.ShapeDtypeStruct((b, nc * q, BRANCH_W), BF16),
            jax.ShapeDtypeStruct((DEPTH, b, HG_HEADS, HG_HEADDIM, HG_HEADDIM), F32),
        ],
        scratch=[
            pltpu.VMEM((bs, HG_HEADS, HG_HEADDIM, HG_HEADDIM), F32),
            pltpu.VMEM((bs, q, BRANCH_W), F32),
        ],
        args=[proj32, proj, proj, proj, s_state, w['hg_lb'], w['hg_norm_w'], consts['tril']],
        group=functools.partial(_hgrn_group, q=q, valid=valid, lowp=lowp, nlev=nlev))


def _chunk_consts(q):
    r = np.arange(q)
    tril = (r[:, None] >= r[None, :]).astype(np.float32)
    expand = np.zeros((128, BRANCH_W), np.float32)
    for h in range(SSD_HEADS):
        expand[h, h * SSD_HEADDIM:(h + 1) * SSD_HEADDIM] = 1.0
    return {'tril': jnp.asarray(tril), 'triu': jnp.asarray(tril.T), 'expand': jnp.asarray(expand)}


def _cmul(ar, ai, br, bi):
    return ar * br - ai * bi, ar * bi + ai * br


def _layer_params(l, p, hg_lb):
    w_in_t = p['w_in'][l].T
    starts = np.cumsum([0] + [n for _, n in _IN_SPLITS])

    def rows(first, last):
        names = [name for name, _ in _IN_SPLITS]
        return w_in_t[starts[names.index(first)]:starts[names.index(last) + 1]]

    small = jnp.concatenate([rows('a_dt', 'a_dt'), rows('c_i', 'c_f')], axis=0)
    w_perm = jnp.concatenate([
        rows('merge', 'merge'), rows('a_z', 'a_x'), rows('b_u', 'b_gate'), rows('c_x', 'c_o'), rows('d_i', 'd_g'),
    ], axis=0).astype(BF16)
    w_perm32 = jnp.concatenate([
        rows('d_f', 'd_f'), rows('a_bc', 'a_bc'), small,
        jnp.zeros((N_PROJ32 - COL32_SMALL - N_SMALL, D_MODEL), F32),
    ], axis=0).astype(BF16)

    def lane_pad(v, off):
        return jnp.zeros((1, 128), F32).at[0, off:off + v.shape[0]].set(v)

    bias_col = (lane_pad(p['ssd_dt_bias'][l], SMALL_DT) + lane_pad(p['ml_i_bias'][l], SMALL_I)
                + lane_pad(p['ml_f_bias'][l], SMALL_F))
    alog_col = lane_pad(p['ssd_A_log'][l], SMALL_DT)

    dt = jnp.exp(p['s5_log_dt'][l])[:, None]
    lr = p['s5_A_re'][l]
    li = p['s5_A_im'][l]
    mag = jnp.exp(lr * dt)
    abr, abi = mag * jnp.cos(li * dt), mag * jnp.sin(li * dt)
    den = lr * lr + li * li
    cr = ((abr - 1.0) * lr + abi * li) / den
    ci = (abi * lr - (abr - 1.0) * li) / den
    bbr, bbi = _cmul(cr[..., None], ci[..., None], p['s5_B_re'][l], p['s5_B_im'][l])
    eye8 = jnp.eye(8, dtype=F32)

    def pack_b(bb):
        return jnp.einsum('jgpc,gh->jgchp', bb.reshape(4, 8, S5_STATE, S5_GROUP), eye8).reshape(4, 128, 512)

    def pack_c(cc):
        return jnp.einsum('jgcp,gh->jgphc', cc.reshape(4, 8, S5_GROUP, S5_STATE), eye8).reshape(4, 512, 128)

    return {
        'norm_w': p['norm_w'][l].reshape(1, D_MODEL),
        'w_in': w_perm, 'w_in32': w_perm32,
        'w_small_t': small.astype(BF16),
        'bias_col': bias_col, 'bias_row': bias_col[0, :N_SMALL].reshape(N_SMALL, 1),
        'alog_col': alog_col, 'alog_row': alog_col[0, :N_SMALL].reshape(N_SMALL, 1),
        'ssd_conv_w': p['ssd_conv_w'][l], 'ssd_conv_b': p['ssd_conv_b'][l].reshape(1, SSD_CONV_CH),
        'ssd_D_exp': jnp.repeat(p['ssd_D'][l], SSD_HEADDIM).reshape(1, BRANCH_W),
        'ssd_norm_w': p['ssd_norm_w'][l].reshape(1, BRANCH_W),
        's5_wbr': pack_b(bbr).astype(BF16), 's5_wbi': pack_b(bbi).astype(BF16),
        's5_wcr': pack_c(p['s5_C_re'][l]).astype(BF16), 's5_wci': pack_c(p['s5_C_im'][l]).astype(BF16),
        's5_ar': jnp.broadcast_to(abr.reshape(1, S5_LANES), (8, S5_LANES)),
        's5_ai': jnp.broadcast_to(abi.reshape(1, S5_LANES), (8, S5_LANES)),
        's5_D': p['s5_D'][l].reshape(1, BRANCH_W), 's5_glu_w': p['s5_glu_w'][l].astype(BF16),
        'ml_conv_w': p['ml_conv_w'][l], 'ml_conv_b': p['ml_conv_b'][l].reshape(1, BRANCH_W),
        'ml_wq': p['ml_wq'][l], 'ml_wk': p['ml_wk'][l], 'ml_wv': p['ml_wv'][l],
        'ml_norm_w': p['ml_norm_w'][l].reshape(1, BRANCH_W), 'ml_skip': p['ml_skip'][l].reshape(1, BRANCH_W),
        'hg_lb': hg_lb[l].reshape(1, BRANCH_W), 'hg_norm_w': p['hg_norm_w'][l].reshape(1, BRANCH_W),
        'w_branch': p['w_branch'][l].astype(BF16), 'w_out': p['w_out'][l].astype(BF16),
    }


_MATMUL_WEIGHTS = ('ml_wq', 'ml_wk', 'ml_wv')
S5_CHUNK = 64
MIXER_PLAN = [(('ssd', 'mlstm', 'hgrn'), 2)]
MIXER_PLAN_PADDED = [(('ssd', 'mlstm', 'hgrn'), 8)]


def _run_group(x3, states, layers, fnw, *, q, valid):
    b, lp, _ = x3.shape
    nc = lp // q
    t = b * lp
    lowp = q >= 16
    consts = _chunk_consts(q)
    ssd_conv, ssd_h, s5_re, s5_im, ml_conv, ml_c, ml_n, ml_m, hg_s = states
    s5_re = s5_re.reshape(DEPTH, b, S5_LANES)
    s5_im = s5_im.reshape(DEPTH, b, S5_LANES)
    if valid < q:
        s5_kw = dict(b=b, lp=lp, q=q, valid=valid, bb=min(b, 128))
    else:
        qs = math.gcd(lp, S5_CHUNK)
        s5_kw = dict(b=b, lp=lp, q=qs, valid=qs, bb=8)
    ml_m = jnp.broadcast_to(jnp.pad(ml_m, ((0, 0), (0, 0), (0, 8 - ML_HEADS)))[..., None], (DEPTH, b, 8, 128))
    plan = [(names, math.gcd(b, n)) for names, n in (MIXER_PLAN_PADDED if valid < q else MIXER_PLAN)]
    kw = dict(b=b, q=q, nc=nc, valid=valid, lowp=lowp)
    x = x3.reshape(t, D_MODEL)
    ys, st, st_s5 = {}, {}, None
    for l, w in enumerate(layers):
        if lowp:
            w = dict(w, **{k: w[k].astype(BF16) for k in _MATMUL_WEIGHTS})
        proj, proj32, small_t = _inproj(x, w['norm_w'], w['w_in'], w['w_in32'], w['w_small_t'])
        proj3 = proj.reshape(b, lp, N_PROJ)
        proj32 = proj32.reshape(b, lp, N_PROJ32)
        srow = small_t.reshape(N_SMALL, b, lp).transpose(1, 0, 2)
        mixers = {
            'ssd': lambda n: _ssd(proj3, proj32, srow, ssd_conv, ssd_h, l, w, consts, bs=n, **kw),
            'mlstm': lambda n: _mlstm(proj3, proj32, srow, ml_conv, ml_c, ml_n, ml_m, l, w, consts, bs=n, **kw),
            'hgrn': lambda n: _hgrn(proj3, proj32, hg_s, l, w, consts, bs=n, **kw),
        }
        for names, n in plan:
            outs = _launch([mixers[k](n) for k in names], None if l == 0 else [st[k] for k in names],
                           grid=(b // n, nc), bs=n, nc=nc, name='_'.join(names))
            for k, o in zip(names, outs):
                ys[k], st[k] = o[0], list(o[1:])
        yb, *st_s5 = _s5(proj, s5_re, s5_im, l, w, st_s5, **s5_kw)
        ya, yc, yd = (ys[k].reshape(t, BRANCH_W) for k in ('ssd', 'mlstm', 'hgrn'))
        x = _merge(proj, (ya, yb, yc, yd), x, w['w_branch'], w['w_out'], fnw, final=(l == DEPTH - 1))
    n_ssd_conv, n_ssd_h = st['ssd']
    n_re, n_im = (s.reshape(DEPTH, b, S5_GROUPS, S5_STATE) for s in st_s5)
    n_ml_conv, n_c, n_n, n_m = st['mlstm']
    new_states = (n_ssd_conv, n_ssd_h, n_re, n_im, n_ml_conv, n_c, n_n, n_m[:, :, :ML_HEADS, 0], st['hgrn'][0])
    return x.reshape(b, lp, D_MODEL), new_states


SAMPLE_PAD = 8


def kernel(x_prompt, x_sample, state_ssd_conv, state_ssd, state_s5_re, state_s5_im, state_mlstm_conv, state_mlstm_C, state_mlstm_n, state_mlstm_m, state_hgrn, norm_w, w_in, ssd_conv_w, ssd_conv_b, ssd_dt_bias, ssd_A_log, ssd_D, ssd_norm_w, s5_A_re, s5_A_im, s5_B_re, s5_B_im, s5_C_re, s5_C_im, s5_D, s5_log_dt, s5_glu_w, ml_conv_w, ml_conv_b, ml_wq, ml_wk, ml_wv, ml_i_bias, ml_f_bias, ml_norm_w, ml_skip, hg_lb_logits, hg_norm_w, w_branch, w_out, final_norm_w):
    p = {'norm_w': norm_w, 'w_in': w_in,
         'ssd_conv_w': ssd_conv_w, 'ssd_conv_b': ssd_conv_b, 'ssd_dt_bias': ssd_dt_bias,
         'ssd_A_log': ssd_A_log, 'ssd_D': ssd_D, 'ssd_norm_w': ssd_norm_w,
         's5_A_re': s5_A_re, 's5_A_im': s5_A_im, 's5_B_re': s5_B_re, 's5_B_im': s5_B_im,
         's5_C_re': s5_C_re, 's5_C_im': s5_C_im, 's5_D': s5_D, 's5_log_dt': s5_log_dt, 's5_glu_w': s5_glu_w,
         'ml_conv_w': ml_conv_w, 'ml_conv_b': ml_conv_b, 'ml_wq': ml_wq, 'ml_wk': ml_wk, 'ml_wv': ml_wv,
         'ml_i_bias': ml_i_bias, 'ml_f_bias': ml_f_bias, 'ml_norm_w': ml_norm_w, 'ml_skip': ml_skip,
         'hg_norm_w': hg_norm_w, 'w_branch': w_branch, 'w_out': w_out}
    lb_cum = jnp.cumsum(jax.nn.softmax(hg_lb_logits, axis=0), axis=0)
    hg_lb = lb_cum - lb_cum[0]
    layers = [_layer_params(l, p, hg_lb) for l in range(DEPTH)]
    fnw = final_norm_w.reshape(1, D_MODEL)

    sample_states = (state_ssd_conv, state_ssd, state_s5_re, state_s5_im, state_mlstm_conv,
                     state_mlstm_C, state_mlstm_n, state_mlstm_m, state_hgrn)
    bp, lp_, _ = x_prompt.shape
    prompt_states = tuple(jnp.zeros((DEPTH, bp) + s.shape[2:], F32) for s in sample_states)
    q_prompt = math.gcd(lp_, 128)
    y_prompt, new_p = _run_group(x_prompt, prompt_states, layers, fnw, q=q_prompt, valid=q_prompt)

    ls = x_sample.shape[1]
    xs_pad = jnp.pad(x_sample, ((0, 0), (0, SAMPLE_PAD - ls), (0, 0)))
    y_sample, new_s = _run_group(xs_pad, sample_states, layers, fnw, q=SAMPLE_PAD, valid=ls)
    y_sample = y_sample[:, :ls]

    out = [y_prompt, y_sample]
    for ps, ss in zip(new_p, new_s):
        out += [ps, ss]
    return tuple(out)
```

```python
import functools
import math
from typing import Callable, NamedTuple

import numpy as np
import jax
import jax.numpy as jnp
from jax import lax
from jax.experimental import pallas as pl
from jax.experimental.pallas import tpu as pltpu

F32 = jnp.float32
BF16 = jnp.bfloat16

D_MODEL = 1024
DEPTH = 2
BRANCH_W = 512
CONV_W = 4
EPS = 1e-6
SSD_HEADS = 8
SSD_HEADDIM = 64
SSD_STATE = 64
SSD_GROUPS = 2
SSD_CONV_CH = 768
S5_GROUPS = 32
S5_GROUP = 16
S5_STATE = 64
S5_LANES = S5_GROUPS * S5_STATE
ML_HEADS = 4
ML_HEADDIM = 128
HG_HEADS = 4
HG_HEADDIM = 128
N_BRANCH = 4

NEG = -1e30

VMEM_LIMIT_BYTES = 52 * 1024 * 1024
INPROJ_ROW_TILE = 1024
INPROJ_COL_STEPS = 3
MERGE_ROW_TILE = 1024

_IN_SPLITS = (('a_z', 512), ('a_x', 512), ('a_bc', 256), ('a_dt', 8), ('b_u', 512), ('b_gate', 512),
              ('c_x', 512), ('c_z', 512), ('c_o', 512), ('c_i', 4), ('c_f', 4),
              ('d_f', 512), ('d_i', 512), ('d_q', 512), ('d_g', 512), ('merge', 4096))

N_PROJ = 9216
COL_MERGE = 0
COL_A_Z = 4096
COL_A_X = 4608
COL_B_U = 5120
COL_B_GATE = 5632
COL_C_X = 6144
COL_C_Z = 6656
COL_C_O = 7168
COL_D_I = 7680
COL_D_Q = 8192
COL_D_G = 8704
N_PROJ32 = 896
COL32_D_F = 0
COL32_A_BC = 512
COL32_SMALL = 768
N_SMALL = 16
SMALL_DT = 0
SMALL_I = 8
SMALL_F = 12

NN = (((1,), (0,)), ((), ()))
NT = (((1,), (1,)), ((), ()))
TN = (((0,), (0,)), ((), ()))


def _mm(a, b, lowp, dims=NN):
    if lowp:
        a = a.astype(BF16)
        b = b.astype(BF16)
    return lax.dot_general(a, b, dims, preferred_element_type=F32)


def _split3(a):
    hi = a.astype(BF16)
    r = a - hi.astype(F32)
    mid = r.astype(BF16)
    lo = (r - mid.astype(F32)).astype(BF16)
    return hi, mid, lo


def _xdot(a, b, dims, split, lowp):
    if not lowp and min((b if split == 0 else a).shape) < 8:
        return lax.dot_general(a, b, dims, precision=lax.Precision.HIGHEST,
                               preferred_element_type=F32)
    if split == 0:
        other = b.astype(BF16)
        return sum(lax.dot_general(p, other, dims, preferred_element_type=F32) for p in _split3(a))
    other = a.astype(BF16)
    return sum(lax.dot_general(other, p, dims, preferred_element_type=F32) for p in _split3(b))


def _cumsum_rows(tl, x, lowp):
    q = x.shape[0]
    if q != 8:
        return _xdot(tl, x, NN, 1, lowp)
    r = lax.broadcasted_iota(jnp.int32, (q, 1), 0)
    for d in (1, 2, 4):
        x = x + jnp.where(r >= d, pltpu.roll(x, d, 0), 0.0)
    return x


def _sigmoid(x):
    return 0.5 * jnp.tanh(0.5 * x) + 0.5


def _silu(x):
    h = 0.5 * x
    return h * jnp.tanh(h) + h


def _softplus(x):
    return jnp.maximum(x, 0.0) + jnp.log1p(jnp.exp(-jnp.abs(x)))


CONV_PAD = 8


def _conv_chunk(xp_scr, cw_ref, cb_ref, q, valid):
    cw = cw_ref[...]
    p = CONV_PAD
    xp = xp_scr[...]
    conv = (cb_ref[...]
            + cw[0:1] * pltpu.roll(xp, 3, 0)[p:p + q, :]
            + cw[1:2] * pltpu.roll(xp, 2, 0)[p:p + q, :]
            + cw[2:3] * pltpu.roll(xp, 1, 0)[p:p + q, :]
            + cw[3:4] * xp[p:p + q, :])
    carry = xp_scr[p - 3 + valid:p + valid, :]
    xp_scr[p - 3:p, :] = carry
    return conv, carry


def _inproj_kernel(x_ref, nw_ref, w_ref, w32_ref, wst_ref, o_ref, o32_ref, st_ref, xn_scr):
    @pl.when(pl.program_id(1) == 0)
    def _():
        x = x_ref[...]
        xn = x * lax.rsqrt(jnp.mean(x * x, -1, keepdims=True) + EPS) * nw_ref[...]
        xb = xn.astype(BF16)
        xn_scr[...] = xb
        o32_ref[...] = lax.dot_general(xb, w32_ref[...], NT, preferred_element_type=F32)
        st_ref[...] = lax.dot_general(wst_ref[...], xb, NT, preferred_element_type=F32)

    o_ref[...] = lax.dot_general(xn_scr[...], w_ref[...], NT, preferred_element_type=F32).astype(BF16)


def _inproj(x, nw, w, w32, wst):
    t = x.shape[0]
    tm = min(INPROJ_ROW_TILE, t)
    tn = N_PROJ // INPROJ_COL_STEPS
    return pl.pallas_call(
        _inproj_kernel,
        grid=(t // tm, N_PROJ // tn),
        in_specs=[
            pl.BlockSpec((tm, D_MODEL), lambda i, j: (i, 0)),
            pl.BlockSpec((1, D_MODEL), lambda i, j: (0, 0)),
            pl.BlockSpec((tn, D_MODEL), lambda i, j: (j, 0)),
            pl.BlockSpec((N_PROJ32, D_MODEL), lambda i, j: (0, 0), pipeline_mode=pl.Buffered(1)),
            pl.BlockSpec((N_SMALL, D_MODEL), lambda i, j: (0, 0), pipeline_mode=pl.Buffered(1)),
        ],
        out_specs=[
            pl.BlockSpec((tm, tn), lambda i, j: (i, j)),
            pl.BlockSpec((tm, N_PROJ32), lambda i, j: (i, 0)),
            pl.BlockSpec((N_SMALL, tm), lambda i, j: (0, i)),
        ],
        out_shape=[
            jax.ShapeDtypeStruct((t, N_PROJ), BF16),
            jax.ShapeDtypeStruct((t, N_PROJ32), F32),
            jax.ShapeDtypeStruct((N_SMALL, t), F32),
        ],
        scratch_shapes=[pltpu.VMEM((tm, D_MODEL), BF16)],
        compiler_params=pltpu.CompilerParams(
            dimension_semantics=("arbitrary", "arbitrary"),
            vmem_limit_bytes=VMEM_LIMIT_BYTES),
        name="inproj",
    )(x, nw, w, w32, wst)


def _merge_kernel(g_ref, ya_ref, yb_ref, yc_ref, yd_ref, x_ref, wb_ref, wo_ref, fnw_ref, o_ref, *, final):
    acc = None
    for i, y_ref in enumerate((ya_ref, yb_ref, yc_ref, yd_ref)):
        gate = _sigmoid(g_ref[:, i * D_MODEL:(i + 1) * D_MODEL].astype(F32))
        term = gate * jnp.dot(y_ref[...], wb_ref[i], preferred_element_type=F32)
        acc = term if acc is None else acc + term
    out = x_ref[...] + jnp.dot(acc.astype(BF16), wo_ref[...], preferred_element_type=F32)
    if final:
        out = out * lax.rsqrt(jnp.mean(out * out, -1, keepdims=True) + EPS) * fnw_ref[...]
    o_ref[...] = out


def _merge(proj, ys, x, wb, wo, fnw, final):
    t = x.shape[0]
    tm = min(MERGE_ROW_TILE, t)
    yspec = pl.BlockSpec((tm, BRANCH_W), lambda i: (i, 0))
    return pl.pallas_call(
        functools.partial(_merge_kernel, final=final),
        grid=(t // tm,),
        in_specs=[
            pl.BlockSpec((tm, N_BRANCH * D_MODEL), lambda i: (i, COL_MERGE // (N_BRANCH * D_MODEL))),
            yspec, yspec, yspec, yspec,
            pl.BlockSpec((tm, D_MODEL), lambda i: (i, 0)),
            pl.BlockSpec((N_BRANCH, BRANCH_W, D_MODEL), lambda i: (0, 0, 0), pipeline_mode=pl.Buffered(1)),
            pl.BlockSpec((D_MODEL, D_MODEL), lambda i: (0, 0), pipeline_mode=pl.Buffered(1)),
            pl.BlockSpec((1, D_MODEL), lambda i: (0, 0)),
        ],
        out_specs=pl.BlockSpec((tm, D_MODEL), lambda i: (i, 0)),
        out_shape=jax.ShapeDtypeStruct((t, D_MODEL), F32),
        compiler_params=pltpu.CompilerParams(
            dimension_semantics=("arbitrary",),
            vmem_limit_bytes=VMEM_LIMIT_BYTES),
        name="merge",
    )(proj, *ys, x, wb, wo, fnw)


def _col_spec(bs, q, col, width):
    return pl.BlockSpec((bs, q, width), lambda b, c: (b, c, col // width))


def _row_spec(bs, q):
    return pl.BlockSpec((bs, N_SMALL, q), lambda b, c: (b, 0, c))


def _const_spec(shape):
    nd = len(shape)
    return pl.BlockSpec(shape, lambda b, c: (0,) * nd)


def _state_in_spec(layer, bs, tail):
    nd = len(tail)
    return pl.BlockSpec((1, bs) + tail, lambda b, c: (layer, b) + (0,) * nd)


def _state_out_spec(layer, bs, tail):
    nd = len(tail)
    return pl.BlockSpec((1, bs) + tail, lambda b, c: (layer, b) + (0,) * nd)


def _y_spec(bs, q):
    return pl.BlockSpec((bs, q, BRANCH_W), lambda b, c: (b, c, 0))


class _Mixer(NamedTuple):
    in_specs: list
    args: list
    out_specs: list
    out_shapes: list
    scratch: list
    group: Callable


_DONE = object()
_MIXER_PARAMS = pltpu.CompilerParams(
    dimension_semantics=("arbitrary", "arbitrary"),
    vmem_limit_bytes=VMEM_LIMIT_BYTES)


def _launch(mixers, prevs, *, grid, bs, nc, name):
    n_in = [len(m.in_specs) for m in mixers]
    n_out = [len(m.out_specs) for m in mixers]
    n_scr = [len(m.scratch) for m in mixers]
    tot_in, tot_out = sum(n_in), sum(n_out)
    state_idx = [sum(n_out[:i]) + k for i in range(len(mixers)) for k in range(1, n_out[i])]
    n_alias = 0 if prevs is None else len(state_idx)

    def kern(*refs):
        ins, outs, scr = refs[:tot_in], refs[tot_in + n_alias:tot_in + n_alias + tot_out], refs[tot_in + n_alias + tot_out:]
        groups = []
        for i, m in enumerate(mixers):
            groups.append(m.group(ins[sum(n_in[:i]):sum(n_in[:i + 1])],
                                  outs[sum(n_out[:i]):sum(n_out[:i + 1])],
                                  scr[sum(n_scr[:i]):sum(n_scr[:i + 1])]))
        c = pl.program_id(1)
        views = [[[r.at[s] for r in g[3]] for s in range(bs)] for g in groups]

        @pl.when(c == 0)
        def _():
            for g, v in zip(groups, views):
                for s in range(bs):
                    g[0](v[s])

        active = [g[1](v[s], g[4]) for g, v in zip(groups, views) for s in range(bs)]
        while active:
            active = [gen for gen in active if next(gen, _DONE) is not _DONE]

        @pl.when(c == nc - 1)
        def _():
            for g, v in zip(groups, views):
                for s in range(bs):
                    g[2](v[s])

    in_specs = [sp for m in mixers for sp in m.in_specs] + [pl.BlockSpec(memory_space=pl.ANY)] * n_alias
    args = [a for m in mixers for a in m.args]
    if prevs is not None:
        args += [a for p in prevs for a in p]
    outs = pl.pallas_call(
        kern,
        grid=grid,
        in_specs=in_specs,
        out_specs=[sp for m in mixers for sp in m.out_specs],
        out_shape=[sh for m in mixers for sh in m.out_shapes],
        scratch_shapes=[sc for m in mixers for sc in m.scratch],
        input_output_aliases={tot_in + k: idx for k, idx in enumerate(state_idx)} if n_alias else {},
        compiler_params=_MIXER_PARAMS,
        name=name,
    )(*args)
    return [outs[sum(n_out[:i]):sum(n_out[:i + 1])] for i in range(len(mixers))]


def _ssd_group(ins, outs, scr, *, q, valid, lowp):
    z_ref, x_ref, bc_ref, sc_ref, sr_ref, conv0_ref, h0_ref, *consts = ins
    y_ref, convo_ref, ho_ref = outs
    xp_scr, h_scr, y_scr = scr

    def init(v):
        v[9][CONV_PAD - 3:CONV_PAD, :] = v[5][...]
        v[10][...] = v[6][...]

    def final(v):
        v[8][...] = v[10][...]

    return (init, functools.partial(_ssd_main, q=q, valid=valid, lowp=lowp), final,
            (z_ref, x_ref, bc_ref, sc_ref, sr_ref, conv0_ref.at[0], h0_ref.at[0],
             y_ref, ho_ref.at[0], xp_scr, h_scr, y_scr, convo_ref.at[0]),
            tuple(consts))


def _ssd_main(v, k, *, q, valid, lowp):
    z_ref, x_ref, bc_ref, sc_ref, sr_ref, _, _, y_ref, _, xp_scr, h_scr, y_scr, convo_ref = v
    cw_ref, cb_ref, bcol_ref, brow_ref, alc_ref, alr_ref, dexp_ref, nw_ref, e_ref, tl_ref, tu_ref = k
    xp_scr[CONV_PAD:CONV_PAD + q, 0:BRANCH_W] = x_ref[...].astype(F32)
    xp_scr[CONV_PAD:CONV_PAD + q, BRANCH_W:SSD_CONV_CH] = bc_ref[...]
    conv, carry = _conv_chunk(xp_scr, cw_ref, cb_ref, q, valid)
    convo_ref[...] = carry
    yield
    xbc = _silu(conv)
    xs = xbc[:, 0:BRANCH_W]
    bm = xbc[:, BRANCH_W:BRANCH_W + 128]
    cm = xbc[:, BRANCH_W + 128:BRANCH_W + 256]

    dt_c = _softplus(sc_ref[...] + bcol_ref[...])
    a_c = dt_c * (-jnp.exp(alc_ref[...]))
    dt_r = _softplus(sr_ref[...] + brow_ref[...])
    a_r = dt_r * (-jnp.exp(alr_ref[...]))
    if valid < q:
        tcol = lax.broadcasted_iota(jnp.int32, (q, 1), 0)
        trow = lax.broadcasted_iota(jnp.int32, (1, q), 1)
        a_c = jnp.where(tcol < valid, a_c, 0.0)
        dt_c = jnp.where(tcol < valid, dt_c, 0.0)
        a_r = jnp.where(trow < valid, a_r, 0.0)

    tl = tl_ref[...]
    acum_c = _cumsum_rows(tl, a_c, lowp)
    acum_r = _xdot(a_r, tu_ref[...], NN, 0, lowp)
    e = e_ref[...]
    dt_e = _xdot(dt_c, e, NN, 0, lowp)
    acum_e = _xdot(acum_c, e, NN, 0, lowp)
    xdt = xs * dt_e
    xend = xdt * jnp.exp(acum_e[q - 1:q, :] - acum_e)
    eac = jnp.exp(acum_e)
    dec_last = jnp.exp(acum_c[q - 1:q, :])
    yield

    fine = q < 16
    tri = (lax.broadcasted_iota(jnp.int32, (q, q), 0) >= lax.broadcasted_iota(jnp.int32, (q, q), 1))
    rpg = SSD_HEADS // SSD_GROUPS
    for g in range(SSD_GROUPS):
        bg = bm[:, g * SSD_STATE:(g + 1) * SSD_STATE]
        cg = cm[:, g * SSD_STATE:(g + 1) * SSD_STATE]
        cb = _mm(cg, bg, lowp, NT)
        for r in range(rpg):
            h = g * rpg + r
            hs = slice(h * SSD_HEADDIM, (h + 1) * SSD_HEADDIM)
            seg = acum_c[:, h:h + 1] - acum_r[h:h + 1, :]
            m = cb * jnp.exp(jnp.where(tri, seg, NEG))
            hp = h_scr[h]
            yh = _mm(m, xdt[:, hs], lowp) + _mm(cg, hp, lowp, NT) * eac[:, hs]
            y_scr[:, hs] = yh
            if fine:
                yield
            h_scr[h] = hp * dec_last[:, h:h + 1] + _mm(xend[:, hs], bg, lowp, TN)
            yield

    y = y_scr[...] + xs * dexp_ref[...]
    y = y * _silu(z_ref[...].astype(F32))
    y = y * lax.rsqrt(jnp.mean(y * y, -1, keepdims=True) + EPS) * nw_ref[...]
    y_ref[...] = y.astype(BF16)


def _ssd(proj, proj32, srow, conv_state, h_state, layer, w, consts, *, b, q, nc, valid, lowp, bs):
    return _Mixer(
        in_specs=[
            _col_spec(bs, q, COL_A_Z, BRANCH_W),
            _col_spec(bs, q, COL_A_X, BRANCH_W),
            _col_spec(bs, q, COL32_A_BC, 256),
            _col_spec(bs, q, COL32_SMALL, 128),
            _row_spec(bs, q),
            _state_in_spec(layer, bs, (CONV_W - 1, SSD_CONV_CH)),
            _state_in_spec(layer, bs, (SSD_HEADS, SSD_HEADDIM, SSD_STATE)),
            _const_spec((CONV_W, SSD_CONV_CH)),
            _const_spec((1, SSD_CONV_CH)),
            _const_spec((1, 128)),
            _const_spec((N_SMALL, 1)),
            _const_spec((1, 128)),
            _const_spec((N_SMALL, 1)),
            _const_spec((1, BRANCH_W)),
            _const_spec((1, BRANCH_W)),
            _const_spec((128, BRANCH_W)),
            _const_spec((q, q)),
            _const_spec((q, q)),
        ],
        out_specs=[
            _y_spec(bs, q),
            _state_out_spec(layer, bs, (CONV_W - 1, SSD_CONV_CH)),
            _state_out_spec(layer, bs, (SSD_HEADS, SSD_HEADDIM, SSD_STATE)),
        ],
        out_shapes=[
            jax.ShapeDtypeStruct((b, nc * q, BRANCH_W), BF16),
            jax.ShapeDtypeStruct((DEPTH, b, CONV_W - 1, SSD_CONV_CH), F32),
            jax.ShapeDtypeStruct((DEPTH, b, SSD_HEADS, SSD_HEADDIM, SSD_STATE), F32),
        ],
        scratch=[
            pltpu.VMEM((bs, q + CONV_PAD, SSD_CONV_CH), F32),
            pltpu.VMEM((bs, SSD_HEADS, SSD_HEADDIM, SSD_STATE), F32),
            pltpu.VMEM((bs, q, BRANCH_W), F32),
        ],
        args=[proj, proj, proj32, proj32, srow, conv_state, h_state,
              w['ssd_conv_w'], w['ssd_conv_b'], w['bias_col'], w['bias_row'], w['alog_col'], w['alog_row'],
              w['ssd_D_exp'], w['ssd_norm_w'], consts['expand'], consts['tril'], consts['triu']],
        group=functools.partial(_ssd_group, q=q, valid=valid, lowp=lowp))


def _s5_kernel(u_ref, gate_ref, hr0_ref, hi0_ref, perm_ref, permt_ref, ar_ref, ai_ref,
               wbr_ref, wbi_ref, wcr_ref, wci_ref, d_ref, glu_ref,
               y_ref, hro_ref, hio_ref,
               hr_scr, hi_scr, cr_scr, ci_scr, *, q, nc, ngrp):
    c = pl.program_id(1)

    @pl.when(c == 0)
    def _():
        cr_scr[...] = hr0_ref[0]
        ci_scr[...] = hi0_ref[0]

    rows_in = perm_ref.shape[1]
    perm = perm_ref[...]
    u_tm = jnp.dot(perm, u_ref[...].reshape(rows_in, BRANCH_W), preferred_element_type=F32).astype(BF16)
    gate_tm = jnp.dot(perm, gate_ref[...].reshape(rows_in, BRANCH_W), preferred_element_type=F32)
    nblk = S5_LANES // 512
    for j in range(nblk):
        uj = u_tm[:, j * 128:(j + 1) * 128]
        hr_scr[:, j * 512:(j + 1) * 512] = jnp.dot(uj, wbr_ref[j], preferred_element_type=F32)
        hi_scr[:, j * 512:(j + 1) * 512] = jnp.dot(uj, wbi_ref[j], preferred_element_type=F32)

    half = S5_LANES // 2
    for hf in range(2):
        sl = slice(hf * half, (hf + 1) * half)
        ar = ar_ref[:, sl]
        ai = ai_ref[:, sl]

        def grp_body(g, carry, sl=sl, ar=ar, ai=ai):
            s0 = pl.multiple_of(g * 8, 8)

            def t_body(t, h):
                r0 = pl.multiple_of(g * (8 * q) + t * 8, 8)
                hr, hi = h
                nr = ar * hr - ai * hi + hr_scr[pl.ds(r0, 8), sl]
                ni = ar * hi + ai * hr + hi_scr[pl.ds(r0, 8), sl]
                hr_scr[pl.ds(r0, 8), sl] = nr
                hi_scr[pl.ds(r0, 8), sl] = ni
                return nr, ni

            h = lax.fori_loop(0, q, t_body, (cr_scr[pl.ds(s0, 8), sl], ci_scr[pl.ds(s0, 8), sl]),
                              unroll=min(q, 4))
            cr_scr[pl.ds(s0, 8), sl] = h[0]
            ci_scr[pl.ds(s0, 8), sl] = h[1]
            return carry

        lax.fori_loop(0, ngrp, grp_body, 0)

    parts = []
    for j in range(nblk):
        sl = slice(j * 512, (j + 1) * 512)
        parts.append(jnp.dot(hr_scr[:, sl].astype(BF16), wcr_ref[j], preferred_element_type=F32)
                     - jnp.dot(hi_scr[:, sl].astype(BF16), wci_ref[j], preferred_element_type=F32))
    y = jnp.concatenate(parts, axis=1) + u_tm.astype(F32) * d_ref[...]
    g = jax.nn.gelu(y)
    y = g * _sigmoid(jnp.dot(g.astype(BF16), glu_ref[...], preferred_element_type=F32))
    y = (y * _silu(gate_tm)).astype(BF16)
    y = jnp.dot(permt_ref[...], y, preferred_element_type=F32).astype(BF16)
    y_ref[...] = y.reshape(y_ref.shape)

    @pl.when(c == nc - 1)
    def _():
        hro_ref[0] = cr_scr[...]
        hio_ref[0] = ci_scr[...]


def _s5_perm(bb, lp, valid):
    perm = np.zeros((bb * valid, bb * lp), np.float32)
    for g in range(bb // 8):
        for t in range(valid):
            for s in range(8):
                perm[g * 8 * valid + t * 8 + s, (g * 8 + s) * lp + t] = 1.0
    return jnp.asarray(perm, BF16), jnp.asarray(perm.T, BF16)


def _s5(proj, hr_state, hi_state, layer, w, prev, *, b, lp, q, valid, bb):
    nc = lp // q
    perm, permt = _s5_perm(bb, q, valid)
    rows = bb * valid
    if nc == 1:
        def col_spec(col):
            return pl.BlockSpec((bb * q, BRANCH_W), lambda i, c: (i, col // BRANCH_W))
        src = proj
        y_spec = pl.BlockSpec((bb * q, BRANCH_W), lambda i, c: (i, 0))
        y_shape = jax.ShapeDtypeStruct((b * lp, BRANCH_W), BF16)
    else:
        def col_spec(col):
            return pl.BlockSpec((bb, q, BRANCH_W), lambda i, c: (i, c, col // BRANCH_W))
        src = proj.reshape(b, lp, N_PROJ)
        y_spec = pl.BlockSpec((bb, q, BRANCH_W), lambda i, c: (i, c, 0))
        y_shape = jax.ShapeDtypeStruct((b, lp, BRANCH_W), BF16)
    st_in = pl.BlockSpec((1, bb, S5_LANES), lambda i, c: (layer, i, 0))
    st_out = st_in
    kern = functools.partial(_s5_kernel, q=valid, nc=nc, ngrp=bb // 8)
    n_in = 14
    alias_kw = {}
    if prev is not None:
        base = kern

        def kern(*refs):
            return base(*refs[:n_in], *refs[n_in + 2:])

        alias_kw = dict(input_output_aliases={n_in: 1, n_in + 1: 2})
    y, hr, hi = pl.pallas_call(
        kern,
        grid=(b // bb, nc),
        **alias_kw,
        in_specs=[
            col_spec(COL_B_U),
            col_spec(COL_B_GATE),
            st_in, st_in,
            _const_spec(perm.shape),
            _const_spec(permt.shape),
            _const_spec((8, S5_LANES)),
            _const_spec((8, S5_LANES)),
            _const_spec((4, 128, 512)),
            _const_spec((4, 128, 512)),
            _const_spec((4, 512, 128)),
            _const_spec((4, 512, 128)),
            _const_spec((1, BRANCH_W)),
            _const_spec((BRANCH_W, BRANCH_W)),
        ] + [pl.BlockSpec(memory_space=pl.ANY)] * (0 if prev is None else 2),
        out_specs=[y_spec, st_out, st_out],
        out_shape=[
            y_shape,
            jax.ShapeDtypeStruct((DEPTH, b, S5_LANES), F32),
            jax.ShapeDtypeStruct((DEPTH, b, S5_LANES), F32),
        ],
        scratch_shapes=[
            pltpu.VMEM((rows, S5_LANES), F32),
            pltpu.VMEM((rows, S5_LANES), F32),
            pltpu.VMEM((bb, S5_LANES), F32),
            pltpu.VMEM((bb, S5_LANES), F32),
        ],
        compiler_params=_MIXER_PARAMS,
        name="s5",
    )(src, src, hr_state, hi_state, perm, permt, w['s5_ar'], w['s5_ai'],
      w['s5_wbr'], w['s5_wbi'], w['s5_wcr'], w['s5_wci'], w['s5_D'], w['s5_glu_w'], *(prev or ()))
    return y.reshape(b * lp, BRANCH_W), hr, hi


def _mlstm_group(ins, outs, scr, *, q, valid, lowp):
    x_ref, z_ref, o_ref, sc_ref, sr_ref, conv0_ref, c0_ref, n0_ref, m0_ref, *consts = ins
    y_ref, convo_ref, co_ref, no_ref, mo_ref = outs
    xp_scr, c_scr, n_scr, m_scr, h_scr = scr

    def init(v):
        v[13][CONV_PAD - 3:CONV_PAD, :] = v[5][...]
        v[14][...] = v[6][...]
        v[15][...] = v[7][...]
        v[16][...] = v[8][...]

    def final(v):
        v[10][...] = v[14][...]
        v[11][...] = v[15][...]
        v[12][...] = v[16][...]

    return (init, functools.partial(_mlstm_main, q=q, valid=valid, lowp=lowp), final,
            (x_ref, z_ref, o_ref, sc_ref, sr_ref, conv0_ref.at[0], c0_ref.at[0], n0_ref.at[0], m0_ref.at[0],
             y_ref, co_ref.at[0], no_ref.at[0], mo_ref.at[0], xp_scr, c_scr, n_scr, m_scr, h_scr,
             convo_ref.at[0]),
            tuple(consts))


def _mlstm_main(v, k, *, q, valid, lowp):
    (x_ref, z_ref, o_ref, sc_ref, sr_ref, _, _, _, _, y_ref, _, _, _,
     xp_scr, c_scr, n_scr, m_scr, h_scr, convo_ref) = v
    cw_ref, cb_ref, wq_ref, wk_ref, wv_ref, bcol_ref, brow_ref, nw_ref, skip_ref, tl_ref, tu_ref = k
    x = x_ref[...].astype(F32)
    xp_scr[CONV_PAD:CONV_PAD + q, :] = x
    conv, carry = _conv_chunk(xp_scr, cw_ref, cb_ref, q, valid)
    convo_ref[...] = carry
    yield
    xc = _silu(conv)

    pre_c = sc_ref[...] + bcol_ref[...]
    pre_r = sr_ref[...] + brow_ref[...]
    ig_c = pre_c
    lf_c = -_softplus(-pre_c)
    ig_r = pre_r
    lf_r = -_softplus(-pre_r)
    if valid < q:
        tcol = lax.broadcasted_iota(jnp.int32, (q, 1), 0)
        trow = lax.broadcasted_iota(jnp.int32, (1, q), 1)
        lf_c = jnp.where(tcol < valid, lf_c, 0.0)
        ig_c = jnp.where(tcol < valid, ig_c, NEG)
        lf_r = jnp.where(trow < valid, lf_r, 0.0)
        ig_r = jnp.where(trow < valid, ig_r, NEG)
    b_c = _cumsum_rows(tl_ref[...], lf_c, lowp)
    b_r = _xdot(lf_r, tu_ref[...], NN, 0, lowp)
    yield

    fine = q < 16
    tri = (lax.broadcasted_iota(jnp.int32, (q, q), 0) >= lax.broadcasted_iota(jnp.int32, (q, q), 1))
    o_all = o_ref[...].astype(F32)
    for h in range(ML_HEADS):
        hs = slice(h * ML_HEADDIM, (h + 1) * ML_HEADDIM)
        xh = xc[:, hs]
        qh = _mm(xh, wq_ref[h], lowp)
        kh = _mm(xh, wk_ref[h], lowp) * (ML_HEADDIM ** -0.5)
        vh = _mm(x[:, hs], wv_ref[h], lowp)
        yield
        bc = b_c[:, SMALL_F + h:SMALL_F + h + 1]
        br = b_r[SMALL_F + h:SMALL_F + h + 1, :]
        ic = ig_c[:, SMALL_I + h:SMALL_I + h + 1]
        ir = ig_r[SMALL_I + h:SMALL_I + h + 1, :]
        mp = m_scr[h:h + 1, 0:1]
        bq = jnp.broadcast_to(bc, (q, q))
        bc = bq if q == ML_HEADDIM else jnp.broadcast_to(bc, (q, ML_HEADDIM))
        ic = jnp.broadcast_to(ic, (q, ML_HEADDIM))
        dlog = jnp.where(tri, bq - br + ir, NEG)
        mrow = jnp.max(dlog, axis=1, keepdims=True)
        inter = bc + mp
        m_t = jnp.maximum(inter, mrow)
        wgt = jnp.exp(dlog - (m_t if q == ML_HEADDIM else jnp.maximum(bq + mp, mrow)))
        yield
        s = _mm(qh, kh, lowp, NT) * wgt
        if fine:
            yield
        scale = jnp.exp(inter - m_t)
        cp = c_scr[h]
        npv = n_scr[h:h + 1, :]
        num = _mm(s, vh, lowp) + scale * _mm(qh, cp, lowp)
        dot =jnp.sum(s, axis=1, keepdims=True) + scale * jnp.sum(qh * npv, axis=1, keepdims=True)
        hh = num / jnp.maximum(jnp.abs(dot), jnp.exp(-m_t))
        yield
        m_new = m_t[q - 1:q, :]
        b_last = bc[q - 1:q, :]
        w_end = jnp.exp(b_last - bc + ic - m_new)
        cs = jnp.exp(b_last + mp - m_new)
        if fine:
            yield
        kw = kh * w_end
        c_scr[h] = cs * cp + _mm(kw, vh, lowp, TN)
        n_scr[h:h + 1, :] = cs * npv + jnp.sum(kw, axis=0, keepdims=True)
        m_scr[h:h + 1, :] = jnp.broadcast_to(m_new, (1, 128))
        yield
        hh = hh * _sigmoid(o_all[:, hs])
        hc = hh - jnp.mean(hh, -1, keepdims=True)
        h_scr[:, hs] = hc * lax.rsqrt(jnp.mean(hc * hc, -1, keepdims=True) + EPS)
        yield

    y = h_scr[...] * nw_ref[...] + skip_ref[...] * xc
    y_ref[...] = (y * _silu(z_ref[...].astype(F32))).astype(BF16)


def _mlstm(proj, proj32, srow, conv_state, c_state, n_state, m_state, layer, w, consts,
           *, b, q, nc, valid, lowp, bs):
    return _Mixer(
        in_specs=[
            _col_spec(bs, q, COL_C_X, BRANCH_W),
            _col_spec(bs, q, COL_C_Z, BRANCH_W),
            _col_spec(bs, q, COL_C_O, BRANCH_W),
            _col_spec(bs, q, COL32_SMALL, 128),
            _row_spec(bs, q),
            _state_in_spec(layer, bs, (CONV_W - 1, BRANCH_W)),
            _state_in_spec(layer, bs, (ML_HEADS, ML_HEADDIM, ML_HEADDIM)),
            _state_in_spec(layer, bs, (ML_HEADS, ML_HEADDIM)),
            _state_in_spec(layer, bs, (8, 128)),
            _const_spec((CONV_W, BRANCH_W)),
            _const_spec((1, BRANCH_W)),
            _const_spec((ML_HEADS, ML_HEADDIM, ML_HEADDIM)),
            _const_spec((ML_HEADS, ML_HEADDIM, ML_HEADDIM)),
            _const_spec((ML_HEADS, ML_HEADDIM, ML_HEADDIM)),
            _const_spec((1, 128)),
            _const_spec((N_SMALL, 1)),
            _const_spec((1, BRANCH_W)),
            _const_spec((1, BRANCH_W)),
            _const_spec((q, q)),
            _const_spec((q, q)),
        ],
        out_specs=[
            _y_spec(bs, q),
            _state_out_spec(layer, bs, (CONV_W - 1, BRANCH_W)),
            _state_out_spec(layer, bs, (ML_HEADS, ML_HEADDIM, ML_HEADDIM)),
            _state_out_spec(layer, bs, (ML_HEADS, ML_HEADDIM)),
            _state_out_spec(layer, bs, (8, 128)),
        ],
        out_shapes=[
            jax.ShapeDtypeStruct((b, nc * q, BRANCH_W), BF16),
            jax.ShapeDtypeStruct((DEPTH, b, CONV_W - 1, BRANCH_W), F32),
            jax.ShapeDtypeStruct((DEPTH, b, ML_HEADS, ML_HEADDIM, ML_HEADDIM), F32),
            jax.ShapeDtypeStruct((DEPTH, b, ML_HEADS, ML_HEADDIM), F32),
            jax.ShapeDtypeStruct((DEPTH, b, 8, 128), F32),
        ],
        scratch=[
            pltpu.VMEM((bs, q + CONV_PAD, BRANCH_W), F32),
            pltpu.VMEM((bs, ML_HEADS, ML_HEADDIM, ML_HEADDIM), F32),
            pltpu.VMEM((bs, ML_HEADS, ML_HEADDIM), F32),
            pltpu.VMEM((bs, 8, 128), F32),
            pltpu.VMEM((bs, q, BRANCH_W), F32),
        ],
        args=[proj, proj, proj, proj32, srow, conv_state, c_state, n_state, m_state,
              w['ml_conv_w'], w['ml_conv_b'], w['ml_wq'], w['ml_wk'], w['ml_wv'], w['bias_col'], w['bias_row'],
              w['ml_norm_w'], w['ml_skip'], consts['tril'], consts['triu']],
        group=functools.partial(_mlstm_group, q=q, valid=valid, lowp=lowp))


def _hgrn_group(ins, outs, scr, *, q, valid, lowp, nlev):
    f_ref, i_ref, q_ref, g_ref, s0_ref, *consts = ins
    y_ref, so_ref = outs
    s_scr, o_scr = scr

    def init(v):
        v[7][...] = v[4][...]

    def final(v):
        v[6][...] = v[7][...]

    return (init, functools.partial(_hgrn_main, q=q, valid=valid, lowp=lowp, nlev=nlev), final,
            (f_ref, i_ref, q_ref, g_ref, s0_ref.at[0], y_ref, so_ref.at[0], s_scr, o_scr),
            tuple(consts))


def _hgrn_main(v, k, *, q, valid, lowp, nlev):
    f_ref, i_ref, q_ref, g_ref, _, y_ref, _, s_scr, o_scr = v
    lb_ref, nw_ref, tl_ref = k
    lb = lb_ref[...]
    fg = lb + (1.0 - lb) * _sigmoid(f_ref[...])
    kk = 1.0 - fg
    qq = _silu(q_ref[...].astype(F32)) * (HG_HEADDIM ** -0.5)
    vv = i_ref[...] if lowp else i_ref[...].astype(F32)
    rcol =lax.broadcasted_iota(jnp.int32, (q, 1), 0)
    if valid < q:
        fg = jnp.where(rcol < valid, fg, 1.0)
        kk = jnp.where(rcol < valid, kk, 0.0)
    lf = jnp.log(fg)

    gcum = _cumsum_rows(tl_ref[...], lf, lowp)
    yield

    fine = q < 16
    rt = lax.broadcasted_iota(jnp.int32, (q, q), 0)
    cs = lax.broadcasted_iota(jnp.int32, (q, q), 1)
    rx = rt ^ cs
    att = [None] * HG_HEADS
    for lev in range(nlev):
        m = q >> (lev + 1)
        second = (rcol & m) != 0
        if m >= 4:
            nb = q // (2 * m)
            gb = jnp.broadcast_to(gcum.reshape(nb, 2 * m, BRANCH_W)[:, m - 1:m, :], (nb, 2 * m, BRANCH_W))
            e = jnp.exp(-jnp.abs(gcum - gb.reshape(q, BRANCH_W)))
        elif m == 2:
            r4 = rcol & 3
            e = jnp.where(r4 == 0, pltpu.roll(fg, q - 1, 0),
                          jnp.where(r4 == 1, 1.0, jnp.where(r4 == 2, fg, fg * pltpu.roll(fg, 1, 0))))
        else:
            e = jnp.where(second, fg, 1.0)
        u = jnp.where(second, qq, kk) * e
        if lowp:
            u = u.astype(BF16)
        pair = (rx >> int(math.log2(m))) == 1
        for h in range(HG_HEADS):
            hs = slice(h * HG_HEADDIM, (h + 1) * HG_HEADDIM)
            s = lax.dot_general(u[:, hs], u[:, hs], NT, preferred_element_type=F32)
            att[h] = jnp.where(pair, s, 0.0 if att[h] is None else att[h])
        yield

    qg = qq * jnp.exp(gcum)
    kend = kk * jnp.exp(gcum[q - 1:q, :] - gcum)
    ones = jnp.ones((q, 128), F32)
    for h in range(HG_HEADS):
        hs = slice(h * HG_HEADDIM, (h + 1) * HG_HEADDIM)
        diag = jnp.sum(qq[:, hs] * kk[:, hs], axis=1, keepdims=True)
        a = jnp.where(rt > cs, att[h], jnp.where(rt == cs, diag, 0.0))
        sp = s_scr[h]
        oh = _mm(a, vv[:, hs], lowp) + _mm(qg[:, hs], sp, lowp)
        if fine:
            yield
        dec = jnp.exp(_xdot(lf[:, hs], ones, TN, 0, lowp))
        s_scr[h] =sp * dec + _mm(kend[:, hs], vv[:, hs], lowp, TN)
        o_scr[:, hs] = oh * lax.rsqrt(jnp.mean(oh * oh, -1, keepdims=True) + EPS)
        yield

    y_ref[...] = (o_scr[...] * nw_ref[...] * _silu(g_ref[...].astype(F32))).astype(BF16)


def _hgrn(proj, proj32, s_state, layer, w, consts, *, b, q, nc, valid, lowp, bs):
    nlev = int(math.log2(q))
    return _Mixer(
        in_specs=[
            _col_spec(bs, q, COL32_D_F, BRANCH_W),
            _col_spec(bs, q, COL_D_I, BRANCH_W),
            _col_spec(bs, q, COL_D_Q, BRANCH_W),
            _col_spec(bs, q, COL_D_G, BRANCH_W),
            _state_in_spec(layer, bs, (HG_HEADS, HG_HEADDIM, HG_HEADDIM)),
            _const_spec((1, BRANCH_W)),
            _const_spec((1, BRANCH_W)),
            _const_spec((q, q)),
        ],
        out_specs=[
            _y_spec(bs, q),
            _state_out_spec(layer, bs, (HG_HEADS, HG_HEADDIM, HG_HEADDIM)),
        ],
        out_shapes=[
            jax.ShapeDtypeStruct((b, nc * q, BRANCH_W), BF16),
            jax.ShapeDtypeStruct((DEPTH, b, HG_HEADS, HG_HEADDIM, HG_HEADDIM), F32),
        ],
        scratch=[
            pltpu.VMEM((bs, HG_HEADS, HG_HEADDIM, HG_HEADDIM), F32),
            pltpu.VMEM((bs, q, BRANCH_W), F32),
        ],
        args=[proj32, proj, proj, proj, s_state, w['hg_lb'], w['hg_norm_w'], consts['tril']],
        group=functools.partial(_hgrn_group, q=q, valid=valid, lowp=lowp, nlev=nlev))


def _chunk_consts(q):
    r = np.arange(q)
    tril = (r[:, None] >= r[None, :]).astype(np.float32)
    expand = np.zeros((128, BRANCH_W), np.float32)
    for h in range(SSD_HEADS):
        expand[h, h * SSD_HEADDIM:(h + 1) * SSD_HEADDIM] = 1.0
    return {'tril': jnp.asarray(tril), 'triu': jnp.asarray(tril.T), 'expand': jnp.asarray(expand)}


def _cmul(ar, ai, br, bi):
    return ar * br - ai * bi, ar * bi + ai * br


def _layer_params(l, p, hg_lb):
    w_in_t = p['w_in'][l].T
    starts = np.cumsum([0] + [n for _, n in _IN_SPLITS])

    def rows(first, last):
        names = [name for name, _ in _IN_SPLITS]
        return w_in_t[starts[names.index(first)]:starts[names.index(last) + 1]]

    small = jnp.concatenate([rows('a_dt', 'a_dt'), rows('c_i', 'c_f')], axis=0)
    w_perm = jnp.concatenate([
        rows('merge', 'merge'), rows('a_z', 'a_x'), rows('b_u', 'b_gate'), rows('c_x', 'c_o'), rows('d_i', 'd_g'),
    ], axis=0).astype(BF16)
    w_perm32 = jnp.concatenate([
        rows('d_f', 'd_f'), rows('a_bc', 'a_bc'), small,
        jnp.zeros((N_PROJ32 - COL32_SMALL - N_SMALL, D_MODEL), F32),
    ], axis=0).astype(BF16)

    def lane_pad(v, off):
        return jnp.zeros((1, 128), F32).at[0, off:off + v.shape[0]].set(v)

    bias_col = (lane_pad(p['ssd_dt_bias'][l], SMALL_DT) + lane_pad(p['ml_i_bias'][l], SMALL_I)
                + lane_pad(p['ml_f_bias'][l], SMALL_F))
    alog_col = lane_pad(p['ssd_A_log'][l], SMALL_DT)

    dt = jnp.exp(p['s5_log_dt'][l])[:, None]
    lr = p['s5_A_re'][l]
    li = p['s5_A_im'][l]
    mag = jnp.exp(lr * dt)
    abr, abi = mag * jnp.cos(li * dt), mag * jnp.sin(li * dt)
    den = lr * lr + li * li
    cr = ((abr - 1.0) * lr + abi * li) / den
    ci = (abi * lr - (abr - 1.0) * li) / den
    bbr, bbi = _cmul(cr[..., None], ci[..., None], p['s5_B_re'][l], p['s5_B_im'][l])
    eye8 = jnp.eye(8, dtype=F32)

    def pack_b(bb):
        return jnp.einsum('jgpc,gh->jgchp', bb.reshape(4, 8, S5_STATE, S5_GROUP), eye8).reshape(4, 128, 512)

    def pack_c(cc):
        return jnp.einsum('jgcp,gh->jgphc', cc.reshape(4, 8, S5_GROUP, S5_STATE), eye8).reshape(4, 512, 128)

    return {
        'norm_w': p['norm_w'][l].reshape(1, D_MODEL),
        'w_in': w_perm, 'w_in32': w_perm32,
        'w_small_t': small.astype(BF16),
        'bias_col': bias_col, 'bias_row': bias_col[0, :N_SMALL].reshape(N_SMALL, 1),
        'alog_col': alog_col, 'alog_row': alog_col[0, :N_SMALL].reshape(N_SMALL, 1),
        'ssd_conv_w': p['ssd_conv_w'][l], 'ssd_conv_b': p['ssd_conv_b'][l].reshape(1, SSD_CONV_CH),
        'ssd_D_exp': jnp.repeat(p['ssd_D'][l], SSD_HEADDIM).reshape(1, BRANCH_W),
        'ssd_norm_w': p['ssd_norm_w'][l].reshape(1, BRANCH_W),
        's5_wbr': pack_b(bbr).astype(BF16), 's5_wbi': pack_b(bbi).astype(BF16),
        's5_wcr': pack_c(p['s5_C_re'][l]).astype(BF16), 's5_wci': pack_c(p['s5_C_im'][l]).astype(BF16),
        's5_ar': jnp.broadcast_to(abr.reshape(1, S5_LANES), (8, S5_LANES)),
        's5_ai': jnp.broadcast_to(abi.reshape(1, S5_LANES), (8, S5_LANES)),
        's5_D': p['s5_D'][l].reshape(1, BRANCH_W), 's5_glu_w': p['s5_glu_w'][l].astype(BF16),
        'ml_conv_w': p['ml_conv_w'][l], 'ml_conv_b': p['ml_conv_b'][l].reshape(1, BRANCH_W),
        'ml_wq': p['ml_wq'][l], 'ml_wk': p['ml_wk'][l], 'ml_wv': p['ml_wv'][l],
        'ml_norm_w': p['ml_norm_w'][l].reshape(1, BRANCH_W), 'ml_skip': p['ml_skip'][l].reshape(1, BRANCH_W),
        'hg_lb': hg_lb[l].reshape(1, BRANCH_W), 'hg_norm_w': p['hg_norm_w'][l].reshape(1, BRANCH_W),
        'w_branch': p['w_branch'][l].astype(BF16), 'w_out': p['w_out'][l].astype(BF16),
    }


_MATMUL_WEIGHTS = ('ml_wq', 'ml_wk', 'ml_wv')
S5_CHUNK = 64
MIXER_PLAN = [(('ssd', 'mlstm', 'hgrn'), 2)]
MIXER_PLAN_PADDED = [(('ssd', 'mlstm', 'hgrn'), 8)]


def _run_group(x3, states, layers, fnw, *, q, valid):
    b, lp, _ = x3.shape
    nc = lp // q
    t = b * lp
    lowp = q >= 16
    consts = _chunk_consts(q)
    ssd_conv, ssd_h, s5_re, s5_im, ml_conv, ml_c, ml_n, ml_m, hg_s = states
    s5_re = s5_re.reshape(DEPTH, b, S5_LANES)
    s5_im = s5_im.reshape(DEPTH, b, S5_LANES)
    if valid < q:
        s5_kw = dict(b=b, lp=lp, q=q, valid=valid, bb=min(b, 128))
    else:
        qs = math.gcd(lp, S5_CHUNK)
        s5_kw = dict(b=b, lp=lp, q=qs, valid=qs, bb=8)
    ml_m = jnp.broadcast_to(jnp.pad(ml_m, ((0, 0), (0, 0), (0, 8 - ML_HEADS)))[..., None], (DEPTH, b, 8, 128))
    plan = [(names, math.gcd(b, n)) for names, n in (MIXER_PLAN_PADDED if valid < q else MIXER_PLAN)]
    kw = dict(b=b, q=q, nc=nc, valid=valid, lowp=lowp)
    x = x3.reshape(t, D_MODEL)
    ys, st, st_s5 = {}, {}, None
    for l, w in enumerate(layers):
        if lowp:
            w = dict(w, **{k: w[k].astype(BF16) for k in _MATMUL_WEIGHTS})
        proj, proj32, small_t = _inproj(x, w['norm_w'], w['w_in'], w['w_in32'], w['w_small_t'])
        proj3 = proj.reshape(b, lp, N_PROJ)
        proj32 = proj32.reshape(b, lp, N_PROJ32)
        srow = small_t.reshape(N_SMALL, b, lp).transpose(1, 0, 2)
        mixers = {
            'ssd': lambda n: _ssd(proj3, proj32, srow, ssd_conv, ssd_h, l, w, consts, bs=n, **kw),
            'mlstm': lambda n: _mlstm(proj3, proj32, srow, ml_conv, ml_c, ml_n, ml_m, l, w, consts, bs=n, **kw),
            'hgrn': lambda n: _hgrn(proj3, proj32, hg_s, l, w, consts, bs=n, **kw),
        }
        for names, n in plan:
            outs = _launch([mixers[k](n) for k in names], None if l == 0 else [st[k] for k in names],
                           grid=(b // n, nc), bs=n, nc=nc, name='_'.join(names))
            for k, o in zip(names, outs):
                ys[k], st[k] = o[0], list(o[1:])
        yb, *st_s5 = _s5(proj, s5_re, s5_im, l, w, st_s5, **s5_kw)
        ya, yc, yd = (ys[k].reshape(t, BRANCH_W) for k in ('ssd', 'mlstm', 'hgrn'))
        x = _merge(proj, (ya, yb, yc, yd), x, w['w_branch'], w['w_out'], fnw, final=(l == DEPTH - 1))
    n_ssd_conv, n_ssd_h = st['ssd']
    n_re, n_im = (s.reshape(DEPTH, b, S5_GROUPS, S5_STATE) for s in st_s5)
    n_ml_conv, n_c, n_n, n_m = st['mlstm']
    new_states = (n_ssd_conv, n_ssd_h, n_re, n_im, n_ml_conv, n_c, n_n, n_m[:, :, :ML_HEADS, 0], st['hgrn'][0])
    return x.reshape(b, lp, D_MODEL), new_states


SAMPLE_PAD = 8


def kernel(x_prompt, x_sample, state_ssd_conv, state_ssd, state_s5_re, state_s5_im, state_mlstm_conv, state_mlstm_C, state_mlstm_n, state_mlstm_m, state_hgrn, norm_w, w_in, ssd_conv_w, ssd_conv_b, ssd_dt_bias, ssd_A_log, ssd_D, ssd_norm_w, s5_A_re, s5_A_im, s5_B_re, s5_B_im, s5_C_re, s5_C_im, s5_D, s5_log_dt, s5_glu_w, ml_conv_w, ml_conv_b, ml_wq, ml_wk, ml_wv, ml_i_bias, ml_f_bias, ml_norm_w, ml_skip, hg_lb_logits, hg_norm_w, w_branch, w_out, final_norm_w):
    p = {'norm_w': norm_w, 'w_in': w_in,
         'ssd_conv_w': ssd_conv_w, 'ssd_conv_b': ssd_conv_b, 'ssd_dt_bias': ssd_dt_bias,
         'ssd_A_log': ssd_A_log, 'ssd_D': ssd_D, 'ssd_norm_w': ssd_norm_w,
         's5_A_re': s5_A_re, 's5_A_im': s5_A_im, 's5_B_re': s5_B_re, 's5_B_im': s5_B_im,
         's5_C_re': s5_C_re, 's5_C_im': s5_C_im, 's5_D': s5_D, 's5_log_dt': s5_log_dt, 's5_glu_w': s5_glu_w,
         'ml_conv_w': ml_conv_w, 'ml_conv_b': ml_conv_b, 'ml_wq': ml_wq, 'ml_wk': ml_wk, 'ml_wv': ml_wv,
         'ml_i_bias': ml_i_bias, 'ml_f_bias': ml_f_bias, 'ml_norm_w': ml_norm_w, 'ml_skip': ml_skip,
         'hg_norm_w': hg_norm_w, 'w_branch': w_branch, 'w_out': w_out}
    lb_cum = jnp.cumsum(jax.nn.softmax(hg_lb_logits, axis=0), axis=0)
    hg_lb = lb_cum - lb_cum[0]
    layers = [_layer_params(l, p, hg_lb) for l in range(DEPTH)]
    fnw = final_norm_w.reshape(1, D_MODEL)

    sample_states = (state_ssd_conv, state_ssd, state_s5_re, state_s5_im, state_mlstm_conv,
                     state_mlstm_C, state_mlstm_n, state_mlstm_m, state_hgrn)
    bp, lp_, _ = x_prompt.shape
    prompt_states = tuple(jnp.zeros((DEPTH, bp) + s.shape[2:], F32) for s in sample_states)
    q_prompt = math.gcd(lp_, 128)
    y_prompt, new_p = _run_group(x_prompt, prompt_states, layers, fnw, q=q_prompt, valid=q_prompt)

    ls = x_sample.shape[1]
    xs_pad = jnp.pad(x_sample, ((0, 0), (0, SAMPLE_PAD - ls), (0, 0)))
    y_sample, new_s = _run_group(xs_pad, sample_states, layers, fnw, q=SAMPLE_PAD, valid=ls)
    y_sample = y_sample[:, :ls]

    out = [y_prompt, y_sample]
    for ps, ss in zip(new_p, new_s):
        out += [ps, ss]
    return tuple(out)
```
